```python
import math
import jax, jax.numpy as jnp
from jax import lax
import numpy as np

D_MODEL = 1024
BATCH = 1
SEQ = 16384
DEPTH = 1

N_HEADS = 8
N_KV_GROUPS = 2
GROUP_SIZE = N_HEADS // N_KV_GROUPS
HEAD_DIM = 64
NSA_WIDTH = N_HEADS * HEAD_DIM
KV_WIDTH = N_KV_GROUPS * HEAD_DIM
CMP_STRIDE = 16
CMP_BLOCK = 2 * CMP_STRIDE
CMP_HIDDEN = 256
SEL_BLOCK = 64
SEL_PER_CMP = SEL_BLOCK // CMP_STRIDE
N_SELECT = 16
WINDOW = 512
Q_BLOCK = 128
LRU_WIDTH = 512
LRU_BLOCKS = 8
LRU_BLOCK_DIM = LRU_WIDTH // LRU_BLOCKS
CONV_WIDTH = 4
LRU_C = 8.0
N_BUCKETS = 32
MAX_DISTANCE = 128
N_BRANCHES = 2
EPS = 1e-6
NEG = -1e30
FORCE_SCORE = 1e9
IN_SIZES = (NSA_WIDTH, 6 * KV_WIDTH, NSA_WIDTH, 3 * N_HEADS, LRU_WIDTH, LRU_WIDTH, N_BRANCHES * D_MODEL)
D_IN = NSA_WIDTH + 6 * KV_WIDTH + NSA_WIDTH + 3 * N_HEADS + LRU_WIDTH + LRU_WIDTH + N_BRANCHES * D_MODEL

kernel_name = "hybrid_nsa_rglru_gated_block"


def rms_norm(x, g):
    xf = x.astype(jnp.float32)
    y = xf * lax.rsqrt(jnp.mean(xf * xf, axis=-1, keepdims=True) + EPS)
    return (y * g.astype(jnp.float32)).astype(x.dtype)


def t5_bucket(dist):
    n = jnp.maximum(dist, 0)
    max_exact = N_BUCKETS // 2
    nf = jnp.maximum(n, 1).astype(jnp.float32)
    large = max_exact + (jnp.log(nf / max_exact) / math.log(MAX_DISTANCE / max_exact)
                         * (N_BUCKETS - max_exact)).astype(jnp.int32)
    large = jnp.minimum(large, N_BUCKETS - 1)
    return jnp.where(n < max_exact, n, large)


def masked_softmax(s, mask):
    s = jnp.where(mask, s.astype(jnp.float32), NEG)
    return jax.nn.softmax(s, axis=-1) * mask


def compress_blocks(kv, pe, w1, b1, w2):
    B, S, G, dh = kv.shape
    ch = kv.reshape(B, S // CMP_STRIDE, CMP_STRIDE, G, dh)
    blocks = jnp.concatenate([ch[:, :-1], ch[:, 1:]], axis=2)
    blocks = blocks + pe[None, None, :, None, :]
    nc = blocks.shape[1]
    flat = blocks.transpose(0, 1, 3, 2, 4).reshape(B, nc, G, CMP_BLOCK * dh)
    return jax.nn.silu(flat @ w1 + b1) @ w2


def setup_inputs(seed: int = 0) -> dict:
    key = jax.random.key(seed)
    ks = jax.random.split(key, 24)
    nrm = lambda k, shape, s: jax.random.normal(k, shape, jnp.float32) * s
    u = jax.random.uniform(ks[17], (LRU_WIDTH,), jnp.float32, 0.9, 0.999)
    sig = u ** (1.0 / LRU_C)
    lru_lambda = jnp.log(sig) - jnp.log1p(-sig)
    return {
        "x": nrm(ks[0], (BATCH, SEQ, D_MODEL), 1.0),
        "norm_gain": 1.0 + nrm(ks[1], (D_MODEL,), 0.1),
        "w_in": nrm(ks[2], (D_MODEL, D_IN), D_MODEL ** -0.5),
        "q_norm_gain": 1.0 + nrm(ks[3], (HEAD_DIM,), 0.1),
        "k_norm_gain": 1.0 + nrm(ks[4], (3, HEAD_DIM), 0.1),
        "cmp_pe": nrm(ks[5], (2, CMP_BLOCK, HEAD_DIM), 0.5),
        "cmp_w1": nrm(ks[6], (2, CMP_BLOCK * HEAD_DIM, CMP_HIDDEN), (CMP_BLOCK * HEAD_DIM) ** -0.5),
        "cmp_b1": nrm(ks[7], (2, CMP_HIDDEN), 0.02),
        "cmp_w2": nrm(ks[8], (2, CMP_HIDDEN, HEAD_DIM), CMP_HIDDEN ** -0.5),
        "rel_bias": nrm(ks[9], (N_BUCKETS, N_HEADS), 0.5),
        "conv_w": nrm(ks[10], (CONV_WIDTH, LRU_WIDTH), CONV_WIDTH ** -0.5),
        "conv_b": nrm(ks[11], (LRU_WIDTH,), 0.02),
        "lru_wa": nrm(ks[12], (LRU_BLOCKS, LRU_BLOCK_DIM, LRU_BLOCK_DIM), LRU_BLOCK_DIM ** -0.5),
        "lru_ba": nrm(ks[13], (LRU_WIDTH,), 0.02),
        "lru_wx": nrm(ks[14], (LRU_BLOCKS, LRU_BLOCK_DIM, LRU_BLOCK_DIM), LRU_BLOCK_DIM ** -0.5),
        "lru_bx": nrm(ks[15], (LRU_WIDTH,), 0.02),
        "lru_lambda": lru_lambda,
        "w_proj_a": nrm(ks[18], (NSA_WIDTH, D_MODEL), NSA_WIDTH ** -0.5),
        "w_proj_b": nrm(ks[19], (LRU_WIDTH, D_MODEL), LRU_WIDTH ** -0.5),
        "w_out": nrm(ks[20], (D_MODEL, D_MODEL), D_MODEL ** -0.5),
    }


def nsa_mixer(q, k_cmp, v_cmp, k_slc, v_slc, k_win, v_win, br_gate, q_norm_gain, k_norm_gain,
              cmp_pe, cmp_w1, cmp_b1, cmp_w2, rel_bias):
    B, S, _ = q.shape
    G, R, dh = N_KV_GROUPS, GROUP_SIZE, HEAD_DIM
    nsb = S // SEL_BLOCK
    n_sel = min(N_SELECT, nsb)
    scale = dh ** -0.5

    q = (rms_norm(q.reshape(B, S, G, R, dh), q_norm_gain) * scale).transpose(0, 2, 3, 1, 4)
    kc = rms_norm(compress_blocks(k_cmp, cmp_pe[0], cmp_w1[0], cmp_b1[0], cmp_w2[0]), k_norm_gain[0])
    vc = compress_blocks(v_cmp, cmp_pe[1], cmp_w1[1], cmp_b1[1], cmp_w2[1])
    kc = kc.transpose(0, 2, 1, 3)
    vc = vc.transpose(0, 2, 1, 3)
    nc = kc.shape[2]
    k_blocks = rms_norm(k_slc, k_norm_gain[1]).transpose(0, 2, 1, 3).reshape(B, G, nsb, SEL_BLOCK, dh)
    v_blocks = v_slc.transpose(0, 2, 1, 3).reshape(B, G, nsb, SEL_BLOCK, dh)
    pad_w = ((0, 0), (0, 0), (WINDOW, 0), (0, 0))
    k_wp = jnp.pad(rms_norm(k_win, k_norm_gain[2]).transpose(0, 2, 1, 3), pad_w)
    v_wp = jnp.pad(v_win.transpose(0, 2, 1, 3), pad_w)

    bias_gr = rel_bias.reshape(N_BUCKETS, G, R)
    def shared_bias(dist):
        return rel_bias[t5_bucket(dist)].transpose(2, 0, 1).reshape(G, R, *dist.shape)
    def group_bias(dist):
        b = jax.vmap(lambda tab, bk: tab[bk], in_axes=(1, 1), out_axes=1)(bias_gr, t5_bucket(dist))
        return jnp.moveaxis(b, -1, 2)

    c_end = jnp.arange(nc, dtype=jnp.int32) * CMP_STRIDE + CMP_BLOCK - 1
    blk = jnp.arange(nsb, dtype=jnp.int32)
    bi = jnp.arange(B)[:, None, None, None]
    gi = jnp.arange(G)[None, :, None, None]
    right_pad = SEL_PER_CMP * nsb + SEL_PER_CMP - 1 - nc

    def block_fn(q0):
        qc = lax.dynamic_slice_in_dim(q, q0, Q_BLOCK, axis=3)
        t = q0 + jnp.arange(Q_BLOCK, dtype=jnp.int32)
        dist = t[:, None] - c_end[None, :]
        s = jnp.einsum('bgrqd,bgcd->bgrqc', qc, kc) + shared_bias(dist)
        p_cmp = masked_softmax(s, dist >= 0)
        o_cmp = jnp.einsum('bgrqc,bgcd->bgrqd', p_cmp.astype(vc.dtype), vc)
        imp = jnp.pad(p_cmp.sum(axis=2), ((0, 0), (0, 0), (0, 0), (1, right_pad)))
        imp_blk = (imp[..., :SEL_PER_CMP * nsb].reshape(B, G, Q_BLOCK, nsb, SEL_PER_CMP).sum(-1)
                   + imp[..., SEL_PER_CMP:SEL_PER_CMP * nsb + SEL_PER_CMP:SEL_PER_CMP])
        cur = (t // SEL_BLOCK)[:, None]
        valid = blk[None, :] <= cur
        force = (blk[None, :] == 0) | (blk[None, :] == cur) | (blk[None, :] == cur - 1)
        score = jnp.where(valid, jnp.where(force, FORCE_SCORE, imp_blk), -1.0)
        _, idx = lax.top_k(score, n_sel)
        kb = k_blocks[bi, gi, idx].reshape(B, G, Q_BLOCK, n_sel * SEL_BLOCK, dh)
        vb = v_blocks[bi, gi, idx].reshape(B, G, Q_BLOCK, n_sel * SEL_BLOCK, dh)
        pos = (idx[..., None] * SEL_BLOCK + jnp.arange(SEL_BLOCK, dtype=jnp.int32)).reshape(B, G, Q_BLOCK, -1)
        dist = t[None, None, :, None] - pos
        s = jnp.einsum('bgrqd,bgqkd->bgrqk', qc, kb) + group_bias(dist)
        p = masked_softmax(s, (dist >= 0)[:, :, None])
        o_slc = jnp.einsum('bgrqk,bgqkd->bgrqd', p.astype(vb.dtype), vb)
        kw = lax.dynamic_slice_in_dim(k_wp, q0, WINDOW + Q_BLOCK, axis=2)
        vw = lax.dynamic_slice_in_dim(v_wp, q0, WINDOW + Q_BLOCK, axis=2)
        pos = q0 - WINDOW + jnp.arange(WINDOW + Q_BLOCK, dtype=jnp.int32)
        dist = t[:, None] - pos[None, :]
        mask = (dist >= 0) & (dist < WINDOW) & (pos[None, :] >= 0)
        s = jnp.einsum('bgrqd,bgkd->bgrqk', qc, kw) + shared_bias(dist)
        p = masked_softmax(s, mask)
        o_win = jnp.einsum('bgrqk,bgkd->bgrqd', p.astype(vw.dtype), vw)
        return o_cmp, o_slc, o_win

    starts = jnp.arange(0, S, Q_BLOCK, dtype=jnp.int32)
    outs = lax.map(block_fn, starts)
    o = jnp.stack([ob.transpose(1, 0, 4, 2, 3, 5).reshape(B, S, N_HEADS, dh) for ob in outs], axis=2)
    g = jax.nn.sigmoid(br_gate.reshape(B, S, 3, N_HEADS, 1))
    return (g * o).sum(axis=2).reshape(B, S, NSA_WIDTH)


def rglru_mixer(u, conv_w, conv_b, lru_wa, lru_ba, lru_wx, lru_bx, lru_lambda):
    B, S, W = u.shape
    up = jnp.pad(u, ((0, 0), (CONV_WIDTH - 1, 0), (0, 0)))
    uc = conv_b + sum(up[:, k:k + S] * conv_w[k] for k in range(CONV_WIDTH))
    ub = uc.reshape(B, S, LRU_BLOCKS, LRU_BLOCK_DIM)
    r = jax.nn.sigmoid(jnp.einsum('bsnd,nde->bsne', ub, lru_wa).reshape(B, S, W) + lru_ba)
    i = jax.nn.sigmoid(jnp.einsum('bsnd,nde->bsne', ub, lru_wx).reshape(B, S, W) + lru_bx)
    log_a = (-LRU_C * r.astype(jnp.float32)) * jax.nn.softplus(-lru_lambda.astype(jnp.float32))
    a = jnp.exp(log_a)
    b = jnp.sqrt(-jnp.expm1(2.0 * log_a)) * (i * uc).astype(jnp.float32)
    def comb(left, right):
        a1, b1 = left
        a2, b2 = right
        return a1 * a2, a2 * b1 + b2
    _, h = lax.associative_scan(comb, (a, b), axis=1)
    return h.astype(u.dtype)


def reference(x, norm_gain, w_in, q_norm_gain, k_norm_gain, cmp_pe, cmp_w1, cmp_b1, cmp_w2, rel_bias,
              conv_w, conv_b, lru_wa, lru_ba, lru_wx, lru_bx, lru_lambda, w_proj_a, w_proj_b, w_out):
    B, S, _ = x.shape
    split_idx = np.cumsum(IN_SIZES)[:-1].tolist()
    for _layer in range(DEPTH):
        h = rms_norm(x, norm_gain)
        proj = h @ w_in
        q, kv_all, g_nsa, br_gate, u_lru, g_lru, merge_g = jnp.split(proj, split_idx, axis=-1)
        kv_all = kv_all.reshape(B, S, 6, N_KV_GROUPS, HEAD_DIM)
        y_a = nsa_mixer(q, kv_all[:, :, 0], kv_all[:, :, 1], kv_all[:, :, 2], kv_all[:, :, 3],
                        kv_all[:, :, 4], kv_all[:, :, 5], br_gate, q_norm_gain, k_norm_gain,
                        cmp_pe, cmp_w1, cmp_b1, cmp_w2, rel_bias)
        y_a = (y_a * jax.nn.silu(g_nsa)) @ w_proj_a
        y_b = rglru_mixer(u_lru, conv_w, conv_b, lru_wa, lru_ba, lru_wx, lru_bx, lru_lambda)
        y_b = (y_b * jax.nn.silu(g_lru)) @ w_proj_b
        gate_a, gate_b = jnp.split(merge_g, N_BRANCHES, axis=-1)
        m = jax.nn.sigmoid(gate_a) * y_a + jax.nn.sigmoid(gate_b) * y_b
        x = x + m @ w_out
    return x
```

```python
import functools
import math

import numpy as np
import jax
import jax.numpy as jnp
from jax import lax
from jax.experimental import pallas as pl
from jax.experimental.pallas import tpu as pltpu

F32 = jnp.float32
BF16 = jnp.bfloat16

D_MODEL = 1024
N_HEADS = 8
N_GROUPS = 2
GROUP_SIZE = N_HEADS // N_GROUPS
HEAD_DIM = 64
NSA_WIDTH = N_HEADS * HEAD_DIM
KV_WIDTH = N_GROUPS * HEAD_DIM
CMP_STRIDE = 16
CMP_BLOCK = 32
CMP_HIDDEN = 256
SEL_BLOCK = 64
SEL_PER_CMP = SEL_BLOCK // CMP_STRIDE
N_SELECT = 16
WINDOW = 512
Q_BLOCK = 128
LRU_WIDTH = 512
LRU_BLOCKS = 8
CONV_WIDTH = 4
LRU_C = 8.0
N_BUCKETS = 32
MAX_DISTANCE = 128
EPS = 1e-6
NEG = -1e30
M_INIT = -5e29
LOG2E = 1.4426950408889634

GQ = GROUP_SIZE * Q_BLOCK
SUPER = 128
KV_CHUNK = 512
SEL_NEAR = 2 * Q_BLOCK
WIN_KEYS = WINDOW + Q_BLOCK
CMP_NEAR = 24
N_PICK = N_SELECT - 3

BR_PAD = 128
COL_SIZES = (NSA_WIDTH, 6 * KV_WIDTH, NSA_WIDTH, BR_PAD, LRU_WIDTH, LRU_WIDTH, 2 * D_MODEL)
COL_OFFS = tuple(int(v) for v in np.cumsum((0,) + COL_SIZES))
D_IN_PAD = COL_OFFS[-1]

VMEM_LIMIT = 56 * 1024 * 1024


def _cparams(n_axes):
    return pltpu.CompilerParams(dimension_semantics=("arbitrary",) * n_axes,
                                vmem_limit_bytes=VMEM_LIMIT)


def _dot(a, b):
    return jnp.dot(a, b, preferred_element_type=F32)


def _inproj_kernel(x_ref, g_ref, w_ref, *out_refs):
    x = x_ref[...]
    ms = jnp.mean(x * x, axis=-1, keepdims=True)
    h = (x * lax.rsqrt(ms + EPS) * g_ref[...]).astype(BF16)
    for ref, a, b in zip(out_refs, COL_OFFS[:-1], COL_OFFS[1:]):
        ref[...] = _dot(h, w_ref[:, a:b])


def _inproj(x2, norm_gain, w_pad, tm=256):
    s = x2.shape[0]
    outs = [jax.ShapeDtypeStruct((s, n), F32) for n in COL_SIZES]
    return pl.pallas_call(
        _inproj_kernel,
        grid=(s // tm,),
        in_specs=[pl.BlockSpec((tm, D_MODEL), lambda i: (i, 0)),
                  pl.BlockSpec((1, D_MODEL), lambda i: (0, 0)),
                  pl.BlockSpec((D_MODEL, D_IN_PAD), lambda i: (0, 0))],
        out_specs=[pl.BlockSpec((tm, n), lambda i: (i, 0)) for n in COL_SIZES],
        out_shape=outs,
        compiler_params=_cparams(1),
        name="inproj",
    )(x2, norm_gain.reshape(1, D_MODEL), w_pad)


PREP_TM = 512


def _group_rms(k, gain_row):
    sq = k * k
    lane = lax.broadcasted_iota(jnp.int32, k.shape, 1)
    lo = lane < HEAD_DIM
    s0 = jnp.sum(jnp.where(lo, sq, 0.0), axis=-1, keepdims=True)
    s1 = jnp.sum(jnp.where(lo, 0.0, sq), axis=-1, keepdims=True)
    inv = jnp.where(lo, lax.rsqrt(s0 / HEAD_DIM + EPS), lax.rsqrt(s1 / HEAD_DIM + EPS))
    return k * inv * gain_row


def _prep_kernel(q_ref, kv_ref, gn_ref, br_ref, qg_ref, kg_ref,
                 qhi_ref, qlo_ref, gnt_ref, brt_ref, kaug_ref, vt512_ref, vt128_ref,
                 kwin_ref, vwt_ref, kcr_ref, vcr_ref):
    i = pl.program_id(0)
    tm = PREP_TM
    qt = q_ref[...].T
    for h in range(N_HEADS):
        blk = qt[h * HEAD_DIM:(h + 1) * HEAD_DIM]
        ms = jnp.mean(blk * blk, axis=0, keepdims=True)
        qn = blk * lax.rsqrt(ms + EPS) * qg_ref[h * HEAD_DIM:(h + 1) * HEAD_DIM, :]
        hi = qn.astype(BF16)
        qhi_ref[h * HEAD_DIM:(h + 1) * HEAD_DIM, :] = hi
        qlo_ref[h * HEAD_DIM:(h + 1) * HEAD_DIM, :] = (qn - hi.astype(F32)).astype(BF16)
    gn = gn_ref[...]
    gnt_ref[...] = (gn * jax.nn.sigmoid(gn)).T
    brt_ref[...] = jax.nn.sigmoid(br_ref[...]).T[:32]

    kv = kv_ref[...]
    piece = lambda j: kv[:, j * KV_WIDTH:(j + 1) * KV_WIDTH]
    lane = lax.broadcasted_iota(jnp.int32, (tm, 128), 1)
    row = lax.broadcasted_iota(jnp.int32, (tm, 128), 0) + i * tm
    lo = lane < HEAD_DIM
    ones_cols = jnp.where((lane == HEAD_DIM) | (lane == HEAD_DIM + 1), 1.0, 0.0)
    onehot = jnp.where(lane == (row // SEL_BLOCK) % SUPER, 1.0, 0.0).astype(BF16)
    kcmp, vcmp = piece(0), piece(1)
    kslc = _group_rms(piece(2), kg_ref[1:2, :])
    kwin = _group_rms(piece(4), kg_ref[2:3, :])
    vslt = piece(3).T
    vwit = piece(5).T
    row_t = lax.broadcasted_iota(jnp.int32, (HEAD_DIM, tm), 0)
    ones_rows = jnp.where(row_t == 0, 1.0, 0.0)
    for g in range(N_GROUPS):
        sh = lambda a: a if g == 0 else pltpu.roll(a, HEAD_DIM, 1)
        kaug_ref[g, :, 0:128] = onehot
        kaug_ref[g, :, 128:256] = jnp.where(lo, sh(kslc), ones_cols).astype(BF16)
        kwin_ref[g] = jnp.where(lo, sh(kwin), 0.0).astype(BF16)
        kcr_ref[g] = sh(kcmp)[:, 0:HEAD_DIM]
        vcr_ref[g] = sh(vcmp)[:, 0:HEAD_DIM]
        vs = jnp.concatenate([vslt[g * HEAD_DIM:(g + 1) * HEAD_DIM], ones_rows], axis=0).astype(BF16)
        vw = jnp.concatenate([vwit[g * HEAD_DIM:(g + 1) * HEAD_DIM], ones_rows], axis=0).astype(BF16)
        for j in range(tm // KV_CHUNK):
            vt512_ref[g, j] = vs[:, j * KV_CHUNK:(j + 1) * KV_CHUNK]
        for j in range(tm // 128):
            vt128_ref[g, j] = vs[:, j * 128:(j + 1) * 128]
            vwt_ref[g, j] = vw[:, j * 128:(j + 1) * 128]


def _prep(q, kv, gn, br, qgain_col, kgain_rows):
    s = q.shape[0]
    tm = PREP_TM
    g = N_GROUPS
    outs = [
        jax.ShapeDtypeStruct((NSA_WIDTH, s), BF16),
        jax.ShapeDtypeStruct((NSA_WIDTH, s), BF16),
        jax.ShapeDtypeStruct((NSA_WIDTH, s), F32),
        jax.ShapeDtypeStruct((32, s), F32),
        jax.ShapeDtypeStruct((g, s, 256), BF16),
        jax.ShapeDtypeStruct((g, s // KV_CHUNK, 128, KV_CHUNK), BF16),
        jax.ShapeDtypeStruct((g, s // 128, 128, 128), BF16),
        jax.ShapeDtypeStruct((g, s, 128), BF16),
        jax.ShapeDtypeStruct((g, s // 128, 128, 128), BF16),
        jax.ShapeDtypeStruct((g, s, HEAD_DIM), F32),
        jax.ShapeDtypeStruct((g, s, HEAD_DIM), F32),
    ]
    out_specs = [
        pl.BlockSpec((NSA_WIDTH, tm), lambda i: (0, i)),
        pl.BlockSpec((NSA_WIDTH, tm), lambda i: (0, i)),
        pl.BlockSpec((NSA_WIDTH, tm), lambda i: (0, i)),
        pl.BlockSpec((32, tm), lambda i: (0, i)),
        pl.BlockSpec((g, tm, 256), lambda i: (0, i, 0)),
        pl.BlockSpec((g, tm // KV_CHUNK, 128, KV_CHUNK), lambda i: (0, i, 0, 0)),
        pl.BlockSpec((g, tm // 128, 128, 128), lambda i: (0, i, 0, 0)),
        pl.BlockSpec((g, tm, 128), lambda i: (0, i, 0)),
        pl.BlockSpec((g, tm // 128, 128, 128), lambda i: (0, i, 0, 0)),
        pl.BlockSpec((g, tm, HEAD_DIM), lambda i: (0, i, 0)),
        pl.BlockSpec((g, tm, HEAD_DIM), lambda i: (0, i, 0)),
    ]
    return pl.pallas_call(
        _prep_kernel,
        grid=(s // tm,),
        in_specs=[pl.BlockSpec((tm, NSA_WIDTH), lambda i: (i, 0)),
                  pl.BlockSpec((tm, 6 * KV_WIDTH), lambda i: (i, 0)),
                  pl.BlockSpec((tm, NSA_WIDTH), lambda i: (i, 0)),
                  pl.BlockSpec((tm, BR_PAD), lambda i: (i, 0)),
                  pl.BlockSpec((NSA_WIDTH, 1), lambda i: (0, 0)),
                  pl.BlockSpec((3, 128), lambda i: (0, 0))],
        out_specs=out_specs,
        out_shape=outs,
        compiler_params=_cparams(1),
        name="prep",
    )(q, kv, gn, br, qgain_col, kgain_rows)


def _compress_kernel(xk_ref, xv_ref, w1_ref, b1_ref, w2_ref, pe_ref, kg_ref, kc_ref, vct_ref):
    ncp = xk_ref.shape[0]
    half = CMP_STRIDE * HEAD_DIM
    lane = lax.broadcasted_iota(jnp.int32, (ncp, 128), 1)
    lo = lane < HEAD_DIM

    def phi(x_ref, j):
        x = x_ref[...].astype(BF16)
        w1 = w1_ref[j].astype(BF16)
        a = _dot(x, w1[:half])
        b = _dot(x, w1[half:])
        b_next = pltpu.roll(b, ncp - 1, 0)
        pe8 = jnp.broadcast_to(pe_ref[j], (8, 2 * half)).astype(BF16)
        pe_term = _dot(pe8, w1)[0:1]
        hid = a + b_next + pe_term + b1_ref[j]
        act = (hid * jax.nn.sigmoid(hid)).astype(BF16)
        return _dot(act, w2_ref[j].astype(BF16))

    kc = phi(xk_ref, 0)
    ms = jnp.sum(kc * kc, axis=-1, keepdims=True) / HEAD_DIM
    kc = kc * lax.rsqrt(ms + EPS) * kg_ref[...]
    hi = kc.astype(BF16).astype(F32)
    lo_part = (kc - hi).astype(BF16).astype(F32)
    ones_cols = jnp.where((lane == HEAD_DIM) | (lane == HEAD_DIM + 1), 1.0, 0.0)
    kc_ref[:, 0:128] = (hi + pltpu.roll(lo_part, HEAD_DIM, 1)).astype(BF16)
    kc_ref[:, 128:256] = jnp.where(lo, hi, ones_cols).astype(BF16)
    vct_ref[...] = phi(xv_ref, 1).T.astype(BF16)


def _compress(xk, xv, w1, b1, w2p, pe_flat, kgain_row):
    g, ncp, _ = xk.shape
    return pl.pallas_call(
        _compress_kernel,
        grid=(g,),
        in_specs=[pl.BlockSpec((None, ncp, CMP_STRIDE * HEAD_DIM), lambda i: (i, 0, 0)),
                  pl.BlockSpec((None, ncp, CMP_STRIDE * HEAD_DIM), lambda i: (i, 0, 0)),
                  pl.BlockSpec(w1.shape, lambda i: (0, 0, 0)),
                  pl.BlockSpec(b1.shape, lambda i: (0, 0, 0)),
                  pl.BlockSpec(w2p.shape, lambda i: (0, 0, 0)),
                  pl.BlockSpec(pe_flat.shape, lambda i: (0, 0, 0)),
                  pl.BlockSpec((1, 128), lambda i: (0, 0))],
        out_specs=[pl.BlockSpec((None, ncp, 256), lambda i: (i, 0, 0)),
                   pl.BlockSpec((None, 128, ncp), lambda i: (i, 0, 0))],
        out_shape=[jax.ShapeDtypeStruct((g, ncp, 256), BF16),
                   jax.ShapeDtypeStruct((g, 128, ncp), BF16)],
        compiler_params=_cparams(1),
        name="compress",
    )(xk, xv, w1, b1, w2p, pe_flat, kgain_row)


def _heads_to_lanes(q):
    return jnp.concatenate([q[r * HEAD_DIM:(r + 1) * HEAD_DIM] for r in range(GROUP_SIZE)], axis=1)


def _store_heads(out_ref, o):
    for r in range(GROUP_SIZE):
        out_ref[r * HEAD_DIM:(r + 1) * HEAD_DIM, :] = o[0:HEAD_DIM, r * Q_BLOCK:(r + 1) * Q_BLOCK]


def _bias_tile(w_ref, off, n):
    return jnp.concatenate([w_ref[r, pl.ds(off, n), :] for r in range(GROUP_SIZE)], axis=1)


def _cmp_kernel(qhi_ref, qlo_ref, kc_ref, vct_ref, wc_ref, mt_ref, ext_ref, ocmp_ref, neg_ref, s_ref):
    qb = pl.program_id(1)
    ncp = kc_ref.shape[0]
    nsbp = mt_ref.shape[0]
    qhi = qhi_ref[...]
    qlo = qlo_ref[...]
    cols = []
    for r in range(GROUP_SIZE):
        a = qhi[r * HEAD_DIM:(r + 1) * HEAD_DIM]
        b = qlo[r * HEAD_DIM:(r + 1) * HEAD_DIM]
        cols.append(jnp.concatenate([a, a, b], axis=0))
    qcat = jnp.concatenate([jnp.concatenate(cols, axis=1), ext_ref[...]], axis=0)
    s_ref[...] = _dot(kc_ref[...], qcat)
    ws = pl.multiple_of(jnp.maximum(8 * qb - 16, 0), 8)
    off = pl.multiple_of(jnp.maximum(16 - 8 * qb, 0), 8)
    s_ref[pl.ds(ws, CMP_NEAR), :] += _bias_tile(wc_ref, off, CMP_NEAR)
    rows = lax.broadcasted_iota(jnp.int32, (ncp, GQ), 0)
    s = jnp.where(rows < ws + CMP_NEAR, s_ref[...], NEG)
    m = jnp.maximum(jnp.max(s, axis=0, keepdims=True), M_INIT)
    p = jnp.exp2(s - m)
    l = jnp.sum(p, axis=0, keepdims=True)
    pn = p * jnp.where(l > 0.0, 1.0 / l, 0.0)
    _store_heads(ocmp_ref, _dot(vct_ref[...], pn.astype(BF16)))
    imp = pn[:, 0:Q_BLOCK]
    for r in range(1, GROUP_SIZE):
        imp = imp + pn[:, r * Q_BLOCK:(r + 1) * Q_BLOCK]
    i0 = imp.astype(BF16)
    r1 = imp - i0.astype(F32)
    i1 = r1.astype(BF16)
    i2 = (r1 - i1.astype(F32)).astype(BF16)
    mt = mt_ref[...]
    impb = _dot(mt, i0) + _dot(mt, i1) + _dot(mt, i2)
    blk = lax.broadcasted_iota(jnp.int32, (nsbp, Q_BLOCK), 0)
    lane = lax.broadcasted_iota(jnp.int32, (nsbp, Q_BLOCK), 1)
    cur = 2 * qb + jnp.where(lane >= SEL_BLOCK, 1, 0)
    forced = jnp.where(blk == 0, 1.0, jnp.where(blk == cur, 1.0, jnp.where(blk == cur - 1, 1.0, 0.0)))
    cand = jnp.where(blk < cur - 1, jnp.where(blk > 0, impb, -1.0), -1.0)
    sel = jnp.where(blk <= cur, forced, 0.0)

    def pick_one(_, carry):
        sc, sl = carry
        mx = jnp.max(sc, axis=0, keepdims=True)
        hit_blk = jnp.where(sc == mx, jnp.where(mx >= 0.0, blk, nsbp), nsbp)
        first = jnp.min(hit_blk, axis=0, keepdims=True)
        pick = blk == first
        return jnp.where(pick, -1.0, sc), jnp.where(pick, 1.0, sl)

    _, sel = lax.fori_loop(0, N_PICK, pick_one, (cand, sel))
    neg_ref[...] = jnp.where(sel > 0.5, 0.0, NEG).astype(BF16)


def _cmp_attention(qhi_t, qlo_t, kc_cat, vct, wc_t, mt, ext_q):
    s = qhi_t.shape[1]
    g = N_GROUPS
    nqb = s // Q_BLOCK
    ncp = kc_cat.shape[1]
    nsbp = mt.shape[0]
    gr = GROUP_SIZE * HEAD_DIM
    return pl.pallas_call(
        _cmp_kernel,
        grid=(g, nqb),
        in_specs=[pl.BlockSpec((gr, Q_BLOCK), lambda gi, qb: (gi, qb)),
                  pl.BlockSpec((gr, Q_BLOCK), lambda gi, qb: (gi, qb)),
                  pl.BlockSpec((None, ncp, 256), lambda gi, qb: (gi, 0, 0)),
                  pl.BlockSpec((None, 128, ncp), lambda gi, qb: (gi, 0, 0)),
                  pl.BlockSpec((GROUP_SIZE,) + wc_t.shape[1:], lambda gi, qb: (gi, 0, 0)),
                  pl.BlockSpec(mt.shape, lambda gi, qb: (0, 0)),
                  pl.BlockSpec((None, HEAD_DIM, GQ), lambda gi, qb: (gi, 0, 0))],
        out_specs=[pl.BlockSpec((gr, Q_BLOCK), lambda gi, qb: (gi, qb)),
                   pl.BlockSpec((None, nsbp, Q_BLOCK), lambda gi, qb: (gi, 0, qb))],
        out_shape=[jax.ShapeDtypeStruct((NSA_WIDTH, s), F32),
                   jax.ShapeDtypeStruct((g, nsbp, s), BF16)],
        scratch_shapes=[pltpu.VMEM((ncp, GQ), F32)],
        compiler_params=_cparams(2),
        name="cmp_select",
    )(qhi_t, qlo_t, kc_cat, vct, wc_t, mt, ext_q)


def _sel_kernel(qhi_ref, neg_ref, kaug_ref, vt512_ref, vt128_ref, ws_ref, ext_ref, out_ref,
                qaug_ref, m_ref, acc_ref):
    qb = pl.program_id(1)
    nsbp = neg_ref.shape[0]
    nsc = nsbp // SUPER
    qx = jnp.concatenate([_heads_to_lanes(qhi_ref[...]), ext_ref[...]], axis=0)
    blk = lax.broadcasted_iota(jnp.int32, (SUPER, Q_BLOCK), 0)

    def aug(neg_rows):
        tiled = jnp.concatenate([neg_rows] * GROUP_SIZE, axis=1).astype(BF16)
        return jnp.concatenate([tiled, qx], axis=0)

    for sc in range(nsc):
        neg_rows = neg_ref[sc * SUPER:(sc + 1) * SUPER, :].astype(F32)
        qaug_ref[sc] = aug(jnp.where(blk + sc * SUPER >= 2 * qb - 2, NEG, neg_rows))

    m_ref[...] = jnp.full(m_ref.shape, M_INIT, F32)
    acc_ref[...] = jnp.zeros(acc_ref.shape, F32)

    def update(s, vt):
        m_old = m_ref[...]
        m_new = jnp.maximum(m_old, jnp.max(s, axis=0, keepdims=True))
        p = jnp.exp2(s - m_new).astype(BF16)
        acc_ref[...] = acc_ref[...] * jnp.exp2(m_old - m_new) + _dot(vt, p)
        m_ref[...] = m_new

    def far_chunk(c, carry):
        k = kaug_ref[pl.ds(pl.multiple_of(c * KV_CHUNK, KV_CHUNK), KV_CHUNK), :]
        sc = (c * (KV_CHUNK // SEL_BLOCK)) // SUPER
        update(_dot(k, qaug_ref[sc]), vt512_ref[c])
        return carry

    n_far = (jnp.maximum(qb - 1, 0) * Q_BLOCK + KV_CHUNK - 1) // KV_CHUNK
    lax.fori_loop(0, n_far, far_chunk, 0)

    kb0 = jnp.maximum(qb - 1, 0)
    ws = pl.multiple_of(kb0 * Q_BLOCK, Q_BLOCK)
    off = pl.multiple_of(jnp.where(qb == 0, Q_BLOCK, 0), Q_BLOCK)
    b_lo = 2 * kb0
    sc_lo = pl.multiple_of((b_lo // SUPER) * SUPER, SUPER)
    sc_hi = pl.multiple_of(((b_lo + 3) // SUPER) * SUPER, SUPER)
    neg_lo = neg_ref[pl.ds(sc_lo, SUPER), :].astype(F32)
    neg_hi = neg_ref[pl.ds(sc_hi, SUPER), :].astype(F32)
    near_neg = jnp.where(blk >= SUPER // 2, neg_lo, neg_hi)
    k = kaug_ref[pl.ds(ws, SEL_NEAR), :]
    s = _dot(k, aug(near_neg)) + _bias_tile(ws_ref, off, SEL_NEAR)
    vt = jnp.concatenate([vt128_ref[kb0], vt128_ref[kb0 + 1]], axis=1)
    update(s, vt)

    acc = acc_ref[...]
    _store_heads(out_ref, acc[0:HEAD_DIM] * (1.0 / acc[HEAD_DIM:HEAD_DIM + 1]))


def _sel_attention(qhi_t, neg_t, kaug, vt512, vt128, ws_t, ext_q):
    s = qhi_t.shape[1]
    g = N_GROUPS
    nqb = s // Q_BLOCK
    nsbp = neg_t.shape[1]
    gr = GROUP_SIZE * HEAD_DIM
    once = pl.Buffered(1)
    return pl.pallas_call(
        _sel_kernel,
        grid=(g, nqb),
        in_specs=[pl.BlockSpec((gr, Q_BLOCK), lambda gi, qb: (gi, qb)),
                  pl.BlockSpec((None, nsbp, Q_BLOCK), lambda gi, qb: (gi, 0, qb)),
                  pl.BlockSpec((None, s, 256), lambda gi, qb: (gi, 0, 0), pipeline_mode=once),
                  pl.BlockSpec((None, s // KV_CHUNK, 128, KV_CHUNK), lambda gi, qb: (gi, 0, 0, 0),
                               pipeline_mode=once),
                  pl.BlockSpec((None, s // 128, 128, 128), lambda gi, qb: (gi, 0, 0, 0),
                               pipeline_mode=once),
                  pl.BlockSpec((GROUP_SIZE,) + ws_t.shape[1:], lambda gi, qb: (gi, 0, 0)),
                  pl.BlockSpec((None, HEAD_DIM, GQ), lambda gi, qb: (gi, 0, 0))],
        out_specs=pl.BlockSpec((gr, Q_BLOCK), lambda gi, qb: (gi, qb)),
        out_shape=jax.ShapeDtypeStruct((NSA_WIDTH, s), F32),
        scratch_shapes=[pltpu.VMEM((nsbp // SUPER, 256, GQ), BF16),
                        pltpu.VMEM((1, GQ), F32),
                        pltpu.VMEM((128, GQ), F32)],
        compiler_params=_cparams(2),
        name="sel_attention",
    )(qhi_t, neg_t, kaug, vt512, vt128, ws_t, ext_q)


def _win_kernel(qhi_ref, kwin_ref, vwt_ref, ww_ref, out_ref):
    qb = pl.program_id(1)
    kb0 = jnp.maximum(qb - WINDOW // Q_BLOCK, 0)
    ws = pl.multiple_of(kb0 * Q_BLOCK, Q_BLOCK)
    off = pl.multiple_of(jnp.maximum(WINDOW - qb * Q_BLOCK, 0), Q_BLOCK)
    qg = _heads_to_lanes(qhi_ref[...])
    qx = jnp.concatenate([qg, jnp.zeros_like(qg)], axis=0)
    s = _dot(kwin_ref[pl.ds(ws, WIN_KEYS), :], qx) + _bias_tile(ww_ref, off, WIN_KEYS)
    m = jnp.max(s, axis=0, keepdims=True)
    p = jnp.exp2(s - m).astype(BF16)
    vt = jnp.concatenate([vwt_ref[kb0 + j] for j in range(WIN_KEYS // Q_BLOCK)], axis=1)
    acc = _dot(vt, p)
    _store_heads(out_ref, acc[0:HEAD_DIM] * (1.0 / acc[HEAD_DIM:HEAD_DIM + 1]))


def _win_attention(qhi_t, kwin, vwt, ww_t):
    s = qhi_t.shape[1]
    g = N_GROUPS
    gr = GROUP_SIZE * HEAD_DIM
    once = pl.Buffered(1)
    return pl.pallas_call(
        _win_kernel,
        grid=(g, s // Q_BLOCK),
        in_specs=[pl.BlockSpec((gr, Q_BLOCK), lambda gi, qb: (gi, qb)),
                  pl.BlockSpec((None, s, 128), lambda gi, qb: (gi, 0, 0), pipeline_mode=once),
                  pl.BlockSpec((None, s // 128, 128, 128), lambda gi, qb: (gi, 0, 0, 0),
                               pipeline_mode=once),
                  pl.BlockSpec((GROUP_SIZE,) + ww_t.shape[1:], lambda gi, qb: (gi, 0, 0))],
        out_specs=pl.BlockSpec((gr, Q_BLOCK), lambda gi, qb: (gi, qb)),
        out_shape=jax.ShapeDtypeStruct((NSA_WIDTH, s), F32),
        compiler_params=_cparams(2),
        name="win_attention",
    )(qhi_t, kwin, vwt, ww_t)


LRU_TM = 256


def _lru_kernel(u_ref, cw_ref, cb_ref, wa_ref, ba_ref, wx_ref, bx_ref, lam_ref, h_ref,
                tail_ref, hprev_ref):
    tm = LRU_TM

    @pl.when(pl.program_id(0) == 0)
    def _():
        tail_ref[...] = jnp.zeros(tail_ref.shape, F32)
        hprev_ref[...] = jnp.zeros(hprev_ref.shape, F32)

    u = u_ref[...]
    tail = tail_ref[...]
    row8 = lax.broadcasted_iota(jnp.int32, (8, LRU_WIDTH), 0)
    uc = cb_ref[...] + u * cw_ref[CONV_WIDTH - 1:CONV_WIDTH, :]
    for j in range(1, CONV_WIDTH):
        sh = pltpu.roll(u, j, 0)
        head = jnp.where(row8 < j, pltpu.roll(tail, j, 0), sh[0:8])
        sh = jnp.concatenate([head, sh[8:]], axis=0)
        uc = uc + sh * cw_ref[CONV_WIDTH - 1 - j:CONV_WIDTH - j, :]
    tail_ref[...] = u[tm - 8:tm]

    ucb = uc.astype(BF16)
    r = jax.nn.sigmoid(_dot(ucb, wa_ref[...]) + ba_ref[...])
    ig = jax.nn.sigmoid(_dot(ucb, wx_ref[...]) + bx_ref[...])
    nl = -lam_ref[...]
    softplus = jnp.maximum(nl, 0.0) + jnp.log(1.0 + jnp.exp(-jnp.abs(nl)))
    log_a = (-LRU_C * r) * softplus
    a = jnp.exp(log_a)
    b = jnp.sqrt(1.0 - jnp.exp(2.0 * log_a)) * (ig * uc)

    row = lax.broadcasted_iota(jnp.int32, (tm, LRU_WIDTH), 0)
    step = 1
    while step < tm:
        a_s = pltpu.roll(a, step, 0)
        b_s = pltpu.roll(b, step, 0)
        ok = row >= step
        b = jnp.where(ok, a * b_s + b, b)
        a = jnp.where(ok, a * a_s, a)
        step *= 2
    h = a * hprev_ref[...] + b
    h_ref[...] = h
    hprev_ref[...] = h[tm - 1:tm]


def _rglru(u, conv_w, conv_b, wa_bd, ba, wx_bd, bx, lam):
    s = u.shape[0]
    tm = LRU_TM
    row = lambda v: v.reshape(1, LRU_WIDTH)
    full = lambda shape: pl.BlockSpec(shape, lambda i: (0, 0))
    return pl.pallas_call(
        _lru_kernel,
        grid=(s // tm,),
        in_specs=[pl.BlockSpec((tm, LRU_WIDTH), lambda i: (i, 0)),
                  full((CONV_WIDTH, LRU_WIDTH)), full((1, LRU_WIDTH)),
                  full((LRU_WIDTH, LRU_WIDTH)), full((1, LRU_WIDTH)),
                  full((LRU_WIDTH, LRU_WIDTH)), full((1, LRU_WIDTH)),
                  full((1, LRU_WIDTH))],
        out_specs=pl.BlockSpec((tm, LRU_WIDTH), lambda i: (i, 0)),
        out_shape=jax.ShapeDtypeStruct((s, LRU_WIDTH), F32),
        scratch_shapes=[pltpu.VMEM((8, LRU_WIDTH), F32), pltpu.VMEM((1, LRU_WIDTH), F32)],
        compiler_params=_cparams(1),
        name="rglru",
    )(u, conv_w, row(conv_b), wa_bd, row(ba), wx_bd, row(bx), row(lam))


OUT_TM = 256


def _out_kernel(x_ref, oc_ref, os_ref, ow_ref, brt_ref, gnt_ref, h_ref, gl_ref, mg_ref,
                wpa_ref, wpb_ref, wo_ref, y_ref):
    tm = OUT_TM
    brt = brt_ref[...]
    parts = []
    for h in range(N_HEADS):
        rows = slice(h * HEAD_DIM, (h + 1) * HEAD_DIM)
        acc = None
        for b, o_ref in enumerate((oc_ref, os_ref, ow_ref)):
            term = brt[b * N_HEADS + h:b * N_HEADS + h + 1, :] * o_ref[rows, :]
            acc = term if acc is None else acc + term
        parts.append(acc)
    ya_t = jnp.concatenate(parts, axis=0) * gnt_ref[...]
    y_a = _dot(ya_t.T.astype(BF16), wpa_ref[...])
    gl = gl_ref[...]
    y_b = _dot((h_ref[...] * (gl * jax.nn.sigmoid(gl))).astype(BF16), wpb_ref[...])
    mg = mg_ref[...]
    m = jax.nn.sigmoid(mg[:, :D_MODEL]) * y_a + jax.nn.sigmoid(mg[:, D_MODEL:]) * y_b
    y_ref[...] = x_ref[...] + _dot(m.astype(BF16), wo_ref[...])


def _output(x2, oc_t, os_t, ow_t, br_t, gn_t, h_lru, g_lru, merge_g, wpa, wpb, wo):
    s = x2.shape[0]
    tm = OUT_TM
    rows = lambda n: pl.BlockSpec((tm, n), lambda i: (i, 0))
    cols = lambda n: pl.BlockSpec((n, tm), lambda i: (0, i))
    full = lambda a: pl.BlockSpec(a.shape, lambda i: (0, 0))
    return pl.pallas_call(
        _out_kernel,
        grid=(s // tm,),
        in_specs=[rows(D_MODEL), cols(NSA_WIDTH), cols(NSA_WIDTH), cols(NSA_WIDTH), cols(32),
                  cols(NSA_WIDTH), rows(LRU_WIDTH), rows(LRU_WIDTH), rows(2 * D_MODEL),
                  full(wpa), full(wpb), full(wo)],
        out_specs=rows(D_MODEL),
        out_shape=jax.ShapeDtypeStruct((s, D_MODEL), F32),
        compiler_params=_cparams(1),
        name="output",
    )(x2, oc_t, os_t, ow_t, br_t, gn_t, h_lru, g_lru, merge_g, wpa, wpb, wo)


def _t5_bucket_table(n_dist):
    n = np.arange(n_dist)
    max_exact = N_BUCKETS // 2
    nf = np.maximum(n, 1).astype(np.float32)
    large = max_exact + (np.log(nf / np.float32(max_exact)) / np.float32(math.log(MAX_DISTANCE / max_exact))
                         * np.float32(N_BUCKETS - max_exact)).astype(np.int32)
    return np.where(n < max_exact, n, np.minimum(large, N_BUCKETS - 1))


def _bias_tables(rel_bias):
    n_dist = 2 * WINDOW
    bias_d = rel_bias[_t5_bucket_table(n_dist), :].T * LOG2E
    b_far = rel_bias[N_BUCKETS - 1] * LOG2E
    i = np.arange(Q_BLOCK)[None, :]

    def tile(dist, valid, rel):
        vals = bias_d[:, np.clip(dist, 0, n_dist - 1)]
        if rel:
            vals = vals - b_far[:, None, None]
        return jnp.where(valid[None], vals, NEG).astype(F32)

    u = np.arange(WINDOW + WIN_KEYS)[:, None]
    d_win = i - u + WINDOW
    ww = tile(d_win, (d_win >= 0) & (d_win < WINDOW), False)
    u = np.arange(SEL_NEAR + Q_BLOCK)[:, None]
    d_sel = i - u + Q_BLOCK
    ws = tile(d_sel, d_sel >= 0, True)
    u = np.arange(CMP_NEAR + 16)[:, None]
    d_cmp = i - CMP_STRIDE * (u - 16) - (CMP_BLOCK - 1)
    wc = tile(d_cmp, d_cmp >= 0, True)
    hi = b_far.astype(BF16)
    lo = (b_far - hi.astype(F32)).astype(BF16)
    ext = jnp.zeros((N_HEADS, HEAD_DIM, Q_BLOCK), BF16)
    ext = ext.at[:, 0, :].set(hi[:, None]).at[:, 1, :].set(lo[:, None])
    ext = ext.reshape(N_GROUPS, GROUP_SIZE, HEAD_DIM, Q_BLOCK).transpose(0, 2, 1, 3)
    return ww, ws, wc, ext.reshape(N_GROUPS, HEAD_DIM, GQ)


def _pool_matrix(nsbp, ncp, nc):
    j = np.arange(nsbp)[:, None]
    c = np.arange(ncp)[None, :]
    hit = (c >= SEL_PER_CMP * j - 1) & (c <= SEL_PER_CMP * j + SEL_PER_CMP - 1) & (c < nc)
    return jnp.asarray(hit.astype(np.float32), dtype=BF16)


def _block_diag(w):
    n, d, e = w.shape
    eye = jnp.eye(n, dtype=w.dtype)
    return (eye[:, None, :, None] * w[:, :, None, :]).reshape(n * d, n * e)


def kernel(x, norm_gain, w_in, q_norm_gain, k_norm_gain, cmp_pe, cmp_w1, cmp_b1, cmp_w2, rel_bias,
           conv_w, conv_b, lru_wa, lru_ba, lru_wx, lru_bx, lru_lambda, w_proj_a, w_proj_b, w_out):
    bsz, s, _ = x.shape
    assert bsz == 1 and s % 1024 == 0 and s >= 1024
    x2 = x.reshape(s, D_MODEL)
    nsb = s // SEL_BLOCK
    nsbp = -(-nsb // SUPER) * SUPER
    ncp = s // CMP_STRIDE

    o = np.cumsum((0, NSA_WIDTH, 6 * KV_WIDTH, NSA_WIDTH, 3 * N_HEADS, LRU_WIDTH, LRU_WIDTH, 2 * D_MODEL))
    pad = jnp.zeros((D_MODEL, BR_PAD - 3 * N_HEADS), w_in.dtype)
    w_pad = jnp.concatenate([w_in[:, :o[4]], pad, w_in[:, o[4]:]], axis=1).astype(BF16)

    q, kv, gn, br, u_lru, g_lru, merge_g = _inproj(x2, norm_gain, w_pad)

    scale = HEAD_DIM ** -0.5 * LOG2E
    qgain_col = jnp.tile(q_norm_gain * scale, N_HEADS).reshape(NSA_WIDTH, 1)
    kgain_rows = jnp.tile(k_norm_gain, (1, N_GROUPS))
    (qhi_t, qlo_t, gn_t, br_t, kaug, vt512, vt128, kwin, vwt, kcr, vcr) = _prep(
        q, kv, gn, br, qgain_col, kgain_rows)

    w2p = jnp.pad(cmp_w2, ((0, 0), (0, 0), (0, 128 - HEAD_DIM)))
    kc_gain = jnp.pad(k_norm_gain[0], (0, 128 - HEAD_DIM)).reshape(1, 128)
    kc_cat, vct = _compress(kcr.reshape(N_GROUPS, ncp, CMP_STRIDE * HEAD_DIM),
                            vcr.reshape(N_GROUPS, ncp, CMP_STRIDE * HEAD_DIM),
                            cmp_w1, cmp_b1.reshape(2, 1, CMP_HIDDEN), w2p,
                            cmp_pe.reshape(2, 1, CMP_BLOCK * HEAD_DIM), kc_gain)

    ww_t, ws_t, wc_t, ext_q = _bias_tables(rel_bias)
    mt = _pool_matrix(nsbp, ncp, ncp - 1)
    oc_t, neg_t = _cmp_attention(qhi_t, qlo_t, kc_cat, vct, wc_t, mt, ext_q)
    os_t = _sel_attention(qhi_t, neg_t, kaug, vt512, vt128, ws_t, ext_q)
    ow_t = _win_attention(qhi_t, kwin, vwt, ww_t)

    h_lru = _rglru(u_lru, conv_w, conv_b, _block_diag(lru_wa).astype(BF16), lru_ba,
                   _block_diag(lru_wx).astype(BF16), lru_bx, lru_lambda)

    y = _output(x2, oc_t, os_t, ow_t, br_t, gn_t, h_lru, g_lru, merge_g,
                w_proj_a.astype(BF16), w_proj_b.astype(BF16), w_out.astype(BF16))
    return y.reshape(bsz, s, D_MODEL)
```

```python
import functools
import math

import numpy as np
import jax
import jax.numpy as jnp
from jax import lax
from jax.experimental import pallas as pl
from jax.experimental.pallas import tpu as pltpu

F32 = jnp.float32
BF16 = jnp.bfloat16

D_MODEL = 1024
N_HEADS = 8
N_GROUPS = 2
GROUP_SIZE = N_HEADS // N_GROUPS
HEAD_DIM = 64
NSA_WIDTH = N_HEADS * HEAD_DIM
KV_WIDTH = N_GROUPS * HEAD_DIM
CMP_STRIDE = 16
CMP_BLOCK = 32
CMP_HIDDEN = 256
SEL_BLOCK = 64
SEL_PER_CMP = SEL_BLOCK // CMP_STRIDE
N_SELECT = 16
WINDOW = 512
Q_BLOCK = 128
LRU_WIDTH = 512
LRU_BLOCKS = 8
CONV_WIDTH = 4
LRU_C = 8.0
N_BUCKETS = 32
MAX_DISTANCE = 128
EPS = 1e-6
NEG = -1e30
M_INIT = -5e29
LOG2E = 1.4426950408889634

GQ = GROUP_SIZE * Q_BLOCK
SUPER = 128
KV_CHUNK = 512
SEL_NEAR = 2 * Q_BLOCK
WIN_KEYS = WINDOW + Q_BLOCK
CMP_NEAR = 24
N_PICK = N_SELECT - 3

BR_PAD = 128
COL_SIZES = (NSA_WIDTH, 6 * KV_WIDTH, NSA_WIDTH, BR_PAD, LRU_WIDTH, LRU_WIDTH, 2 * D_MODEL)
COL_OFFS = tuple(int(v) for v in np.cumsum((0,) + COL_SIZES))
D_IN_PAD = COL_OFFS[-1]

VMEM_LIMIT = 56 * 1024 * 1024


def _cparams(n_axes):
    return pltpu.CompilerParams(dimension_semantics=("arbitrary",) * n_axes,
                                vmem_limit_bytes=VMEM_LIMIT)


def _dot(a, b):
    return jnp.dot(a, b, preferred_element_type=F32)


def _inproj_kernel(x_ref, g_ref, w_ref, *out_refs):
    x = x_ref[...]
    ms = jnp.mean(x * x, axis=-1, keepdims=True)
    h = (x * lax.rsqrt(ms + EPS) * g_ref[...]).astype(BF16)
    for ref, a, b in zip(out_refs, COL_OFFS[:-1], COL_OFFS[1:]):
        ref[...] = _dot(h, w_ref[:, a:b])


def _inproj(x2, norm_gain, w_pad, tm=256):
    s = x2.shape[0]
    outs = [jax.ShapeDtypeStruct((s, n), F32) for n in COL_SIZES]
    return pl.pallas_call(
        _inproj_kernel,
        grid=(s // tm,),
        in_specs=[pl.BlockSpec((tm, D_MODEL), lambda i: (i, 0)),
                  pl.BlockSpec((1, D_MODEL), lambda i: (0, 0)),
                  pl.BlockSpec((D_MODEL, D_IN_PAD), lambda i: (0, 0))],
        out_specs=[pl.BlockSpec((tm, n), lambda i: (i, 0)) for n in COL_SIZES],
        out_shape=outs,
        compiler_params=_cparams(1),
        name="inproj",
    )(x2, norm_gain.reshape(1, D_MODEL), w_pad)


PREP_TM = 512


def _group_rms(k, gain_row):
    sq = k * k
    lane = lax.broadcasted_iota(jnp.int32, k.shape, 1)
    lo = lane < HEAD_DIM
    s0 = jnp.sum(jnp.where(lo, sq, 0.0), axis=-1, keepdims=True)
    s1 = jnp.sum(jnp.where(lo, 0.0, sq), axis=-1, keepdims=True)
    inv = jnp.where(lo, lax.rsqrt(s0 / HEAD_DIM + EPS), lax.rsqrt(s1 / HEAD_DIM + EPS))
    return k * inv * gain_row


def _prep_kernel(q_ref, kv_ref, gn_ref, br_ref, qg_ref, kg_ref,
                 qhi_ref, qlo_ref, gnt_ref, brt_ref, kaug_ref, vt512_ref, vt128_ref,
                 kwin_ref, vwt_ref, kcr_ref, vcr_ref):
    i = pl.program_id(0)
    tm = PREP_TM
    qt = q_ref[...].T
    for h in range(N_HEADS):
        blk = qt[h * HEAD_DIM:(h + 1) * HEAD_DIM]
        ms = jnp.mean(blk * blk, axis=0, keepdims=True)
        qn = blk * lax.rsqrt(ms + EPS) * qg_ref[h * HEAD_DIM:(h + 1) * HEAD_DIM, :]
        hi = qn.astype(BF16)
        qhi_ref[h * HEAD_DIM:(h + 1) * HEAD_DIM, :] = hi
        qlo_ref[h * HEAD_DIM:(h + 1) * HEAD_DIM, :] = (qn - hi.astype(F32)).astype(BF16)
    gn = gn_ref[...]
    gnt_ref[...] = (gn * jax.nn.sigmoid(gn)).T
    brt_ref[...] = jax.nn.sigmoid(br_ref[...]).T[:32]

    kv = kv_ref[...]
    piece = lambda j: kv[:, j * KV_WIDTH:(j + 1) * KV_WIDTH]
    lane = lax.broadcasted_iota(jnp.int32, (tm, 128), 1)
    row = lax.broadcasted_iota(jnp.int32, (tm, 128), 0) + i * tm
    lo = lane < HEAD_DIM
    ones_cols = jnp.where((lane == HEAD_DIM) | (lane == HEAD_DIM + 1), 1.0, 0.0)
    onehot = jnp.where(lane == (row // SEL_BLOCK) % SUPER, 1.0, 0.0).astype(BF16)
    kcmp, vcmp = piece(0), piece(1)
    kslc = _group_rms(piece(2), kg_ref[1:2, :])
    kwin = _group_rms(piece(4), kg_ref[2:3, :])
    vslt = piece(3).T
    vwit = piece(5).T
    row_t = lax.broadcasted_iota(jnp.int32, (HEAD_DIM, tm), 0)
    ones_rows = jnp.where(row_t == 0, 1.0, 0.0)
    for g in range(N_GROUPS):
        sh = lambda a: a if g == 0 else pltpu.roll(a, HEAD_DIM, 1)
        kaug_ref[g, :, 0:128] = onehot
        kaug_ref[g, :, 128:256] = jnp.where(lo, sh(kslc), ones_cols).astype(BF16)
        kwin_ref[g] = jnp.where(lo, sh(kwin), 0.0).astype(BF16)
        kcr_ref[g] = sh(kcmp)[:, 0:HEAD_DIM]
        vcr_ref[g] = sh(vcmp)[:, 0:HEAD_DIM]
        vs = jnp.concatenate([vslt[g * HEAD_DIM:(g + 1) * HEAD_DIM], ones_rows], axis=0).astype(BF16)
        vw = jnp.concatenate([vwit[g * HEAD_DIM:(g + 1) * HEAD_DIM], ones_rows], axis=0).astype(BF16)
        for j in range(tm // KV_CHUNK):
            vt512_ref[g, j] = vs[:, j * KV_CHUNK:(j + 1) * KV_CHUNK]
        for j in range(tm // 128):
            vt128_ref[g, j] = vs[:, j * 128:(j + 1) * 128]
            vwt_ref[g, j] = vw[:, j * 128:(j + 1) * 128]


def _prep(q, kv, gn, br, qgain_col, kgain_rows):
    s = q.shape[0]
    tm = PREP_TM
    g = N_GROUPS
    outs = [
        jax.ShapeDtypeStruct((NSA_WIDTH, s), BF16),
        jax.ShapeDtypeStruct((NSA_WIDTH, s), BF16),
        jax.ShapeDtypeStruct((NSA_WIDTH, s), F32),
        jax.ShapeDtypeStruct((32, s), F32),
        jax.ShapeDtypeStruct((g, s, 256), BF16),
        jax.ShapeDtypeStruct((g, s // KV_CHUNK, 128, KV_CHUNK), BF16),
        jax.ShapeDtypeStruct((g, s // 128, 128, 128), BF16),
        jax.ShapeDtypeStruct((g, s, 128), BF16),
        jax.ShapeDtypeStruct((g, s // 128, 128, 128), BF16),
        jax.ShapeDtypeStruct((g, s, HEAD_DIM), F32),
        jax.ShapeDtypeStruct((g, s, HEAD_DIM), F32),
    ]
    out_specs = [
        pl.BlockSpec((NSA_WIDTH, tm), lambda i: (0, i)),
        pl.BlockSpec((NSA_WIDTH, tm), lambda i: (0, i)),
        pl.BlockSpec((NSA_WIDTH, tm), lambda i: (0, i)),
        pl.BlockSpec((32, tm), lambda i: (0, i)),
        pl.BlockSpec((g, tm, 256), lambda i: (0, i, 0)),
        pl.BlockSpec((g, tm // KV_CHUNK, 128, KV_CHUNK), lambda i: (0, i, 0, 0)),
        pl.BlockSpec((g, tm // 128, 128, 128), lambda i: (0, i, 0, 0)),
        pl.BlockSpec((g, tm, 128), lambda i: (0, i, 0)),
        pl.BlockSpec((g, tm // 128, 128, 128), lambda i: (0, i, 0, 0)),
        pl.BlockSpec((g, tm, HEAD_DIM), lambda i: (0, i, 0)),
        pl.BlockSpec((g, tm, HEAD_DIM), lambda i: (0, i, 0)),
    ]
    return pl.pallas_call(
        _prep_kernel,
        grid=(s // tm,),
        in_specs=[pl.BlockSpec((tm, NSA_WIDTH), lambda i: (i, 0)),
                  pl.BlockSpec((tm, 6 * KV_WIDTH), lambda i: (i, 0)),
                  pl.BlockSpec((tm, NSA_WIDTH), lambda i: (i, 0)),
                  pl.BlockSpec((tm, BR_PAD), lambda i: (i, 0)),
                  pl.BlockSpec((NSA_WIDTH, 1), lambda i: (0, 0)),
                  pl.BlockSpec((3, 128), lambda i: (0, 0))],
        out_specs=out_specs,
        out_shape=outs,
        compiler_params=_cparams(1),
        name="prep",
    )(q, kv, gn, br, qgain_col, kgain_rows)


def _compress_kernel(xk_ref, xv_ref, w1_ref, b1_ref, w2_ref, pe_ref, kg_ref, kc_ref, vct_ref):
    ncp = xk_ref.shape[0]
    half = CMP_STRIDE * HEAD_DIM
    lane = lax.broadcasted_iota(jnp.int32, (ncp, 128), 1)
    lo = lane < HEAD_DIM

    def phi(x_ref, j):
        x = x_ref[...].astype(BF16)
        w1 = w1_ref[j].astype(BF16)
        a = _dot(x, w1[:half])
        b = _dot(x, w1[half:])
        b_next = pltpu.roll(b, ncp - 1, 0)
        pe8 = jnp.broadcast_to(pe_ref[j], (8, 2 * half)).astype(BF16)
        pe_term = _dot(pe8, w1)[0:1]
        hid = a + b_next + pe_term + b1_ref[j]
        act = (hid * jax.nn.sigmoid(hid)).astype(BF16)
        return _dot(act, w2_ref[j].astype(BF16))

    kc = phi(xk_ref, 0)
    ms = jnp.sum(kc * kc, axis=-1, keepdims=True) / HEAD_DIM
    kc = kc * lax.rsqrt(ms + EPS) * kg_ref[...]
    hi = kc.astype(BF16).astype(F32)
    lo_part = (kc - hi).astype(BF16).astype(F32)
    ones_cols = jnp.where((lane == HEAD_DIM) | (lane == HEAD_DIM + 1), 1.0, 0.0)
    kc_ref[:, 0:128] = (hi + pltpu.roll(lo_part, HEAD_DIM, 1)).astype(BF16)
    kc_ref[:, 128:256] = jnp.where(lo, hi, ones_cols).astype(BF16)
    vct_ref[...] = phi(xv_ref, 1).T.astype(BF16)


def _compress(xk, xv, w1, b1, w2p, pe_flat, kgain_row):
    g, ncp, _ = xk.shape
    return pl.pallas_call(
        _compress_kernel,
        grid=(g,),
        in_specs=[pl.BlockSpec((None, ncp, CMP_STRIDE * HEAD_DIM), lambda i: (i, 0, 0)),
                  pl.BlockSpec((None, ncp, CMP_STRIDE * HEAD_DIM), lambda i: (i, 0, 0)),
                  pl.BlockSpec(w1.shape, lambda i: (0, 0, 0)),
                  pl.BlockSpec(b1.shape, lambda i: (0, 0, 0)),
                  pl.BlockSpec(w2p.shape, lambda i: (0, 0, 0)),
                  pl.BlockSpec(pe_flat.shape, lambda i: (0, 0, 0)),
                  pl.BlockSpec((1, 128), lambda i: (0, 0))],
        out_specs=[pl.BlockSpec((None, ncp, 256), lambda i: (i, 0, 0)),
                   pl.BlockSpec((None, 128, ncp), lambda i: (i, 0, 0))],
        out_shape=[jax.ShapeDtypeStruct((g, ncp, 256), BF16),
                   jax.ShapeDtypeStruct((g, 128, ncp), BF16)],
        compiler_params=_cparams(1),
        name="compress",
    )(xk, xv, w1, b1, w2p, pe_flat, kgain_row)


def _heads_to_lanes(q):
    return jnp.concatenate([q[r * HEAD_DIM:(r + 1) * HEAD_DIM] for r in range(GROUP_SIZE)], axis=1)


def _store_heads(out_ref, o):
    for r in range(GROUP_SIZE):
        out_ref[r * HEAD_DIM:(r + 1) * HEAD_DIM, :] = o[0:HEAD_DIM, r * Q_BLOCK:(r + 1) * Q_BLOCK]


def _bias_tile(w_ref, off, n):
    return jnp.concatenate([w_ref[r, pl.ds(off, n), :] for r in range(GROUP_SIZE)], axis=1)


def _cmp_kernel(qhi_ref, qlo_ref, kc_ref, vct_ref, wc_ref, mt_ref, ext_ref, ocmp_ref, neg_ref, s_ref):
    qb = pl.program_id(1)
    ncp = kc_ref.shape[0]
    nsbp = mt_ref.shape[0]
    qhi = qhi_ref[...]
    qlo = qlo_ref[...]
    cols = []
    for r in range(GROUP_SIZE):
        a = qhi[r * HEAD_DIM:(r + 1) * HEAD_DIM]
        b = qlo[r * HEAD_DIM:(r + 1) * HEAD_DIM]
        cols.append(jnp.concatenate([a, a, b], axis=0))
    qcat = jnp.concatenate([jnp.concatenate(cols, axis=1), ext_ref[...]], axis=0)
    s_ref[...] = _dot(kc_ref[...], qcat)
    ws = pl.multiple_of(jnp.maximum(8 * qb - 16, 0), 8)
    off = pl.multiple_of(jnp.maximum(16 - 8 * qb, 0), 8)
    s_ref[pl.ds(ws, CMP_NEAR), :] += _bias_tile(wc_ref, off, CMP_NEAR)
    rows = lax.broadcasted_iota(jnp.int32, (ncp, GQ), 0)
    s = jnp.where(rows < ws + CMP_NEAR, s_ref[...], NEG)
    m = jnp.maximum(jnp.max(s, axis=0, keepdims=True), M_INIT)
    p = jnp.exp2(s - m)
    l = jnp.sum(p, axis=0, keepdims=True)
    pn = p * jnp.where(l > 0.0, 1.0 / l, 0.0)
    _store_heads(ocmp_ref, _dot(vct_ref[...], pn.astype(BF16)))
    imp = pn[:, 0:Q_BLOCK]
    for r in range(1, GROUP_SIZE):
        imp = imp + pn[:, r * Q_BLOCK:(r + 1) * Q_BLOCK]
    i0 = imp.astype(BF16)
    r1 = imp - i0.astype(F32)
    i1 = r1.astype(BF16)
    i2 = (r1 - i1.astype(F32)).astype(BF16)
    mt = mt_ref[...]
    impb = _dot(mt, i0) + _dot(mt, i1) + _dot(mt, i2)
    blk = lax.broadcasted_iota(jnp.int32, (nsbp, Q_BLOCK), 0)
    lane = lax.broadcasted_iota(jnp.int32, (nsbp, Q_BLOCK), 1)
    cur = 2 * qb + jnp.where(lane >= SEL_BLOCK, 1, 0)
    forced = jnp.where(blk == 0, 1.0, jnp.where(blk == cur, 1.0, jnp.where(blk == cur - 1, 1.0, 0.0)))
    cand = jnp.where(blk < cur - 1, jnp.where(blk > 0, impb, -1.0), -1.0)
    sel = jnp.where(blk <= cur, forced, 0.0)

    def pick_one(_, carry):
        sc, sl = carry
        mx = jnp.max(sc, axis=0, keepdims=True)
        hit_blk = jnp.where(sc == mx, jnp.where(mx >= 0.0, blk, nsbp), nsbp)
        first = jnp.min(hit_blk, axis=0, keepdims=True)
        pick = blk == first
        return jnp.where(pick, -1.0, sc), jnp.where(pick, 1.0, sl)

    _, sel = lax.fori_loop(0, N_PICK, pick_one, (cand, sel))
    neg_ref[...] = jnp.where(sel > 0.5, 0.0, NEG).astype(BF16)


def _cmp_attention(qhi_t, qlo_t, kc_cat, vct, wc_t, mt, ext_q):
    s = qhi_t.shape[1]
    g = N_GROUPS
    nqb = s // Q_BLOCK
    ncp = kc_cat.shape[1]
    nsbp = mt.shape[0]
    gr = GROUP_SIZE * HEAD_DIM
    return pl.pallas_call(
        _cmp_kernel,
        grid=(g, nqb),
        in_specs=[pl.BlockSpec((gr, Q_BLOCK), lambda gi, qb: (gi, qb)),
                  pl.BlockSpec((gr, Q_BLOCK), lambda gi, qb: (gi, qb)),
                  pl.BlockSpec((None, ncp, 256), lambda gi, qb: (gi, 0, 0)),
                  pl.BlockSpec((None, 128, ncp), lambda gi, qb: (gi, 0, 0)),
                  pl.BlockSpec((GROUP_SIZE,) + wc_t.shape[1:], lambda gi, qb: (gi, 0, 0)),
                  pl.BlockSpec(mt.shape, lambda gi, qb: (0, 0)),
                  pl.BlockSpec((None, HEAD_DIM, GQ), lambda gi, qb: (gi, 0, 0))],
        out_specs=[pl.BlockSpec((gr, Q_BLOCK), lambda gi, qb: (gi, qb)),
                   pl.BlockSpec((None, nsbp, Q_BLOCK), lambda gi, qb: (gi, 0, qb))],
        out_shape=[jax.ShapeDtypeStruct((NSA_WIDTH, s), F32),
                   jax.ShapeDtypeStruct((g, nsbp, s), BF16)],
        scratch_shapes=[pltpu.VMEM((ncp, GQ), F32)],
        compiler_params=_cparams(2),
        name="cmp_select",
    )(qhi_t, qlo_t, kc_cat, vct, wc_t, mt, ext_q)


def _sel_kernel(qhi_ref, neg_ref, kaug_ref, vt512_ref, vt128_ref, ws_ref, ext_ref, out_ref,
                qaug_ref, m_ref, acc_ref, s0_ref, s1_ref):
    qb = pl.program_id(1)
    nsbp = neg_ref.shape[0]
    nsc = nsbp // SUPER
    qx = jnp.concatenate([_heads_to_lanes(qhi_ref[...]), ext_ref[...]], axis=0)
    blk = lax.broadcasted_iota(jnp.int32, (SUPER, Q_BLOCK), 0)

    def aug(neg_rows):
        tiled = jnp.concatenate([neg_rows] * GROUP_SIZE, axis=1).astype(BF16)
        return jnp.concatenate([tiled, qx], axis=0)

    for sc in range(nsc):
        neg_rows = neg_ref[sc * SUPER:(sc + 1) * SUPER, :].astype(F32)
        qaug_ref[sc] = aug(jnp.where(blk + sc * SUPER >= 2 * qb - 2, NEG, neg_rows))

    m_ref[...] = jnp.full(m_ref.shape, M_INIT, F32)
    acc_ref[...] = jnp.zeros(acc_ref.shape, F32)

    def update(s, vt):
        m_old = m_ref[...]
        m_new = jnp.maximum(m_old, jnp.max(s, axis=0, keepdims=True))
        p = jnp.exp2(s - m_new).astype(BF16)
        acc_ref[...] = acc_ref[...] * jnp.exp2(m_old - m_new) + _dot(vt, p)
        m_ref[...] = m_new

    n_far = (jnp.maximum(qb - 1, 0) * Q_BLOCK + KV_CHUNK - 1) // KV_CHUNK

    def scores(c, dst_ref):
        k = kaug_ref[pl.ds(pl.multiple_of(c * KV_CHUNK, KV_CHUNK), KV_CHUNK), :]
        sc = (c * (KV_CHUNK // SEL_BLOCK)) // SUPER
        dst_ref[...] = _dot(k, qaug_ref[sc])

    scores(0, s0_ref)

    def far_pair(i, carry):
        c = 2 * i
        scores(c + 1, s1_ref)
        update(s0_ref[...], vt512_ref[c])
        scores(jnp.minimum(c + 2, n_far - 1), s0_ref)
        update(s1_ref[...], vt512_ref[c + 1])
        return carry

    lax.fori_loop(0, n_far // 2, far_pair, 0)

    @pl.when(n_far % 2 == 1)
    def _():
        update(s0_ref[...], vt512_ref[n_far - 1])

    kb0 = jnp.maximum(qb - 1, 0)
    ws = pl.multiple_of(kb0 * Q_BLOCK, Q_BLOCK)
    off = pl.multiple_of(jnp.where(qb == 0, Q_BLOCK, 0), Q_BLOCK)
    b_lo = 2 * kb0
    sc_lo = pl.multiple_of((b_lo // SUPER) * SUPER, SUPER)
    sc_hi = pl.multiple_of(((b_lo + 3) // SUPER) * SUPER, SUPER)
    neg_lo = neg_ref[pl.ds(sc_lo, SUPER), :].astype(F32)
    neg_hi = neg_ref[pl.ds(sc_hi, SUPER), :].astype(F32)
    near_neg = jnp.where(blk >= SUPER // 2, neg_lo, neg_hi)
    k = kaug_ref[pl.ds(ws, SEL_NEAR), :]
    s = _dot(k, aug(near_neg)) + _bias_tile(ws_ref, off, SEL_NEAR)
    vt = jnp.concatenate([vt128_ref[kb0], vt128_ref[kb0 + 1]], axis=1)
    update(s, vt)

    acc = acc_ref[...]
    _store_heads(out_ref, acc[0:HEAD_DIM] * (1.0 / acc[HEAD_DIM:HEAD_DIM + 1]))


def _sel_attention(qhi_t, neg_t, kaug, vt512, vt128, ws_t, ext_q):
    s = qhi_t.shape[1]
    g = N_GROUPS
    nqb = s // Q_BLOCK
    nsbp = neg_t.shape[1]
    gr = GROUP_SIZE * HEAD_DIM
    once = pl.Buffered(1)
    return pl.pallas_call(
        _sel_kernel,
        grid=(g, nqb),
        in_specs=[pl.BlockSpec((gr, Q_BLOCK), lambda gi, qb: (gi, qb)),
                  pl.BlockSpec((None, nsbp, Q_BLOCK), lambda gi, qb: (gi, 0, qb)),
                  pl.BlockSpec((None, s, 256), lambda gi, qb: (gi, 0, 0), pipeline_mode=once),
                  pl.BlockSpec((None, s // KV_CHUNK, 128, KV_CHUNK), lambda gi, qb: (gi, 0, 0, 0),
                               pipeline_mode=once),
                  pl.BlockSpec((None, s // 128, 128, 128), lambda gi, qb: (gi, 0, 0, 0),
                               pipeline_mode=once),
                  pl.BlockSpec((GROUP_SIZE,) + ws_t.shape[1:], lambda gi, qb: (gi, 0, 0)),
                  pl.BlockSpec((None, HEAD_DIM, GQ), lambda gi, qb: (gi, 0, 0))],
        out_specs=pl.BlockSpec((gr, Q_BLOCK), lambda gi, qb: (gi, qb)),
        out_shape=jax.ShapeDtypeStruct((NSA_WIDTH, s), F32),
        scratch_shapes=[pltpu.VMEM((nsbp // SUPER, 256, GQ), BF16),
                        pltpu.VMEM((1, GQ), F32),
                        pltpu.VMEM((128, GQ), F32),
                        pltpu.VMEM((KV_CHUNK, GQ), F32),
                        pltpu.VMEM((KV_CHUNK, GQ), F32)],
        compiler_params=_cparams(2),
        name="sel_attention",
    )(qhi_t, neg_t, kaug, vt512, vt128, ws_t, ext_q)


def _win_kernel(qhi_ref, kwin_ref, vwt_ref, ww_ref, out_ref):
    qb = pl.program_id(1)
    kb0 = jnp.maximum(qb - WINDOW // Q_BLOCK, 0)
    ws = pl.multiple_of(kb0 * Q_BLOCK, Q_BLOCK)
    off = pl.multiple_of(jnp.maximum(WINDOW - qb * Q_BLOCK, 0), Q_BLOCK)
    qg = _heads_to_lanes(qhi_ref[...])
    qx = jnp.concatenate([qg, jnp.zeros_like(qg)], axis=0)
    s = _dot(kwin_ref[pl.ds(ws, WIN_KEYS), :], qx) + _bias_tile(ww_ref, off, WIN_KEYS)
    m = jnp.max(s, axis=0, keepdims=True)
    p = jnp.exp2(s - m).astype(BF16)
    vt = jnp.concatenate([vwt_ref[kb0 + j] for j in range(WIN_KEYS // Q_BLOCK)], axis=1)
    acc = _dot(vt, p)
    _store_heads(out_ref, acc[0:HEAD_DIM] * (1.0 / acc[HEAD_DIM:HEAD_DIM + 1]))


def _win_attention(qhi_t, kwin, vwt, ww_t):
    s = qhi_t.shape[1]
    g = N_GROUPS
    gr = GROUP_SIZE * HEAD_DIM
    once = pl.Buffered(1)
    return pl.pallas_call(
        _win_kernel,
        grid=(g, s // Q_BLOCK),
        in_specs=[pl.BlockSpec((gr, Q_BLOCK), lambda gi, qb: (gi, qb)),
                  pl.BlockSpec((None, s, 128), lambda gi, qb: (gi, 0, 0), pipeline_mode=once),
                  pl.BlockSpec((None, s // 128, 128, 128), lambda gi, qb: (gi, 0, 0, 0),
                               pipeline_mode=once),
                  pl.BlockSpec((GROUP_SIZE,) + ww_t.shape[1:], lambda gi, qb: (gi, 0, 0))],
        out_specs=pl.BlockSpec((gr, Q_BLOCK), lambda gi, qb: (gi, qb)),
        out_shape=jax.ShapeDtypeStruct((NSA_WIDTH, s), F32),
        compiler_params=_cparams(2),
        name="win_attention",
    )(qhi_t, kwin, vwt, ww_t)


LRU_TM = 256


def _lru_kernel(u_ref, cw_ref, cb_ref, wa_ref, ba_ref, wx_ref, bx_ref, lam_ref, h_ref,
                tail_ref, hprev_ref):
    tm = LRU_TM

    @pl.when(pl.program_id(0) == 0)
    def _():
        tail_ref[...] = jnp.zeros(tail_ref.shape, F32)
        hprev_ref[...] = jnp.zeros(hprev_ref.shape, F32)

    u = u_ref[...]
    tail = tail_ref[...]
    row8 = lax.broadcasted_iota(jnp.int32, (8, LRU_WIDTH), 0)
    uc = cb_ref[...] + u * cw_ref[CONV_WIDTH - 1:CONV_WIDTH, :]
    for j in range(1, CONV_WIDTH):
        sh = pltpu.roll(u, j, 0)
        head = jnp.where(row8 < j, pltpu.roll(tail, j, 0), sh[0:8])
        sh = jnp.concatenate([head, sh[8:]], axis=0)
        uc = uc + sh * cw_ref[CONV_WIDTH - 1 - j:CONV_WIDTH - j, :]
    tail_ref[...] = u[tm - 8:tm]

    ucb = uc.astype(BF16)
    r = jax.nn.sigmoid(_dot(ucb, wa_ref[...]) + ba_ref[...])
    ig = jax.nn.sigmoid(_dot(ucb, wx_ref[...]) + bx_ref[...])
    nl = -lam_ref[...]
    softplus = jnp.maximum(nl, 0.0) + jnp.log(1.0 + jnp.exp(-jnp.abs(nl)))
    log_a = (-LRU_C * r) * softplus
    a = jnp.exp(log_a)
    b = jnp.sqrt(1.0 - jnp.exp(2.0 * log_a)) * (ig * uc)

    row = lax.broadcasted_iota(jnp.int32, (tm, LRU_WIDTH), 0)
    step = 1
    while step < tm:
        a_s = pltpu.roll(a, step, 0)
        b_s = pltpu.roll(b, step, 0)
        ok = row >= step
        b = jnp.where(ok, a * b_s + b, b)
        a = jnp.where(ok, a * a_s, a)
        step *= 2
    h = a * hprev_ref[...] + b
    h_ref[...] = h
    hprev_ref[...] = h[tm - 1:tm]


def _rglru(u, conv_w, conv_b, wa_bd, ba, wx_bd, bx, lam):
    s = u.shape[0]
    tm = LRU_TM
    row = lambda v: v.reshape(1, LRU_WIDTH)
    full = lambda shape: pl.BlockSpec(shape, lambda i: (0, 0))
    return pl.pallas_call(
        _lru_kernel,
        grid=(s // tm,),
        in_specs=[pl.BlockSpec((tm, LRU_WIDTH), lambda i: (i, 0)),
                  full((CONV_WIDTH, LRU_WIDTH)), full((1, LRU_WIDTH)),
                  full((LRU_WIDTH, LRU_WIDTH)), full((1, LRU_WIDTH)),
                  full((LRU_WIDTH, LRU_WIDTH)), full((1, LRU_WIDTH)),
                  full((1, LRU_WIDTH))],
        out_specs=pl.BlockSpec((tm, LRU_WIDTH), lambda i: (i, 0)),
        out_shape=jax.ShapeDtypeStruct((s, LRU_WIDTH), F32),
        scratch_shapes=[pltpu.VMEM((8, LRU_WIDTH), F32), pltpu.VMEM((1, LRU_WIDTH), F32)],
        compiler_params=_cparams(1),
        name="rglru",
    )(u, conv_w, row(conv_b), wa_bd, row(ba), wx_bd, row(bx), row(lam))


OUT_TM = 256


def _out_kernel(x_ref, oc_ref, os_ref, ow_ref, brt_ref, gnt_ref, h_ref, gl_ref, mg_ref,
                wpa_ref, wpb_ref, wo_ref, y_ref):
    tm = OUT_TM
    brt = brt_ref[...]
    parts = []
    for h in range(N_HEADS):
        rows = slice(h * HEAD_DIM, (h + 1) * HEAD_DIM)
        acc = None
        for b, o_ref in enumerate((oc_ref, os_ref, ow_ref)):
            term = brt[b * N_HEADS + h:b * N_HEADS + h + 1, :] * o_ref[rows, :]
            acc = term if acc is None else acc + term
        parts.append(acc)
    ya_t = jnp.concatenate(parts, axis=0) * gnt_ref[...]
    y_a = _dot(ya_t.T.astype(BF16), wpa_ref[...])
    gl = gl_ref[...]
    y_b = _dot((h_ref[...] * (gl * jax.nn.sigmoid(gl))).astype(BF16), wpb_ref[...])
    mg = mg_ref[...]
    m = jax.nn.sigmoid(mg[:, :D_MODEL]) * y_a + jax.nn.sigmoid(mg[:, D_MODEL:]) * y_b
    y_ref[...] = x_ref[...] + _dot(m.astype(BF16), wo_ref[...])


def _output(x2, oc_t, os_t, ow_t, br_t, gn_t, h_lru, g_lru, merge_g, wpa, wpb, wo):
    s = x2.shape[0]
    tm = OUT_TM
    rows = lambda n: pl.BlockSpec((tm, n), lambda i: (i, 0))
    cols = lambda n: pl.BlockSpec((n, tm), lambda i: (0, i))
    full = lambda a: pl.BlockSpec(a.shape, lambda i: (0, 0))
    return pl.pallas_call(
        _out_kernel,
        grid=(s // tm,),
        in_specs=[rows(D_MODEL), cols(NSA_WIDTH), cols(NSA_WIDTH), cols(NSA_WIDTH), cols(32),
                  cols(NSA_WIDTH), rows(LRU_WIDTH), rows(LRU_WIDTH), rows(2 * D_MODEL),
                  full(wpa), full(wpb), full(wo)],
        out_specs=rows(D_MODEL),
        out_shape=jax.ShapeDtypeStruct((s, D_MODEL), F32),
        compiler_params=_cparams(1),
        name="output",
    )(x2, oc_t, os_t, ow_t, br_t, gn_t, h_lru, g_lru, merge_g, wpa, wpb, wo)


def _t5_bucket_table(n_dist):
    n = np.arange(n_dist)
    max_exact = N_BUCKETS // 2
    nf = np.maximum(n, 1).astype(np.float32)
    large = max_exact + (np.log(nf / np.float32(max_exact)) / np.float32(math.log(MAX_DISTANCE / max_exact))
                         * np.float32(N_BUCKETS - max_exact)).astype(np.int32)
    return np.where(n < max_exact, n, np.minimum(large, N_BUCKETS - 1))


def _bias_tiles_kernel(relb_ref, bw_ref, bs_ref, bc_ref, ww_ref, ws_ref, wc_ref):
    h = pl.program_id(0)
    far = relb_ref[N_BUCKETS - 1, h]
    for bk_ref, out_ref, rel in ((bw_ref, ww_ref, False), (bs_ref, ws_ref, True), (bc_ref, wc_ref, True)):
        bk = bk_ref[...]
        acc = jnp.full(bk.shape, NEG, F32)
        for b in range(N_BUCKETS):
            val = relb_ref[b, h] - far if rel else relb_ref[b, h]
            acc = jnp.where(bk == b, val * LOG2E, acc)
        out_ref[...] = acc


def _bias_tables(rel_bias):
    bucket = _t5_bucket_table(2 * WINDOW)
    i = np.arange(Q_BLOCK)[None, :]

    def index_tile(dist, valid):
        return jnp.asarray(np.where(valid, bucket[np.clip(dist, 0, bucket.size - 1)], -1).astype(np.int32))

    u = np.arange(WINDOW + WIN_KEYS)[:, None]
    d_win = i - u + WINDOW
    bw = index_tile(d_win, (d_win >= 0) & (d_win < WINDOW))
    u = np.arange(SEL_NEAR + Q_BLOCK)[:, None]
    d_sel = i - u + Q_BLOCK
    bs = index_tile(d_sel, d_sel >= 0)
    u = np.arange(CMP_NEAR + 16)[:, None]
    d_cmp = i - CMP_STRIDE * (u - 16) - (CMP_BLOCK - 1)
    bc = index_tile(d_cmp, d_cmp >= 0)
    full = lambda a: pl.BlockSpec(a.shape, lambda h: (0, 0))
    per_head = lambda a: pl.BlockSpec((None,) + a.shape, lambda h: (h, 0, 0))
    ww, ws, wc = pl.pallas_call(
        _bias_tiles_kernel,
        grid=(N_HEADS,),
        in_specs=[pl.BlockSpec(memory_space=pltpu.SMEM), full(bw), full(bs), full(bc)],
        out_specs=[per_head(bw), per_head(bs), per_head(bc)],
        out_shape=[jax.ShapeDtypeStruct((N_HEADS,) + a.shape, F32) for a in (bw, bs, bc)],
        compiler_params=_cparams(1),
        name="bias_tiles",
    )(rel_bias, bw, bs, bc)
    b_far = rel_bias[N_BUCKETS - 1] * LOG2E
    hi = b_far.astype(BF16)
    lo = (b_far - hi.astype(F32)).astype(BF16)
    ext = jnp.zeros((N_HEADS, HEAD_DIM, Q_BLOCK), BF16)
    ext = ext.at[:, 0, :].set(hi[:, None]).at[:, 1, :].set(lo[:, None])
    ext = ext.reshape(N_GROUPS, GROUP_SIZE, HEAD_DIM, Q_BLOCK).transpose(0, 2, 1, 3)
    return ww, ws, wc, ext.reshape(N_GROUPS, HEAD_DIM, GQ)


def _pool_matrix(nsbp, ncp, nc):
    j = np.arange(nsbp)[:, None]
    c = np.arange(ncp)[None, :]
    hit = (c >= SEL_PER_CMP * j - 1) & (c <= SEL_PER_CMP * j + SEL_PER_CMP - 1) & (c < nc)
    return jnp.asarray(hit.astype(np.float32), dtype=BF16)


def _block_diag(w):
    n, d, e = w.shape
    eye = jnp.eye(n, dtype=w.dtype)
    return (eye[:, None, :, None] * w[:, :, None, :]).reshape(n * d, n * e)


def kernel(x, norm_gain, w_in, q_norm_gain, k_norm_gain, cmp_pe, cmp_w1, cmp_b1, cmp_w2, rel_bias,
           conv_w, conv_b, lru_wa, lru_ba, lru_wx, lru_bx, lru_lambda, w_proj_a, w_proj_b, w_out):
    bsz, s, _ = x.shape
    assert bsz == 1 and s % 1024 == 0 and s >= 1024
    x2 = x.reshape(s, D_MODEL)
    nsb = s // SEL_BLOCK
    nsbp = -(-nsb // SUPER) * SUPER
    ncp = s // CMP_STRIDE

    o = np.cumsum((0, NSA_WIDTH, 6 * KV_WIDTH, NSA_WIDTH, 3 * N_HEADS, LRU_WIDTH, LRU_WIDTH, 2 * D_MODEL))
    pad = jnp.zeros((D_MODEL, BR_PAD - 3 * N_HEADS), w_in.dtype)
    w_pad = jnp.concatenate([w_in[:, :o[4]], pad, w_in[:, o[4]:]], axis=1).astype(BF16)

    q, kv, gn, br, u_lru, g_lru, merge_g = _inproj(x2, norm_gain, w_pad)

    scale = HEAD_DIM ** -0.5 * LOG2E
    qgain_col = jnp.tile(q_norm_gain * scale, N_HEADS).reshape(NSA_WIDTH, 1)
    kgain_rows = jnp.tile(k_norm_gain, (1, N_GROUPS))
    (qhi_t, qlo_t, gn_t, br_t, kaug, vt512, vt128, kwin, vwt, kcr, vcr) = _prep(
        q, kv, gn, br, qgain_col, kgain_rows)

    w2p = jnp.pad(cmp_w2, ((0, 0), (0, 0), (0, 128 - HEAD_DIM)))
    kc_gain = jnp.pad(k_norm_gain[0], (0, 128 - HEAD_DIM)).reshape(1, 128)
    kc_cat, vct = _compress(kcr.reshape(N_GROUPS, ncp, CMP_STRIDE * HEAD_DIM),
                            vcr.reshape(N_GROUPS, ncp, CMP_STRIDE * HEAD_DIM),
                            cmp_w1, cmp_b1.reshape(2, 1, CMP_HIDDEN), w2p,
                            cmp_pe.reshape(2, 1, CMP_BLOCK * HEAD_DIM), kc_gain)

    ww_t, ws_t, wc_t, ext_q = _bias_tables(rel_bias)
    mt = _pool_matrix(nsbp, ncp, ncp - 1)
    oc_t, neg_t = _cmp_attention(qhi_t, qlo_t, kc_cat, vct, wc_t, mt, ext_q)
    os_t = _sel_attention(qhi_t, neg_t, kaug, vt512, vt128, ws_t, ext_q)
    ow_t = _win_attention(qhi_t, kwin, vwt, ww_t)

    h_lru = _rglru(u_lru, conv_w, conv_b, _block_diag(lru_wa).astype(BF16), lru_ba,
                   _block_diag(lru_wx).astype(BF16), lru_bx, lru_lambda)

    y = _output(x2, oc_t, os_t, ow_t, br_t, gn_t, h_lru, g_lru, merge_g,
                w_proj_a.astype(BF16), w_proj_b.astype(BF16), w_out.astype(BF16))
    return y.reshape(bsz, s, D_MODEL)
```

```python
import functools
import math

import numpy as np
import jax
import jax.numpy as jnp
from jax import lax
from jax.experimental import pallas as pl
from jax.experimental.pallas import tpu as pltpu

F32 = jnp.float32
BF16 = jnp.bfloat16

D_MODEL = 1024
N_HEADS = 8
N_GROUPS = 2
GROUP_SIZE = N_HEADS // N_GROUPS
HEAD_DIM = 64
NSA_WIDTH = N_HEADS * HEAD_DIM
KV_WIDTH = N_GROUPS * HEAD_DIM
CMP_STRIDE = 16
CMP_BLOCK = 32
CMP_HIDDEN = 256
SEL_BLOCK = 64
SEL_PER_CMP = SEL_BLOCK // CMP_STRIDE
N_SELECT = 16
WINDOW = 512
Q_BLOCK = 128
LRU_WIDTH = 512
LRU_BLOCKS = 8
CONV_WIDTH = 4
LRU_C = 8.0
N_BUCKETS = 32
MAX_DISTANCE = 128
EPS = 1e-6
NEG = -1e30
M_INIT = -5e29
LOG2E = 1.4426950408889634

GQ = GROUP_SIZE * Q_BLOCK
SUPER = 128
KV_CHUNK = 512
SEL_NEAR = 2 * Q_BLOCK
WIN_KEYS = WINDOW + Q_BLOCK
CMP_NEAR = 24
N_PICK = N_SELECT - 3
VT_ROWS = 80

BR_PAD = 128
COL_SIZES = (NSA_WIDTH, 6 * KV_WIDTH, NSA_WIDTH, BR_PAD, LRU_WIDTH, LRU_WIDTH, 2 * D_MODEL)
COL_OFFS = tuple(int(v) for v in np.cumsum((0,) + COL_SIZES))
D_IN_PAD = COL_OFFS[-1]

VMEM_LIMIT = 56 * 1024 * 1024


def _cparams(n_axes):
    return pltpu.CompilerParams(dimension_semantics=("arbitrary",) * n_axes,
                                vmem_limit_bytes=VMEM_LIMIT)


def _dot(a, b):
    return jnp.dot(a, b, preferred_element_type=F32)


def _inproj_kernel(x_ref, g_ref, w_ref, *out_refs):
    x = x_ref[...]
    ms = jnp.mean(x * x, axis=-1, keepdims=True)
    h = (x * lax.rsqrt(ms + EPS) * g_ref[...]).astype(BF16)
    for ref, a, b in zip(out_refs, COL_OFFS[:-1], COL_OFFS[1:]):
        ref[...] = _dot(h, w_ref[:, a:b])


def _inproj(x2, norm_gain, w_pad, tm=256):
    s = x2.shape[0]
    outs = [jax.ShapeDtypeStruct((s, n), F32) for n in COL_SIZES]
    return pl.pallas_call(
        _inproj_kernel,
        grid=(s // tm,),
        in_specs=[pl.BlockSpec((tm, D_MODEL), lambda i: (i, 0)),
                  pl.BlockSpec((1, D_MODEL), lambda i: (0, 0)),
                  pl.BlockSpec((D_MODEL, D_IN_PAD), lambda i: (0, 0))],
        out_specs=[pl.BlockSpec((tm, n), lambda i: (i, 0)) for n in COL_SIZES],
        out_shape=outs,
        compiler_params=_cparams(1),
        name="inproj",
    )(x2, norm_gain.reshape(1, D_MODEL), w_pad)


PREP_TM = 512


def _group_rms(k, gain_row):
    sq = k * k
    lane = lax.broadcasted_iota(jnp.int32, k.shape, 1)
    lo = lane < HEAD_DIM
    s0 = jnp.sum(jnp.where(lo, sq, 0.0), axis=-1, keepdims=True)
    s1 = jnp.sum(jnp.where(lo, 0.0, sq), axis=-1, keepdims=True)
    inv = jnp.where(lo, lax.rsqrt(s0 / HEAD_DIM + EPS), lax.rsqrt(s1 / HEAD_DIM + EPS))
    return k * inv * gain_row


def _prep_kernel(q_ref, kv_ref, gn_ref, br_ref, qg_ref, kg_ref,
                 qhi_ref, qlo_ref, gnt_ref, brt_ref, kaug_ref, vt512_ref, vt128_ref,
                 kwin_ref, vwt_ref, kcr_ref, vcr_ref):
    i = pl.program_id(0)
    tm = PREP_TM
    qt = q_ref[...].T
    for h in range(N_HEADS):
        blk = qt[h * HEAD_DIM:(h + 1) * HEAD_DIM]
        ms = jnp.mean(blk * blk, axis=0, keepdims=True)
        qn = blk * lax.rsqrt(ms + EPS) * qg_ref[h * HEAD_DIM:(h + 1) * HEAD_DIM, :]
        hi = qn.astype(BF16)
        qhi_ref[h * HEAD_DIM:(h + 1) * HEAD_DIM, :] = hi
        qlo_ref[h * HEAD_DIM:(h + 1) * HEAD_DIM, :] = (qn - hi.astype(F32)).astype(BF16)
    gn = gn_ref[...]
    gnt_ref[...] = (gn * jax.nn.sigmoid(gn)).T
    brt_ref[...] = jax.nn.sigmoid(br_ref[...]).T[:32]

    kv = kv_ref[...]
    piece = lambda j: kv[:, j * KV_WIDTH:(j + 1) * KV_WIDTH]
    lane = lax.broadcasted_iota(jnp.int32, (tm, 128), 1)
    row = lax.broadcasted_iota(jnp.int32, (tm, 128), 0) + i * tm
    lo = lane < HEAD_DIM
    ones_cols = jnp.where((lane == HEAD_DIM) | (lane == HEAD_DIM + 1), 1.0, 0.0)
    onehot = jnp.where(lane == (row // SEL_BLOCK) % SUPER, 1.0, 0.0).astype(BF16)
    kcmp, vcmp = piece(0), piece(1)
    kslc = _group_rms(piece(2), kg_ref[1:2, :])
    kwin = _group_rms(piece(4), kg_ref[2:3, :])
    vslt = piece(3).T
    vwit = piece(5).T
    row_t = lax.broadcasted_iota(jnp.int32, (VT_ROWS - HEAD_DIM, tm), 0)
    ones_rows = jnp.where(row_t == 0, 1.0, 0.0)
    for g in range(N_GROUPS):
        sh = lambda a: a if g == 0 else pltpu.roll(a, HEAD_DIM, 1)
        kaug_ref[g, :, 0:128] = onehot
        kaug_ref[g, :, 128:256] = jnp.where(lo, sh(kslc), ones_cols).astype(BF16)
        kwin_ref[g] = jnp.where(lo, sh(kwin), 0.0).astype(BF16)
        kcr_ref[g] = sh(kcmp)[:, 0:HEAD_DIM]
        vcr_ref[g] = sh(vcmp)[:, 0:HEAD_DIM]
        vs = jnp.concatenate([vslt[g * HEAD_DIM:(g + 1) * HEAD_DIM], ones_rows], axis=0).astype(BF16)
        vw = jnp.concatenate([vwit[g * HEAD_DIM:(g + 1) * HEAD_DIM], ones_rows], axis=0).astype(BF16)
        for j in range(tm // KV_CHUNK):
            vt512_ref[g, j] = vs[:, j * KV_CHUNK:(j + 1) * KV_CHUNK]
        for j in range(tm // 128):
            vt128_ref[g, j] = vs[:, j * 128:(j + 1) * 128]
            vwt_ref[g, j] = vw[:, j * 128:(j + 1) * 128]


def _prep(q, kv, gn, br, qgain_col, kgain_rows):
    s = q.shape[0]
    tm = PREP_TM
    g = N_GROUPS
    outs = [
        jax.ShapeDtypeStruct((NSA_WIDTH, s), BF16),
        jax.ShapeDtypeStruct((NSA_WIDTH, s), BF16),
        jax.ShapeDtypeStruct((NSA_WIDTH, s), F32),
        jax.ShapeDtypeStruct((32, s), F32),
        jax.ShapeDtypeStruct((g, s, 256), BF16),
        jax.ShapeDtypeStruct((g, s // KV_CHUNK, VT_ROWS, KV_CHUNK), BF16),
        jax.ShapeDtypeStruct((g, s // 128, VT_ROWS, 128), BF16),
        jax.ShapeDtypeStruct((g, s, 128), BF16),
        jax.ShapeDtypeStruct((g, s // 128, VT_ROWS, 128), BF16),
        jax.ShapeDtypeStruct((g, s, HEAD_DIM), F32),
        jax.ShapeDtypeStruct((g, s, HEAD_DIM), F32),
    ]
    out_specs = [
        pl.BlockSpec((NSA_WIDTH, tm), lambda i: (0, i)),
        pl.BlockSpec((NSA_WIDTH, tm), lambda i: (0, i)),
        pl.BlockSpec((NSA_WIDTH, tm), lambda i: (0, i)),
        pl.BlockSpec((32, tm), lambda i: (0, i)),
        pl.BlockSpec((g, tm, 256), lambda i: (0, i, 0)),
        pl.BlockSpec((g, tm // KV_CHUNK, VT_ROWS, KV_CHUNK), lambda i: (0, i, 0, 0)),
        pl.BlockSpec((g, tm // 128, VT_ROWS, 128), lambda i: (0, i, 0, 0)),
        pl.BlockSpec((g, tm, 128), lambda i: (0, i, 0)),
        pl.BlockSpec((g, tm // 128, VT_ROWS, 128), lambda i: (0, i, 0, 0)),
        pl.BlockSpec((g, tm, HEAD_DIM), lambda i: (0, i, 0)),
        pl.BlockSpec((g, tm, HEAD_DIM), lambda i: (0, i, 0)),
    ]
    return pl.pallas_call(
        _prep_kernel,
        grid=(s // tm,),
        in_specs=[pl.BlockSpec((tm, NSA_WIDTH), lambda i: (i, 0)),
                  pl.BlockSpec((tm, 6 * KV_WIDTH), lambda i: (i, 0)),
                  pl.BlockSpec((tm, NSA_WIDTH), lambda i: (i, 0)),
                  pl.BlockSpec((tm, BR_PAD), lambda i: (i, 0)),
                  pl.BlockSpec((NSA_WIDTH, 1), lambda i: (0, 0)),
                  pl.BlockSpec((3, 128), lambda i: (0, 0))],
        out_specs=out_specs,
        out_shape=outs,
        compiler_params=_cparams(1),
        name="prep",
    )(q, kv, gn, br, qgain_col, kgain_rows)


def _compress_kernel(xk_ref, xv_ref, w1_ref, b1_ref, w2_ref, pe_ref, kg_ref, kc_ref, vct_ref):
    ncp = xk_ref.shape[0]
    half = CMP_STRIDE * HEAD_DIM
    lane = lax.broadcasted_iota(jnp.int32, (ncp, 128), 1)
    lo = lane < HEAD_DIM

    def phi(x_ref, j):
        x = x_ref[...].astype(BF16)
        w1 = w1_ref[j].astype(BF16)
        a = _dot(x, w1[:half])
        b = _dot(x, w1[half:])
        b_next = pltpu.roll(b, ncp - 1, 0)
        pe8 = jnp.broadcast_to(pe_ref[j], (8, 2 * half)).astype(BF16)
        pe_term = _dot(pe8, w1)[0:1]
        hid = a + b_next + pe_term + b1_ref[j]
        act = (hid * jax.nn.sigmoid(hid)).astype(BF16)
        return _dot(act, w2_ref[j].astype(BF16))

    kc = phi(xk_ref, 0)
    ms = jnp.sum(kc * kc, axis=-1, keepdims=True) / HEAD_DIM
    kc = kc * lax.rsqrt(ms + EPS) * kg_ref[...]
    hi = kc.astype(BF16).astype(F32)
    lo_part = (kc - hi).astype(BF16).astype(F32)
    ones_cols = jnp.where((lane == HEAD_DIM) | (lane == HEAD_DIM + 1), 1.0, 0.0)
    kc_ref[:, 0:128] = (hi + pltpu.roll(lo_part, HEAD_DIM, 1)).astype(BF16)
    kc_ref[:, 128:256] = jnp.where(lo, hi, ones_cols).astype(BF16)
    vct_ref[...] = phi(xv_ref, 1).T.astype(BF16)


def _compress(xk, xv, w1, b1, w2p, pe_flat, kgain_row):
    g, ncp, _ = xk.shape
    return pl.pallas_call(
        _compress_kernel,
        grid=(g,),
        in_specs=[pl.BlockSpec((None, ncp, CMP_STRIDE * HEAD_DIM), lambda i: (i, 0, 0)),
                  pl.BlockSpec((None, ncp, CMP_STRIDE * HEAD_DIM), lambda i: (i, 0, 0)),
                  pl.BlockSpec(w1.shape, lambda i: (0, 0, 0)),
                  pl.BlockSpec(b1.shape, lambda i: (0, 0, 0)),
                  pl.BlockSpec(w2p.shape, lambda i: (0, 0, 0)),
                  pl.BlockSpec(pe_flat.shape, lambda i: (0, 0, 0)),
                  pl.BlockSpec((1, 128), lambda i: (0, 0))],
        out_specs=[pl.BlockSpec((None, ncp, 256), lambda i: (i, 0, 0)),
                   pl.BlockSpec((None, 128, ncp), lambda i: (i, 0, 0))],
        out_shape=[jax.ShapeDtypeStruct((g, ncp, 256), BF16),
                   jax.ShapeDtypeStruct((g, 128, ncp), BF16)],
        compiler_params=_cparams(1),
        name="compress",
    )(xk, xv, w1, b1, w2p, pe_flat, kgain_row)


def _heads_to_lanes(q):
    return jnp.concatenate([q[r * HEAD_DIM:(r + 1) * HEAD_DIM] for r in range(GROUP_SIZE)], axis=1)


def _store_heads(out_ref, o):
    for r in range(GROUP_SIZE):
        out_ref[r * HEAD_DIM:(r + 1) * HEAD_DIM, :] = o[0:HEAD_DIM, r * Q_BLOCK:(r + 1) * Q_BLOCK]


def _bias_tile(w_ref, off, n):
    return jnp.concatenate([w_ref[r, pl.ds(off, n), :] for r in range(GROUP_SIZE)], axis=1)


def _cmp_kernel(qhi_ref, qlo_ref, kc_ref, vct_ref, wc_ref, mt_ref, ext_ref, ocmp_ref, neg_ref, s_ref):
    qb = pl.program_id(1)
    ncp = kc_ref.shape[0]
    nsbp = mt_ref.shape[0]
    qhi = qhi_ref[...]
    qlo = qlo_ref[...]
    cols = []
    for r in range(GROUP_SIZE):
        a = qhi[r * HEAD_DIM:(r + 1) * HEAD_DIM]
        b = qlo[r * HEAD_DIM:(r + 1) * HEAD_DIM]
        cols.append(jnp.concatenate([a, a, b], axis=0))
    qcat = jnp.concatenate([jnp.concatenate(cols, axis=1), ext_ref[...]], axis=0)
    s_ref[...] = _dot(kc_ref[...], qcat)
    ws = pl.multiple_of(jnp.maximum(8 * qb - 16, 0), 8)
    off = pl.multiple_of(jnp.maximum(16 - 8 * qb, 0), 8)
    s_ref[pl.ds(ws, CMP_NEAR), :] += _bias_tile(wc_ref, off, CMP_NEAR)
    rows = lax.broadcasted_iota(jnp.int32, (ncp, GQ), 0)
    s = jnp.where(rows < ws + CMP_NEAR, s_ref[...], NEG)
    m = jnp.maximum(jnp.max(s, axis=0, keepdims=True), M_INIT)
    p = jnp.exp2(s - m)
    l = jnp.sum(p, axis=0, keepdims=True)
    pn = p * jnp.where(l > 0.0, 1.0 / l, 0.0)
    _store_heads(ocmp_ref, _dot(vct_ref[...], pn.astype(BF16)))
    imp = pn[:, 0:Q_BLOCK]
    for r in range(1, GROUP_SIZE):
        imp = imp + pn[:, r * Q_BLOCK:(r + 1) * Q_BLOCK]
    i0 = imp.astype(BF16)
    r1 = imp - i0.astype(F32)
    i1 = r1.astype(BF16)
    i2 = (r1 - i1.astype(F32)).astype(BF16)
    mt = mt_ref[...]
    impb = _dot(mt, i0) + _dot(mt, i1) + _dot(mt, i2)
    blk = lax.broadcasted_iota(jnp.int32, (nsbp, Q_BLOCK), 0)
    lane = lax.broadcasted_iota(jnp.int32, (nsbp, Q_BLOCK), 1)
    cur = 2 * qb + jnp.where(lane >= SEL_BLOCK, 1, 0)
    forced = jnp.where(blk == 0, 1.0, jnp.where(blk == cur, 1.0, jnp.where(blk == cur - 1, 1.0, 0.0)))
    cand = jnp.where(blk < cur - 1, jnp.where(blk > 0, impb, -1.0), -1.0)
    sel = jnp.where(blk <= cur, forced, 0.0)

    def pick_one(_, carry):
        sc, sl = carry
        mx = jnp.max(sc, axis=0, keepdims=True)
        hit_blk = jnp.where(sc == mx, jnp.where(mx >= 0.0, blk, nsbp), nsbp)
        first = jnp.min(hit_blk, axis=0, keepdims=True)
        pick = blk == first
        return jnp.where(pick, -1.0, sc), jnp.where(pick, 1.0, sl)

    _, sel = lax.fori_loop(0, N_PICK, pick_one, (cand, sel))
    neg_ref[...] = jnp.where(sel > 0.5, 0.0, NEG).astype(BF16)


def _cmp_attention(qhi_t, qlo_t, kc_cat, vct, wc_t, mt, ext_q):
    s = qhi_t.shape[1]
    g = N_GROUPS
    nqb = s // Q_BLOCK
    ncp = kc_cat.shape[1]
    nsbp = mt.shape[0]
    gr = GROUP_SIZE * HEAD_DIM
    return pl.pallas_call(
        _cmp_kernel,
        grid=(g, nqb),
        in_specs=[pl.BlockSpec((gr, Q_BLOCK), lambda gi, qb: (gi, qb)),
                  pl.BlockSpec((gr, Q_BLOCK), lambda gi, qb: (gi, qb)),
                  pl.BlockSpec((None, ncp, 256), lambda gi, qb: (gi, 0, 0)),
                  pl.BlockSpec((None, 128, ncp), lambda gi, qb: (gi, 0, 0)),
                  pl.BlockSpec((GROUP_SIZE,) + wc_t.shape[1:], lambda gi, qb: (gi, 0, 0)),
                  pl.BlockSpec(mt.shape, lambda gi, qb: (0, 0)),
                  pl.BlockSpec((None, HEAD_DIM, GQ), lambda gi, qb: (gi, 0, 0))],
        out_specs=[pl.BlockSpec((gr, Q_BLOCK), lambda gi, qb: (gi, qb)),
                   pl.BlockSpec((None, nsbp, Q_BLOCK), lambda gi, qb: (gi, 0, qb))],
        out_shape=[jax.ShapeDtypeStruct((NSA_WIDTH, s), F32),
                   jax.ShapeDtypeStruct((g, nsbp, s), BF16)],
        scratch_shapes=[pltpu.VMEM((ncp, GQ), F32)],
        compiler_params=_cparams(2),
        name="cmp_select",
    )(qhi_t, qlo_t, kc_cat, vct, wc_t, mt, ext_q)


def _sel_kernel(qhi_ref, neg_ref, kaug_ref, vt512_ref, vt128_ref, ws_ref, ext_ref, out_ref,
                qaug_ref, m_ref, acc_ref, s0_ref, s1_ref):
    qb = pl.program_id(1)
    nsbp = neg_ref.shape[0]
    nsc = nsbp // SUPER
    qx = jnp.concatenate([_heads_to_lanes(qhi_ref[...]), ext_ref[...]], axis=0)
    blk = lax.broadcasted_iota(jnp.int32, (SUPER, Q_BLOCK), 0)

    def aug(neg_rows):
        tiled = jnp.concatenate([neg_rows] * GROUP_SIZE, axis=1).astype(BF16)
        return jnp.concatenate([tiled, qx], axis=0)

    for sc in range(nsc):
        neg_rows = neg_ref[sc * SUPER:(sc + 1) * SUPER, :].astype(F32)
        qaug_ref[sc] = aug(jnp.where(blk + sc * SUPER >= 2 * qb - 2, NEG, neg_rows))

    m_ref[...] = jnp.full(m_ref.shape, M_INIT, F32)
    acc_ref[...] = jnp.zeros(acc_ref.shape, F32)

    def update(s, vt):
        m_old = m_ref[...]
        m_new = jnp.maximum(m_old, jnp.max(s, axis=0, keepdims=True))
        p = jnp.exp2(s - m_new).astype(BF16)
        acc_ref[...] = acc_ref[...] * jnp.exp2(m_old - m_new) + _dot(vt, p)
        m_ref[...] = m_new

    n_far = (jnp.maximum(qb - 1, 0) * Q_BLOCK + KV_CHUNK - 1) // KV_CHUNK

    def scores(c, dst_ref):
        k = kaug_ref[pl.ds(pl.multiple_of(c * KV_CHUNK, KV_CHUNK), KV_CHUNK), :]
        sc = (c * (KV_CHUNK // SEL_BLOCK)) // SUPER
        dst_ref[...] = _dot(k, qaug_ref[sc])

    kb0 = jnp.maximum(qb - 1, 0)
    ws = pl.multiple_of(kb0 * Q_BLOCK, Q_BLOCK)
    off = pl.multiple_of(jnp.where(qb == 0, Q_BLOCK, 0), Q_BLOCK)
    b_lo = 2 * kb0
    sc_lo = pl.multiple_of((b_lo // SUPER) * SUPER, SUPER)
    sc_hi = pl.multiple_of(((b_lo + 3) // SUPER) * SUPER, SUPER)
    neg_lo = neg_ref[pl.ds(sc_lo, SUPER), :].astype(F32)
    neg_hi = neg_ref[pl.ds(sc_hi, SUPER), :].astype(F32)
    near_neg = jnp.where(blk >= SUPER // 2, neg_lo, neg_hi)
    k = kaug_ref[pl.ds(ws, SEL_NEAR), :]
    s_near = _dot(k, aug(near_neg)) + _bias_tile(ws_ref, off, SEL_NEAR)
    vt_near = jnp.concatenate([vt128_ref[kb0], vt128_ref[kb0 + 1]], axis=1)

    scores(0, s0_ref)
    update(s_near, vt_near)

    def far_pair(c):
        scores(c + 1, s1_ref)
        update(s0_ref[...], vt512_ref[c])
        scores(jnp.minimum(c + 2, n_far - 1), s0_ref)
        update(s1_ref[...], vt512_ref[c + 1])

    def far_quad(i, carry):
        far_pair(4 * i)
        far_pair(4 * i + 2)
        return carry

    lax.fori_loop(0, n_far // 4, far_quad, 0)

    @pl.when(n_far % 4 >= 2)
    def _():
        far_pair((n_far // 4) * 4)

    @pl.when(n_far % 2 == 1)
    def _():
        update(s0_ref[...], vt512_ref[n_far - 1])

    acc = acc_ref[...]
    _store_heads(out_ref, acc[0:HEAD_DIM] * (1.0 / acc[HEAD_DIM:HEAD_DIM + 1]))


def _sel_attention(qhi_t, neg_t, kaug, vt512, vt128, ws_t, ext_q):
    s = qhi_t.shape[1]
    g = N_GROUPS
    nqb = s // Q_BLOCK
    nsbp = neg_t.shape[1]
    gr = GROUP_SIZE * HEAD_DIM
    once = pl.Buffered(1)
    return pl.pallas_call(
        _sel_kernel,
        grid=(g, nqb),
        in_specs=[pl.BlockSpec((gr, Q_BLOCK), lambda gi, qb: (gi, qb)),
                  pl.BlockSpec((None, nsbp, Q_BLOCK), lambda gi, qb: (gi, 0, qb)),
                  pl.BlockSpec((None, s, 256), lambda gi, qb: (gi, 0, 0), pipeline_mode=once),
                  pl.BlockSpec((None, s // KV_CHUNK, VT_ROWS, KV_CHUNK), lambda gi, qb: (gi, 0, 0, 0),
                               pipeline_mode=once),
                  pl.BlockSpec((None, s // 128, VT_ROWS, 128), lambda gi, qb: (gi, 0, 0, 0),
                               pipeline_mode=once),
                  pl.BlockSpec((GROUP_SIZE,) + ws_t.shape[1:], lambda gi, qb: (gi, 0, 0)),
                  pl.BlockSpec((None, HEAD_DIM, GQ), lambda gi, qb: (gi, 0, 0))],
        out_specs=pl.BlockSpec((gr, Q_BLOCK), lambda gi, qb: (gi, qb)),
        out_shape=jax.ShapeDtypeStruct((NSA_WIDTH, s), F32),
        scratch_shapes=[pltpu.VMEM((nsbp // SUPER, 256, GQ), BF16),
                        pltpu.VMEM((1, GQ), F32),
                        pltpu.VMEM((VT_ROWS, GQ), F32),
                        pltpu.VMEM((KV_CHUNK, GQ), F32),
                        pltpu.VMEM((KV_CHUNK, GQ), F32)],
        compiler_params=_cparams(2),
        name="sel_attention",
    )(qhi_t, neg_t, kaug, vt512, vt128, ws_t, ext_q)


def _win_kernel(qhi_ref, kwin_ref, vwt_ref, ww_ref, out_ref):
    qb = pl.program_id(1)
    kb0 = jnp.maximum(qb - WINDOW // Q_BLOCK, 0)
    ws = pl.multiple_of(kb0 * Q_BLOCK, Q_BLOCK)
    off = pl.multiple_of(jnp.maximum(WINDOW - qb * Q_BLOCK, 0), Q_BLOCK)
    qg = _heads_to_lanes(qhi_ref[...])
    qx = jnp.concatenate([qg, jnp.zeros_like(qg)], axis=0)
    s = _dot(kwin_ref[pl.ds(ws, WIN_KEYS), :], qx) + _bias_tile(ww_ref, off, WIN_KEYS)
    m = jnp.max(s, axis=0, keepdims=True)
    p = jnp.exp2(s - m).astype(BF16)
    vt = jnp.concatenate([vwt_ref[kb0 + j] for j in range(WIN_KEYS // Q_BLOCK)], axis=1)
    acc = _dot(vt, p)
    _store_heads(out_ref, acc[0:HEAD_DIM] * (1.0 / acc[HEAD_DIM:HEAD_DIM + 1]))


def _win_attention(qhi_t, kwin, vwt, ww_t):
    s = qhi_t.shape[1]
    g = N_GROUPS
    gr = GROUP_SIZE * HEAD_DIM
    once = pl.Buffered(1)
    return pl.pallas_call(
        _win_kernel,
        grid=(g, s // Q_BLOCK),
        in_specs=[pl.BlockSpec((gr, Q_BLOCK), lambda gi, qb: (gi, qb)),
                  pl.BlockSpec((None, s, 128), lambda gi, qb: (gi, 0, 0), pipeline_mode=once),
                  pl.BlockSpec((None, s // 128, VT_ROWS, 128), lambda gi, qb: (gi, 0, 0, 0),
                               pipeline_mode=once),
                  pl.BlockSpec((GROUP_SIZE,) + ww_t.shape[1:], lambda gi, qb: (gi, 0, 0))],
        out_specs=pl.BlockSpec((gr, Q_BLOCK), lambda gi, qb: (gi, qb)),
        out_shape=jax.ShapeDtypeStruct((NSA_WIDTH, s), F32),
        compiler_params=_cparams(2),
        name="win_attention",
    )(qhi_t, kwin, vwt, ww_t)


LRU_TM = 256


def _lru_kernel(u_ref, cw_ref, cb_ref, wa_ref, ba_ref, wx_ref, bx_ref, lam_ref, h_ref,
                tail_ref, hprev_ref):
    tm = LRU_TM

    @pl.when(pl.program_id(0) == 0)
    def _():
        tail_ref[...] = jnp.zeros(tail_ref.shape, F32)
        hprev_ref[...] = jnp.zeros(hprev_ref.shape, F32)

    u = u_ref[...]
    tail = tail_ref[...]
    row8 = lax.broadcasted_iota(jnp.int32, (8, LRU_WIDTH), 0)
    uc = cb_ref[...] + u * cw_ref[CONV_WIDTH - 1:CONV_WIDTH, :]
    for j in range(1, CONV_WIDTH):
        sh = pltpu.roll(u, j, 0)
        head = jnp.where(row8 < j, pltpu.roll(tail, j, 0), sh[0:8])
        sh = jnp.concatenate([head, sh[8:]], axis=0)
        uc = uc + sh * cw_ref[CONV_WIDTH - 1 - j:CONV_WIDTH - j, :]
    tail_ref[...] = u[tm - 8:tm]

    ucb = uc.astype(BF16)
    r = jax.nn.sigmoid(_dot(ucb, wa_ref[...]) + ba_ref[...])
    ig = jax.nn.sigmoid(_dot(ucb, wx_ref[...]) + bx_ref[...])
    nl = -lam_ref[...]
    softplus = jnp.maximum(nl, 0.0) + jnp.log(1.0 + jnp.exp(-jnp.abs(nl)))
    log_a = (-LRU_C * r) * softplus
    a = jnp.exp(log_a)
    b = jnp.sqrt(1.0 - jnp.exp(2.0 * log_a)) * (ig * uc)

    row = lax.broadcasted_iota(jnp.int32, (tm, LRU_WIDTH), 0)
    step = 1
    while step < tm:
        a_s = pltpu.roll(a, step, 0)
        b_s = pltpu.roll(b, step, 0)
        ok = row >= step
        b = jnp.where(ok, a * b_s + b, b)
        a = jnp.where(ok, a * a_s, a)
        step *= 2
    h = a * hprev_ref[...] + b
    h_ref[...] = h
    hprev_ref[...] = h[tm - 1:tm]


def _rglru(u, conv_w, conv_b, wa_bd, ba, wx_bd, bx, lam):
    s = u.shape[0]
    tm = LRU_TM
    row = lambda v: v.reshape(1, LRU_WIDTH)
    full = lambda shape: pl.BlockSpec(shape, lambda i: (0, 0))
    return pl.pallas_call(
        _lru_kernel,
        grid=(s // tm,),
        in_specs=[pl.BlockSpec((tm, LRU_WIDTH), lambda i: (i, 0)),
                  full((CONV_WIDTH, LRU_WIDTH)), full((1, LRU_WIDTH)),
                  full((LRU_WIDTH, LRU_WIDTH)), full((1, LRU_WIDTH)),
                  full((LRU_WIDTH, LRU_WIDTH)), full((1, LRU_WIDTH)),
                  full((1, LRU_WIDTH))],
        out_specs=pl.BlockSpec((tm, LRU_WIDTH), lambda i: (i, 0)),
        out_shape=jax.ShapeDtypeStruct((s, LRU_WIDTH), F32),
        scratch_shapes=[pltpu.VMEM((8, LRU_WIDTH), F32), pltpu.VMEM((1, LRU_WIDTH), F32)],
        compiler_params=_cparams(1),
        name="rglru",
    )(u, conv_w, row(conv_b), wa_bd, row(ba), wx_bd, row(bx), row(lam))


OUT_TM = 256


def _out_kernel(x_ref, oc_ref, os_ref, ow_ref, brt_ref, gnt_ref, h_ref, gl_ref, mg_ref,
                wpa_ref, wpb_ref, wo_ref, y_ref):
    tm = OUT_TM
    brt = brt_ref[...]
    parts = []
    for h in range(N_HEADS):
        rows = slice(h * HEAD_DIM, (h + 1) * HEAD_DIM)
        acc = None
        for b, o_ref in enumerate((oc_ref, os_ref, ow_ref)):
            term = brt[b * N_HEADS + h:b * N_HEADS + h + 1, :] * o_ref[rows, :]
            acc = term if acc is None else acc + term
        parts.append(acc)
    ya_t = jnp.concatenate(parts, axis=0) * gnt_ref[...]
    y_a = _dot(ya_t.T.astype(BF16), wpa_ref[...])
    gl = gl_ref[...]
    y_b = _dot((h_ref[...] * (gl * jax.nn.sigmoid(gl))).astype(BF16), wpb_ref[...])
    mg = mg_ref[...]
    m = jax.nn.sigmoid(mg[:, :D_MODEL]) * y_a + jax.nn.sigmoid(mg[:, D_MODEL:]) * y_b
    y_ref[...] = x_ref[...] + _dot(m.astype(BF16), wo_ref[...])


def _output(x2, oc_t, os_t, ow_t, br_t, gn_t, h_lru, g_lru, merge_g, wpa, wpb, wo):
    s = x2.shape[0]
    tm = OUT_TM
    rows = lambda n: pl.BlockSpec((tm, n), lambda i: (i, 0))
    cols = lambda n: pl.BlockSpec((n, tm), lambda i: (0, i))
    full = lambda a: pl.BlockSpec(a.shape, lambda i: (0, 0))
    return pl.pallas_call(
        _out_kernel,
        grid=(s // tm,),
        in_specs=[rows(D_MODEL), cols(NSA_WIDTH), cols(NSA_WIDTH), cols(NSA_WIDTH), cols(32),
                  cols(NSA_WIDTH), rows(LRU_WIDTH), rows(LRU_WIDTH), rows(2 * D_MODEL),
                  full(wpa), full(wpb), full(wo)],
        out_specs=rows(D_MODEL),
        out_shape=jax.ShapeDtypeStruct((s, D_MODEL), F32),
        compiler_params=_cparams(1),
        name="output",
    )(x2, oc_t, os_t, ow_t, br_t, gn_t, h_lru, g_lru, merge_g, wpa, wpb, wo)


def _t5_bucket_table(n_dist):
    n = np.arange(n_dist)
    max_exact = N_BUCKETS // 2
    nf = np.maximum(n, 1).astype(np.float32)
    large = max_exact + (np.log(nf / np.float32(max_exact)) / np.float32(math.log(MAX_DISTANCE / max_exact))
                         * np.float32(N_BUCKETS - max_exact)).astype(np.int32)
    return np.where(n < max_exact, n, np.minimum(large, N_BUCKETS - 1))


def _bias_tiles_kernel(relb_ref, bw_ref, bs_ref, bc_ref, ww_ref, ws_ref, wc_ref):
    h = pl.program_id(0)
    far = relb_ref[N_BUCKETS - 1, h]
    for bk_ref, out_ref, rel in ((bw_ref, ww_ref, False), (bs_ref, ws_ref, True), (bc_ref, wc_ref, True)):
        bk = bk_ref[...]
        acc = jnp.full(bk.shape, NEG, F32)
        for b in range(N_BUCKETS):
            val = relb_ref[b, h] - far if rel else relb_ref[b, h]
            acc = jnp.where(bk == b, val * LOG2E, acc)
        out_ref[...] = acc


def _bias_tables(rel_bias):
    bucket = _t5_bucket_table(2 * WINDOW)
    i = np.arange(Q_BLOCK)[None, :]

    def index_tile(dist, valid):
        return jnp.asarray(np.where(valid, bucket[np.clip(dist, 0, bucket.size - 1)], -1).astype(np.int32))

    u = np.arange(WINDOW + WIN_KEYS)[:, None]
    d_win = i - u + WINDOW
    bw = index_tile(d_win, (d_win >= 0) & (d_win < WINDOW))
    u = np.arange(SEL_NEAR + Q_BLOCK)[:, None]
    d_sel = i - u + Q_BLOCK
    bs = index_tile(d_sel, d_sel >= 0)
    u = np.arange(CMP_NEAR + 16)[:, None]
    d_cmp = i - CMP_STRIDE * (u - 16) - (CMP_BLOCK - 1)
    bc = index_tile(d_cmp, d_cmp >= 0)
    full = lambda a: pl.BlockSpec(a.shape, lambda h: (0, 0))
    per_head = lambda a: pl.BlockSpec((None,) + a.shape, lambda h: (h, 0, 0))
    ww, ws, wc = pl.pallas_call(
        _bias_tiles_kernel,
        grid=(N_HEADS,),
        in_specs=[pl.BlockSpec(memory_space=pltpu.SMEM), full(bw), full(bs), full(bc)],
        out_specs=[per_head(bw), per_head(bs), per_head(bc)],
        out_shape=[jax.ShapeDtypeStruct((N_HEADS,) + a.shape, F32) for a in (bw, bs, bc)],
        compiler_params=_cparams(1),
        name="bias_tiles",
    )(rel_bias, bw, bs, bc)
    b_far = rel_bias[N_BUCKETS - 1] * LOG2E
    hi = b_far.astype(BF16)
    lo = (b_far - hi.astype(F32)).astype(BF16)
    ext = jnp.zeros((N_HEADS, HEAD_DIM, Q_BLOCK), BF16)
    ext = ext.at[:, 0, :].set(hi[:, None]).at[:, 1, :].set(lo[:, None])
    ext = ext.reshape(N_GROUPS, GROUP_SIZE, HEAD_DIM, Q_BLOCK).transpose(0, 2, 1, 3)
    return ww, ws, wc, ext.reshape(N_GROUPS, HEAD_DIM, GQ)


def _pool_matrix(nsbp, ncp, nc):
    j = np.arange(nsbp)[:, None]
    c = np.arange(ncp)[None, :]
    hit = (c >= SEL_PER_CMP * j - 1) & (c <= SEL_PER_CMP * j + SEL_PER_CMP - 1) & (c < nc)
    return jnp.asarray(hit.astype(np.float32), dtype=BF16)


def _block_diag(w):
    n, d, e = w.shape
    eye = jnp.eye(n, dtype=w.dtype)
    return (eye[:, None, :, None] * w[:, :, None, :]).reshape(n * d, n * e)


def kernel(x, norm_gain, w_in, q_norm_gain, k_norm_gain, cmp_pe, cmp_w1, cmp_b1, cmp_w2, rel_bias,
           conv_w, conv_b, lru_wa, lru_ba, lru_wx, lru_bx, lru_lambda, w_proj_a, w_proj_b, w_out):
    bsz, s, _ = x.shape
    assert bsz == 1 and s % 1024 == 0 and s >= 1024
    x2 = x.reshape(s, D_MODEL)
    nsb = s // SEL_BLOCK
    nsbp = -(-nsb // SUPER) * SUPER
    ncp = s // CMP_STRIDE

    o = np.cumsum((0, NSA_WIDTH, 6 * KV_WIDTH, NSA_WIDTH, 3 * N_HEADS, LRU_WIDTH, LRU_WIDTH, 2 * D_MODEL))
    pad = jnp.zeros((D_MODEL, BR_PAD - 3 * N_HEADS), w_in.dtype)
    w_pad = jnp.concatenate([w_in[:, :o[4]], pad, w_in[:, o[4]:]], axis=1).astype(BF16)

    q, kv, gn, br, u_lru, g_lru, merge_g = _inproj(x2, norm_gain, w_pad)

    scale = HEAD_DIM ** -0.5 * LOG2E
    qgain_col = jnp.tile(q_norm_gain * scale, N_HEADS).reshape(NSA_WIDTH, 1)
    kgain_rows = jnp.tile(k_norm_gain, (1, N_GROUPS))
    (qhi_t, qlo_t, gn_t, br_t, kaug, vt512, vt128, kwin, vwt, kcr, vcr) = _prep(
        q, kv, gn, br, qgain_col, kgain_rows)

    w2p = jnp.pad(cmp_w2, ((0, 0), (0, 0), (0, 128 - HEAD_DIM)))
    kc_gain = jnp.pad(k_norm_gain[0], (0, 128 - HEAD_DIM)).reshape(1, 128)
    kc_cat, vct = _compress(kcr.reshape(N_GROUPS, ncp, CMP_STRIDE * HEAD_DIM),
                            vcr.reshape(N_GROUPS, ncp, CMP_STRIDE * HEAD_DIM),
                            cmp_w1, cmp_b1.reshape(2, 1, CMP_HIDDEN), w2p,
                            cmp_pe.reshape(2, 1, CMP_BLOCK * HEAD_DIM), kc_gain)

    ww_t, ws_t, wc_t, ext_q = _bias_tables(rel_bias)
    mt = _pool_matrix(nsbp, ncp, ncp - 1)
    oc_t, neg_t = _cmp_attention(qhi_t, qlo_t, kc_cat, vct, wc_t, mt, ext_q)
    os_t = _sel_attention(qhi_t, neg_t, kaug, vt512, vt128, ws_t, ext_q)
    ow_t = _win_attention(qhi_t, kwin, vwt, ww_t)

    h_lru = _rglru(u_lru, conv_w, conv_b, _block_diag(lru_wa).astype(BF16), lru_ba,
                   _block_diag(lru_wx).astype(BF16), lru_bx, lru_lambda)

    y = _output(x2, oc_t, os_t, ow_t, br_t, gn_t, h_lru, g_lru, merge_g,
                w_proj_a.astype(BF16), w_proj_b.astype(BF16), w_out.astype(BF16))
    return y.reshape(bsz, s, D_MODEL)
```

```python
import functools
import math

import numpy as np
import jax
import jax.numpy as jnp
from jax import lax
from jax.experimental import pallas as pl
from jax.experimental.pallas import tpu as pltpu

F32 = jnp.float32
BF16 = jnp.bfloat16

D_MODEL = 1024
N_HEADS = 8
N_GROUPS = 2
GROUP_SIZE = N_HEADS // N_GROUPS
HEAD_DIM = 64
NSA_WIDTH = N_HEADS * HEAD_DIM
KV_WIDTH = N_GROUPS * HEAD_DIM
CMP_STRIDE = 16
CMP_BLOCK = 32
CMP_HIDDEN = 256
SEL_BLOCK = 64
SEL_PER_CMP = SEL_BLOCK // CMP_STRIDE
N_SELECT = 16
WINDOW = 512
Q_BLOCK = 128
LRU_WIDTH = 512
LRU_BLOCKS = 8
CONV_WIDTH = 4
LRU_C = 8.0
N_BUCKETS = 32
MAX_DISTANCE = 128
EPS = 1e-6
NEG = -1e30
M_INIT = -5e29
LOG2E = 1.4426950408889634

GQ = GROUP_SIZE * Q_BLOCK
SUPER = 128
KV_CHUNK = 512
SEL_NEAR = 2 * Q_BLOCK
WIN_KEYS = WINDOW + Q_BLOCK
CMP_NEAR = 24
N_PICK = N_SELECT - 3
VT_ROWS = 80

BR_PAD = 128
COL_SIZES = (NSA_WIDTH, 6 * KV_WIDTH, NSA_WIDTH, BR_PAD, LRU_WIDTH, LRU_WIDTH, 2 * D_MODEL)
COL_OFFS = tuple(int(v) for v in np.cumsum((0,) + COL_SIZES))
D_IN_PAD = COL_OFFS[-1]

VMEM_LIMIT = 56 * 1024 * 1024


def _cparams(n_axes):
    return pltpu.CompilerParams(dimension_semantics=("arbitrary",) * n_axes,
                                vmem_limit_bytes=VMEM_LIMIT)


def _dot(a, b):
    return jnp.dot(a, b, preferred_element_type=F32)


def _inproj_kernel(x_ref, g_ref, w_ref, *out_refs):
    x = x_ref[...]
    ms = jnp.mean(x * x, axis=-1, keepdims=True)
    h = (x * lax.rsqrt(ms + EPS) * g_ref[...]).astype(BF16)
    for ref, a, b in zip(out_refs, COL_OFFS[:-1], COL_OFFS[1:]):
        ref[...] = _dot(h, w_ref[:, a:b])


def _inproj(x2, norm_gain, w_pad, tm=256):
    s = x2.shape[0]
    outs = [jax.ShapeDtypeStruct((s, n), F32) for n in COL_SIZES]
    return pl.pallas_call(
        _inproj_kernel,
        grid=(s // tm,),
        in_specs=[pl.BlockSpec((tm, D_MODEL), lambda i: (i, 0)),
                  pl.BlockSpec((1, D_MODEL), lambda i: (0, 0)),
                  pl.BlockSpec((D_MODEL, D_IN_PAD), lambda i: (0, 0))],
        out_specs=[pl.BlockSpec((tm, n), lambda i: (i, 0)) for n in COL_SIZES],
        out_shape=outs,
        compiler_params=_cparams(1),
        name="inproj",
    )(x2, norm_gain.reshape(1, D_MODEL), w_pad)


PREP_TM = 512


def _group_rms(k, gain_row):
    sq = k * k
    lane = lax.broadcasted_iota(jnp.int32, k.shape, 1)
    lo = lane < HEAD_DIM
    s0 = jnp.sum(jnp.where(lo, sq, 0.0), axis=-1, keepdims=True)
    s1 = jnp.sum(jnp.where(lo, 0.0, sq), axis=-1, keepdims=True)
    inv = jnp.where(lo, lax.rsqrt(s0 / HEAD_DIM + EPS), lax.rsqrt(s1 / HEAD_DIM + EPS))
    return k * inv * gain_row


def _prep_kernel(q_ref, kv_ref, gn_ref, br_ref, qg_ref, kg_ref,
                 qhi_ref, qlo_ref, gnt_ref, brt_ref, kaug_ref, vt512_ref, vt128_ref,
                 kwin_ref, vwt_ref, kcr_ref, vcr_ref):
    i = pl.program_id(0)
    tm = PREP_TM
    qt = q_ref[...].T
    for h in range(N_HEADS):
        blk = qt[h * HEAD_DIM:(h + 1) * HEAD_DIM]
        ms = jnp.mean(blk * blk, axis=0, keepdims=True)
        qn = blk * lax.rsqrt(ms + EPS) * qg_ref[h * HEAD_DIM:(h + 1) * HEAD_DIM, :]
        hi = qn.astype(BF16)
        qhi_ref[h * HEAD_DIM:(h + 1) * HEAD_DIM, :] = hi
        qlo_ref[h * HEAD_DIM:(h + 1) * HEAD_DIM, :] = (qn - hi.astype(F32)).astype(BF16)
    gn = gn_ref[...]
    gnt_ref[...] = (gn * jax.nn.sigmoid(gn)).T
    brt_ref[...] = jax.nn.sigmoid(br_ref[...]).T[:32]

    kv = kv_ref[...]
    piece = lambda j: kv[:, j * KV_WIDTH:(j + 1) * KV_WIDTH]
    lane = lax.broadcasted_iota(jnp.int32, (tm, 128), 1)
    row = lax.broadcasted_iota(jnp.int32, (tm, 128), 0) + i * tm
    lo = lane < HEAD_DIM
    ones_cols = jnp.where((lane == HEAD_DIM) | (lane == HEAD_DIM + 1), 1.0, 0.0)
    onehot = jnp.where(lane == (row // SEL_BLOCK) % SUPER, 1.0, 0.0).astype(BF16)
    kcmp, vcmp = piece(0), piece(1)
    kslc = _group_rms(piece(2), kg_ref[1:2, :])
    kwin = _group_rms(piece(4), kg_ref[2:3, :])
    vslt = piece(3).T
    vwit = piece(5).T
    row_t = lax.broadcasted_iota(jnp.int32, (VT_ROWS - HEAD_DIM, tm), 0)
    ones_rows = jnp.where(row_t == 0, 1.0, 0.0)
    for g in range(N_GROUPS):
        sh = lambda a: a if g == 0 else pltpu.roll(a, HEAD_DIM, 1)
        kaug_ref[g, :, 0:128] = onehot
        kaug_ref[g, :, 128:256] = jnp.where(lo, sh(kslc), ones_cols).astype(BF16)
        kwin_ref[g] = jnp.where(lo, sh(kwin), 0.0).astype(BF16)
        kcr_ref[g] = sh(kcmp)[:, 0:HEAD_DIM]
        vcr_ref[g] = sh(vcmp)[:, 0:HEAD_DIM]
        vs = jnp.concatenate([vslt[g * HEAD_DIM:(g + 1) * HEAD_DIM], ones_rows], axis=0).astype(BF16)
        vw = jnp.concatenate([vwit[g * HEAD_DIM:(g + 1) * HEAD_DIM], ones_rows], axis=0).astype(BF16)
        for j in range(tm // KV_CHUNK):
            vt512_ref[g, j] = vs[:, j * KV_CHUNK:(j + 1) * KV_CHUNK]
        for j in range(tm // 128):
            vt128_ref[g, j] = vs[:, j * 128:(j + 1) * 128]
            vwt_ref[g, j] = vw[:, j * 128:(j + 1) * 128]


def _prep(q, kv, gn, br, qgain_col, kgain_rows):
    s = q.shape[0]
    tm = PREP_TM
    g = N_GROUPS
    outs = [
        jax.ShapeDtypeStruct((NSA_WIDTH, s), BF16),
        jax.ShapeDtypeStruct((NSA_WIDTH, s), BF16),
        jax.ShapeDtypeStruct((NSA_WIDTH, s), F32),
        jax.ShapeDtypeStruct((32, s), F32),
        jax.ShapeDtypeStruct((g, s, 256), BF16),
        jax.ShapeDtypeStruct((g, s // KV_CHUNK, VT_ROWS, KV_CHUNK), BF16),
        jax.ShapeDtypeStruct((g, s // 128, VT_ROWS, 128), BF16),
        jax.ShapeDtypeStruct((g, s, 128), BF16),
        jax.ShapeDtypeStruct((g, s // 128, VT_ROWS, 128), BF16),
        jax.ShapeDtypeStruct((g, s, HEAD_DIM), F32),
        jax.ShapeDtypeStruct((g, s, HEAD_DIM), F32),
    ]
    out_specs = [
        pl.BlockSpec((NSA_WIDTH, tm), lambda i: (0, i)),
        pl.BlockSpec((NSA_WIDTH, tm), lambda i: (0, i)),
        pl.BlockSpec((NSA_WIDTH, tm), lambda i: (0, i)),
        pl.BlockSpec((32, tm), lambda i: (0, i)),
        pl.BlockSpec((g, tm, 256), lambda i: (0, i, 0)),
        pl.BlockSpec((g, tm // KV_CHUNK, VT_ROWS, KV_CHUNK), lambda i: (0, i, 0, 0)),
        pl.BlockSpec((g, tm // 128, VT_ROWS, 128), lambda i: (0, i, 0, 0)),
        pl.BlockSpec((g, tm, 128), lambda i: (0, i, 0)),
        pl.BlockSpec((g, tm // 128, VT_ROWS, 128), lambda i: (0, i, 0, 0)),
        pl.BlockSpec((g, tm, HEAD_DIM), lambda i: (0, i, 0)),
        pl.BlockSpec((g, tm, HEAD_DIM), lambda i: (0, i, 0)),
    ]
    return pl.pallas_call(
        _prep_kernel,
        grid=(s // tm,),
        in_specs=[pl.BlockSpec((tm, NSA_WIDTH), lambda i: (i, 0)),
                  pl.BlockSpec((tm, 6 * KV_WIDTH), lambda i: (i, 0)),
                  pl.BlockSpec((tm, NSA_WIDTH), lambda i: (i, 0)),
                  pl.BlockSpec((tm, BR_PAD), lambda i: (i, 0)),
                  pl.BlockSpec((NSA_WIDTH, 1), lambda i: (0, 0)),
                  pl.BlockSpec((3, 128), lambda i: (0, 0))],
        out_specs=out_specs,
        out_shape=outs,
        compiler_params=_cparams(1),
        name="prep",
    )(q, kv, gn, br, qgain_col, kgain_rows)


def _compress_kernel(xk_ref, xv_ref, w1_ref, b1_ref, w2_ref, pe_ref, kg_ref, kc_ref, vct_ref):
    ncp = xk_ref.shape[0]
    half = CMP_STRIDE * HEAD_DIM
    lane = lax.broadcasted_iota(jnp.int32, (ncp, 128), 1)
    lo = lane < HEAD_DIM

    def phi(x_ref, j):
        x = x_ref[...].astype(BF16)
        w1 = w1_ref[j].astype(BF16)
        a = _dot(x, w1[:half])
        b = _dot(x, w1[half:])
        b_next = pltpu.roll(b, ncp - 1, 0)
        pe8 = jnp.broadcast_to(pe_ref[j], (8, 2 * half)).astype(BF16)
        pe_term = _dot(pe8, w1)[0:1]
        hid = a + b_next + pe_term + b1_ref[j]
        act = (hid * jax.nn.sigmoid(hid)).astype(BF16)
        return _dot(act, w2_ref[j].astype(BF16))

    kc = phi(xk_ref, 0)
    ms = jnp.sum(kc * kc, axis=-1, keepdims=True) / HEAD_DIM
    kc = kc * lax.rsqrt(ms + EPS) * kg_ref[...]
    hi = kc.astype(BF16).astype(F32)
    lo_part = (kc - hi).astype(BF16).astype(F32)
    ones_cols = jnp.where((lane == HEAD_DIM) | (lane == HEAD_DIM + 1), 1.0, 0.0)
    kc_ref[:, 0:128] = (hi + pltpu.roll(lo_part, HEAD_DIM, 1)).astype(BF16)
    kc_ref[:, 128:256] = jnp.where(lo, hi, ones_cols).astype(BF16)
    vct_ref[...] = phi(xv_ref, 1).T.astype(BF16)


def _compress(xk, xv, w1, b1, w2p, pe_flat, kgain_row):
    g, ncp, _ = xk.shape
    return pl.pallas_call(
        _compress_kernel,
        grid=(g,),
        in_specs=[pl.BlockSpec((None, ncp, CMP_STRIDE * HEAD_DIM), lambda i: (i, 0, 0)),
                  pl.BlockSpec((None, ncp, CMP_STRIDE * HEAD_DIM), lambda i: (i, 0, 0)),
                  pl.BlockSpec(w1.shape, lambda i: (0, 0, 0)),
                  pl.BlockSpec(b1.shape, lambda i: (0, 0, 0)),
                  pl.BlockSpec(w2p.shape, lambda i: (0, 0, 0)),
                  pl.BlockSpec(pe_flat.shape, lambda i: (0, 0, 0)),
                  pl.BlockSpec((1, 128), lambda i: (0, 0))],
        out_specs=[pl.BlockSpec((None, ncp, 256), lambda i: (i, 0, 0)),
                   pl.BlockSpec((None, 128, ncp), lambda i: (i, 0, 0))],
        out_shape=[jax.ShapeDtypeStruct((g, ncp, 256), BF16),
                   jax.ShapeDtypeStruct((g, 128, ncp), BF16)],
        compiler_params=_cparams(1),
        name="compress",
    )(xk, xv, w1, b1, w2p, pe_flat, kgain_row)


def _heads_to_lanes(q):
    return jnp.concatenate([q[r * HEAD_DIM:(r + 1) * HEAD_DIM] for r in range(GROUP_SIZE)], axis=1)


def _store_heads(out_ref, o):
    for r in range(GROUP_SIZE):
        out_ref[r * HEAD_DIM:(r + 1) * HEAD_DIM, :] = o[0:HEAD_DIM, r * Q_BLOCK:(r + 1) * Q_BLOCK]


def _bias_tile(w_ref, off, n):
    return jnp.concatenate([w_ref[r, pl.ds(off, n), :] for r in range(GROUP_SIZE)], axis=1)


def _cmp_kernel(qhi_ref, qlo_ref, kc_ref, vct_ref, wc_ref, ext_ref, ocmp_ref, neg_ref, s_ref, imp_ref):
    qb = pl.program_id(1)
    ncp = kc_ref.shape[0]
    nsbp = neg_ref.shape[0]
    qhi = qhi_ref[...]
    qlo = qlo_ref[...]
    cols = []
    for r in range(GROUP_SIZE):
        a = qhi[r * HEAD_DIM:(r + 1) * HEAD_DIM]
        b = qlo[r * HEAD_DIM:(r + 1) * HEAD_DIM]
        cols.append(jnp.concatenate([a, a, b], axis=0))
    qcat = jnp.concatenate([jnp.concatenate(cols, axis=1), ext_ref[...]], axis=0)
    ws = pl.multiple_of(jnp.maximum(8 * qb - 16, 0), 8)
    off = pl.multiple_of(jnp.maximum(16 - 8 * qb, 0), 8)
    lim = ws + CMP_NEAR

    def attend(nrows):
        nblk = nrows // SEL_PER_CMP
        s_ref[0:nrows, :] = _dot(kc_ref[0:nrows, :], qcat)
        s_ref[pl.ds(ws, CMP_NEAR), :] += _bias_tile(wc_ref, off, CMP_NEAR)
        rows = lax.broadcasted_iota(jnp.int32, (nrows, GQ), 0)
        s = jnp.where(rows < lim, s_ref[0:nrows, :], NEG)
        m = jnp.maximum(jnp.max(s, axis=0, keepdims=True), M_INIT)
        p = jnp.exp2(s - m)
        l = jnp.sum(p, axis=0, keepdims=True)
        pn = p * jnp.where(l > 0.0, 1.0 / l, 0.0)
        _store_heads(ocmp_ref, _dot(vct_ref[:, 0:nrows], pn.astype(BF16)))
        imp = pn[:, 0:Q_BLOCK]
        for r in range(1, GROUP_SIZE):
            imp = imp + pn[:, r * Q_BLOCK:(r + 1) * Q_BLOCK]
        imp_ref[0:nrows, :] = imp
        strided = [imp_ref[pl.ds(k, nblk, stride=SEL_PER_CMP), :] for k in range(SEL_PER_CMP)]
        blk = lax.broadcasted_iota(jnp.int32, (nblk, Q_BLOCK), 0)
        prev_last = jnp.where(blk == 0, 0.0, pltpu.roll(strided[SEL_PER_CMP - 1], 1, 0))
        impb = prev_last + strided[0] + strided[1] + strided[2] + strided[3]
        lane = lax.broadcasted_iota(jnp.int32, (nblk, Q_BLOCK), 1)
        cur = 2 * qb + jnp.where(lane >= SEL_BLOCK, 1, 0)
        forced = jnp.where(blk == 0, 1.0, jnp.where(blk == cur, 1.0, jnp.where(blk == cur - 1, 1.0, 0.0)))
        cand = jnp.where(blk < cur - 1, jnp.where(blk > 0, impb, -1.0), -1.0)
        sel = jnp.where(blk <= cur, forced, 0.0)

        def pick_one(_, carry):
            sc, sl = carry
            mx = jnp.max(sc, axis=0, keepdims=True)
            hit_blk = jnp.where(sc == mx, jnp.where(mx >= 0.0, blk, nblk), nblk)
            first = jnp.min(hit_blk, axis=0, keepdims=True)
            pick = blk == first
            return jnp.where(pick, -1.0, sc), jnp.where(pick, 1.0, sl)

        _, sel = lax.fori_loop(0, N_PICK, pick_one, (cand, sel))
        neg_ref[0:nblk, :] = jnp.where(sel > 0.5, 0.0, NEG).astype(BF16)
        if nblk < nsbp:
            neg_ref[nblk:nsbp, :] = jnp.full((nsbp - nblk, Q_BLOCK), NEG, BF16)

    n_cls = min(4, ncp // 128)
    per_cls = ncp // n_cls
    cls = (lim + per_cls - 1) // per_cls
    for k in range(1, n_cls + 1):
        pl.when(cls == k)(functools.partial(attend, k * per_cls))


def _cmp_attention(qhi_t, qlo_t, kc_cat, vct, wc_t, ext_q, nsbp):
    s = qhi_t.shape[1]
    g = N_GROUPS
    nqb = s // Q_BLOCK
    ncp = kc_cat.shape[1]
    gr = GROUP_SIZE * HEAD_DIM
    return pl.pallas_call(
        _cmp_kernel,
        grid=(g, nqb),
        in_specs=[pl.BlockSpec((gr, Q_BLOCK), lambda gi, qb: (gi, qb)),
                  pl.BlockSpec((gr, Q_BLOCK), lambda gi, qb: (gi, qb)),
                  pl.BlockSpec((None, ncp, 256), lambda gi, qb: (gi, 0, 0)),
                  pl.BlockSpec((None, 128, ncp), lambda gi, qb: (gi, 0, 0)),
                  pl.BlockSpec((GROUP_SIZE,) + wc_t.shape[1:], lambda gi, qb: (gi, 0, 0)),
                  pl.BlockSpec((None, HEAD_DIM, GQ), lambda gi, qb: (gi, 0, 0))],
        out_specs=[pl.BlockSpec((gr, Q_BLOCK), lambda gi, qb: (gi, qb)),
                   pl.BlockSpec((None, nsbp, Q_BLOCK), lambda gi, qb: (gi, 0, qb))],
        out_shape=[jax.ShapeDtypeStruct((NSA_WIDTH, s), F32),
                   jax.ShapeDtypeStruct((g, nsbp, s), BF16)],
        scratch_shapes=[pltpu.VMEM((ncp, GQ), F32), pltpu.VMEM((ncp, Q_BLOCK), F32)],
        compiler_params=_cparams(2),
        name="cmp_select",
    )(qhi_t, qlo_t, kc_cat, vct, wc_t, ext_q)


def _sel_kernel(qhi_ref, neg_ref, kaug_ref, vt512_ref, vt128_ref, ws_ref, ext_ref, out_ref,
                qaug_ref, m_ref, acc_ref, s0_ref, s1_ref):
    qb = pl.program_id(1)
    nsbp = neg_ref.shape[0]
    nsc = nsbp // SUPER
    qx = jnp.concatenate([_heads_to_lanes(qhi_ref[...]), ext_ref[...]], axis=0)
    blk = lax.broadcasted_iota(jnp.int32, (SUPER, Q_BLOCK), 0)

    def aug(neg_rows):
        tiled = jnp.concatenate([neg_rows] * GROUP_SIZE, axis=1).astype(BF16)
        return jnp.concatenate([tiled, qx], axis=0)

    for sc in range(nsc):
        neg_rows = neg_ref[sc * SUPER:(sc + 1) * SUPER, :].astype(F32)
        qaug_ref[sc] = aug(jnp.where(blk + sc * SUPER >= 2 * qb - 2, NEG, neg_rows))

    m_ref[...] = jnp.full(m_ref.shape, M_INIT, F32)
    acc_ref[...] = jnp.zeros(acc_ref.shape, F32)

    def update(s, vt):
        m_old = m_ref[...]
        m_new = jnp.maximum(m_old, jnp.max(s, axis=0, keepdims=True))
        p = jnp.exp2(s - m_new).astype(BF16)
        acc_ref[...] = acc_ref[...] * jnp.exp2(m_old - m_new) + _dot(vt, p)
        m_ref[...] = m_new

    n_far = (jnp.maximum(qb - 1, 0) * Q_BLOCK + KV_CHUNK - 1) // KV_CHUNK

    def scores(c, dst_ref):
        k = kaug_ref[pl.ds(pl.multiple_of(c * KV_CHUNK, KV_CHUNK), KV_CHUNK), :]
        sc = (c * (KV_CHUNK // SEL_BLOCK)) // SUPER
        dst_ref[...] = _dot(k, qaug_ref[sc])

    kb0 = jnp.maximum(qb - 1, 0)
    ws = pl.multiple_of(kb0 * Q_BLOCK, Q_BLOCK)
    off = pl.multiple_of(jnp.where(qb == 0, Q_BLOCK, 0), Q_BLOCK)
    b_lo = 2 * kb0
    sc_lo = pl.multiple_of((b_lo // SUPER) * SUPER, SUPER)
    sc_hi = pl.multiple_of(((b_lo + 3) // SUPER) * SUPER, SUPER)
    neg_lo = neg_ref[pl.ds(sc_lo, SUPER), :].astype(F32)
    neg_hi = neg_ref[pl.ds(sc_hi, SUPER), :].astype(F32)
    near_neg = jnp.where(blk >= SUPER // 2, neg_lo, neg_hi)
    k = kaug_ref[pl.ds(ws, SEL_NEAR), :]
    s_near = _dot(k, aug(near_neg)) + _bias_tile(ws_ref, off, SEL_NEAR)
    vt_near = jnp.concatenate([vt128_ref[kb0], vt128_ref[kb0 + 1]], axis=1)

    scores(0, s0_ref)
    update(s_near, vt_near)

    def far_pair(c):
        scores(c + 1, s1_ref)
        update(s0_ref[...], vt512_ref[c])
        scores(jnp.minimum(c + 2, n_far - 1), s0_ref)
        update(s1_ref[...], vt512_ref[c + 1])

    def far_quad(i, carry):
        far_pair(4 * i)
        far_pair(4 * i + 2)
        return carry

    lax.fori_loop(0, n_far // 4, far_quad, 0)

    @pl.when(n_far % 4 >= 2)
    def _():
        far_pair((n_far // 4) * 4)

    @pl.when(n_far % 2 == 1)
    def _():
        update(s0_ref[...], vt512_ref[n_far - 1])

    acc = acc_ref[...]
    _store_heads(out_ref, acc[0:HEAD_DIM] * (1.0 / acc[HEAD_DIM:HEAD_DIM + 1]))


def _sel_attention(qhi_t, neg_t, kaug, vt512, vt128, ws_t, ext_q):
    s = qhi_t.shape[1]
    g = N_GROUPS
    nqb = s // Q_BLOCK
    nsbp = neg_t.shape[1]
    gr = GROUP_SIZE * HEAD_DIM
    once = pl.Buffered(1)
    return pl.pallas_call(
        _sel_kernel,
        grid=(g, nqb),
        in_specs=[pl.BlockSpec((gr, Q_BLOCK), lambda gi, qb: (gi, qb)),
                  pl.BlockSpec((None, nsbp, Q_BLOCK), lambda gi, qb: (gi, 0, qb)),
                  pl.BlockSpec((None, s, 256), lambda gi, qb: (gi, 0, 0), pipeline_mode=once),
                  pl.BlockSpec((None, s // KV_CHUNK, VT_ROWS, KV_CHUNK), lambda gi, qb: (gi, 0, 0, 0),
                               pipeline_mode=once),
                  pl.BlockSpec((None, s // 128, VT_ROWS, 128), lambda gi, qb: (gi, 0, 0, 0),
                               pipeline_mode=once),
                  pl.BlockSpec((GROUP_SIZE,) + ws_t.shape[1:], lambda gi, qb: (gi, 0, 0)),
                  pl.BlockSpec((None, HEAD_DIM, GQ), lambda gi, qb: (gi, 0, 0))],
        out_specs=pl.BlockSpec((gr, Q_BLOCK), lambda gi, qb: (gi, qb)),
        out_shape=jax.ShapeDtypeStruct((NSA_WIDTH, s), F32),
        scratch_shapes=[pltpu.VMEM((nsbp // SUPER, 256, GQ), BF16),
                        pltpu.VMEM((1, GQ), F32),
                        pltpu.VMEM((VT_ROWS, GQ), F32),
                        pltpu.VMEM((KV_CHUNK, GQ), F32),
                        pltpu.VMEM((KV_CHUNK, GQ), F32)],
        compiler_params=_cparams(2),
        name="sel_attention",
    )(qhi_t, neg_t, kaug, vt512, vt128, ws_t, ext_q)


def _win_kernel(qhi_ref, kwin_ref, vwt_ref, ww_ref, out_ref):
    qb = pl.program_id(1)
    kb0 = jnp.maximum(qb - WINDOW // Q_BLOCK, 0)
    ws = pl.multiple_of(kb0 * Q_BLOCK, Q_BLOCK)
    off = pl.multiple_of(jnp.maximum(WINDOW - qb * Q_BLOCK, 0), Q_BLOCK)
    qg = _heads_to_lanes(qhi_ref[...])
    qx = jnp.concatenate([qg, jnp.zeros_like(qg)], axis=0)
    s = _dot(kwin_ref[pl.ds(ws, WIN_KEYS), :], qx) + _bias_tile(ww_ref, off, WIN_KEYS)
    m = jnp.max(s, axis=0, keepdims=True)
    p = jnp.exp2(s - m).astype(BF16)
    vt = jnp.concatenate([vwt_ref[kb0 + j] for j in range(WIN_KEYS // Q_BLOCK)], axis=1)
    acc = _dot(vt, p)
    _store_heads(out_ref, acc[0:HEAD_DIM] * (1.0 / acc[HEAD_DIM:HEAD_DIM + 1]))


def _win_attention(qhi_t, kwin, vwt, ww_t):
    s = qhi_t.shape[1]
    g = N_GROUPS
    gr = GROUP_SIZE * HEAD_DIM
    once = pl.Buffered(1)
    return pl.pallas_call(
        _win_kernel,
        grid=(g, s // Q_BLOCK),
        in_specs=[pl.BlockSpec((gr, Q_BLOCK), lambda gi, qb: (gi, qb)),
                  pl.BlockSpec((None, s, 128), lambda gi, qb: (gi, 0, 0), pipeline_mode=once),
                  pl.BlockSpec((None, s // 128, VT_ROWS, 128), lambda gi, qb: (gi, 0, 0, 0),
                               pipeline_mode=once),
                  pl.BlockSpec((GROUP_SIZE,) + ww_t.shape[1:], lambda gi, qb: (gi, 0, 0))],
        out_specs=pl.BlockSpec((gr, Q_BLOCK), lambda gi, qb: (gi, qb)),
        out_shape=jax.ShapeDtypeStruct((NSA_WIDTH, s), F32),
        compiler_params=_cparams(2),
        name="win_attention",
    )(qhi_t, kwin, vwt, ww_t)


LRU_TM = 256


def _lru_kernel(u_ref, cw_ref, cb_ref, wa_ref, ba_ref, wx_ref, bx_ref, lam_ref, h_ref,
                tail_ref, hprev_ref):
    tm = LRU_TM

    @pl.when(pl.program_id(0) == 0)
    def _():
        tail_ref[...] = jnp.zeros(tail_ref.shape, F32)
        hprev_ref[...] = jnp.zeros(hprev_ref.shape, F32)

    u = u_ref[...]
    tail = tail_ref[...]
    row8 = lax.broadcasted_iota(jnp.int32, (8, LRU_WIDTH), 0)
    uc = cb_ref[...] + u * cw_ref[CONV_WIDTH - 1:CONV_WIDTH, :]
    for j in range(1, CONV_WIDTH):
        sh = pltpu.roll(u, j, 0)
        head = jnp.where(row8 < j, pltpu.roll(tail, j, 0), sh[0:8])
        sh = jnp.concatenate([head, sh[8:]], axis=0)
        uc = uc + sh * cw_ref[CONV_WIDTH - 1 - j:CONV_WIDTH - j, :]
    tail_ref[...] = u[tm - 8:tm]

    ucb = uc.astype(BF16)
    r = jax.nn.sigmoid(_dot(ucb, wa_ref[...]) + ba_ref[...])
    ig = jax.nn.sigmoid(_dot(ucb, wx_ref[...]) + bx_ref[...])
    nl = -lam_ref[...]
    softplus = jnp.maximum(nl, 0.0) + jnp.log(1.0 + jnp.exp(-jnp.abs(nl)))
    log_a = (-LRU_C * r) * softplus
    a = jnp.exp(log_a)
    b = jnp.sqrt(1.0 - jnp.exp(2.0 * log_a)) * (ig * uc)

    row = lax.broadcasted_iota(jnp.int32, (tm, LRU_WIDTH), 0)
    step = 1
    while step < tm:
        a_s = pltpu.roll(a, step, 0)
        b_s = pltpu.roll(b, step, 0)
        ok = row >= step
        b = jnp.where(ok, a * b_s + b, b)
        a = jnp.where(ok, a * a_s, a)
        step *= 2
    h = a * hprev_ref[...] + b
    h_ref[...] = h
    hprev_ref[...] = h[tm - 1:tm]


def _rglru(u, conv_w, conv_b, wa_bd, ba, wx_bd, bx, lam):
    s = u.shape[0]
    tm = LRU_TM
    row = lambda v: v.reshape(1, LRU_WIDTH)
    full = lambda shape: pl.BlockSpec(shape, lambda i: (0, 0))
    return pl.pallas_call(
        _lru_kernel,
        grid=(s // tm,),
        in_specs=[pl.BlockSpec((tm, LRU_WIDTH), lambda i: (i, 0)),
                  full((CONV_WIDTH, LRU_WIDTH)), full((1, LRU_WIDTH)),
                  full((LRU_WIDTH, LRU_WIDTH)), full((1, LRU_WIDTH)),
                  full((LRU_WIDTH, LRU_WIDTH)), full((1, LRU_WIDTH)),
                  full((1, LRU_WIDTH))],
        out_specs=pl.BlockSpec((tm, LRU_WIDTH), lambda i: (i, 0)),
        out_shape=jax.ShapeDtypeStruct((s, LRU_WIDTH), F32),
        scratch_shapes=[pltpu.VMEM((8, LRU_WIDTH), F32), pltpu.VMEM((1, LRU_WIDTH), F32)],
        compiler_params=_cparams(1),
        name="rglru",
    )(u, conv_w, row(conv_b), wa_bd, row(ba), wx_bd, row(bx), row(lam))


OUT_TM = 256


def _out_kernel(x_ref, oc_ref, os_ref, ow_ref, brt_ref, gnt_ref, h_ref, gl_ref, mg_ref,
                wpa_ref, wpb_ref, wo_ref, y_ref):
    tm = OUT_TM
    brt = brt_ref[...]
    parts = []
    for h in range(N_HEADS):
        rows = slice(h * HEAD_DIM, (h + 1) * HEAD_DIM)
        acc = None
        for b, o_ref in enumerate((oc_ref, os_ref, ow_ref)):
            term = brt[b * N_HEADS + h:b * N_HEADS + h + 1, :] * o_ref[rows, :]
            acc = term if acc is None else acc + term
        parts.append(acc)
    ya_t = jnp.concatenate(parts, axis=0) * gnt_ref[...]
    y_a = _dot(ya_t.T.astype(BF16), wpa_ref[...])
    gl = gl_ref[...]
    y_b = _dot((h_ref[...] * (gl * jax.nn.sigmoid(gl))).astype(BF16), wpb_ref[...])
    mg = mg_ref[...]
    m = jax.nn.sigmoid(mg[:, :D_MODEL]) * y_a + jax.nn.sigmoid(mg[:, D_MODEL:]) * y_b
    y_ref[...] = x_ref[...] + _dot(m.astype(BF16), wo_ref[...])


def _output(x2, oc_t, os_t, ow_t, br_t, gn_t, h_lru, g_lru, merge_g, wpa, wpb, wo):
    s = x2.shape[0]
    tm = OUT_TM
    rows = lambda n: pl.BlockSpec((tm, n), lambda i: (i, 0))
    cols = lambda n: pl.BlockSpec((n, tm), lambda i: (0, i))
    full = lambda a: pl.BlockSpec(a.shape, lambda i: (0, 0))
    return pl.pallas_call(
        _out_kernel,
        grid=(s // tm,),
        in_specs=[rows(D_MODEL), cols(NSA_WIDTH), cols(NSA_WIDTH), cols(NSA_WIDTH), cols(32),
                  cols(NSA_WIDTH), rows(LRU_WIDTH), rows(LRU_WIDTH), rows(2 * D_MODEL),
                  full(wpa), full(wpb), full(wo)],
        out_specs=rows(D_MODEL),
        out_shape=jax.ShapeDtypeStruct((s, D_MODEL), F32),
        compiler_params=_cparams(1),
        name="output",
    )(x2, oc_t, os_t, ow_t, br_t, gn_t, h_lru, g_lru, merge_g, wpa, wpb, wo)


def _t5_bucket_table(n_dist):
    n = np.arange(n_dist)
    max_exact = N_BUCKETS // 2
    nf = np.maximum(n, 1).astype(np.float32)
    large = max_exact + (np.log(nf / np.float32(max_exact)) / np.float32(math.log(MAX_DISTANCE / max_exact))
                         * np.float32(N_BUCKETS - max_exact)).astype(np.int32)
    return np.where(n < max_exact, n, np.minimum(large, N_BUCKETS - 1))


def _bias_tiles_kernel(relb_ref, bw_ref, bs_ref, bc_ref, ww_ref, ws_ref, wc_ref):
    h = pl.program_id(0)
    far = relb_ref[N_BUCKETS - 1, h]
    for bk_ref, out_ref, rel in ((bw_ref, ww_ref, False), (bs_ref, ws_ref, True), (bc_ref, wc_ref, True)):
        bk = bk_ref[...]
        acc = jnp.full(bk.shape, NEG, F32)
        for b in range(N_BUCKETS):
            val = relb_ref[b, h] - far if rel else relb_ref[b, h]
            acc = jnp.where(bk == b, val * LOG2E, acc)
        out_ref[...] = acc


def _bias_tables(rel_bias):
    bucket = _t5_bucket_table(2 * WINDOW)
    i = np.arange(Q_BLOCK)[None, :]

    def index_tile(dist, valid):
        return jnp.asarray(np.where(valid, bucket[np.clip(dist, 0, bucket.size - 1)], -1).astype(np.int32))

    u = np.arange(WINDOW + WIN_KEYS)[:, None]
    d_win = i - u + WINDOW
    bw = index_tile(d_win, (d_win >= 0) & (d_win < WINDOW))
    u = np.arange(SEL_NEAR + Q_BLOCK)[:, None]
    d_sel = i - u + Q_BLOCK
    bs = index_tile(d_sel, d_sel >= 0)
    u = np.arange(CMP_NEAR + 16)[:, None]
    d_cmp = i - CMP_STRIDE * (u - 16) - (CMP_BLOCK - 1)
    bc = index_tile(d_cmp, d_cmp >= 0)
    full = lambda a: pl.BlockSpec(a.shape, lambda h: (0, 0))
    per_head = lambda a: pl.BlockSpec((None,) + a.shape, lambda h: (h, 0, 0))
    ww, ws, wc = pl.pallas_call(
        _bias_tiles_kernel,
        grid=(N_HEADS,),
        in_specs=[pl.BlockSpec(memory_space=pltpu.SMEM), full(bw), full(bs), full(bc)],
        out_specs=[per_head(bw), per_head(bs), per_head(bc)],
        out_shape=[jax.ShapeDtypeStruct((N_HEADS,) + a.shape, F32) for a in (bw, bs, bc)],
        compiler_params=_cparams(1),
        name="bias_tiles",
    )(rel_bias, bw, bs, bc)
    b_far = rel_bias[N_BUCKETS - 1] * LOG2E
    hi = b_far.astype(BF16)
    lo = (b_far - hi.astype(F32)).astype(BF16)
    ext = jnp.zeros((N_HEADS, HEAD_DIM, Q_BLOCK), BF16)
    ext = ext.at[:, 0, :].set(hi[:, None]).at[:, 1, :].set(lo[:, None])
    ext = ext.reshape(N_GROUPS, GROUP_SIZE, HEAD_DIM, Q_BLOCK).transpose(0, 2, 1, 3)
    return ww, ws, wc, ext.reshape(N_GROUPS, HEAD_DIM, GQ)


def _block_diag(w):
    n, d, e = w.shape
    eye = jnp.eye(n, dtype=w.dtype)
    return (eye[:, None, :, None] * w[:, :, None, :]).reshape(n * d, n * e)


def kernel(x, norm_gain, w_in, q_norm_gain, k_norm_gain, cmp_pe, cmp_w1, cmp_b1, cmp_w2, rel_bias,
           conv_w, conv_b, lru_wa, lru_ba, lru_wx, lru_bx, lru_lambda, w_proj_a, w_proj_b, w_out):
    bsz, s, _ = x.shape
    assert bsz == 1 and s % 1024 == 0 and s >= 1024
    x2 = x.reshape(s, D_MODEL)
    nsb = s // SEL_BLOCK
    nsbp = -(-nsb // SUPER) * SUPER
    ncp = s // CMP_STRIDE

    o = np.cumsum((0, NSA_WIDTH, 6 * KV_WIDTH, NSA_WIDTH, 3 * N_HEADS, LRU_WIDTH, LRU_WIDTH, 2 * D_MODEL))
    pad = jnp.zeros((D_MODEL, BR_PAD - 3 * N_HEADS), w_in.dtype)
    w_pad = jnp.concatenate([w_in[:, :o[4]], pad, w_in[:, o[4]:]], axis=1).astype(BF16)

    q, kv, gn, br, u_lru, g_lru, merge_g = _inproj(x2, norm_gain, w_pad)

    scale = HEAD_DIM ** -0.5 * LOG2E
    qgain_col = jnp.tile(q_norm_gain * scale, N_HEADS).reshape(NSA_WIDTH, 1)
    kgain_rows = jnp.tile(k_norm_gain, (1, N_GROUPS))
    (qhi_t, qlo_t, gn_t, br_t, kaug, vt512, vt128, kwin, vwt, kcr, vcr) = _prep(
        q, kv, gn, br, qgain_col, kgain_rows)

    w2p = jnp.pad(cmp_w2, ((0, 0), (0, 0), (0, 128 - HEAD_DIM)))
    kc_gain = jnp.pad(k_norm_gain[0], (0, 128 - HEAD_DIM)).reshape(1, 128)
    kc_cat, vct = _compress(kcr.reshape(N_GROUPS, ncp, CMP_STRIDE * HEAD_DIM),
                            vcr.reshape(N_GROUPS, ncp, CMP_STRIDE * HEAD_DIM),
                            cmp_w1, cmp_b1.reshape(2, 1, CMP_HIDDEN), w2p,
                            cmp_pe.reshape(2, 1, CMP_BLOCK * HEAD_DIM), kc_gain)

    ww_t, ws_t, wc_t, ext_q = _bias_tables(rel_bias)
    oc_t, neg_t = _cmp_attention(qhi_t, qlo_t, kc_cat, vct, wc_t, ext_q, nsbp)
    os_t = _sel_attention(qhi_t, neg_t, kaug, vt512, vt128, ws_t, ext_q)
    ow_t = _win_attention(qhi_t, kwin, vwt, ww_t)

    h_lru = _rglru(u_lru, conv_w, conv_b, _block_diag(lru_wa).astype(BF16), lru_ba,
                   _block_diag(lru_wx).astype(BF16), lru_bx, lru_lambda)

    y = _output(x2, oc_t, os_t, ow_t, br_t, gn_t, h_lru, g_lru, merge_g,
                w_proj_a.astype(BF16), w_proj_b.astype(BF16), w_out.astype(BF16))
    return y.reshape(bsz, s, D_MODEL)
```

```python
import functools
import math

import numpy as np
import jax
import jax.numpy as jnp
from jax import lax
from jax.experimental import pallas as pl
from jax.experimental.pallas import tpu as pltpu

F32 = jnp.float32
BF16 = jnp.bfloat16

D_MODEL = 1024
N_HEADS = 8
N_GROUPS = 2
GROUP_SIZE = N_HEADS // N_GROUPS
HEAD_DIM = 64
NSA_WIDTH = N_HEADS * HEAD_DIM
KV_WIDTH = N_GROUPS * HEAD_DIM
CMP_STRIDE = 16
CMP_BLOCK = 32
CMP_HIDDEN = 256
SEL_BLOCK = 64
SEL_PER_CMP = SEL_BLOCK // CMP_STRIDE
N_SELECT = 16
WINDOW = 512
Q_BLOCK = 128
LRU_WIDTH = 512
LRU_BLOCKS = 8
CONV_WIDTH = 4
LRU_C = 8.0
N_BUCKETS = 32
MAX_DISTANCE = 128
EPS = 1e-6
NEG = -1e30
M_INIT = -5e29
LOG2E = 1.4426950408889634

GQ = GROUP_SIZE * Q_BLOCK
SUPER = 128
KV_CHUNK = 512
SEL_NEAR = 2 * Q_BLOCK
WIN_KEYS = WINDOW + Q_BLOCK
CMP_NEAR = 24
N_PICK = N_SELECT - 3
VT_ROWS = 80

BR_PAD = 128
COL_SIZES = (NSA_WIDTH, 6 * KV_WIDTH, NSA_WIDTH, BR_PAD, LRU_WIDTH, LRU_WIDTH, 2 * D_MODEL)
COL_OFFS = tuple(int(v) for v in np.cumsum((0,) + COL_SIZES))
D_IN_PAD = COL_OFFS[-1]

VMEM_LIMIT = 56 * 1024 * 1024


def _cparams(n_axes):
    return pltpu.CompilerParams(dimension_semantics=("arbitrary",) * n_axes,
                                vmem_limit_bytes=VMEM_LIMIT)


def _dot(a, b):
    return jnp.dot(a, b, preferred_element_type=F32)


def _inproj_kernel(x_ref, g_ref, w_ref, *out_refs):
    x = x_ref[...]
    ms = jnp.mean(x * x, axis=-1, keepdims=True)
    h = (x * lax.rsqrt(ms + EPS) * g_ref[...]).astype(BF16)
    for ref, a, b in zip(out_refs, COL_OFFS[:-1], COL_OFFS[1:]):
        ref[...] = _dot(h, w_ref[:, a:b])


def _inproj(x2, norm_gain, w_pad, tm=256):
    s = x2.shape[0]
    outs = [jax.ShapeDtypeStruct((s, n), F32) for n in COL_SIZES]
    return pl.pallas_call(
        _inproj_kernel,
        grid=(s // tm,),
        in_specs=[pl.BlockSpec((tm, D_MODEL), lambda i: (i, 0)),
                  pl.BlockSpec((1, D_MODEL), lambda i: (0, 0)),
                  pl.BlockSpec((D_MODEL, D_IN_PAD), lambda i: (0, 0))],
        out_specs=[pl.BlockSpec((tm, n), lambda i: (i, 0)) for n in COL_SIZES],
        out_shape=outs,
        compiler_params=_cparams(1),
        name="inproj",
    )(x2, norm_gain.reshape(1, D_MODEL), w_pad)


PREP_TM = 512


def _group_rms(k, gain_row):
    sq = k * k
    lane = lax.broadcasted_iota(jnp.int32, k.shape, 1)
    lo = lane < HEAD_DIM
    s0 = jnp.sum(jnp.where(lo, sq, 0.0), axis=-1, keepdims=True)
    s1 = jnp.sum(jnp.where(lo, 0.0, sq), axis=-1, keepdims=True)
    inv = jnp.where(lo, lax.rsqrt(s0 / HEAD_DIM + EPS), lax.rsqrt(s1 / HEAD_DIM + EPS))
    return k * inv * gain_row


def _prep_kernel(q_ref, kv_ref, gn_ref, br_ref, qg_ref, kg_ref,
                 qhi_ref, qlo_ref, gnt_ref, brt_ref, kaug_ref, vt512_ref, vt128_ref,
                 kwin_ref, vwt_ref, kcr_ref, vcr_ref):
    i = pl.program_id(0)
    tm = PREP_TM
    qt = q_ref[...].T
    for h in range(N_HEADS):
        blk = qt[h * HEAD_DIM:(h + 1) * HEAD_DIM]
        ms = jnp.mean(blk * blk, axis=0, keepdims=True)
        qn = blk * lax.rsqrt(ms + EPS) * qg_ref[h * HEAD_DIM:(h + 1) * HEAD_DIM, :]
        hi = qn.astype(BF16)
        qhi_ref[h * HEAD_DIM:(h + 1) * HEAD_DIM, :] = hi
        qlo_ref[h * HEAD_DIM:(h + 1) * HEAD_DIM, :] = (qn - hi.astype(F32)).astype(BF16)
    gn = gn_ref[...]
    gnt_ref[...] = (gn * jax.nn.sigmoid(gn)).T
    brt_ref[...] = jax.nn.sigmoid(br_ref[...]).T[:32]

    kv = kv_ref[...]
    piece = lambda j: kv[:, j * KV_WIDTH:(j + 1) * KV_WIDTH]
    lane = lax.broadcasted_iota(jnp.int32, (tm, 128), 1)
    row = lax.broadcasted_iota(jnp.int32, (tm, 128), 0) + i * tm
    lo = lane < HEAD_DIM
    ones_cols = jnp.where((lane == HEAD_DIM) | (lane == HEAD_DIM + 1), 1.0, 0.0)
    onehot = jnp.where(lane == (row // SEL_BLOCK) % SUPER, 1.0, 0.0).astype(BF16)
    kcmp, vcmp = piece(0), piece(1)
    kslc = _group_rms(piece(2), kg_ref[1:2, :])
    kwin = _group_rms(piece(4), kg_ref[2:3, :])
    vslt = piece(3).T
    vwit = piece(5).T
    row_t = lax.broadcasted_iota(jnp.int32, (VT_ROWS - HEAD_DIM, tm), 0)
    ones_rows = jnp.where(row_t == 0, 1.0, 0.0)
    for g in range(N_GROUPS):
        sh = lambda a: a if g == 0 else pltpu.roll(a, HEAD_DIM, 1)
        kaug_ref[g, :, 0:128] = onehot
        kaug_ref[g, :, 128:256] = jnp.where(lo, sh(kslc), ones_cols).astype(BF16)
        kwin_ref[g] = jnp.where(lo, sh(kwin), 0.0).astype(BF16)
        kcr_ref[g] = sh(kcmp)[:, 0:HEAD_DIM]
        vcr_ref[g] = sh(vcmp)[:, 0:HEAD_DIM]
        vs = jnp.concatenate([vslt[g * HEAD_DIM:(g + 1) * HEAD_DIM], ones_rows], axis=0).astype(BF16)
        vw = jnp.concatenate([vwit[g * HEAD_DIM:(g + 1) * HEAD_DIM], ones_rows], axis=0).astype(BF16)
        for j in range(tm // KV_CHUNK):
            vt512_ref[g, j] = vs[:, j * KV_CHUNK:(j + 1) * KV_CHUNK]
        for j in range(tm // 128):
            vt128_ref[g, j] = vs[:, j * 128:(j + 1) * 128]
            vwt_ref[g, j] = vw[:, j * 128:(j + 1) * 128]


def _prep(q, kv, gn, br, qgain_col, kgain_rows):
    s = q.shape[0]
    tm = PREP_TM
    g = N_GROUPS
    outs = [
        jax.ShapeDtypeStruct((NSA_WIDTH, s), BF16),
        jax.ShapeDtypeStruct((NSA_WIDTH, s), BF16),
        jax.ShapeDtypeStruct((NSA_WIDTH, s), F32),
        jax.ShapeDtypeStruct((32, s), F32),
        jax.ShapeDtypeStruct((g, s, 256), BF16),
        jax.ShapeDtypeStruct((g, s // KV_CHUNK, VT_ROWS, KV_CHUNK), BF16),
        jax.ShapeDtypeStruct((g, s // 128, VT_ROWS, 128), BF16),
        jax.ShapeDtypeStruct((g, s, 128), BF16),
        jax.ShapeDtypeStruct((g, s // 128, VT_ROWS, 128), BF16),
        jax.ShapeDtypeStruct((g, s, HEAD_DIM), F32),
        jax.ShapeDtypeStruct((g, s, HEAD_DIM), F32),
    ]
    out_specs = [
        pl.BlockSpec((NSA_WIDTH, tm), lambda i: (0, i)),
        pl.BlockSpec((NSA_WIDTH, tm), lambda i: (0, i)),
        pl.BlockSpec((NSA_WIDTH, tm), lambda i: (0, i)),
        pl.BlockSpec((32, tm), lambda i: (0, i)),
        pl.BlockSpec((g, tm, 256), lambda i: (0, i, 0)),
        pl.BlockSpec((g, tm // KV_CHUNK, VT_ROWS, KV_CHUNK), lambda i: (0, i, 0, 0)),
        pl.BlockSpec((g, tm // 128, VT_ROWS, 128), lambda i: (0, i, 0, 0)),
        pl.BlockSpec((g, tm, 128), lambda i: (0, i, 0)),
        pl.BlockSpec((g, tm // 128, VT_ROWS, 128), lambda i: (0, i, 0, 0)),
        pl.BlockSpec((g, tm, HEAD_DIM), lambda i: (0, i, 0)),
        pl.BlockSpec((g, tm, HEAD_DIM), lambda i: (0, i, 0)),
    ]
    return pl.pallas_call(
        _prep_kernel,
        grid=(s // tm,),
        in_specs=[pl.BlockSpec((tm, NSA_WIDTH), lambda i: (i, 0)),
                  pl.BlockSpec((tm, 6 * KV_WIDTH), lambda i: (i, 0)),
                  pl.BlockSpec((tm, NSA_WIDTH), lambda i: (i, 0)),
                  pl.BlockSpec((tm, BR_PAD), lambda i: (i, 0)),
                  pl.BlockSpec((NSA_WIDTH, 1), lambda i: (0, 0)),
                  pl.BlockSpec((3, 128), lambda i: (0, 0))],
        out_specs=out_specs,
        out_shape=outs,
        compiler_params=_cparams(1),
        name="prep",
    )(q, kv, gn, br, qgain_col, kgain_rows)


def _compress_kernel(xk_ref, xv_ref, w1_ref, b1_ref, w2_ref, pe_ref, kg_ref, kc_ref, vct_ref):
    ncp = xk_ref.shape[0]
    half = CMP_STRIDE * HEAD_DIM
    lane = lax.broadcasted_iota(jnp.int32, (ncp, 128), 1)
    lo = lane < HEAD_DIM

    def phi(x_ref, j):
        x = x_ref[...].astype(BF16)
        w1 = w1_ref[j].astype(BF16)
        a = _dot(x, w1[:half])
        b = _dot(x, w1[half:])
        b_next = pltpu.roll(b, ncp - 1, 0)
        pe8 = jnp.broadcast_to(pe_ref[j], (8, 2 * half)).astype(BF16)
        pe_term = _dot(pe8, w1)[0:1]
        hid = a + b_next + pe_term + b1_ref[j]
        act = (hid * jax.nn.sigmoid(hid)).astype(BF16)
        return _dot(act, w2_ref[j].astype(BF16))

    kc = phi(xk_ref, 0)
    ms = jnp.sum(kc * kc, axis=-1, keepdims=True) / HEAD_DIM
    kc = kc * lax.rsqrt(ms + EPS) * kg_ref[...]
    hi = kc.astype(BF16).astype(F32)
    lo_part = (kc - hi).astype(BF16).astype(F32)
    ones_cols = jnp.where((lane == HEAD_DIM) | (lane == HEAD_DIM + 1), 1.0, 0.0)
    kc_ref[:, 0:128] = (hi + pltpu.roll(lo_part, HEAD_DIM, 1)).astype(BF16)
    kc_ref[:, 128:256] = jnp.where(lo, hi, ones_cols).astype(BF16)
    vct_ref[...] = phi(xv_ref, 1).T.astype(BF16)


def _compress(xk, xv, w1, b1, w2p, pe_flat, kgain_row):
    g, ncp, _ = xk.shape
    return pl.pallas_call(
        _compress_kernel,
        grid=(g,),
        in_specs=[pl.BlockSpec((None, ncp, CMP_STRIDE * HEAD_DIM), lambda i: (i, 0, 0)),
                  pl.BlockSpec((None, ncp, CMP_STRIDE * HEAD_DIM), lambda i: (i, 0, 0)),
                  pl.BlockSpec(w1.shape, lambda i: (0, 0, 0)),
                  pl.BlockSpec(b1.shape, lambda i: (0, 0, 0)),
                  pl.BlockSpec(w2p.shape, lambda i: (0, 0, 0)),
                  pl.BlockSpec(pe_flat.shape, lambda i: (0, 0, 0)),
                  pl.BlockSpec((1, 128), lambda i: (0, 0))],
        out_specs=[pl.BlockSpec((None, ncp, 256), lambda i: (i, 0, 0)),
                   pl.BlockSpec((None, 128, ncp), lambda i: (i, 0, 0))],
        out_shape=[jax.ShapeDtypeStruct((g, ncp, 256), BF16),
                   jax.ShapeDtypeStruct((g, 128, ncp), BF16)],
        compiler_params=_cparams(1),
        name="compress",
    )(xk, xv, w1, b1, w2p, pe_flat, kgain_row)


def _heads_to_lanes(q):
    return jnp.concatenate([q[r * HEAD_DIM:(r + 1) * HEAD_DIM] for r in range(GROUP_SIZE)], axis=1)


def _store_heads(out_ref, o):
    for r in range(GROUP_SIZE):
        out_ref[r * HEAD_DIM:(r + 1) * HEAD_DIM, :] = o[0:HEAD_DIM, r * Q_BLOCK:(r + 1) * Q_BLOCK]


def _bias_tile(w_ref, off, n):
    return jnp.concatenate([w_ref[r, pl.ds(off, n), :] for r in range(GROUP_SIZE)], axis=1)


def _cmp_kernel(qhi_ref, qlo_ref, kc_ref, vct_ref, wc_ref, ext_ref, ocmp_ref, neg_ref, s_ref, imp_ref):
    qb = pl.program_id(1)
    ncp = kc_ref.shape[0]
    nsbp = neg_ref.shape[0]
    qhi = qhi_ref[...]
    qlo = qlo_ref[...]
    cols = []
    for r in range(GROUP_SIZE):
        a = qhi[r * HEAD_DIM:(r + 1) * HEAD_DIM]
        b = qlo[r * HEAD_DIM:(r + 1) * HEAD_DIM]
        cols.append(jnp.concatenate([a, a, b], axis=0))
    qcat = jnp.concatenate([jnp.concatenate(cols, axis=1), ext_ref[...]], axis=0)
    ws = pl.multiple_of(jnp.maximum(8 * qb - 16, 0), 8)
    off = pl.multiple_of(jnp.maximum(16 - 8 * qb, 0), 8)
    lim = ws + CMP_NEAR

    def attend(nrows):
        nblk = nrows // SEL_PER_CMP
        s_ref[0:nrows, :] = _dot(kc_ref[0:nrows, :], qcat)
        s_ref[pl.ds(ws, CMP_NEAR), :] += _bias_tile(wc_ref, off, CMP_NEAR)
        rows = lax.broadcasted_iota(jnp.int32, (nrows, GQ), 0)
        s = jnp.where(rows < lim, s_ref[0:nrows, :], NEG)
        m = jnp.maximum(jnp.max(s, axis=0, keepdims=True), M_INIT)
        p = jnp.exp2(s - m)
        l = jnp.sum(p, axis=0, keepdims=True)
        pn = p * jnp.where(l > 0.0, 1.0 / l, 0.0)
        _store_heads(ocmp_ref, _dot(vct_ref[:, 0:nrows], pn.astype(BF16)))
        imp = pn[:, 0:Q_BLOCK]
        for r in range(1, GROUP_SIZE):
            imp = imp + pn[:, r * Q_BLOCK:(r + 1) * Q_BLOCK]
        imp_ref[0:nrows, :] = imp
        strided = [imp_ref[pl.ds(k, nblk, stride=SEL_PER_CMP), :] for k in range(SEL_PER_CMP)]
        blk = lax.broadcasted_iota(jnp.int32, (nblk, Q_BLOCK), 0)
        prev_last = jnp.where(blk == 0, 0.0, pltpu.roll(strided[SEL_PER_CMP - 1], 1, 0))
        impb = prev_last + strided[0] + strided[1] + strided[2] + strided[3]
        lane = lax.broadcasted_iota(jnp.int32, (nblk, Q_BLOCK), 1)
        cur = 2 * qb + jnp.where(lane >= SEL_BLOCK, 1, 0)
        forced = jnp.where(blk == 0, 1.0, jnp.where(blk == cur, 1.0, jnp.where(blk == cur - 1, 1.0, 0.0)))
        cand = jnp.where(blk < cur - 1, jnp.where(blk > 0, impb, -1.0), -1.0)
        sel = jnp.where(blk <= cur, forced, 0.0)

        def pick_one(_, carry):
            sc, sl = carry
            mx = jnp.max(sc, axis=0, keepdims=True)
            hit_blk = jnp.where(sc == mx, jnp.where(mx >= 0.0, blk, nblk), nblk)
            first = jnp.min(hit_blk, axis=0, keepdims=True)
            pick = blk == first
            return jnp.where(pick, -1.0, sc), jnp.where(pick, 1.0, sl)

        _, sel = lax.fori_loop(0, N_PICK, pick_one, (cand, sel))
        neg_ref[0:nblk, :] = jnp.where(sel > 0.5, 0.0, NEG).astype(BF16)
        if nblk < nsbp:
            neg_ref[nblk:nsbp, :] = jnp.full((nsbp - nblk, Q_BLOCK), NEG, BF16)

    n_cls = min(4, ncp // 128)
    per_cls = ncp // n_cls
    cls = (lim + per_cls - 1) // per_cls
    for k in range(1, n_cls + 1):
        pl.when(cls == k)(functools.partial(attend, k * per_cls))


def _cmp_attention(qhi_t, qlo_t, kc_cat, vct, wc_t, ext_q, nsbp):
    s = qhi_t.shape[1]
    g = N_GROUPS
    nqb = s // Q_BLOCK
    ncp = kc_cat.shape[1]
    gr = GROUP_SIZE * HEAD_DIM
    return pl.pallas_call(
        _cmp_kernel,
        grid=(g, nqb),
        in_specs=[pl.BlockSpec((gr, Q_BLOCK), lambda gi, qb: (gi, qb)),
                  pl.BlockSpec((gr, Q_BLOCK), lambda gi, qb: (gi, qb)),
                  pl.BlockSpec((None, ncp, 256), lambda gi, qb: (gi, 0, 0)),
                  pl.BlockSpec((None, 128, ncp), lambda gi, qb: (gi, 0, 0)),
                  pl.BlockSpec((GROUP_SIZE,) + wc_t.shape[1:], lambda gi, qb: (gi, 0, 0)),
                  pl.BlockSpec((None, HEAD_DIM, GQ), lambda gi, qb: (gi, 0, 0))],
        out_specs=[pl.BlockSpec((gr, Q_BLOCK), lambda gi, qb: (gi, qb)),
                   pl.BlockSpec((None, nsbp, Q_BLOCK), lambda gi, qb: (gi, 0, qb))],
        out_shape=[jax.ShapeDtypeStruct((NSA_WIDTH, s), F32),
                   jax.ShapeDtypeStruct((g, nsbp, s), BF16)],
        scratch_shapes=[pltpu.VMEM((ncp, GQ), F32), pltpu.VMEM((ncp, Q_BLOCK), F32)],
        compiler_params=_cparams(2),
        name="cmp_select",
    )(qhi_t, qlo_t, kc_cat, vct, wc_t, ext_q)


def _sel_kernel(qhi_ref, neg_ref, kaug_ref, vt512_ref, vt128_ref, ws_ref, ext_ref,
                kwin_ref, vwt_ref, ww_ref, out_ref, owin_ref,
                qaug_ref, m_ref, acc_ref, s0_ref, s1_ref):
    qb = pl.program_id(1)
    nsbp = neg_ref.shape[0]
    nsc = nsbp // SUPER
    qg = _heads_to_lanes(qhi_ref[...])
    qx = jnp.concatenate([qg, ext_ref[...]], axis=0)
    blk = lax.broadcasted_iota(jnp.int32, (SUPER, Q_BLOCK), 0)

    def aug(neg_rows):
        tiled = jnp.concatenate([neg_rows] * GROUP_SIZE, axis=1).astype(BF16)
        return jnp.concatenate([tiled, qx], axis=0)

    for sc in range(nsc):
        neg_rows = neg_ref[sc * SUPER:(sc + 1) * SUPER, :].astype(F32)
        qaug_ref[sc] = aug(jnp.where(blk + sc * SUPER >= 2 * qb - 2, NEG, neg_rows))

    m_ref[...] = jnp.full(m_ref.shape, M_INIT, F32)
    acc_ref[...] = jnp.zeros(acc_ref.shape, F32)

    def update(s, vt):
        m_old = m_ref[...]
        m_new = jnp.maximum(m_old, jnp.max(s, axis=0, keepdims=True))
        p = jnp.exp2(s - m_new).astype(BF16)
        acc_ref[...] = acc_ref[...] * jnp.exp2(m_old - m_new) + _dot(vt, p)
        m_ref[...] = m_new

    n_far = (jnp.maximum(qb - 1, 0) * Q_BLOCK + KV_CHUNK - 1) // KV_CHUNK

    def scores(c, dst_ref):
        k = kaug_ref[pl.ds(pl.multiple_of(c * KV_CHUNK, KV_CHUNK), KV_CHUNK), :]
        sc = (c * (KV_CHUNK // SEL_BLOCK)) // SUPER
        dst_ref[...] = _dot(k, qaug_ref[sc])

    kb0 = jnp.maximum(qb - 1, 0)
    ws = pl.multiple_of(kb0 * Q_BLOCK, Q_BLOCK)
    off = pl.multiple_of(jnp.where(qb == 0, Q_BLOCK, 0), Q_BLOCK)
    b_lo = 2 * kb0
    sc_lo = pl.multiple_of((b_lo // SUPER) * SUPER, SUPER)
    sc_hi = pl.multiple_of(((b_lo + 3) // SUPER) * SUPER, SUPER)
    neg_lo = neg_ref[pl.ds(sc_lo, SUPER), :].astype(F32)
    neg_hi = neg_ref[pl.ds(sc_hi, SUPER), :].astype(F32)
    near_neg = jnp.where(blk >= SUPER // 2, neg_lo, neg_hi)
    k = kaug_ref[pl.ds(ws, SEL_NEAR), :]
    s_near = _dot(k, aug(near_neg)) + _bias_tile(ws_ref, off, SEL_NEAR)
    vt_near = jnp.concatenate([vt128_ref[kb0], vt128_ref[kb0 + 1]], axis=1)

    scores(0, s0_ref)
    update(s_near, vt_near)
    _window_branch(qb, qg, kwin_ref, vwt_ref, ww_ref, owin_ref)

    def far_pair(c):
        scores(c + 1, s1_ref)
        update(s0_ref[...], vt512_ref[c])
        scores(jnp.minimum(c + 2, n_far - 1), s0_ref)
        update(s1_ref[...], vt512_ref[c + 1])

    def far_oct(i, carry):
        for j in range(4):
            far_pair(8 * i + 2 * j)
        return carry

    lax.fori_loop(0, n_far // 8, far_oct, 0)

    @pl.when(n_far % 8 >= 4)
    def _():
        far_pair((n_far // 8) * 8)
        far_pair((n_far // 8) * 8 + 2)

    @pl.when(n_far % 4 >= 2)
    def _():
        far_pair((n_far // 4) * 4)

    @pl.when(n_far % 2 == 1)
    def _():
        update(s0_ref[...], vt512_ref[n_far - 1])

    acc = acc_ref[...]
    _store_heads(out_ref, acc[0:HEAD_DIM] * (1.0 / acc[HEAD_DIM:HEAD_DIM + 1]))


def _sel_win_attention(qhi_t, neg_t, kaug, vt512, vt128, ws_t, ext_q, kwin, vwt, ww_t):
    s = qhi_t.shape[1]
    g = N_GROUPS
    nqb = s // Q_BLOCK
    nsbp = neg_t.shape[1]
    gr = GROUP_SIZE * HEAD_DIM
    once = pl.Buffered(1)
    return pl.pallas_call(
        _sel_kernel,
        grid=(g, nqb),
        in_specs=[pl.BlockSpec((gr, Q_BLOCK), lambda gi, qb: (gi, qb)),
                  pl.BlockSpec((None, nsbp, Q_BLOCK), lambda gi, qb: (gi, 0, qb)),
                  pl.BlockSpec((None, s, 256), lambda gi, qb: (gi, 0, 0), pipeline_mode=once),
                  pl.BlockSpec((None, s // KV_CHUNK, VT_ROWS, KV_CHUNK), lambda gi, qb: (gi, 0, 0, 0),
                               pipeline_mode=once),
                  pl.BlockSpec((None, s // 128, VT_ROWS, 128), lambda gi, qb: (gi, 0, 0, 0),
                               pipeline_mode=once),
                  pl.BlockSpec((GROUP_SIZE,) + ws_t.shape[1:], lambda gi, qb: (gi, 0, 0)),
                  pl.BlockSpec((None, HEAD_DIM, GQ), lambda gi, qb: (gi, 0, 0)),
                  pl.BlockSpec((None, s, 128), lambda gi, qb: (gi, 0, 0), pipeline_mode=once),
                  pl.BlockSpec((None, s // 128, VT_ROWS, 128), lambda gi, qb: (gi, 0, 0, 0),
                               pipeline_mode=once),
                  pl.BlockSpec((GROUP_SIZE,) + ww_t.shape[1:], lambda gi, qb: (gi, 0, 0))],
        out_specs=[pl.BlockSpec((gr, Q_BLOCK), lambda gi, qb: (gi, qb)),
                   pl.BlockSpec((gr, Q_BLOCK), lambda gi, qb: (gi, qb))],
        out_shape=[jax.ShapeDtypeStruct((NSA_WIDTH, s), F32),
                   jax.ShapeDtypeStruct((NSA_WIDTH, s), F32)],
        scratch_shapes=[pltpu.VMEM((nsbp // SUPER, 256, GQ), BF16),
                        pltpu.VMEM((1, GQ), F32),
                        pltpu.VMEM((VT_ROWS, GQ), F32),
                        pltpu.VMEM((KV_CHUNK, GQ), F32),
                        pltpu.VMEM((KV_CHUNK, GQ), F32)],
        compiler_params=_cparams(2),
        name="sel_win_attention",
    )(qhi_t, neg_t, kaug, vt512, vt128, ws_t, ext_q, kwin, vwt, ww_t)


def _window_branch(qb, qg, kwin_ref, vwt_ref, ww_ref, out_ref):
    kb0 = jnp.maximum(qb - WINDOW // Q_BLOCK, 0)
    ws = pl.multiple_of(kb0 * Q_BLOCK, Q_BLOCK)
    off = pl.multiple_of(jnp.maximum(WINDOW - qb * Q_BLOCK, 0), Q_BLOCK)
    qx = jnp.concatenate([qg, jnp.zeros_like(qg)], axis=0)
    s = _dot(kwin_ref[pl.ds(ws, WIN_KEYS), :], qx) + _bias_tile(ww_ref, off, WIN_KEYS)
    m = jnp.max(s, axis=0, keepdims=True)
    p = jnp.exp2(s - m).astype(BF16)
    vt = jnp.concatenate([vwt_ref[kb0 + j] for j in range(WIN_KEYS // Q_BLOCK)], axis=1)
    acc = _dot(vt, p)
    _store_heads(out_ref, acc[0:HEAD_DIM] * (1.0 / acc[HEAD_DIM:HEAD_DIM + 1]))


LRU_TM = 256


def _lru_kernel(u_ref, cw_ref, cb_ref, wa_ref, ba_ref, wx_ref, bx_ref, lam_ref, h_ref,
                tail_ref, hprev_ref):
    tm = LRU_TM

    @pl.when(pl.program_id(0) == 0)
    def _():
        tail_ref[...] = jnp.zeros(tail_ref.shape, F32)
        hprev_ref[...] = jnp.zeros(hprev_ref.shape, F32)

    u = u_ref[...]
    tail = tail_ref[...]
    row8 = lax.broadcasted_iota(jnp.int32, (8, LRU_WIDTH), 0)
    uc = cb_ref[...] + u * cw_ref[CONV_WIDTH - 1:CONV_WIDTH, :]
    for j in range(1, CONV_WIDTH):
        sh = pltpu.roll(u, j, 0)
        head = jnp.where(row8 < j, pltpu.roll(tail, j, 0), sh[0:8])
        sh = jnp.concatenate([head, sh[8:]], axis=0)
        uc = uc + sh * cw_ref[CONV_WIDTH - 1 - j:CONV_WIDTH - j, :]
    tail_ref[...] = u[tm - 8:tm]

    ucb = uc.astype(BF16)
    r = jax.nn.sigmoid(_dot(ucb, wa_ref[...]) + ba_ref[...])
    ig = jax.nn.sigmoid(_dot(ucb, wx_ref[...]) + bx_ref[...])
    nl = -lam_ref[...]
    softplus = jnp.maximum(nl, 0.0) + jnp.log(1.0 + jnp.exp(-jnp.abs(nl)))
    log_a = (-LRU_C * r) * softplus
    a = jnp.exp(log_a)
    b = jnp.sqrt(1.0 - jnp.exp(2.0 * log_a)) * (ig * uc)

    row = lax.broadcasted_iota(jnp.int32, (tm, LRU_WIDTH), 0)
    step = 1
    while step < tm:
        a_s = pltpu.roll(a, step, 0)
        b_s = pltpu.roll(b, step, 0)
        ok = row >= step
        b = jnp.where(ok, a * b_s + b, b)
        a = jnp.where(ok, a * a_s, a)
        step *= 2
    h = a * hprev_ref[...] + b
    h_ref[...] = h
    hprev_ref[...] = h[tm - 1:tm]


def _rglru(u, conv_w, conv_b, wa_bd, ba, wx_bd, bx, lam):
    s = u.shape[0]
    tm = LRU_TM
    row = lambda v: v.reshape(1, LRU_WIDTH)
    full = lambda shape: pl.BlockSpec(shape, lambda i: (0, 0))
    return pl.pallas_call(
        _lru_kernel,
        grid=(s // tm,),
        in_specs=[pl.BlockSpec((tm, LRU_WIDTH), lambda i: (i, 0)),
                  full((CONV_WIDTH, LRU_WIDTH)), full((1, LRU_WIDTH)),
                  full((LRU_WIDTH, LRU_WIDTH)), full((1, LRU_WIDTH)),
                  full((LRU_WIDTH, LRU_WIDTH)), full((1, LRU_WIDTH)),
                  full((1, LRU_WIDTH))],
        out_specs=pl.BlockSpec((tm, LRU_WIDTH), lambda i: (i, 0)),
        out_shape=jax.ShapeDtypeStruct((s, LRU_WIDTH), F32),
        scratch_shapes=[pltpu.VMEM((8, LRU_WIDTH), F32), pltpu.VMEM((1, LRU_WIDTH), F32)],
        compiler_params=_cparams(1),
        name="rglru",
    )(u, conv_w, row(conv_b), wa_bd, row(ba), wx_bd, row(bx), row(lam))


OUT_TM = 256


def _out_kernel(x_ref, oc_ref, os_ref, ow_ref, brt_ref, gnt_ref, h_ref, gl_ref, mg_ref,
                wpa_ref, wpb_ref, wo_ref, y_ref):
    tm = OUT_TM
    brt = brt_ref[...]
    parts = []
    for h in range(N_HEADS):
        rows = slice(h * HEAD_DIM, (h + 1) * HEAD_DIM)
        acc = None
        for b, o_ref in enumerate((oc_ref, os_ref, ow_ref)):
            term = brt[b * N_HEADS + h:b * N_HEADS + h + 1, :] * o_ref[rows, :]
            acc = term if acc is None else acc + term
        parts.append(acc)
    ya_t = jnp.concatenate(parts, axis=0) * gnt_ref[...]
    y_a = _dot(ya_t.T.astype(BF16), wpa_ref[...])
    gl = gl_ref[...]
    y_b = _dot((h_ref[...] * (gl * jax.nn.sigmoid(gl))).astype(BF16), wpb_ref[...])
    mg = mg_ref[...]
    m = jax.nn.sigmoid(mg[:, :D_MODEL]) * y_a + jax.nn.sigmoid(mg[:, D_MODEL:]) * y_b
    y_ref[...] = x_ref[...] + _dot(m.astype(BF16), wo_ref[...])


def _output(x2, oc_t, os_t, ow_t, br_t, gn_t, h_lru, g_lru, merge_g, wpa, wpb, wo):
    s = x2.shape[0]
    tm = OUT_TM
    rows = lambda n: pl.BlockSpec((tm, n), lambda i: (i, 0))
    cols = lambda n: pl.BlockSpec((n, tm), lambda i: (0, i))
    full = lambda a: pl.BlockSpec(a.shape, lambda i: (0, 0))
    return pl.pallas_call(
        _out_kernel,
        grid=(s // tm,),
        in_specs=[rows(D_MODEL), cols(NSA_WIDTH), cols(NSA_WIDTH), cols(NSA_WIDTH), cols(32),
                  cols(NSA_WIDTH), rows(LRU_WIDTH), rows(LRU_WIDTH), rows(2 * D_MODEL),
                  full(wpa), full(wpb), full(wo)],
        out_specs=rows(D_MODEL),
        out_shape=jax.ShapeDtypeStruct((s, D_MODEL), F32),
        compiler_params=_cparams(1),
        name="output",
    )(x2, oc_t, os_t, ow_t, br_t, gn_t, h_lru, g_lru, merge_g, wpa, wpb, wo)


def _t5_bucket_table(n_dist):
    n = np.arange(n_dist)
    max_exact = N_BUCKETS // 2
    nf = np.maximum(n, 1).astype(np.float32)
    large = max_exact + (np.log(nf / np.float32(max_exact)) / np.float32(math.log(MAX_DISTANCE / max_exact))
                         * np.float32(N_BUCKETS - max_exact)).astype(np.int32)
    return np.where(n < max_exact, n, np.minimum(large, N_BUCKETS - 1))


def _bias_tiles_kernel(relb_ref, bw_ref, bs_ref, bc_ref, ww_ref, ws_ref, wc_ref):
    h = pl.program_id(0)
    far = relb_ref[N_BUCKETS - 1, h]
    for bk_ref, out_ref, rel in ((bw_ref, ww_ref, False), (bs_ref, ws_ref, True), (bc_ref, wc_ref, True)):
        bk = bk_ref[...]
        acc = jnp.full(bk.shape, NEG, F32)
        for b in range(N_BUCKETS):
            val = relb_ref[b, h] - far if rel else relb_ref[b, h]
            acc = jnp.where(bk == b, val * LOG2E, acc)
        out_ref[...] = acc


def _bias_tables(rel_bias):
    bucket = _t5_bucket_table(2 * WINDOW)
    i = np.arange(Q_BLOCK)[None, :]

    def index_tile(dist, valid):
        return jnp.asarray(np.where(valid, bucket[np.clip(dist, 0, bucket.size - 1)], -1).astype(np.int32))

    u = np.arange(WINDOW + WIN_KEYS)[:, None]
    d_win = i - u + WINDOW
    bw = index_tile(d_win, (d_win >= 0) & (d_win < WINDOW))
    u = np.arange(SEL_NEAR + Q_BLOCK)[:, None]
    d_sel = i - u + Q_BLOCK
    bs = index_tile(d_sel, d_sel >= 0)
    u = np.arange(CMP_NEAR + 16)[:, None]
    d_cmp = i - CMP_STRIDE * (u - 16) - (CMP_BLOCK - 1)
    bc = index_tile(d_cmp, d_cmp >= 0)
    full = lambda a: pl.BlockSpec(a.shape, lambda h: (0, 0))
    per_head = lambda a: pl.BlockSpec((None,) + a.shape, lambda h: (h, 0, 0))
    ww, ws, wc = pl.pallas_call(
        _bias_tiles_kernel,
        grid=(N_HEADS,),
        in_specs=[pl.BlockSpec(memory_space=pltpu.SMEM), full(bw), full(bs), full(bc)],
        out_specs=[per_head(bw), per_head(bs), per_head(bc)],
        out_shape=[jax.ShapeDtypeStruct((N_HEADS,) + a.shape, F32) for a in (bw, bs, bc)],
        compiler_params=_cparams(1),
        name="bias_tiles",
    )(rel_bias, bw, bs, bc)
    b_far = rel_bias[N_BUCKETS - 1] * LOG2E
    hi = b_far.astype(BF16)
    lo = (b_far - hi.astype(F32)).astype(BF16)
    ext = jnp.zeros((N_HEADS, HEAD_DIM, Q_BLOCK), BF16)
    ext = ext.at[:, 0, :].set(hi[:, None]).at[:, 1, :].set(lo[:, None])
    ext = ext.reshape(N_GROUPS, GROUP_SIZE, HEAD_DIM, Q_BLOCK).transpose(0, 2, 1, 3)
    return ww, ws, wc, ext.reshape(N_GROUPS, HEAD_DIM, GQ)


def _block_diag(w):
    n, d, e = w.shape
    eye = jnp.eye(n, dtype=w.dtype)
    return (eye[:, None, :, None] * w[:, :, None, :]).reshape(n * d, n * e)


def kernel(x, norm_gain, w_in, q_norm_gain, k_norm_gain, cmp_pe, cmp_w1, cmp_b1, cmp_w2, rel_bias,
           conv_w, conv_b, lru_wa, lru_ba, lru_wx, lru_bx, lru_lambda, w_proj_a, w_proj_b, w_out):
    bsz, s, _ = x.shape
    assert bsz == 1 and s % 1024 == 0 and s >= 1024
    x2 = x.reshape(s, D_MODEL)
    nsb = s // SEL_BLOCK
    nsbp = -(-nsb // SUPER) * SUPER
    ncp = s // CMP_STRIDE

    o = np.cumsum((0, NSA_WIDTH, 6 * KV_WIDTH, NSA_WIDTH, 3 * N_HEADS, LRU_WIDTH, LRU_WIDTH, 2 * D_MODEL))
    pad = jnp.zeros((D_MODEL, BR_PAD - 3 * N_HEADS), w_in.dtype)
    w_pad = jnp.concatenate([w_in[:, :o[4]], pad, w_in[:, o[4]:]], axis=1).astype(BF16)

    q, kv, gn, br, u_lru, g_lru, merge_g = _inproj(x2, norm_gain, w_pad)

    scale = HEAD_DIM ** -0.5 * LOG2E
    qgain_col = jnp.tile(q_norm_gain * scale, N_HEADS).reshape(NSA_WIDTH, 1)
    kgain_rows = jnp.tile(k_norm_gain, (1, N_GROUPS))
    (qhi_t, qlo_t, gn_t, br_t, kaug, vt512, vt128, kwin, vwt, kcr, vcr) = _prep(
        q, kv, gn, br, qgain_col, kgain_rows)

    w2p = jnp.pad(cmp_w2, ((0, 0), (0, 0), (0, 128 - HEAD_DIM)))
    kc_gain = jnp.pad(k_norm_gain[0], (0, 128 - HEAD_DIM)).reshape(1, 128)
    kc_cat, vct = _compress(kcr.reshape(N_GROUPS, ncp, CMP_STRIDE * HEAD_DIM),
                            vcr.reshape(N_GROUPS, ncp, CMP_STRIDE * HEAD_DIM),
                            cmp_w1, cmp_b1.reshape(2, 1, CMP_HIDDEN), w2p,
                            cmp_pe.reshape(2, 1, CMP_BLOCK * HEAD_DIM), kc_gain)

    ww_t, ws_t, wc_t, ext_q = _bias_tables(rel_bias)
    oc_t, neg_t = _cmp_attention(qhi_t, qlo_t, kc_cat, vct, wc_t, ext_q, nsbp)
    os_t, ow_t = _sel_win_attention(qhi_t, neg_t, kaug, vt512, vt128, ws_t, ext_q, kwin, vwt, ww_t)

    h_lru = _rglru(u_lru, conv_w, conv_b, _block_diag(lru_wa).astype(BF16), lru_ba,
                   _block_diag(lru_wx).astype(BF16), lru_bx, lru_lambda)

    y = _output(x2, oc_t, os_t, ow_t, br_t, gn_t, h_lru, g_lru, merge_g,
                w_proj_a.astype(BF16), w_proj_b.astype(BF16), w_out.astype(BF16))
    return y.reshape(bsz, s, D_MODEL)
```

```python
import functools
import math

import numpy as np
import jax
import jax.numpy as jnp
from jax import lax
from jax.experimental import pallas as pl
from jax.experimental.pallas import tpu as pltpu

F32 = jnp.float32
BF16 = jnp.bfloat16

D_MODEL = 1024
N_HEADS = 8
N_GROUPS = 2
GROUP_SIZE = N_HEADS // N_GROUPS
HEAD_DIM = 64
NSA_WIDTH = N_HEADS * HEAD_DIM
KV_WIDTH = N_GROUPS * HEAD_DIM
CMP_STRIDE = 16
CMP_BLOCK = 32
CMP_HIDDEN = 256
SEL_BLOCK = 64
SEL_PER_CMP = SEL_BLOCK // CMP_STRIDE
N_SELECT = 16
WINDOW = 512
Q_BLOCK = 128
LRU_WIDTH = 512
LRU_BLOCKS = 8
CONV_WIDTH = 4
LRU_C = 8.0
N_BUCKETS = 32
MAX_DISTANCE = 128
EPS = 1e-6
NEG = -1e30
M_INIT = -5e29
LOG2E = 1.4426950408889634

GQ = GROUP_SIZE * Q_BLOCK
SUPER = 128
KV_CHUNK = 512
SEL_NEAR = 2 * Q_BLOCK
WIN_KEYS = WINDOW + Q_BLOCK
CMP_NEAR = 24
N_PICK = N_SELECT - 3
VT_ROWS = 80

BR_PAD = 128
COL_SIZES = (NSA_WIDTH, 6 * KV_WIDTH, NSA_WIDTH, BR_PAD, LRU_WIDTH, LRU_WIDTH, 2 * D_MODEL)
COL_OFFS = tuple(int(v) for v in np.cumsum((0,) + COL_SIZES))
D_IN_PAD = COL_OFFS[-1]

VMEM_LIMIT = 56 * 1024 * 1024


def _cparams(n_axes):
    return pltpu.CompilerParams(dimension_semantics=("arbitrary",) * n_axes,
                                vmem_limit_bytes=VMEM_LIMIT)


def _dot(a, b):
    return jnp.dot(a, b, preferred_element_type=F32)


def _inproj_kernel(x_ref, g_ref, w_ref, *out_refs):
    x = x_ref[...]
    ms = jnp.mean(x * x, axis=-1, keepdims=True)
    h = (x * lax.rsqrt(ms + EPS) * g_ref[...]).astype(BF16)
    for ref, a, b in zip(out_refs, COL_OFFS[:-1], COL_OFFS[1:]):
        ref[...] = _dot(h, w_ref[:, a:b])


def _inproj(x2, norm_gain, w_pad, tm=256):
    s = x2.shape[0]
    outs = [jax.ShapeDtypeStruct((s, n), F32) for n in COL_SIZES]
    return pl.pallas_call(
        _inproj_kernel,
        grid=(s // tm,),
        in_specs=[pl.BlockSpec((tm, D_MODEL), lambda i: (i, 0)),
                  pl.BlockSpec((1, D_MODEL), lambda i: (0, 0)),
                  pl.BlockSpec((D_MODEL, D_IN_PAD), lambda i: (0, 0))],
        out_specs=[pl.BlockSpec((tm, n), lambda i: (i, 0)) for n in COL_SIZES],
        out_shape=outs,
        compiler_params=_cparams(1),
        name="inproj",
    )(x2, norm_gain.reshape(1, D_MODEL), w_pad)


PREP_TM = 512


def _group_rms(k, gain_row):
    sq = k * k
    lane = lax.broadcasted_iota(jnp.int32, k.shape, 1)
    lo = lane < HEAD_DIM
    s0 = jnp.sum(jnp.where(lo, sq, 0.0), axis=-1, keepdims=True)
    s1 = jnp.sum(jnp.where(lo, 0.0, sq), axis=-1, keepdims=True)
    inv = jnp.where(lo, lax.rsqrt(s0 / HEAD_DIM + EPS), lax.rsqrt(s1 / HEAD_DIM + EPS))
    return k * inv * gain_row


def _prep_kernel(q_ref, kv_ref, gn_ref, br_ref, qg_ref, kg_ref,
                 qhi_ref, qlo_ref, gnt_ref, brt_ref, kaug_ref, vt512_ref, vt128_ref,
                 kwin_ref, vwt_ref, kcr_ref, vcr_ref):
    i = pl.program_id(0)
    tm = PREP_TM
    qt = q_ref[...].T
    for h in range(N_HEADS):
        blk = qt[h * HEAD_DIM:(h + 1) * HEAD_DIM]
        ms = jnp.mean(blk * blk, axis=0, keepdims=True)
        qn = blk * lax.rsqrt(ms + EPS) * qg_ref[h * HEAD_DIM:(h + 1) * HEAD_DIM, :]
        hi = qn.astype(BF16)
        qhi_ref[h * HEAD_DIM:(h + 1) * HEAD_DIM, :] = hi
        qlo_ref[h * HEAD_DIM:(h + 1) * HEAD_DIM, :] = (qn - hi.astype(F32)).astype(BF16)
    gn = gn_ref[...]
    gnt_ref[...] = (gn * jax.nn.sigmoid(gn)).T
    brt_ref[...] = jax.nn.sigmoid(br_ref[...]).T[:32]

    kv = kv_ref[...]
    piece = lambda j: kv[:, j * KV_WIDTH:(j + 1) * KV_WIDTH]
    lane = lax.broadcasted_iota(jnp.int32, (tm, 128), 1)
    row = lax.broadcasted_iota(jnp.int32, (tm, 128), 0) + i * tm
    lo = lane < HEAD_DIM
    ones_cols = jnp.where((lane == HEAD_DIM) | (lane == HEAD_DIM + 1), 1.0, 0.0)
    onehot = jnp.where(lane == (row // SEL_BLOCK) % SUPER, 1.0, 0.0).astype(BF16)
    kcmp, vcmp = piece(0), piece(1)
    kslc = _group_rms(piece(2), kg_ref[1:2, :])
    kwin = _group_rms(piece(4), kg_ref[2:3, :])
    vslt = piece(3).T
    vwit = piece(5).T
    row_t = lax.broadcasted_iota(jnp.int32, (VT_ROWS - HEAD_DIM, tm), 0)
    ones_rows = jnp.where(row_t == 0, 1.0, 0.0)
    for g in range(N_GROUPS):
        sh = lambda a: a if g == 0 else pltpu.roll(a, HEAD_DIM, 1)
        kaug_ref[g, :, 0:128] = onehot
        kaug_ref[g, :, 128:256] = jnp.where(lo, sh(kslc), ones_cols).astype(BF16)
        kwin_ref[g] = jnp.where(lo, sh(kwin), 0.0).astype(BF16)
        kcr_ref[g] = sh(kcmp)[:, 0:HEAD_DIM]
        vcr_ref[g] = sh(vcmp)[:, 0:HEAD_DIM]
        vs = jnp.concatenate([vslt[g * HEAD_DIM:(g + 1) * HEAD_DIM], ones_rows], axis=0).astype(BF16)
        vw = jnp.concatenate([vwit[g * HEAD_DIM:(g + 1) * HEAD_DIM], ones_rows], axis=0).astype(BF16)
        for j in range(tm // KV_CHUNK):
            vt512_ref[g, j] = vs[:, j * KV_CHUNK:(j + 1) * KV_CHUNK]
        for j in range(tm // 128):
            vt128_ref[g, j] = vs[:, j * 128:(j + 1) * 128]
            vwt_ref[g, j] = vw[:, j * 128:(j + 1) * 128]


def _prep(q, kv, gn, br, qgain_col, kgain_rows):
    s = q.shape[0]
    tm = PREP_TM
    g = N_GROUPS
    outs = [
        jax.ShapeDtypeStruct((NSA_WIDTH, s), BF16),
        jax.ShapeDtypeStruct((NSA_WIDTH, s), BF16),
        jax.ShapeDtypeStruct((NSA_WIDTH, s), F32),
        jax.ShapeDtypeStruct((32, s), F32),
        jax.ShapeDtypeStruct((g, s, 256), BF16),
        jax.ShapeDtypeStruct((g, s // KV_CHUNK, VT_ROWS, KV_CHUNK), BF16),
        jax.ShapeDtypeStruct((g, s // 128, VT_ROWS, 128), BF16),
        jax.ShapeDtypeStruct((g, s, 128), BF16),
        jax.ShapeDtypeStruct((g, s // 128, VT_ROWS, 128), BF16),
        jax.ShapeDtypeStruct((g, s, HEAD_DIM), F32),
        jax.ShapeDtypeStruct((g, s, HEAD_DIM), F32),
    ]
    out_specs = [
        pl.BlockSpec((NSA_WIDTH, tm), lambda i: (0, i)),
        pl.BlockSpec((NSA_WIDTH, tm), lambda i: (0, i)),
        pl.BlockSpec((NSA_WIDTH, tm), lambda i: (0, i)),
        pl.BlockSpec((32, tm), lambda i: (0, i)),
        pl.BlockSpec((g, tm, 256), lambda i: (0, i, 0)),
        pl.BlockSpec((g, tm // KV_CHUNK, VT_ROWS, KV_CHUNK), lambda i: (0, i, 0, 0)),
        pl.BlockSpec((g, tm // 128, VT_ROWS, 128), lambda i: (0, i, 0, 0)),
        pl.BlockSpec((g, tm, 128), lambda i: (0, i, 0)),
        pl.BlockSpec((g, tm // 128, VT_ROWS, 128), lambda i: (0, i, 0, 0)),
        pl.BlockSpec((g, tm, HEAD_DIM), lambda i: (0, i, 0)),
        pl.BlockSpec((g, tm, HEAD_DIM), lambda i: (0, i, 0)),
    ]
    return pl.pallas_call(
        _prep_kernel,
        grid=(s // tm,),
        in_specs=[pl.BlockSpec((tm, NSA_WIDTH), lambda i: (i, 0)),
                  pl.BlockSpec((tm, 6 * KV_WIDTH), lambda i: (i, 0)),
                  pl.BlockSpec((tm, NSA_WIDTH), lambda i: (i, 0)),
                  pl.BlockSpec((tm, BR_PAD), lambda i: (i, 0)),
                  pl.BlockSpec((NSA_WIDTH, 1), lambda i: (0, 0)),
                  pl.BlockSpec((3, 128), lambda i: (0, 0))],
        out_specs=out_specs,
        out_shape=outs,
        compiler_params=_cparams(1),
        name="prep",
    )(q, kv, gn, br, qgain_col, kgain_rows)


def _compress_kernel(xk_ref, xv_ref, w1_ref, b1_ref, w2_ref, pe_ref, kg_ref, kc_ref, vct_ref):
    ncp = xk_ref.shape[0]
    half = CMP_STRIDE * HEAD_DIM
    lane = lax.broadcasted_iota(jnp.int32, (ncp, 128), 1)
    lo = lane < HEAD_DIM

    def phi(x_ref, j):
        x = x_ref[...].astype(BF16)
        w1 = w1_ref[j].astype(BF16)
        a = _dot(x, w1[:half])
        b = _dot(x, w1[half:])
        b_next = pltpu.roll(b, ncp - 1, 0)
        pe8 = jnp.broadcast_to(pe_ref[j], (8, 2 * half)).astype(BF16)
        pe_term = _dot(pe8, w1)[0:1]
        hid = a + b_next + pe_term + b1_ref[j]
        act = (hid * jax.nn.sigmoid(hid)).astype(BF16)
        return _dot(act, w2_ref[j].astype(BF16))

    kc = phi(xk_ref, 0)
    ms = jnp.sum(kc * kc, axis=-1, keepdims=True) / HEAD_DIM
    kc = kc * lax.rsqrt(ms + EPS) * kg_ref[...]
    hi = kc.astype(BF16).astype(F32)
    lo_part = (kc - hi).astype(BF16).astype(F32)
    ones_cols = jnp.where((lane == HEAD_DIM) | (lane == HEAD_DIM + 1), 1.0, 0.0)
    kc_ref[:, 0:128] = (hi + pltpu.roll(lo_part, HEAD_DIM, 1)).astype(BF16)
    kc_ref[:, 128:256] = jnp.where(lo, hi, ones_cols).astype(BF16)
    vct_ref[...] = phi(xv_ref, 1).T.astype(BF16)


def _compress(xk, xv, w1, b1, w2p, pe_flat, kgain_row):
    g, ncp, _ = xk.shape
    return pl.pallas_call(
        _compress_kernel,
        grid=(g,),
        in_specs=[pl.BlockSpec((None, ncp, CMP_STRIDE * HEAD_DIM), lambda i: (i, 0, 0)),
                  pl.BlockSpec((None, ncp, CMP_STRIDE * HEAD_DIM), lambda i: (i, 0, 0)),
                  pl.BlockSpec(w1.shape, lambda i: (0, 0, 0)),
                  pl.BlockSpec(b1.shape, lambda i: (0, 0, 0)),
                  pl.BlockSpec(w2p.shape, lambda i: (0, 0, 0)),
                  pl.BlockSpec(pe_flat.shape, lambda i: (0, 0, 0)),
                  pl.BlockSpec((1, 128), lambda i: (0, 0))],
        out_specs=[pl.BlockSpec((None, ncp, 256), lambda i: (i, 0, 0)),
                   pl.BlockSpec((None, 128, ncp), lambda i: (i, 0, 0))],
        out_shape=[jax.ShapeDtypeStruct((g, ncp, 256), BF16),
                   jax.ShapeDtypeStruct((g, 128, ncp), BF16)],
        compiler_params=_cparams(1),
        name="compress",
    )(xk, xv, w1, b1, w2p, pe_flat, kgain_row)


def _heads_to_lanes(q):
    return jnp.concatenate([q[r * HEAD_DIM:(r + 1) * HEAD_DIM] for r in range(GROUP_SIZE)], axis=1)


def _store_heads(out_ref, o):
    for r in range(GROUP_SIZE):
        out_ref[r * HEAD_DIM:(r + 1) * HEAD_DIM, :] = o[0:HEAD_DIM, r * Q_BLOCK:(r + 1) * Q_BLOCK]


def _bias_tile(w_ref, off, n):
    return jnp.concatenate([w_ref[r, pl.ds(off, n), :] for r in range(GROUP_SIZE)], axis=1)


def _cmp_kernel(qhi_ref, qlo_ref, kc_ref, vct_ref, wc_ref, ext_ref, ocmp_ref, neg_ref, s_ref, imp_ref):
    qb = pl.program_id(1)
    ncp = kc_ref.shape[0]
    nsbp = neg_ref.shape[0]
    qhi = qhi_ref[...]
    qlo = qlo_ref[...]
    cols = []
    for r in range(GROUP_SIZE):
        a = qhi[r * HEAD_DIM:(r + 1) * HEAD_DIM]
        b = qlo[r * HEAD_DIM:(r + 1) * HEAD_DIM]
        cols.append(jnp.concatenate([a, a, b], axis=0))
    qcat = jnp.concatenate([jnp.concatenate(cols, axis=1), ext_ref[...]], axis=0)
    ws = pl.multiple_of(jnp.maximum(8 * qb - 16, 0), 8)
    off = pl.multiple_of(jnp.maximum(16 - 8 * qb, 0), 8)
    lim = ws + CMP_NEAR

    def attend(nrows):
        nblk = nrows // SEL_PER_CMP
        s_ref[0:nrows, :] = _dot(kc_ref[0:nrows, :], qcat)
        s_ref[pl.ds(ws, CMP_NEAR), :] += _bias_tile(wc_ref, off, CMP_NEAR)
        rows = lax.broadcasted_iota(jnp.int32, (nrows, GQ), 0)
        s = jnp.where(rows < lim, s_ref[0:nrows, :], NEG)
        m = jnp.maximum(jnp.max(s, axis=0, keepdims=True), M_INIT)
        p = jnp.exp2(s - m)
        l = jnp.sum(p, axis=0, keepdims=True)
        pn = p * jnp.where(l > 0.0, 1.0 / l, 0.0)
        _store_heads(ocmp_ref, _dot(vct_ref[:, 0:nrows], pn.astype(BF16)))
        imp = pn[:, 0:Q_BLOCK]
        for r in range(1, GROUP_SIZE):
            imp = imp + pn[:, r * Q_BLOCK:(r + 1) * Q_BLOCK]
        imp_ref[0:nrows, :] = imp
        strided = [imp_ref[pl.ds(k, nblk, stride=SEL_PER_CMP), :] for k in range(SEL_PER_CMP)]
        blk = lax.broadcasted_iota(jnp.int32, (nblk, Q_BLOCK), 0)
        prev_last = jnp.where(blk == 0, 0.0, pltpu.roll(strided[SEL_PER_CMP - 1], 1, 0))
        impb = prev_last + strided[0] + strided[1] + strided[2] + strided[3]
        lane = lax.broadcasted_iota(jnp.int32, (nblk, Q_BLOCK), 1)
        cur = 2 * qb + jnp.where(lane >= SEL_BLOCK, 1, 0)
        forced = jnp.where(blk == 0, 1.0, jnp.where(blk == cur, 1.0, jnp.where(blk == cur - 1, 1.0, 0.0)))
        cand = jnp.where(blk < cur - 1, jnp.where(blk > 0, impb, -1.0), -1.0)
        sel = jnp.where(blk <= cur, forced, 0.0)

        def pick_one(_, carry):
            sc, sl = carry
            mx = jnp.max(sc, axis=0, keepdims=True)
            hit_blk = jnp.where(sc == mx, jnp.where(mx >= 0.0, blk, nblk), nblk)
            first = jnp.min(hit_blk, axis=0, keepdims=True)
            pick = blk == first
            return jnp.where(pick, -1.0, sc), jnp.where(pick, 1.0, sl)

        _, sel = lax.fori_loop(0, N_PICK, pick_one, (cand, sel))
        neg_ref[0:nblk, :] = jnp.where(sel > 0.5, 0.0, NEG).astype(BF16)
        if nblk < nsbp:
            neg_ref[nblk:nsbp, :] = jnp.full((nsbp - nblk, Q_BLOCK), NEG, BF16)

    n_cls = min(4, ncp // 128)
    per_cls = ncp // n_cls
    cls = (lim + per_cls - 1) // per_cls
    for k in range(1, n_cls + 1):
        pl.when(cls == k)(functools.partial(attend, k * per_cls))


def _cmp_attention(qhi_t, qlo_t, kc_cat, vct, wc_t, ext_q, nsbp):
    s = qhi_t.shape[1]
    g = N_GROUPS
    nqb = s // Q_BLOCK
    ncp = kc_cat.shape[1]
    gr = GROUP_SIZE * HEAD_DIM
    return pl.pallas_call(
        _cmp_kernel,
        grid=(g, nqb),
        in_specs=[pl.BlockSpec((gr, Q_BLOCK), lambda gi, qb: (gi, qb)),
                  pl.BlockSpec((gr, Q_BLOCK), lambda gi, qb: (gi, qb)),
                  pl.BlockSpec((None, ncp, 256), lambda gi, qb: (gi, 0, 0)),
                  pl.BlockSpec((None, 128, ncp), lambda gi, qb: (gi, 0, 0)),
                  pl.BlockSpec((GROUP_SIZE,) + wc_t.shape[1:], lambda gi, qb: (gi, 0, 0)),
                  pl.BlockSpec((None, HEAD_DIM, GQ), lambda gi, qb: (gi, 0, 0))],
        out_specs=[pl.BlockSpec((gr, Q_BLOCK), lambda gi, qb: (gi, qb)),
                   pl.BlockSpec((None, nsbp, Q_BLOCK), lambda gi, qb: (gi, 0, qb))],
        out_shape=[jax.ShapeDtypeStruct((NSA_WIDTH, s), F32),
                   jax.ShapeDtypeStruct((g, nsbp, s), BF16)],
        scratch_shapes=[pltpu.VMEM((ncp, GQ), F32), pltpu.VMEM((ncp, Q_BLOCK), F32)],
        compiler_params=_cparams(2),
        name="cmp_select",
    )(qhi_t, qlo_t, kc_cat, vct, wc_t, ext_q)


def _sel_kernel(qhi_ref, neg_ref, kaug_ref, vt512_ref, vt128_ref, ws_ref, ext_ref,
                kwin_ref, vwt_ref, ww_ref, out_ref, owin_ref,
                qaug_ref, m_ref, acc_ref, s0_ref, s1_ref, s2_ref, cm0_ref, cm1_ref, cm2_ref):
    qb = pl.program_id(1)
    nsbp = neg_ref.shape[0]
    nsc = nsbp // SUPER
    qg = _heads_to_lanes(qhi_ref[...])
    qx = jnp.concatenate([qg, ext_ref[...]], axis=0)
    blk = lax.broadcasted_iota(jnp.int32, (SUPER, Q_BLOCK), 0)

    def aug(neg_rows):
        tiled = jnp.concatenate([neg_rows] * GROUP_SIZE, axis=1).astype(BF16)
        return jnp.concatenate([tiled, qx], axis=0)

    for sc in range(nsc):
        neg_rows = neg_ref[sc * SUPER:(sc + 1) * SUPER, :].astype(F32)
        qaug_ref[sc] = aug(jnp.where(blk + sc * SUPER >= 2 * qb - 2, NEG, neg_rows))

    m_ref[...] = jnp.full(m_ref.shape, M_INIT, F32)
    acc_ref[...] = jnp.zeros(acc_ref.shape, F32)

    def update(s, col_max, vt):
        m_old = m_ref[...]
        m_new = jnp.maximum(m_old, col_max)
        p = jnp.exp2(s - m_new).astype(BF16)
        acc_ref[...] = acc_ref[...] * jnp.exp2(m_old - m_new) + _dot(vt, p)
        m_ref[...] = m_new

    n_far = (jnp.maximum(qb - 1, 0) * Q_BLOCK + KV_CHUNK - 1) // KV_CHUNK

    def scores(c, dst_ref, max_ref):
        k = kaug_ref[pl.ds(pl.multiple_of(c * KV_CHUNK, KV_CHUNK), KV_CHUNK), :]
        sc = (c * (KV_CHUNK // SEL_BLOCK)) // SUPER
        s = _dot(k, qaug_ref[sc])
        dst_ref[...] = s
        max_ref[...] = jnp.max(s, axis=0, keepdims=True)

    kb0 = jnp.maximum(qb - 1, 0)
    ws = pl.multiple_of(kb0 * Q_BLOCK, Q_BLOCK)
    off = pl.multiple_of(jnp.where(qb == 0, Q_BLOCK, 0), Q_BLOCK)
    b_lo = 2 * kb0
    sc_lo = pl.multiple_of((b_lo // SUPER) * SUPER, SUPER)
    sc_hi = pl.multiple_of(((b_lo + 3) // SUPER) * SUPER, SUPER)
    neg_lo = neg_ref[pl.ds(sc_lo, SUPER), :].astype(F32)
    neg_hi = neg_ref[pl.ds(sc_hi, SUPER), :].astype(F32)
    near_neg = jnp.where(blk >= SUPER // 2, neg_lo, neg_hi)
    k = kaug_ref[pl.ds(ws, SEL_NEAR), :]
    s_near = _dot(k, aug(near_neg)) + _bias_tile(ws_ref, off, SEL_NEAR)
    vt_near = jnp.concatenate([vt128_ref[kb0], vt128_ref[kb0 + 1]], axis=1)

    bufs = ((s0_ref, cm0_ref), (s1_ref, cm1_ref), (s2_ref, cm2_ref))
    last = jnp.maximum(n_far - 1, 0)
    scores(0, *bufs[0])
    scores(jnp.minimum(1, last), *bufs[1])
    update(s_near, jnp.max(s_near, axis=0, keepdims=True), vt_near)
    _window_branch(qb, qg, kwin_ref, vwt_ref, ww_ref, owin_ref)

    def far_triple(c):
        for j in range(3):
            scores(jnp.minimum(c + j + 2, last), *bufs[(j + 2) % 3])
            s_ref, cm_ref = bufs[j]
            update(s_ref[...], cm_ref[...], vt512_ref[c + j])

    def far_six(i, carry):
        far_triple(6 * i)
        far_triple(6 * i + 3)
        return carry

    lax.fori_loop(0, n_far // 6, far_six, 0)
    done = (n_far // 6) * 6

    @pl.when(n_far - done >= 3)
    def _():
        far_triple(done)

    done = (n_far // 3) * 3
    for j in range(2):
        @pl.when(n_far - done > j)
        def _():
            s_ref, cm_ref = bufs[j]
            update(s_ref[...], cm_ref[...], vt512_ref[done + j])

    acc = acc_ref[...]
    _store_heads(out_ref, acc[0:HEAD_DIM] * (1.0 / acc[HEAD_DIM:HEAD_DIM + 1]))


def _sel_win_attention(qhi_t, neg_t, kaug, vt512, vt128, ws_t, ext_q, kwin, vwt, ww_t):
    s = qhi_t.shape[1]
    g = N_GROUPS
    nqb = s // Q_BLOCK
    nsbp = neg_t.shape[1]
    gr = GROUP_SIZE * HEAD_DIM
    once = pl.Buffered(1)
    return pl.pallas_call(
        _sel_kernel,
        grid=(g, nqb),
        in_specs=[pl.BlockSpec((gr, Q_BLOCK), lambda gi, qb: (gi, qb)),
                  pl.BlockSpec((None, nsbp, Q_BLOCK), lambda gi, qb: (gi, 0, qb)),
                  pl.BlockSpec((None, s, 256), lambda gi, qb: (gi, 0, 0), pipeline_mode=once),
                  pl.BlockSpec((None, s // KV_CHUNK, VT_ROWS, KV_CHUNK), lambda gi, qb: (gi, 0, 0, 0),
                               pipeline_mode=once),
                  pl.BlockSpec((None, s // 128, VT_ROWS, 128), lambda gi, qb: (gi, 0, 0, 0),
                               pipeline_mode=once),
                  pl.BlockSpec((GROUP_SIZE,) + ws_t.shape[1:], lambda gi, qb: (gi, 0, 0)),
                  pl.BlockSpec((None, HEAD_DIM, GQ), lambda gi, qb: (gi, 0, 0)),
                  pl.BlockSpec((None, s, 128), lambda gi, qb: (gi, 0, 0), pipeline_mode=once),
                  pl.BlockSpec((None, s // 128, VT_ROWS, 128), lambda gi, qb: (gi, 0, 0, 0),
                               pipeline_mode=once),
                  pl.BlockSpec((GROUP_SIZE,) + ww_t.shape[1:], lambda gi, qb: (gi, 0, 0))],
        out_specs=[pl.BlockSpec((gr, Q_BLOCK), lambda gi, qb: (gi, qb)),
                   pl.BlockSpec((gr, Q_BLOCK), lambda gi, qb: (gi, qb))],
        out_shape=[jax.ShapeDtypeStruct((NSA_WIDTH, s), F32),
                   jax.ShapeDtypeStruct((NSA_WIDTH, s), F32)],
        scratch_shapes=[pltpu.VMEM((nsbp // SUPER, 256, GQ), BF16),
                        pltpu.VMEM((1, GQ), F32),
                        pltpu.VMEM((VT_ROWS, GQ), F32),
                        pltpu.VMEM((KV_CHUNK, GQ), F32),
                        pltpu.VMEM((KV_CHUNK, GQ), F32),
                        pltpu.VMEM((KV_CHUNK, GQ), F32),
                        pltpu.VMEM((1, GQ), F32),
                        pltpu.VMEM((1, GQ), F32),
                        pltpu.VMEM((1, GQ), F32)],
        compiler_params=_cparams(2),
        name="sel_win_attention",
    )(qhi_t, neg_t, kaug, vt512, vt128, ws_t, ext_q, kwin, vwt, ww_t)


def _window_branch(qb, qg, kwin_ref, vwt_ref, ww_ref, out_ref):
    kb0 = jnp.maximum(qb - WINDOW // Q_BLOCK, 0)
    ws = pl.multiple_of(kb0 * Q_BLOCK, Q_BLOCK)
    off = pl.multiple_of(jnp.maximum(WINDOW - qb * Q_BLOCK, 0), Q_BLOCK)
    qx = jnp.concatenate([qg, jnp.zeros_like(qg)], axis=0)
    s = _dot(kwin_ref[pl.ds(ws, WIN_KEYS), :], qx) + _bias_tile(ww_ref, off, WIN_KEYS)
    m = jnp.max(s, axis=0, keepdims=True)
    p = jnp.exp2(s - m).astype(BF16)
    vt = jnp.concatenate([vwt_ref[kb0 + j] for j in range(WIN_KEYS // Q_BLOCK)], axis=1)
    acc = _dot(vt, p)
    _store_heads(out_ref, acc[0:HEAD_DIM] * (1.0 / acc[HEAD_DIM:HEAD_DIM + 1]))


LRU_TM = 256


def _lru_kernel(u_ref, cw_ref, cb_ref, wa_ref, ba_ref, wx_ref, bx_ref, lam_ref, h_ref,
                tail_ref, hprev_ref):
    tm = LRU_TM

    @pl.when(pl.program_id(0) == 0)
    def _():
        tail_ref[...] = jnp.zeros(tail_ref.shape, F32)
        hprev_ref[...] = jnp.zeros(hprev_ref.shape, F32)

    u = u_ref[...]
    tail = tail_ref[...]
    row8 = lax.broadcasted_iota(jnp.int32, (8, LRU_WIDTH), 0)
    uc = cb_ref[...] + u * cw_ref[CONV_WIDTH - 1:CONV_WIDTH, :]
    for j in range(1, CONV_WIDTH):
        sh = pltpu.roll(u, j, 0)
        head = jnp.where(row8 < j, pltpu.roll(tail, j, 0), sh[0:8])
        sh = jnp.concatenate([head, sh[8:]], axis=0)
        uc = uc + sh * cw_ref[CONV_WIDTH - 1 - j:CONV_WIDTH - j, :]
    tail_ref[...] = u[tm - 8:tm]

    ucb = uc.astype(BF16)
    r = jax.nn.sigmoid(_dot(ucb, wa_ref[...]) + ba_ref[...])
    ig = jax.nn.sigmoid(_dot(ucb, wx_ref[...]) + bx_ref[...])
    nl = -lam_ref[...]
    softplus = jnp.maximum(nl, 0.0) + jnp.log(1.0 + jnp.exp(-jnp.abs(nl)))
    log_a = (-LRU_C * r) * softplus
    a = jnp.exp(log_a)
    b = jnp.sqrt(1.0 - jnp.exp(2.0 * log_a)) * (ig * uc)

    row = lax.broadcasted_iota(jnp.int32, (tm, LRU_WIDTH), 0)
    step = 1
    while step < tm:
        a_s = pltpu.roll(a, step, 0)
        b_s = pltpu.roll(b, step, 0)
        ok = row >= step
        b = jnp.where(ok, a * b_s + b, b)
        a = jnp.where(ok, a * a_s, a)
        step *= 2
    h = a * hprev_ref[...] + b
    h_ref[...] = h
    hprev_ref[...] = h[tm - 1:tm]


def _rglru(u, conv_w, conv_b, wa_bd, ba, wx_bd, bx, lam):
    s = u.shape[0]
    tm = LRU_TM
    row = lambda v: v.reshape(1, LRU_WIDTH)
    full = lambda shape: pl.BlockSpec(shape, lambda i: (0, 0))
    return pl.pallas_call(
        _lru_kernel,
        grid=(s // tm,),
        in_specs=[pl.BlockSpec((tm, LRU_WIDTH), lambda i: (i, 0)),
                  full((CONV_WIDTH, LRU_WIDTH)), full((1, LRU_WIDTH)),
                  full((LRU_WIDTH, LRU_WIDTH)), full((1, LRU_WIDTH)),
                  full((LRU_WIDTH, LRU_WIDTH)), full((1, LRU_WIDTH)),
                  full((1, LRU_WIDTH))],
        out_specs=pl.BlockSpec((tm, LRU_WIDTH), lambda i: (i, 0)),
        out_shape=jax.ShapeDtypeStruct((s, LRU_WIDTH), F32),
        scratch_shapes=[pltpu.VMEM((8, LRU_WIDTH), F32), pltpu.VMEM((1, LRU_WIDTH), F32)],
        compiler_params=_cparams(1),
        name="rglru",
    )(u, conv_w, row(conv_b), wa_bd, row(ba), wx_bd, row(bx), row(lam))


OUT_TM = 256


def _out_kernel(x_ref, oc_ref, os_ref, ow_ref, brt_ref, gnt_ref, h_ref, gl_ref, mg_ref,
                wpa_ref, wpb_ref, wo_ref, y_ref):
    tm = OUT_TM
    brt = brt_ref[...]
    parts = []
    for h in range(N_HEADS):
        rows = slice(h * HEAD_DIM, (h + 1) * HEAD_DIM)
        acc = None
        for b, o_ref in enumerate((oc_ref, os_ref, ow_ref)):
            term = brt[b * N_HEADS + h:b * N_HEADS + h + 1, :] * o_ref[rows, :]
            acc = term if acc is None else acc + term
        parts.append(acc)
    ya_t = jnp.concatenate(parts, axis=0) * gnt_ref[...]
    y_a = _dot(ya_t.T.astype(BF16), wpa_ref[...])
    gl = gl_ref[...]
    y_b = _dot((h_ref[...] * (gl * jax.nn.sigmoid(gl))).astype(BF16), wpb_ref[...])
    mg = mg_ref[...]
    m = jax.nn.sigmoid(mg[:, :D_MODEL]) * y_a + jax.nn.sigmoid(mg[:, D_MODEL:]) * y_b
    y_ref[...] = x_ref[...] + _dot(m.astype(BF16), wo_ref[...])


def _output(x2, oc_t, os_t, ow_t, br_t, gn_t, h_lru, g_lru, merge_g, wpa, wpb, wo):
    s = x2.shape[0]
    tm = OUT_TM
    rows = lambda n: pl.BlockSpec((tm, n), lambda i: (i, 0))
    cols = lambda n: pl.BlockSpec((n, tm), lambda i: (0, i))
    full = lambda a: pl.BlockSpec(a.shape, lambda i: (0, 0))
    return pl.pallas_call(
        _out_kernel,
        grid=(s // tm,),
        in_specs=[rows(D_MODEL), cols(NSA_WIDTH), cols(NSA_WIDTH), cols(NSA_WIDTH), cols(32),
                  cols(NSA_WIDTH), rows(LRU_WIDTH), rows(LRU_WIDTH), rows(2 * D_MODEL),
                  full(wpa), full(wpb), full(wo)],
        out_specs=rows(D_MODEL),
        out_shape=jax.ShapeDtypeStruct((s, D_MODEL), F32),
        compiler_params=_cparams(1),
        name="output",
    )(x2, oc_t, os_t, ow_t, br_t, gn_t, h_lru, g_lru, merge_g, wpa, wpb, wo)


def _t5_bucket_table(n_dist):
    n = np.arange(n_dist)
    max_exact = N_BUCKETS // 2
    nf = np.maximum(n, 1).astype(np.float32)
    large = max_exact + (np.log(nf / np.float32(max_exact)) / np.float32(math.log(MAX_DISTANCE / max_exact))
                         * np.float32(N_BUCKETS - max_exact)).astype(np.int32)
    return np.where(n < max_exact, n, np.minimum(large, N_BUCKETS - 1))


def _bias_tiles_kernel(relb_ref, bw_ref, bs_ref, bc_ref, ww_ref, ws_ref, wc_ref):
    h = pl.program_id(0)
    far = relb_ref[N_BUCKETS - 1, h]
    for bk_ref, out_ref, rel in ((bw_ref, ww_ref, False), (bs_ref, ws_ref, True), (bc_ref, wc_ref, True)):
        bk = bk_ref[...]
        acc = jnp.full(bk.shape, NEG, F32)
        for b in range(N_BUCKETS):
            val = relb_ref[b, h] - far if rel else relb_ref[b, h]
            acc = jnp.where(bk == b, val * LOG2E, acc)
        out_ref[...] = acc


def _bias_tables(rel_bias):
    bucket = _t5_bucket_table(2 * WINDOW)
    i = np.arange(Q_BLOCK)[None, :]

    def index_tile(dist, valid):
        return jnp.asarray(np.where(valid, bucket[np.clip(dist, 0, bucket.size - 1)], -1).astype(np.int32))

    u = np.arange(WINDOW + WIN_KEYS)[:, None]
    d_win = i - u + WINDOW
    bw = index_tile(d_win, (d_win >= 0) & (d_win < WINDOW))
    u = np.arange(SEL_NEAR + Q_BLOCK)[:, None]
    d_sel = i - u + Q_BLOCK
    bs = index_tile(d_sel, d_sel >= 0)
    u = np.arange(CMP_NEAR + 16)[:, None]
    d_cmp = i - CMP_STRIDE * (u - 16) - (CMP_BLOCK - 1)
    bc = index_tile(d_cmp, d_cmp >= 0)
    full = lambda a: pl.BlockSpec(a.shape, lambda h: (0, 0))
    per_head = lambda a: pl.BlockSpec((None,) + a.shape, lambda h: (h, 0, 0))
    ww, ws, wc = pl.pallas_call(
        _bias_tiles_kernel,
        grid=(N_HEADS,),
        in_specs=[pl.BlockSpec(memory_space=pltpu.SMEM), full(bw), full(bs), full(bc)],
        out_specs=[per_head(bw), per_head(bs), per_head(bc)],
        out_shape=[jax.ShapeDtypeStruct((N_HEADS,) + a.shape, F32) for a in (bw, bs, bc)],
        compiler_params=_cparams(1),
        name="bias_tiles",
    )(rel_bias, bw, bs, bc)
    b_far = rel_bias[N_BUCKETS - 1] * LOG2E
    hi = b_far.astype(BF16)
    lo = (b_far - hi.astype(F32)).astype(BF16)
    ext = jnp.zeros((N_HEADS, HEAD_DIM, Q_BLOCK), BF16)
    ext = ext.at[:, 0, :].set(hi[:, None]).at[:, 1, :].set(lo[:, None])
    ext = ext.reshape(N_GROUPS, GROUP_SIZE, HEAD_DIM, Q_BLOCK).transpose(0, 2, 1, 3)
    return ww, ws, wc, ext.reshape(N_GROUPS, HEAD_DIM, GQ)


def _block_diag(w):
    n, d, e = w.shape
    eye = jnp.eye(n, dtype=w.dtype)
    return (eye[:, None, :, None] * w[:, :, None, :]).reshape(n * d, n * e)


def kernel(x, norm_gain, w_in, q_norm_gain, k_norm_gain, cmp_pe, cmp_w1, cmp_b1, cmp_w2, rel_bias,
           conv_w, conv_b, lru_wa, lru_ba, lru_wx, lru_bx, lru_lambda, w_proj_a, w_proj_b, w_out):
    bsz, s, _ = x.shape
    assert bsz == 1 and s % 1024 == 0 and s >= 1024
    x2 = x.reshape(s, D_MODEL)
    nsb = s // SEL_BLOCK
    nsbp = -(-nsb // SUPER) * SUPER
    ncp = s // CMP_STRIDE

    o = np.cumsum((0, NSA_WIDTH, 6 * KV_WIDTH, NSA_WIDTH, 3 * N_HEADS, LRU_WIDTH, LRU_WIDTH, 2 * D_MODEL))
    pad = jnp.zeros((D_MODEL, BR_PAD - 3 * N_HEADS), w_in.dtype)
    w_pad = jnp.concatenate([w_in[:, :o[4]], pad, w_in[:, o[4]:]], axis=1).astype(BF16)

    q, kv, gn, br, u_lru, g_lru, merge_g = _inproj(x2, norm_gain, w_pad)

    scale = HEAD_DIM ** -0.5 * LOG2E
    qgain_col = jnp.tile(q_norm_gain * scale, N_HEADS).reshape(NSA_WIDTH, 1)
    kgain_rows = jnp.tile(k_norm_gain, (1, N_GROUPS))
    (qhi_t, qlo_t, gn_t, br_t, kaug, vt512, vt128, kwin, vwt, kcr, vcr) = _prep(
        q, kv, gn, br, qgain_col, kgain_rows)

    w2p = jnp.pad(cmp_w2, ((0, 0), (0, 0), (0, 128 - HEAD_DIM)))
    kc_gain = jnp.pad(k_norm_gain[0], (0, 128 - HEAD_DIM)).reshape(1, 128)
    kc_cat, vct = _compress(kcr.reshape(N_GROUPS, ncp, CMP_STRIDE * HEAD_DIM),
                            vcr.reshape(N_GROUPS, ncp, CMP_STRIDE * HEAD_DIM),
                            cmp_w1, cmp_b1.reshape(2, 1, CMP_HIDDEN), w2p,
                            cmp_pe.reshape(2, 1, CMP_BLOCK * HEAD_DIM), kc_gain)

    ww_t, ws_t, wc_t, ext_q = _bias_tables(rel_bias)
    oc_t, neg_t = _cmp_attention(qhi_t, qlo_t, kc_cat, vct, wc_t, ext_q, nsbp)
    os_t, ow_t = _sel_win_attention(qhi_t, neg_t, kaug, vt512, vt128, ws_t, ext_q, kwin, vwt, ww_t)

    h_lru = _rglru(u_lru, conv_w, conv_b, _block_diag(lru_wa).astype(BF16), lru_ba,
                   _block_diag(lru_wx).astype(BF16), lru_bx, lru_lambda)

    y = _output(x2, oc_t, os_t, ow_t, br_t, gn_t, h_lru, g_lru, merge_g,
                w_proj_a.astype(BF16), w_proj_b.astype(BF16), w_out.astype(BF16))
    return y.reshape(bsz, s, D_MODEL)
```

```python
import functools
import math

import numpy as np
import jax
import jax.numpy as jnp
from jax import lax
from jax.experimental import pallas as pl
from jax.experimental.pallas import tpu as pltpu

F32 = jnp.float32
BF16 = jnp.bfloat16

D_MODEL = 1024
N_HEADS = 8
N_GROUPS = 2
GROUP_SIZE = N_HEADS // N_GROUPS
HEAD_DIM = 64
NSA_WIDTH = N_HEADS * HEAD_DIM
KV_WIDTH = N_GROUPS * HEAD_DIM
CMP_STRIDE = 16
CMP_BLOCK = 32
CMP_HIDDEN = 256
SEL_BLOCK = 64
SEL_PER_CMP = SEL_BLOCK // CMP_STRIDE
N_SELECT = 16
WINDOW = 512
Q_BLOCK = 128
LRU_WIDTH = 512
LRU_BLOCKS = 8
CONV_WIDTH = 4
LRU_C = 8.0
N_BUCKETS = 32
MAX_DISTANCE = 128
EPS = 1e-6
NEG = -1e30
M_INIT = -5e29
LOG2E = 1.4426950408889634

GQ = GROUP_SIZE * Q_BLOCK
SUPER = 128
KV_CHUNK = 512
SEL_NEAR = 2 * Q_BLOCK
WIN_KEYS = WINDOW + Q_BLOCK
CMP_NEAR = 24
N_PICK = N_SELECT - 3
PICKED = -2.0
VT_ROWS = 80

BR_PAD = 128
COL_SIZES = (NSA_WIDTH, 6 * KV_WIDTH, NSA_WIDTH, BR_PAD, LRU_WIDTH, LRU_WIDTH, 2 * D_MODEL)
COL_OFFS = tuple(int(v) for v in np.cumsum((0,) + COL_SIZES))
D_IN_PAD = COL_OFFS[-1]

VMEM_LIMIT = 56 * 1024 * 1024


def _cparams(n_axes):
    return pltpu.CompilerParams(dimension_semantics=("arbitrary",) * n_axes,
                                vmem_limit_bytes=VMEM_LIMIT)


def _dot(a, b):
    return jnp.dot(a, b, preferred_element_type=F32)


def _inproj_kernel(x_ref, g_ref, w_ref, *out_refs):
    x = x_ref[...]
    ms = jnp.mean(x * x, axis=-1, keepdims=True)
    h = (x * lax.rsqrt(ms + EPS) * g_ref[...]).astype(BF16)
    for ref, a, b in zip(out_refs, COL_OFFS[:-1], COL_OFFS[1:]):
        ref[...] = _dot(h, w_ref[:, a:b])


def _inproj(x2, norm_gain, w_pad, tm=256):
    s = x2.shape[0]
    outs = [jax.ShapeDtypeStruct((s, n), F32) for n in COL_SIZES]
    return pl.pallas_call(
        _inproj_kernel,
        grid=(s // tm,),
        in_specs=[pl.BlockSpec((tm, D_MODEL), lambda i: (i, 0)),
                  pl.BlockSpec((1, D_MODEL), lambda i: (0, 0)),
                  pl.BlockSpec((D_MODEL, D_IN_PAD), lambda i: (0, 0))],
        out_specs=[pl.BlockSpec((tm, n), lambda i: (i, 0)) for n in COL_SIZES],
        out_shape=outs,
        compiler_params=_cparams(1),
        name="inproj",
    )(x2, norm_gain.reshape(1, D_MODEL), w_pad)


PREP_TM = 512


def _group_rms(k, gain_row):
    sq = k * k
    lane = lax.broadcasted_iota(jnp.int32, k.shape, 1)
    lo = lane < HEAD_DIM
    s0 = jnp.sum(jnp.where(lo, sq, 0.0), axis=-1, keepdims=True)
    s1 = jnp.sum(jnp.where(lo, 0.0, sq), axis=-1, keepdims=True)
    inv = jnp.where(lo, lax.rsqrt(s0 / HEAD_DIM + EPS), lax.rsqrt(s1 / HEAD_DIM + EPS))
    return k * inv * gain_row


def _prep_kernel(q_ref, kv_ref, gn_ref, br_ref, qg_ref, kg_ref,
                 qhi_ref, qlo_ref, gnt_ref, brt_ref, kaug_ref, vt512_ref, vt128_ref,
                 kwin_ref, vwt_ref, kcr_ref, vcr_ref):
    i = pl.program_id(0)
    tm = PREP_TM
    qt = q_ref[...].T
    for h in range(N_HEADS):
        blk = qt[h * HEAD_DIM:(h + 1) * HEAD_DIM]
        ms = jnp.mean(blk * blk, axis=0, keepdims=True)
        qn = blk * lax.rsqrt(ms + EPS) * qg_ref[h * HEAD_DIM:(h + 1) * HEAD_DIM, :]
        hi = qn.astype(BF16)
        qhi_ref[h * HEAD_DIM:(h + 1) * HEAD_DIM, :] = hi
        qlo_ref[h * HEAD_DIM:(h + 1) * HEAD_DIM, :] = (qn - hi.astype(F32)).astype(BF16)
    gn = gn_ref[...]
    gnt_ref[...] = (gn * jax.nn.sigmoid(gn)).T
    brt_ref[...] = jax.nn.sigmoid(br_ref[...]).T[:32]

    kv = kv_ref[...]
    piece = lambda j: kv[:, j * KV_WIDTH:(j + 1) * KV_WIDTH]
    lane = lax.broadcasted_iota(jnp.int32, (tm, 128), 1)
    row = lax.broadcasted_iota(jnp.int32, (tm, 128), 0) + i * tm
    lo = lane < HEAD_DIM
    ones_cols = jnp.where((lane == HEAD_DIM) | (lane == HEAD_DIM + 1), 1.0, 0.0)
    onehot = jnp.where(lane == (row // SEL_BLOCK) % SUPER, 1.0, 0.0).astype(BF16)
    kcmp, vcmp = piece(0), piece(1)
    kslc = _group_rms(piece(2), kg_ref[1:2, :])
    kwin = _group_rms(piece(4), kg_ref[2:3, :])
    vslt = piece(3).T
    vwit = piece(5).T
    row_t = lax.broadcasted_iota(jnp.int32, (VT_ROWS - HEAD_DIM, tm), 0)
    ones_rows = jnp.where(row_t == 0, 1.0, 0.0)
    for g in range(N_GROUPS):
        sh = lambda a: a if g == 0 else pltpu.roll(a, HEAD_DIM, 1)
        kaug_ref[g, :, 0:128] = onehot
        kaug_ref[g, :, 128:256] = jnp.where(lo, sh(kslc), ones_cols).astype(BF16)
        kwin_ref[g] = jnp.where(lo, sh(kwin), 0.0).astype(BF16)
        kcr_ref[g] = sh(kcmp)[:, 0:HEAD_DIM]
        vcr_ref[g] = sh(vcmp)[:, 0:HEAD_DIM]
        vs = jnp.concatenate([vslt[g * HEAD_DIM:(g + 1) * HEAD_DIM], ones_rows], axis=0).astype(BF16)
        vw = jnp.concatenate([vwit[g * HEAD_DIM:(g + 1) * HEAD_DIM], ones_rows], axis=0).astype(BF16)
        for j in range(tm // KV_CHUNK):
            vt512_ref[g, j] = vs[:, j * KV_CHUNK:(j + 1) * KV_CHUNK]
        for j in range(tm // 128):
            vt128_ref[g, j] = vs[:, j * 128:(j + 1) * 128]
            vwt_ref[g, j] = vw[:, j * 128:(j + 1) * 128]


def _prep(q, kv, gn, br, qgain_col, kgain_rows):
    s = q.shape[0]
    tm = PREP_TM
    g = N_GROUPS
    outs = [
        jax.ShapeDtypeStruct((NSA_WIDTH, s), BF16),
        jax.ShapeDtypeStruct((NSA_WIDTH, s), BF16),
        jax.ShapeDtypeStruct((NSA_WIDTH, s), F32),
        jax.ShapeDtypeStruct((32, s), F32),
        jax.ShapeDtypeStruct((g, s, 256), BF16),
        jax.ShapeDtypeStruct((g, s // KV_CHUNK, VT_ROWS, KV_CHUNK), BF16),
        jax.ShapeDtypeStruct((g, s // 128, VT_ROWS, 128), BF16),
        jax.ShapeDtypeStruct((g, s, 128), BF16),
        jax.ShapeDtypeStruct((g, s // 128, VT_ROWS, 128), BF16),
        jax.ShapeDtypeStruct((g, s, HEAD_DIM), F32),
        jax.ShapeDtypeStruct((g, s, HEAD_DIM), F32),
    ]
    out_specs = [
        pl.BlockSpec((NSA_WIDTH, tm), lambda i: (0, i)),
        pl.BlockSpec((NSA_WIDTH, tm), lambda i: (0, i)),
        pl.BlockSpec((NSA_WIDTH, tm), lambda i: (0, i)),
        pl.BlockSpec((32, tm), lambda i: (0, i)),
        pl.BlockSpec((g, tm, 256), lambda i: (0, i, 0)),
        pl.BlockSpec((g, tm // KV_CHUNK, VT_ROWS, KV_CHUNK), lambda i: (0, i, 0, 0)),
        pl.BlockSpec((g, tm // 128, VT_ROWS, 128), lambda i: (0, i, 0, 0)),
        pl.BlockSpec((g, tm, 128), lambda i: (0, i, 0)),
        pl.BlockSpec((g, tm // 128, VT_ROWS, 128), lambda i: (0, i, 0, 0)),
        pl.BlockSpec((g, tm, HEAD_DIM), lambda i: (0, i, 0)),
        pl.BlockSpec((g, tm, HEAD_DIM), lambda i: (0, i, 0)),
    ]
    return pl.pallas_call(
        _prep_kernel,
        grid=(s // tm,),
        in_specs=[pl.BlockSpec((tm, NSA_WIDTH), lambda i: (i, 0)),
                  pl.BlockSpec((tm, 6 * KV_WIDTH), lambda i: (i, 0)),
                  pl.BlockSpec((tm, NSA_WIDTH), lambda i: (i, 0)),
                  pl.BlockSpec((tm, BR_PAD), lambda i: (i, 0)),
                  pl.BlockSpec((NSA_WIDTH, 1), lambda i: (0, 0)),
                  pl.BlockSpec((3, 128), lambda i: (0, 0))],
        out_specs=out_specs,
        out_shape=outs,
        compiler_params=_cparams(1),
        name="prep",
    )(q, kv, gn, br, qgain_col, kgain_rows)


def _compress_kernel(xk_ref, xv_ref, w1_ref, b1_ref, w2_ref, pe_ref, kg_ref, kc_ref, vct_ref):
    ncp = xk_ref.shape[0]
    half = CMP_STRIDE * HEAD_DIM
    lane = lax.broadcasted_iota(jnp.int32, (ncp, 128), 1)
    lo = lane < HEAD_DIM

    def phi(x_ref, j):
        x = x_ref[...].astype(BF16)
        w1 = w1_ref[j].astype(BF16)
        a = _dot(x, w1[:half])
        b = _dot(x, w1[half:])
        b_next = pltpu.roll(b, ncp - 1, 0)
        pe8 = jnp.broadcast_to(pe_ref[j], (8, 2 * half)).astype(BF16)
        pe_term = _dot(pe8, w1)[0:1]
        hid = a + b_next + pe_term + b1_ref[j]
        act = (hid * jax.nn.sigmoid(hid)).astype(BF16)
        return _dot(act, w2_ref[j].astype(BF16))

    kc = phi(xk_ref, 0)
    ms = jnp.sum(kc * kc, axis=-1, keepdims=True) / HEAD_DIM
    kc = kc * lax.rsqrt(ms + EPS) * kg_ref[...]
    hi = kc.astype(BF16).astype(F32)
    lo_part = (kc - hi).astype(BF16).astype(F32)
    ones_cols = jnp.where((lane == HEAD_DIM) | (lane == HEAD_DIM + 1), 1.0, 0.0)
    kc_ref[:, 0:128] = (hi + pltpu.roll(lo_part, HEAD_DIM, 1)).astype(BF16)
    kc_ref[:, 128:256] = jnp.where(lo, hi, ones_cols).astype(BF16)
    vt = phi(xv_ref, 1).T
    row_t = lax.broadcasted_iota(jnp.int32, vt.shape, 0)
    vct_ref[...] = jnp.where(row_t == HEAD_DIM, 1.0, vt).astype(BF16)


def _compress(xk, xv, w1, b1, w2p, pe_flat, kgain_row):
    g, ncp, _ = xk.shape
    return pl.pallas_call(
        _compress_kernel,
        grid=(g,),
        in_specs=[pl.BlockSpec((None, ncp, CMP_STRIDE * HEAD_DIM), lambda i: (i, 0, 0)),
                  pl.BlockSpec((None, ncp, CMP_STRIDE * HEAD_DIM), lambda i: (i, 0, 0)),
                  pl.BlockSpec(w1.shape, lambda i: (0, 0, 0)),
                  pl.BlockSpec(b1.shape, lambda i: (0, 0, 0)),
                  pl.BlockSpec(w2p.shape, lambda i: (0, 0, 0)),
                  pl.BlockSpec(pe_flat.shape, lambda i: (0, 0, 0)),
                  pl.BlockSpec((1, 128), lambda i: (0, 0))],
        out_specs=[pl.BlockSpec((None, ncp, 256), lambda i: (i, 0, 0)),
                   pl.BlockSpec((None, 128, ncp), lambda i: (i, 0, 0))],
        out_shape=[jax.ShapeDtypeStruct((g, ncp, 256), BF16),
                   jax.ShapeDtypeStruct((g, 128, ncp), BF16)],
        compiler_params=_cparams(1),
        name="compress",
    )(xk, xv, w1, b1, w2p, pe_flat, kgain_row)


def _heads_to_lanes(q):
    return jnp.concatenate([q[r * HEAD_DIM:(r + 1) * HEAD_DIM] for r in range(GROUP_SIZE)], axis=1)


def _store_heads(out_ref, o):
    for r in range(GROUP_SIZE):
        out_ref[r * HEAD_DIM:(r + 1) * HEAD_DIM, :] = o[0:HEAD_DIM, r * Q_BLOCK:(r + 1) * Q_BLOCK]


def _bias_tile(w_ref, off, n, head0=0):
    return jnp.concatenate([w_ref[head0 + r, pl.ds(off, n), :] for r in range(GROUP_SIZE)], axis=1)


def _cmp_kernel(qhi_ref, qlo_ref, kc_ref, vct_ref, wc_ref, ext_ref, ocmp_ref, neg_ref, s_ref, imp_ref):
    qb = pl.program_id(0)
    ncp = kc_ref.shape[1]
    nsbp = neg_ref.shape[1]
    gr = GROUP_SIZE * HEAD_DIM
    ws = pl.multiple_of(jnp.maximum(8 * qb - 16, 0), 8)
    off = pl.multiple_of(jnp.maximum(16 - 8 * qb, 0), 8)
    lim = ws + CMP_NEAR
    n_cls = min(4, ncp // 128)
    per_cls = ncp // n_cls

    def attend(nrows):
        nblk = nrows // SEL_PER_CMP
        band = nrows - per_cls
        rows = lax.broadcasted_iota(jnp.int32, (per_cls, GQ), 0) + band
        blk = lax.broadcasted_iota(jnp.int32, (nblk, Q_BLOCK), 0)
        lane = lax.broadcasted_iota(jnp.int32, (nblk, Q_BLOCK), 1)
        cur = 2 * qb + jnp.where(lane >= SEL_BLOCK, 1, 0)
        forced = jnp.where(blk == 0, 1.0, jnp.where(blk == cur, 1.0, jnp.where(blk == cur - 1, 1.0, 0.0)))
        carry = []
        for g in range(N_GROUPS):
            qhi = qhi_ref[g * gr:(g + 1) * gr, :]
            qlo = qlo_ref[g * gr:(g + 1) * gr, :]
            cols = []
            for r in range(GROUP_SIZE):
                a = qhi[r * HEAD_DIM:(r + 1) * HEAD_DIM]
                b = qlo[r * HEAD_DIM:(r + 1) * HEAD_DIM]
                cols.append(jnp.concatenate([a, a, b], axis=0))
            qcat = jnp.concatenate([jnp.concatenate(cols, axis=1), ext_ref[g]], axis=0)
            s_g = s_ref.at[g]
            s_g[0:nrows, :] = _dot(kc_ref[g, 0:nrows, :], qcat)
            s_g[pl.ds(ws, CMP_NEAR), :] += _bias_tile(wc_ref, off, CMP_NEAR, g * GROUP_SIZE)
            s_g[band:nrows, :] = jnp.where(rows < lim, s_g[band:nrows, :], NEG)
            s = s_g[0:nrows, :]
            m = jnp.maximum(jnp.max(s, axis=0, keepdims=True), M_INIT)
            p = jnp.exp2(s - m)
            acc = _dot(vct_ref[g, 0:VT_ROWS, 0:nrows], p.astype(BF16))
            l = acc[HEAD_DIM:HEAD_DIM + 1]
            inv_l = jnp.where(l > 0.0, 1.0 / l, 0.0)
            _store_heads(ocmp_ref.at[g * gr:(g + 1) * gr], acc[0:HEAD_DIM] * inv_l)
            imp = None
            for r in range(GROUP_SIZE):
                cs = slice(r * Q_BLOCK, (r + 1) * Q_BLOCK)
                term = p[:, cs] * inv_l[:, cs]
                imp = term if imp is None else imp + term
            imp_g = imp_ref.at[g]
            imp_g[0:nrows, :] = imp
            strided = [imp_g[pl.ds(k, nblk, stride=SEL_PER_CMP), :] for k in range(SEL_PER_CMP)]
            prev_last = jnp.where(blk == 0, 0.0, pltpu.roll(strided[SEL_PER_CMP - 1], 1, 0))
            impb = prev_last + strided[0] + strided[1] + strided[2] + strided[3]
            carry.append(jnp.where(blk < cur - 1, jnp.where(blk > 0, impb, -1.0), -1.0))

        def pick_one(_, cands):
            out = []
            for sc in cands:
                mx = jnp.max(sc, axis=0, keepdims=True)
                hit_blk = jnp.where(sc == mx, jnp.where(mx >= 0.0, blk, nblk), nblk)
                first = jnp.min(hit_blk, axis=0, keepdims=True)
                out.append(jnp.where(blk == first, PICKED, sc))
            return tuple(out)

        if nblk * N_GROUPS <= 2 * SUPER:
            carry = lax.fori_loop(0, N_PICK, pick_one, tuple(carry))
        else:
            carry = [lax.fori_loop(0, N_PICK, pick_one, (c,))[0] for c in carry]
        for g in range(N_GROUPS):
            picked = carry[g] == PICKED
            sel_neg = jnp.where(picked, 0.0, jnp.where(blk <= cur, jnp.where(forced > 0.5, 0.0, NEG), NEG))
            neg_ref[g, 0:nblk, :] = sel_neg.astype(BF16)
            if nblk < nsbp:
                neg_ref[g, nblk:nsbp, :] = jnp.full((nsbp - nblk, Q_BLOCK), NEG, BF16)

    cls = (lim + per_cls - 1) // per_cls
    for k in range(1, n_cls + 1):
        pl.when(cls == k)(functools.partial(attend, k * per_cls))


def _cmp_attention(qhi_t, qlo_t, kc_cat, vct, wc_t, ext_q, nsbp):
    s = qhi_t.shape[1]
    g = N_GROUPS
    ncp = kc_cat.shape[1]
    full = lambda a: pl.BlockSpec(a.shape, lambda qb: (0,) * a.ndim)
    return pl.pallas_call(
        _cmp_kernel,
        grid=(s // Q_BLOCK,),
        in_specs=[pl.BlockSpec((NSA_WIDTH, Q_BLOCK), lambda qb: (0, qb)),
                  pl.BlockSpec((NSA_WIDTH, Q_BLOCK), lambda qb: (0, qb)),
                  full(kc_cat), full(vct), full(wc_t), full(ext_q)],
        out_specs=[pl.BlockSpec((NSA_WIDTH, Q_BLOCK), lambda qb: (0, qb)),
                   pl.BlockSpec((g, nsbp, Q_BLOCK), lambda qb: (0, 0, qb))],
        out_shape=[jax.ShapeDtypeStruct((NSA_WIDTH, s), F32),
                   jax.ShapeDtypeStruct((g, nsbp, s), BF16)],
        scratch_shapes=[pltpu.VMEM((g, ncp, GQ), F32), pltpu.VMEM((g, ncp, Q_BLOCK), F32)],
        compiler_params=_cparams(1),
        name="cmp_select",
    )(qhi_t, qlo_t, kc_cat, vct, wc_t, ext_q)


def _sel_kernel(qhi_ref, neg_ref, kaug_ref, vt512_ref, vt128_ref, ws_ref, ext_ref,
                kwin_ref, vwt_ref, ww_ref, out_ref, owin_ref,
                qaug_ref, m_ref, acc_ref, s0_ref, s1_ref, s2_ref, cm0_ref, cm1_ref, cm2_ref):
    qb = pl.program_id(1)
    nsbp = neg_ref.shape[0]
    nsc = nsbp // SUPER
    qg = _heads_to_lanes(qhi_ref[...])
    qx = jnp.concatenate([qg, ext_ref[...]], axis=0)
    blk = lax.broadcasted_iota(jnp.int32, (SUPER, Q_BLOCK), 0)

    def aug(neg_rows):
        tiled = jnp.concatenate([neg_rows] * GROUP_SIZE, axis=1).astype(BF16)
        return jnp.concatenate([tiled, qx], axis=0)

    for sc in range(nsc):
        neg_rows = neg_ref[sc * SUPER:(sc + 1) * SUPER, :].astype(F32)
        qaug_ref[sc] = aug(jnp.where(blk + sc * SUPER >= 2 * qb - 2, NEG, neg_rows))

    m_ref[...] = jnp.full(m_ref.shape, M_INIT, F32)
    acc_ref[...] = jnp.zeros(acc_ref.shape, F32)

    def update(s, col_max, vt):
        m_old = m_ref[...]
        m_new = jnp.maximum(m_old, col_max)
        p = jnp.exp2(s - m_new).astype(BF16)
        acc_ref[...] = acc_ref[...] * jnp.exp2(m_old - m_new) + _dot(vt, p)
        m_ref[...] = m_new

    n_far = (jnp.maximum(qb - 1, 0) * Q_BLOCK + KV_CHUNK - 1) // KV_CHUNK

    def scores(c, dst_ref, max_ref):
        k = kaug_ref[pl.ds(pl.multiple_of(c * KV_CHUNK, KV_CHUNK), KV_CHUNK), :]
        sc = (c * (KV_CHUNK // SEL_BLOCK)) // SUPER
        s = _dot(k, qaug_ref[sc])
        dst_ref[...] = s
        max_ref[...] = jnp.max(s, axis=0, keepdims=True)

    kb0 = jnp.maximum(qb - 1, 0)
    ws = pl.multiple_of(kb0 * Q_BLOCK, Q_BLOCK)
    off = pl.multiple_of(jnp.where(qb == 0, Q_BLOCK, 0), Q_BLOCK)
    b_lo = 2 * kb0
    sc_lo = pl.multiple_of((b_lo // SUPER) * SUPER, SUPER)
    sc_hi = pl.multiple_of(((b_lo + 3) // SUPER) * SUPER, SUPER)
    neg_lo = neg_ref[pl.ds(sc_lo, SUPER), :].astype(F32)
    neg_hi = neg_ref[pl.ds(sc_hi, SUPER), :].astype(F32)
    near_neg = jnp.where(blk >= SUPER // 2, neg_lo, neg_hi)
    k = kaug_ref[pl.ds(ws, SEL_NEAR), :]
    s_near = _dot(k, aug(near_neg)) + _bias_tile(ws_ref, off, SEL_NEAR)
    vt_near = jnp.concatenate([vt128_ref[kb0], vt128_ref[kb0 + 1]], axis=1)

    bufs = ((s0_ref, cm0_ref), (s1_ref, cm1_ref), (s2_ref, cm2_ref))
    last = jnp.maximum(n_far - 1, 0)
    scores(0, *bufs[0])
    scores(jnp.minimum(1, last), *bufs[1])
    update(s_near, jnp.max(s_near, axis=0, keepdims=True), vt_near)
    _window_branch(qb, qg, kwin_ref, vwt_ref, ww_ref, owin_ref)

    def far_triple(c):
        for j in range(3):
            scores(jnp.minimum(c + j + 2, last), *bufs[(j + 2) % 3])
            s_ref, cm_ref = bufs[j]
            update(s_ref[...], cm_ref[...], vt512_ref[c + j])

    def far_six(i, carry):
        far_triple(6 * i)
        far_triple(6 * i + 3)
        return carry

    lax.fori_loop(0, n_far // 6, far_six, 0)
    done = (n_far // 6) * 6

    @pl.when(n_far - done >= 3)
    def _():
        far_triple(done)

    done = (n_far // 3) * 3
    for j in range(2):
        @pl.when(n_far - done > j)
        def _():
            s_ref, cm_ref = bufs[j]
            update(s_ref[...], cm_ref[...], vt512_ref[done + j])

    acc = acc_ref[...]
    _store_heads(out_ref, acc[0:HEAD_DIM] * (1.0 / acc[HEAD_DIM:HEAD_DIM + 1]))


def _sel_win_attention(qhi_t, neg_t, kaug, vt512, vt128, ws_t, ext_q, kwin, vwt, ww_t):
    s = qhi_t.shape[1]
    g = N_GROUPS
    nqb = s // Q_BLOCK
    nsbp = neg_t.shape[1]
    gr = GROUP_SIZE * HEAD_DIM
    once = pl.Buffered(1)
    return pl.pallas_call(
        _sel_kernel,
        grid=(g, nqb),
        in_specs=[pl.BlockSpec((gr, Q_BLOCK), lambda gi, qb: (gi, qb)),
                  pl.BlockSpec((None, nsbp, Q_BLOCK), lambda gi, qb: (gi, 0, qb)),
                  pl.BlockSpec((None, s, 256), lambda gi, qb: (gi, 0, 0), pipeline_mode=once),
                  pl.BlockSpec((None, s // KV_CHUNK, VT_ROWS, KV_CHUNK), lambda gi, qb: (gi, 0, 0, 0),
                               pipeline_mode=once),
                  pl.BlockSpec((None, s // 128, VT_ROWS, 128), lambda gi, qb: (gi, 0, 0, 0),
                               pipeline_mode=once),
                  pl.BlockSpec((GROUP_SIZE,) + ws_t.shape[1:], lambda gi, qb: (gi, 0, 0)),
                  pl.BlockSpec((None, HEAD_DIM, GQ), lambda gi, qb: (gi, 0, 0)),
                  pl.BlockSpec((None, s, 128), lambda gi, qb: (gi, 0, 0), pipeline_mode=once),
                  pl.BlockSpec((None, s // 128, VT_ROWS, 128), lambda gi, qb: (gi, 0, 0, 0),
                               pipeline_mode=once),
                  pl.BlockSpec((GROUP_SIZE,) + ww_t.shape[1:], lambda gi, qb: (gi, 0, 0))],
        out_specs=[pl.BlockSpec((gr, Q_BLOCK), lambda gi, qb: (gi, qb)),
                   pl.BlockSpec((gr, Q_BLOCK), lambda gi, qb: (gi, qb))],
        out_shape=[jax.ShapeDtypeStruct((NSA_WIDTH, s), F32),
                   jax.ShapeDtypeStruct((NSA_WIDTH, s), F32)],
        scratch_shapes=[pltpu.VMEM((nsbp // SUPER, 256, GQ), BF16),
                        pltpu.VMEM((1, GQ), F32),
                        pltpu.VMEM((VT_ROWS, GQ), F32),
                        pltpu.VMEM((KV_CHUNK, GQ), F32),
                        pltpu.VMEM((KV_CHUNK, GQ), F32),
                        pltpu.VMEM((KV_CHUNK, GQ), F32),
                        pltpu.VMEM((1, GQ), F32),
                        pltpu.VMEM((1, GQ), F32),
                        pltpu.VMEM((1, GQ), F32)],
        compiler_params=_cparams(2),
        name="sel_win_attention",
    )(qhi_t, neg_t, kaug, vt512, vt128, ws_t, ext_q, kwin, vwt, ww_t)


def _window_branch(qb, qg, kwin_ref, vwt_ref, ww_ref, out_ref):
    kb0 = jnp.maximum(qb - WINDOW // Q_BLOCK, 0)
    ws = pl.multiple_of(kb0 * Q_BLOCK, Q_BLOCK)
    off = pl.multiple_of(jnp.maximum(WINDOW - qb * Q_BLOCK, 0), Q_BLOCK)
    qx = jnp.concatenate([qg, jnp.zeros_like(qg)], axis=0)
    s = _dot(kwin_ref[pl.ds(ws, WIN_KEYS), :], qx) + _bias_tile(ww_ref, off, WIN_KEYS)
    m = jnp.max(s, axis=0, keepdims=True)
    p = jnp.exp2(s - m).astype(BF16)
    vt = jnp.concatenate([vwt_ref[kb0 + j] for j in range(WIN_KEYS // Q_BLOCK)], axis=1)
    acc = _dot(vt, p)
    _store_heads(out_ref, acc[0:HEAD_DIM] * (1.0 / acc[HEAD_DIM:HEAD_DIM + 1]))


LRU_TM = 256


def _lru_kernel(u_ref, cw_ref, cb_ref, wa_ref, ba_ref, wx_ref, bx_ref, lam_ref, h_ref,
                tail_ref, hprev_ref):
    tm = LRU_TM

    @pl.when(pl.program_id(0) == 0)
    def _():
        tail_ref[...] = jnp.zeros(tail_ref.shape, F32)
        hprev_ref[...] = jnp.zeros(hprev_ref.shape, F32)

    u = u_ref[...]
    tail = tail_ref[...]
    row8 = lax.broadcasted_iota(jnp.int32, (8, LRU_WIDTH), 0)
    uc = cb_ref[...] + u * cw_ref[CONV_WIDTH - 1:CONV_WIDTH, :]
    for j in range(1, CONV_WIDTH):
        sh = pltpu.roll(u, j, 0)
        head = jnp.where(row8 < j, pltpu.roll(tail, j, 0), sh[0:8])
        sh = jnp.concatenate([head, sh[8:]], axis=0)
        uc = uc + sh * cw_ref[CONV_WIDTH - 1 - j:CONV_WIDTH - j, :]
    tail_ref[...] = u[tm - 8:tm]

    ucb = uc.astype(BF16)
    r = jax.nn.sigmoid(_dot(ucb, wa_ref[...]) + ba_ref[...])
    ig = jax.nn.sigmoid(_dot(ucb, wx_ref[...]) + bx_ref[...])
    nl = -lam_ref[...]
    softplus = jnp.maximum(nl, 0.0) + jnp.log(1.0 + jnp.exp(-jnp.abs(nl)))
    log_a = (-LRU_C * r) * softplus
    a = jnp.exp(log_a)
    b = jnp.sqrt(1.0 - jnp.exp(2.0 * log_a)) * (ig * uc)

    row = lax.broadcasted_iota(jnp.int32, (tm, LRU_WIDTH), 0)
    step = 1
    while step < tm:
        a_s = pltpu.roll(a, step, 0)
        b_s = pltpu.roll(b, step, 0)
        ok = row >= step
        b = jnp.where(ok, a * b_s + b, b)
        a = jnp.where(ok, a * a_s, a)
        step *= 2
    h = a * hprev_ref[...] + b
    h_ref[...] = h
    hprev_ref[...] = h[tm - 1:tm]


def _rglru(u, conv_w, conv_b, wa_bd, ba, wx_bd, bx, lam):
    s = u.shape[0]
    tm = LRU_TM
    row = lambda v: v.reshape(1, LRU_WIDTH)
    full = lambda shape: pl.BlockSpec(shape, lambda i: (0, 0))
    return pl.pallas_call(
        _lru_kernel,
        grid=(s // tm,),
        in_specs=[pl.BlockSpec((tm, LRU_WIDTH), lambda i: (i, 0)),
                  full((CONV_WIDTH, LRU_WIDTH)), full((1, LRU_WIDTH)),
                  full((LRU_WIDTH, LRU_WIDTH)), full((1, LRU_WIDTH)),
                  full((LRU_WIDTH, LRU_WIDTH)), full((1, LRU_WIDTH)),
                  full((1, LRU_WIDTH))],
        out_specs=pl.BlockSpec((tm, LRU_WIDTH), lambda i: (i, 0)),
        out_shape=jax.ShapeDtypeStruct((s, LRU_WIDTH), F32),
        scratch_shapes=[pltpu.VMEM((8, LRU_WIDTH), F32), pltpu.VMEM((1, LRU_WIDTH), F32)],
        compiler_params=_cparams(1),
        name="rglru",
    )(u, conv_w, row(conv_b), wa_bd, row(ba), wx_bd, row(bx), row(lam))


OUT_TM = 256


def _out_kernel(x_ref, oc_ref, os_ref, ow_ref, brt_ref, gnt_ref, h_ref, gl_ref, mg_ref,
                wpa_ref, wpb_ref, wo_ref, y_ref):
    tm = OUT_TM
    brt = brt_ref[...]
    parts = []
    for h in range(N_HEADS):
        rows = slice(h * HEAD_DIM, (h + 1) * HEAD_DIM)
        acc = None
        for b, o_ref in enumerate((oc_ref, os_ref, ow_ref)):
            term = brt[b * N_HEADS + h:b * N_HEADS + h + 1, :] * o_ref[rows, :]
            acc = term if acc is None else acc + term
        parts.append(acc)
    ya_t = jnp.concatenate(parts, axis=0) * gnt_ref[...]
    y_a = _dot(ya_t.T.astype(BF16), wpa_ref[...])
    gl = gl_ref[...]
    y_b = _dot((h_ref[...] * (gl * jax.nn.sigmoid(gl))).astype(BF16), wpb_ref[...])
    mg = mg_ref[...]
    m = jax.nn.sigmoid(mg[:, :D_MODEL]) * y_a + jax.nn.sigmoid(mg[:, D_MODEL:]) * y_b
    y_ref[...] = x_ref[...] + _dot(m.astype(BF16), wo_ref[...])


def _output(x2, oc_t, os_t, ow_t, br_t, gn_t, h_lru, g_lru, merge_g, wpa, wpb, wo):
    s = x2.shape[0]
    tm = OUT_TM
    rows = lambda n: pl.BlockSpec((tm, n), lambda i: (i, 0))
    cols = lambda n: pl.BlockSpec((n, tm), lambda i: (0, i))
    full = lambda a: pl.BlockSpec(a.shape, lambda i: (0, 0))
    return pl.pallas_call(
        _out_kernel,
        grid=(s // tm,),
        in_specs=[rows(D_MODEL), cols(NSA_WIDTH), cols(NSA_WIDTH), cols(NSA_WIDTH), cols(32),
                  cols(NSA_WIDTH), rows(LRU_WIDTH), rows(LRU_WIDTH), rows(2 * D_MODEL),
                  full(wpa), full(wpb), full(wo)],
        out_specs=rows(D_MODEL),
        out_shape=jax.ShapeDtypeStruct((s, D_MODEL), F32),
        compiler_params=_cparams(1),
        name="output",
    )(x2, oc_t, os_t, ow_t, br_t, gn_t, h_lru, g_lru, merge_g, wpa, wpb, wo)


def _t5_bucket_table(n_dist):
    n = np.arange(n_dist)
    max_exact = N_BUCKETS // 2
    nf = np.maximum(n, 1).astype(np.float32)
    large = max_exact + (np.log(nf / np.float32(max_exact)) / np.float32(math.log(MAX_DISTANCE / max_exact))
                         * np.float32(N_BUCKETS - max_exact)).astype(np.int32)
    return np.where(n < max_exact, n, np.minimum(large, N_BUCKETS - 1))


def _bias_tiles_kernel(relb_ref, bw_ref, bs_ref, bc_ref, ww_ref, ws_ref, wc_ref):
    h = pl.program_id(0)
    far = relb_ref[N_BUCKETS - 1, h]
    for bk_ref, out_ref, rel in ((bw_ref, ww_ref, False), (bs_ref, ws_ref, True), (bc_ref, wc_ref, True)):
        bk = bk_ref[...]
        acc = jnp.full(bk.shape, NEG, F32)
        for b in range(N_BUCKETS):
            val = relb_ref[b, h] - far if rel else relb_ref[b, h]
            acc = jnp.where(bk == b, val * LOG2E, acc)
        out_ref[...] = acc


def _bias_tables(rel_bias):
    bucket = _t5_bucket_table(2 * WINDOW)
    i = np.arange(Q_BLOCK)[None, :]

    def index_tile(dist, valid):
        return jnp.asarray(np.where(valid, bucket[np.clip(dist, 0, bucket.size - 1)], -1).astype(np.int32))

    u = np.arange(WINDOW + WIN_KEYS)[:, None]
    d_win = i - u + WINDOW
    bw = index_tile(d_win, (d_win >= 0) & (d_win < WINDOW))
    u = np.arange(SEL_NEAR + Q_BLOCK)[:, None]
    d_sel = i - u + Q_BLOCK
    bs = index_tile(d_sel, d_sel >= 0)
    u = np.arange(CMP_NEAR + 16)[:, None]
    d_cmp = i - CMP_STRIDE * (u - 16) - (CMP_BLOCK - 1)
    bc = index_tile(d_cmp, d_cmp >= 0)
    full = lambda a: pl.BlockSpec(a.shape, lambda h: (0, 0))
    per_head = lambda a: pl.BlockSpec((None,) + a.shape, lambda h: (h, 0, 0))
    ww, ws, wc = pl.pallas_call(
        _bias_tiles_kernel,
        grid=(N_HEADS,),
        in_specs=[pl.BlockSpec(memory_space=pltpu.SMEM), full(bw), full(bs), full(bc)],
        out_specs=[per_head(bw), per_head(bs), per_head(bc)],
        out_shape=[jax.ShapeDtypeStruct((N_HEADS,) + a.shape, F32) for a in (bw, bs, bc)],
        compiler_params=_cparams(1),
        name="bias_tiles",
    )(rel_bias, bw, bs, bc)
    b_far = rel_bias[N_BUCKETS - 1] * LOG2E
    hi = b_far.astype(BF16)
    lo = (b_far - hi.astype(F32)).astype(BF16)
    ext = jnp.zeros((N_HEADS, HEAD_DIM, Q_BLOCK), BF16)
    ext = ext.at[:, 0, :].set(hi[:, None]).at[:, 1, :].set(lo[:, None])
    ext = ext.reshape(N_GROUPS, GROUP_SIZE, HEAD_DIM, Q_BLOCK).transpose(0, 2, 1, 3)
    return ww, ws, wc, ext.reshape(N_GROUPS, HEAD_DIM, GQ)


def _block_diag(w):
    n, d, e = w.shape
    eye = jnp.eye(n, dtype=w.dtype)
    return (eye[:, None, :, None] * w[:, :, None, :]).reshape(n * d, n * e)


def kernel(x, norm_gain, w_in, q_norm_gain, k_norm_gain, cmp_pe, cmp_w1, cmp_b1, cmp_w2, rel_bias,
           conv_w, conv_b, lru_wa, lru_ba, lru_wx, lru_bx, lru_lambda, w_proj_a, w_proj_b, w_out):
    bsz, s, _ = x.shape
    assert bsz == 1 and s % 1024 == 0 and s >= 1024
    x2 = x.reshape(s, D_MODEL)
    nsb = s // SEL_BLOCK
    nsbp = -(-nsb // SUPER) * SUPER
    ncp = s // CMP_STRIDE

    o = np.cumsum((0, NSA_WIDTH, 6 * KV_WIDTH, NSA_WIDTH, 3 * N_HEADS, LRU_WIDTH, LRU_WIDTH, 2 * D_MODEL))
    pad = jnp.zeros((D_MODEL, BR_PAD - 3 * N_HEADS), w_in.dtype)
    w_pad = jnp.concatenate([w_in[:, :o[4]], pad, w_in[:, o[4]:]], axis=1).astype(BF16)

    q, kv, gn, br, u_lru, g_lru, merge_g = _inproj(x2, norm_gain, w_pad)

    scale = HEAD_DIM ** -0.5 * LOG2E
    qgain_col = jnp.tile(q_norm_gain * scale, N_HEADS).reshape(NSA_WIDTH, 1)
    kgain_rows = jnp.tile(k_norm_gain, (1, N_GROUPS))
    (qhi_t, qlo_t, gn_t, br_t, kaug, vt512, vt128, kwin, vwt, kcr, vcr) = _prep(
        q, kv, gn, br, qgain_col, kgain_rows)

    w2p = jnp.pad(cmp_w2, ((0, 0), (0, 0), (0, 128 - HEAD_DIM)))
    kc_gain = jnp.pad(k_norm_gain[0], (0, 128 - HEAD_DIM)).reshape(1, 128)
    kc_cat, vct = _compress(kcr.reshape(N_GROUPS, ncp, CMP_STRIDE * HEAD_DIM),
                            vcr.reshape(N_GROUPS, ncp, CMP_STRIDE * HEAD_DIM),
                            cmp_w1, cmp_b1.reshape(2, 1, CMP_HIDDEN), w2p,
                            cmp_pe.reshape(2, 1, CMP_BLOCK * HEAD_DIM), kc_gain)

    ww_t, ws_t, wc_t, ext_q = _bias_tables(rel_bias)
    oc_t, neg_t = _cmp_attention(qhi_t, qlo_t, kc_cat, vct, wc_t, ext_q, nsbp)
    os_t, ow_t = _sel_win_attention(qhi_t, neg_t, kaug, vt512, vt128, ws_t, ext_q, kwin, vwt, ww_t)

    h_lru = _rglru(u_lru, conv_w, conv_b, _block_diag(lru_wa).astype(BF16), lru_ba,
                   _block_diag(lru_wx).astype(BF16), lru_bx, lru_lambda)

    y = _output(x2, oc_t, os_t, ow_t, br_t, gn_t, h_lru, g_lru, merge_g,
                w_proj_a.astype(BF16), w_proj_b.astype(BF16), w_out.astype(BF16))
    return y.reshape(bsz, s, D_MODEL)
```

```python
import functools
import math

import numpy as np
import jax
import jax.numpy as jnp
from jax import lax
from jax.experimental import pallas as pl
from jax.experimental.pallas import tpu as pltpu

F32 = jnp.float32
BF16 = jnp.bfloat16

D_MODEL = 1024
N_HEADS = 8
N_GROUPS = 2
GROUP_SIZE = N_HEADS // N_GROUPS
HEAD_DIM = 64
NSA_WIDTH = N_HEADS * HEAD_DIM
KV_WIDTH = N_GROUPS * HEAD_DIM
CMP_STRIDE = 16
CMP_BLOCK = 32
CMP_HIDDEN = 256
SEL_BLOCK = 64
SEL_PER_CMP = SEL_BLOCK // CMP_STRIDE
N_SELECT = 16
WINDOW = 512
Q_BLOCK = 128
LRU_WIDTH = 512
LRU_BLOCKS = 8
CONV_WIDTH = 4
LRU_C = 8.0
N_BUCKETS = 32
MAX_DISTANCE = 128
EPS = 1e-6
NEG = -1e30
M_INIT = -5e29
LOG2E = 1.4426950408889634

GQ = GROUP_SIZE * Q_BLOCK
SUPER = 128
KV_CHUNK = 512
SEL_NEAR = 2 * Q_BLOCK
WIN_KEYS = WINDOW + Q_BLOCK
CMP_NEAR = 24
N_PICK = N_SELECT - 3
PICKED = -2.0
VT_ROWS = 80

IN_OFFS = tuple(int(v) for v in np.cumsum(
    (0, NSA_WIDTH, 6 * KV_WIDTH, NSA_WIDTH, 3 * N_HEADS, LRU_WIDTH, LRU_WIDTH, 2 * D_MODEL)))
BR_PAD = 128
FRONT_OFFS = tuple(int(v) for v in np.cumsum((0, NSA_WIDTH, 6 * KV_WIDTH, BR_PAD, LRU_WIDTH)))
GATE_OFFS = tuple(int(v) for v in np.cumsum((0, NSA_WIDTH, LRU_WIDTH, 2 * D_MODEL)))

VMEM_LIMIT = 56 * 1024 * 1024


def _cparams(n_axes):
    return pltpu.CompilerParams(dimension_semantics=("arbitrary",) * n_axes,
                                vmem_limit_bytes=VMEM_LIMIT)


def _dot(a, b):
    return jnp.dot(a, b, preferred_element_type=F32)


FRONT_TM = 512


def _normed_input(x_ref, g_ref):
    x = x_ref[...]
    ms = jnp.mean(x * x, axis=-1, keepdims=True)
    return (x * lax.rsqrt(ms + EPS) * g_ref[...]).astype(BF16)


def _group_rms(k, gain_row):
    sq = k * k
    lane = lax.broadcasted_iota(jnp.int32, k.shape, 1)
    lo = lane < HEAD_DIM
    s0 = jnp.sum(jnp.where(lo, sq, 0.0), axis=-1, keepdims=True)
    s1 = jnp.sum(jnp.where(lo, 0.0, sq), axis=-1, keepdims=True)
    inv = jnp.where(lo, lax.rsqrt(s0 / HEAD_DIM + EPS), lax.rsqrt(s1 / HEAD_DIM + EPS))
    return k * inv * gain_row


def _front_kernel(x_ref, g_ref, w_ref, qg_ref, kg_ref,
                  qhi_ref, qlo_ref, brt_ref, kaug_ref, vt512_ref, vt128_ref,
                  kwin_ref, vwt_ref, kcr_ref, vcr_ref, u_ref):
    i = pl.program_id(0)
    tm = FRONT_TM
    h_in = _normed_input(x_ref, g_ref)
    proj = lambda a, b: _dot(h_in, w_ref[:, a:b])
    qt = proj(FRONT_OFFS[0], FRONT_OFFS[1]).T
    for h in range(N_HEADS):
        blk = qt[h * HEAD_DIM:(h + 1) * HEAD_DIM]
        ms = jnp.mean(blk * blk, axis=0, keepdims=True)
        qn = blk * lax.rsqrt(ms + EPS) * qg_ref[h * HEAD_DIM:(h + 1) * HEAD_DIM, :]
        hi = qn.astype(BF16)
        qhi_ref[h * HEAD_DIM:(h + 1) * HEAD_DIM, :] = hi
        qlo_ref[h * HEAD_DIM:(h + 1) * HEAD_DIM, :] = (qn - hi.astype(F32)).astype(BF16)
    brt_ref[...] = jax.nn.sigmoid(proj(FRONT_OFFS[2], FRONT_OFFS[3])).T[:32]
    u_ref[...] = proj(FRONT_OFFS[3], FRONT_OFFS[4])

    kv = proj(FRONT_OFFS[1], FRONT_OFFS[2])
    piece = lambda j: kv[:, j * KV_WIDTH:(j + 1) * KV_WIDTH]
    lane = lax.broadcasted_iota(jnp.int32, (tm, 128), 1)
    row = lax.broadcasted_iota(jnp.int32, (tm, 128), 0) + i * tm
    lo = lane < HEAD_DIM
    ones_cols = jnp.where((lane == HEAD_DIM) | (lane == HEAD_DIM + 1), 1.0, 0.0)
    onehot = jnp.where(lane == (row // SEL_BLOCK) % SUPER, 1.0, 0.0).astype(BF16)
    kcmp, vcmp = piece(0), piece(1)
    kslc = _group_rms(piece(2), kg_ref[1:2, :])
    kwin = _group_rms(piece(4), kg_ref[2:3, :])
    vslt = piece(3).T
    vwit = piece(5).T
    row_t = lax.broadcasted_iota(jnp.int32, (VT_ROWS - HEAD_DIM, tm), 0)
    ones_rows = jnp.where(row_t == 0, 1.0, 0.0)
    for g in range(N_GROUPS):
        sh = lambda a: a if g == 0 else pltpu.roll(a, HEAD_DIM, 1)
        kaug_ref[g, :, 0:128] = onehot
        kaug_ref[g, :, 128:256] = jnp.where(lo, sh(kslc), ones_cols).astype(BF16)
        kwin_ref[g] = jnp.where(lo, sh(kwin), 0.0).astype(BF16)
        kcr_ref[g] = sh(kcmp)[:, 0:HEAD_DIM]
        vcr_ref[g] = sh(vcmp)[:, 0:HEAD_DIM]
        vs = jnp.concatenate([vslt[g * HEAD_DIM:(g + 1) * HEAD_DIM], ones_rows], axis=0).astype(BF16)
        vw = jnp.concatenate([vwit[g * HEAD_DIM:(g + 1) * HEAD_DIM], ones_rows], axis=0).astype(BF16)
        for j in range(tm // KV_CHUNK):
            vt512_ref[g, j] = vs[:, j * KV_CHUNK:(j + 1) * KV_CHUNK]
        for j in range(tm // 128):
            vt128_ref[g, j] = vs[:, j * 128:(j + 1) * 128]
            vwt_ref[g, j] = vw[:, j * 128:(j + 1) * 128]


def _front(x2, norm_gain, w_front, qgain_col, kgain_rows):
    s = x2.shape[0]
    tm = FRONT_TM
    g = N_GROUPS
    outs = [
        jax.ShapeDtypeStruct((NSA_WIDTH, s), BF16),
        jax.ShapeDtypeStruct((NSA_WIDTH, s), BF16),
        jax.ShapeDtypeStruct((32, s), F32),
        jax.ShapeDtypeStruct((g, s, 256), BF16),
        jax.ShapeDtypeStruct((g, s // KV_CHUNK, VT_ROWS, KV_CHUNK), BF16),
        jax.ShapeDtypeStruct((g, s // 128, VT_ROWS, 128), BF16),
        jax.ShapeDtypeStruct((g, s, 128), BF16),
        jax.ShapeDtypeStruct((g, s // 128, VT_ROWS, 128), BF16),
        jax.ShapeDtypeStruct((g, s, HEAD_DIM), F32),
        jax.ShapeDtypeStruct((g, s, HEAD_DIM), F32),
        jax.ShapeDtypeStruct((s, LRU_WIDTH), F32),
    ]
    out_specs = [
        pl.BlockSpec((NSA_WIDTH, tm), lambda i: (0, i)),
        pl.BlockSpec((NSA_WIDTH, tm), lambda i: (0, i)),
        pl.BlockSpec((32, tm), lambda i: (0, i)),
        pl.BlockSpec((g, tm, 256), lambda i: (0, i, 0)),
        pl.BlockSpec((g, tm // KV_CHUNK, VT_ROWS, KV_CHUNK), lambda i: (0, i, 0, 0)),
        pl.BlockSpec((g, tm // 128, VT_ROWS, 128), lambda i: (0, i, 0, 0)),
        pl.BlockSpec((g, tm, 128), lambda i: (0, i, 0)),
        pl.BlockSpec((g, tm // 128, VT_ROWS, 128), lambda i: (0, i, 0, 0)),
        pl.BlockSpec((g, tm, HEAD_DIM), lambda i: (0, i, 0)),
        pl.BlockSpec((g, tm, HEAD_DIM), lambda i: (0, i, 0)),
        pl.BlockSpec((tm, LRU_WIDTH), lambda i: (i, 0)),
    ]
    return pl.pallas_call(
        _front_kernel,
        grid=(s // tm,),
        in_specs=[pl.BlockSpec((tm, D_MODEL), lambda i: (i, 0)),
                  pl.BlockSpec((1, D_MODEL), lambda i: (0, 0)),
                  pl.BlockSpec(w_front.shape, lambda i: (0, 0)),
                  pl.BlockSpec((NSA_WIDTH, 1), lambda i: (0, 0)),
                  pl.BlockSpec((3, 128), lambda i: (0, 0))],
        out_specs=out_specs,
        out_shape=outs,
        compiler_params=_cparams(1),
        name="front",
    )(x2, norm_gain.reshape(1, D_MODEL), w_front, qgain_col, kgain_rows)


def _compress_kernel(xk_ref, xv_ref, w1_ref, b1_ref, w2_ref, pe_ref, kg_ref, kc_ref, vct_ref):
    ncp = xk_ref.shape[0]
    half = CMP_STRIDE * HEAD_DIM
    lane = lax.broadcasted_iota(jnp.int32, (ncp, 128), 1)
    lo = lane < HEAD_DIM

    def phi(x_ref, j):
        x = x_ref[...].astype(BF16)
        w1 = w1_ref[j].astype(BF16)
        a = _dot(x, w1[:half])
        b = _dot(x, w1[half:])
        b_next = pltpu.roll(b, ncp - 1, 0)
        pe8 = jnp.broadcast_to(pe_ref[j], (8, 2 * half)).astype(BF16)
        pe_term = _dot(pe8, w1)[0:1]
        hid = a + b_next + pe_term + b1_ref[j]
        act = (hid * jax.nn.sigmoid(hid)).astype(BF16)
        return _dot(act, w2_ref[j].astype(BF16))

    kc = phi(xk_ref, 0)
    ms = jnp.sum(kc * kc, axis=-1, keepdims=True) / HEAD_DIM
    kc = kc * lax.rsqrt(ms + EPS) * kg_ref[...]
    hi = kc.astype(BF16).astype(F32)
    lo_part = (kc - hi).astype(BF16).astype(F32)
    ones_cols = jnp.where((lane == HEAD_DIM) | (lane == HEAD_DIM + 1), 1.0, 0.0)
    kc_ref[:, 0:128] = (hi + pltpu.roll(lo_part, HEAD_DIM, 1)).astype(BF16)
    kc_ref[:, 128:256] = jnp.where(lo, hi, ones_cols).astype(BF16)
    vt = phi(xv_ref, 1).T
    row_t = lax.broadcasted_iota(jnp.int32, vt.shape, 0)
    vct_ref[...] = jnp.where(row_t == HEAD_DIM, 1.0, vt).astype(BF16)


def _compress(xk, xv, w1, b1, w2p, pe_flat, kgain_row):
    g, ncp, _ = xk.shape
    return pl.pallas_call(
        _compress_kernel,
        grid=(g,),
        in_specs=[pl.BlockSpec((None, ncp, CMP_STRIDE * HEAD_DIM), lambda i: (i, 0, 0)),
                  pl.BlockSpec((None, ncp, CMP_STRIDE * HEAD_DIM), lambda i: (i, 0, 0)),
                  pl.BlockSpec(w1.shape, lambda i: (0, 0, 0)),
                  pl.BlockSpec(b1.shape, lambda i: (0, 0, 0)),
                  pl.BlockSpec(w2p.shape, lambda i: (0, 0, 0)),
                  pl.BlockSpec(pe_flat.shape, lambda i: (0, 0, 0)),
                  pl.BlockSpec((1, 128), lambda i: (0, 0))],
        out_specs=[pl.BlockSpec((None, ncp, 256), lambda i: (i, 0, 0)),
                   pl.BlockSpec((None, 128, ncp), lambda i: (i, 0, 0))],
        out_shape=[jax.ShapeDtypeStruct((g, ncp, 256), BF16),
                   jax.ShapeDtypeStruct((g, 128, ncp), BF16)],
        compiler_params=_cparams(1),
        name="compress",
    )(xk, xv, w1, b1, w2p, pe_flat, kgain_row)


def _heads_to_lanes(q):
    return jnp.concatenate([q[r * HEAD_DIM:(r + 1) * HEAD_DIM] for r in range(GROUP_SIZE)], axis=1)


def _store_heads(out_ref, o):
    for r in range(GROUP_SIZE):
        out_ref[r * HEAD_DIM:(r + 1) * HEAD_DIM, :] = o[0:HEAD_DIM, r * Q_BLOCK:(r + 1) * Q_BLOCK]


def _bias_tile(w_ref, off, n, head0=0):
    return jnp.concatenate([w_ref[head0 + r, pl.ds(off, n), :] for r in range(GROUP_SIZE)], axis=1)


def _cmp_kernel(qhi_ref, qlo_ref, kc_ref, vct_ref, wc_ref, ext_ref, ocmp_ref, neg_ref, s_ref, imp_ref):
    qb = pl.program_id(0)
    ncp = kc_ref.shape[1]
    nsbp = neg_ref.shape[1]
    gr = GROUP_SIZE * HEAD_DIM
    ws = pl.multiple_of(jnp.maximum(8 * qb - 16, 0), 8)
    off = pl.multiple_of(jnp.maximum(16 - 8 * qb, 0), 8)
    lim = ws + CMP_NEAR
    n_cls = min(4, ncp // 128)
    per_cls = ncp // n_cls

    def attend(nrows):
        nblk = nrows // SEL_PER_CMP
        band = nrows - per_cls
        rows = lax.broadcasted_iota(jnp.int32, (per_cls, GQ), 0) + band
        blk = lax.broadcasted_iota(jnp.int32, (nblk, Q_BLOCK), 0)
        lane = lax.broadcasted_iota(jnp.int32, (nblk, Q_BLOCK), 1)
        cur = 2 * qb + jnp.where(lane >= SEL_BLOCK, 1, 0)
        forced = jnp.where(blk == 0, 1.0, jnp.where(blk == cur, 1.0, jnp.where(blk == cur - 1, 1.0, 0.0)))
        carry = []
        for g in range(N_GROUPS):
            qhi = qhi_ref[g * gr:(g + 1) * gr, :]
            qlo = qlo_ref[g * gr:(g + 1) * gr, :]
            cols = []
            for r in range(GROUP_SIZE):
                a = qhi[r * HEAD_DIM:(r + 1) * HEAD_DIM]
                b = qlo[r * HEAD_DIM:(r + 1) * HEAD_DIM]
                cols.append(jnp.concatenate([a, a, b], axis=0))
            qcat = jnp.concatenate([jnp.concatenate(cols, axis=1), ext_ref[g]], axis=0)
            s_g = s_ref.at[g]
            s_g[0:nrows, :] = _dot(kc_ref[g, 0:nrows, :], qcat)
            s_g[pl.ds(ws, CMP_NEAR), :] += _bias_tile(wc_ref, off, CMP_NEAR, g * GROUP_SIZE)
            s_g[band:nrows, :] = jnp.where(rows < lim, s_g[band:nrows, :], NEG)
            s = s_g[0:nrows, :]
            m = jnp.maximum(jnp.max(s, axis=0, keepdims=True), M_INIT)
            p = jnp.exp2(s - m)
            acc = _dot(vct_ref[g, 0:VT_ROWS, 0:nrows], p.astype(BF16))
            l = acc[HEAD_DIM:HEAD_DIM + 1]
            inv_l = jnp.where(l > 0.0, 1.0 / l, 0.0)
            _store_heads(ocmp_ref.at[g * gr:(g + 1) * gr], acc[0:HEAD_DIM] * inv_l)
            imp = None
            for r in range(GROUP_SIZE):
                cs = slice(r * Q_BLOCK, (r + 1) * Q_BLOCK)
                term = p[:, cs] * inv_l[:, cs]
                imp = term if imp is None else imp + term
            imp_g = imp_ref.at[g]
            imp_g[0:nrows, :] = imp
            strided = [imp_g[pl.ds(k, nblk, stride=SEL_PER_CMP), :] for k in range(SEL_PER_CMP)]
            prev_last = jnp.where(blk == 0, 0.0, pltpu.roll(strided[SEL_PER_CMP - 1], 1, 0))
            impb = prev_last + strided[0] + strided[1] + strided[2] + strided[3]
            carry.append(jnp.where(blk < cur - 1, jnp.where(blk > 0, impb, -1.0), -1.0))

        def pick_one(_, cands):
            out = []
            for sc in cands:
                mx = jnp.max(sc, axis=0, keepdims=True)
                hit_blk = jnp.where(sc == mx, jnp.where(mx >= 0.0, blk, nblk), nblk)
                first = jnp.min(hit_blk, axis=0, keepdims=True)
                out.append(jnp.where(blk == first, PICKED, sc))
            return tuple(out)

        if nblk * N_GROUPS <= 2 * SUPER:
            carry = lax.fori_loop(0, N_PICK, pick_one, tuple(carry))
        else:
            carry = [lax.fori_loop(0, N_PICK, pick_one, (c,))[0] for c in carry]
        for g in range(N_GROUPS):
            picked = carry[g] == PICKED
            sel_neg = jnp.where(picked, 0.0, jnp.where(blk <= cur, jnp.where(forced > 0.5, 0.0, NEG), NEG))
            neg_ref[g, 0:nblk, :] = sel_neg.astype(BF16)
            if nblk < nsbp:
                neg_ref[g, nblk:nsbp, :] = jnp.full((nsbp - nblk, Q_BLOCK), NEG, BF16)

    cls = (lim + per_cls - 1) // per_cls
    for k in range(1, n_cls + 1):
        pl.when(cls == k)(functools.partial(attend, k * per_cls))


def _cmp_attention(qhi_t, qlo_t, kc_cat, vct, wc_t, ext_q, nsbp):
    s = qhi_t.shape[1]
    g = N_GROUPS
    ncp = kc_cat.shape[1]
    full = lambda a: pl.BlockSpec(a.shape, lambda qb: (0,) * a.ndim)
    return pl.pallas_call(
        _cmp_kernel,
        grid=(s // Q_BLOCK,),
        in_specs=[pl.BlockSpec((NSA_WIDTH, Q_BLOCK), lambda qb: (0, qb)),
                  pl.BlockSpec((NSA_WIDTH, Q_BLOCK), lambda qb: (0, qb)),
                  full(kc_cat), full(vct), full(wc_t), full(ext_q)],
        out_specs=[pl.BlockSpec((NSA_WIDTH, Q_BLOCK), lambda qb: (0, qb)),
                   pl.BlockSpec((g, nsbp, Q_BLOCK), lambda qb: (0, 0, qb))],
        out_shape=[jax.ShapeDtypeStruct((NSA_WIDTH, s), F32),
                   jax.ShapeDtypeStruct((g, nsbp, s), BF16)],
        scratch_shapes=[pltpu.VMEM((g, ncp, GQ), F32), pltpu.VMEM((g, ncp, Q_BLOCK), F32)],
        compiler_params=_cparams(1),
        name="cmp_select",
    )(qhi_t, qlo_t, kc_cat, vct, wc_t, ext_q)


def _sel_kernel(qhi_ref, neg_ref, kaug_ref, vt512_ref, vt128_ref, ws_ref, ext_ref,
                kwin_ref, vwt_ref, ww_ref, out_ref, owin_ref,
                qaug_ref, m_ref, acc_ref, s0_ref, s1_ref, s2_ref, cm0_ref, cm1_ref, cm2_ref):
    qb = pl.program_id(1)
    nsbp = neg_ref.shape[0]
    nsc = nsbp // SUPER
    qg = _heads_to_lanes(qhi_ref[...])
    qx = jnp.concatenate([qg, ext_ref[...]], axis=0)
    blk = lax.broadcasted_iota(jnp.int32, (SUPER, Q_BLOCK), 0)

    def aug(neg_rows):
        tiled = jnp.concatenate([neg_rows] * GROUP_SIZE, axis=1).astype(BF16)
        return jnp.concatenate([tiled, qx], axis=0)

    for sc in range(nsc):
        neg_rows = neg_ref[sc * SUPER:(sc + 1) * SUPER, :].astype(F32)
        qaug_ref[sc] = aug(jnp.where(blk + sc * SUPER >= 2 * qb - 2, NEG, neg_rows))

    m_ref[...] = jnp.full(m_ref.shape, M_INIT, F32)
    acc_ref[...] = jnp.zeros(acc_ref.shape, F32)

    def update(s, col_max, vt):
        m_old = m_ref[...]
        m_new = jnp.maximum(m_old, col_max)
        p = jnp.exp2(s - m_new).astype(BF16)
        acc_ref[...] = acc_ref[...] * jnp.exp2(m_old - m_new) + _dot(vt, p)
        m_ref[...] = m_new

    n_far = (jnp.maximum(qb - 1, 0) * Q_BLOCK + KV_CHUNK - 1) // KV_CHUNK

    def scores(c, dst_ref, max_ref):
        k = kaug_ref[pl.ds(pl.multiple_of(c * KV_CHUNK, KV_CHUNK), KV_CHUNK), :]
        sc = (c * (KV_CHUNK // SEL_BLOCK)) // SUPER
        s = _dot(k, qaug_ref[sc])
        dst_ref[...] = s
        max_ref[...] = jnp.max(s, axis=0, keepdims=True)

    kb0 = jnp.maximum(qb - 1, 0)
    ws = pl.multiple_of(kb0 * Q_BLOCK, Q_BLOCK)
    off = pl.multiple_of(jnp.where(qb == 0, Q_BLOCK, 0), Q_BLOCK)
    b_lo = 2 * kb0
    sc_lo = pl.multiple_of((b_lo // SUPER) * SUPER, SUPER)
    sc_hi = pl.multiple_of(((b_lo + 3) // SUPER) * SUPER, SUPER)
    neg_lo = neg_ref[pl.ds(sc_lo, SUPER), :].astype(F32)
    neg_hi = neg_ref[pl.ds(sc_hi, SUPER), :].astype(F32)
    near_neg = jnp.where(blk >= SUPER // 2, neg_lo, neg_hi)
    k = kaug_ref[pl.ds(ws, SEL_NEAR), :]
    s_near = _dot(k, aug(near_neg)) + _bias_tile(ws_ref, off, SEL_NEAR)
    vt_near = jnp.concatenate([vt128_ref[kb0], vt128_ref[kb0 + 1]], axis=1)

    bufs = ((s0_ref, cm0_ref), (s1_ref, cm1_ref), (s2_ref, cm2_ref))
    last = jnp.maximum(n_far - 1, 0)
    scores(0, *bufs[0])
    scores(jnp.minimum(1, last), *bufs[1])
    update(s_near, jnp.max(s_near, axis=0, keepdims=True), vt_near)
    _window_branch(qb, qg, kwin_ref, vwt_ref, ww_ref, owin_ref)

    def far_triple(c):
        for j in range(3):
            scores(jnp.minimum(c + j + 2, last), *bufs[(j + 2) % 3])
            s_ref, cm_ref = bufs[j]
            update(s_ref[...], cm_ref[...], vt512_ref[c + j])

    def far_six(i, carry):
        far_triple(6 * i)
        far_triple(6 * i + 3)
        return carry

    lax.fori_loop(0, n_far // 6, far_six, 0)
    done = (n_far // 6) * 6

    @pl.when(n_far - done >= 3)
    def _():
        far_triple(done)

    done = (n_far // 3) * 3
    for j in range(2):
        @pl.when(n_far - done > j)
        def _():
            s_ref, cm_ref = bufs[j]
            update(s_ref[...], cm_ref[...], vt512_ref[done + j])

    acc = acc_ref[...]
    _store_heads(out_ref, acc[0:HEAD_DIM] * (1.0 / acc[HEAD_DIM:HEAD_DIM + 1]))


def _sel_win_attention(qhi_t, neg_t, kaug, vt512, vt128, ws_t, ext_q, kwin, vwt, ww_t):
    s = qhi_t.shape[1]
    g = N_GROUPS
    nqb = s // Q_BLOCK
    nsbp = neg_t.shape[1]
    gr = GROUP_SIZE * HEAD_DIM
    once = pl.Buffered(1)
    return pl.pallas_call(
        _sel_kernel,
        grid=(g, nqb),
        in_specs=[pl.BlockSpec((gr, Q_BLOCK), lambda gi, qb: (gi, qb)),
                  pl.BlockSpec((None, nsbp, Q_BLOCK), lambda gi, qb: (gi, 0, qb)),
                  pl.BlockSpec((None, s, 256), lambda gi, qb: (gi, 0, 0), pipeline_mode=once),
                  pl.BlockSpec((None, s // KV_CHUNK, VT_ROWS, KV_CHUNK), lambda gi, qb: (gi, 0, 0, 0),
                               pipeline_mode=once),
                  pl.BlockSpec((None, s // 128, VT_ROWS, 128), lambda gi, qb: (gi, 0, 0, 0),
                               pipeline_mode=once),
                  pl.BlockSpec((GROUP_SIZE,) + ws_t.shape[1:], lambda gi, qb: (gi, 0, 0)),
                  pl.BlockSpec((None, HEAD_DIM, GQ), lambda gi, qb: (gi, 0, 0)),
                  pl.BlockSpec((None, s, 128), lambda gi, qb: (gi, 0, 0), pipeline_mode=once),
                  pl.BlockSpec((None, s // 128, VT_ROWS, 128), lambda gi, qb: (gi, 0, 0, 0),
                               pipeline_mode=once),
                  pl.BlockSpec((GROUP_SIZE,) + ww_t.shape[1:], lambda gi, qb: (gi, 0, 0))],
        out_specs=[pl.BlockSpec((gr, Q_BLOCK), lambda gi, qb: (gi, qb)),
                   pl.BlockSpec((gr, Q_BLOCK), lambda gi, qb: (gi, qb))],
        out_shape=[jax.ShapeDtypeStruct((NSA_WIDTH, s), F32),
                   jax.ShapeDtypeStruct((NSA_WIDTH, s), F32)],
        scratch_shapes=[pltpu.VMEM((nsbp // SUPER, 256, GQ), BF16),
                        pltpu.VMEM((1, GQ), F32),
                        pltpu.VMEM((VT_ROWS, GQ), F32),
                        pltpu.VMEM((KV_CHUNK, GQ), F32),
                        pltpu.VMEM((KV_CHUNK, GQ), F32),
                        pltpu.VMEM((KV_CHUNK, GQ), F32),
                        pltpu.VMEM((1, GQ), F32),
                        pltpu.VMEM((1, GQ), F32),
                        pltpu.VMEM((1, GQ), F32)],
        compiler_params=_cparams(2),
        name="sel_win_attention",
    )(qhi_t, neg_t, kaug, vt512, vt128, ws_t, ext_q, kwin, vwt, ww_t)


def _window_branch(qb, qg, kwin_ref, vwt_ref, ww_ref, out_ref):
    kb0 = jnp.maximum(qb - WINDOW // Q_BLOCK, 0)
    ws = pl.multiple_of(kb0 * Q_BLOCK, Q_BLOCK)
    off = pl.multiple_of(jnp.maximum(WINDOW - qb * Q_BLOCK, 0), Q_BLOCK)
    qx = jnp.concatenate([qg, jnp.zeros_like(qg)], axis=0)
    s = _dot(kwin_ref[pl.ds(ws, WIN_KEYS), :], qx) + _bias_tile(ww_ref, off, WIN_KEYS)
    m = jnp.max(s, axis=0, keepdims=True)
    p = jnp.exp2(s - m).astype(BF16)
    vt = jnp.concatenate([vwt_ref[kb0 + j] for j in range(WIN_KEYS // Q_BLOCK)], axis=1)
    acc = _dot(vt, p)
    _store_heads(out_ref, acc[0:HEAD_DIM] * (1.0 / acc[HEAD_DIM:HEAD_DIM + 1]))


LRU_TM = 256


def _lru_kernel(u_ref, cw_ref, cb_ref, wa_ref, ba_ref, wx_ref, bx_ref, lam_ref, h_ref,
                tail_ref, hprev_ref):
    tm = LRU_TM

    @pl.when(pl.program_id(0) == 0)
    def _():
        tail_ref[...] = jnp.zeros(tail_ref.shape, F32)
        hprev_ref[...] = jnp.zeros(hprev_ref.shape, F32)

    u = u_ref[...]
    tail = tail_ref[...]
    row8 = lax.broadcasted_iota(jnp.int32, (8, LRU_WIDTH), 0)
    uc = cb_ref[...] + u * cw_ref[CONV_WIDTH - 1:CONV_WIDTH, :]
    for j in range(1, CONV_WIDTH):
        sh = pltpu.roll(u, j, 0)
        head = jnp.where(row8 < j, pltpu.roll(tail, j, 0), sh[0:8])
        sh = jnp.concatenate([head, sh[8:]], axis=0)
        uc = uc + sh * cw_ref[CONV_WIDTH - 1 - j:CONV_WIDTH - j, :]
    tail_ref[...] = u[tm - 8:tm]

    ucb = uc.astype(BF16)
    r = jax.nn.sigmoid(_dot(ucb, wa_ref[...]) + ba_ref[...])
    ig = jax.nn.sigmoid(_dot(ucb, wx_ref[...]) + bx_ref[...])
    nl = -lam_ref[...]
    softplus = jnp.maximum(nl, 0.0) + jnp.log(1.0 + jnp.exp(-jnp.abs(nl)))
    log_a = (-LRU_C * r) * softplus
    a = jnp.exp(log_a)
    b = jnp.sqrt(1.0 - jnp.exp(2.0 * log_a)) * (ig * uc)

    row = lax.broadcasted_iota(jnp.int32, (tm, LRU_WIDTH), 0)
    step = 1
    while step < tm:
        a_s = pltpu.roll(a, step, 0)
        b_s = pltpu.roll(b, step, 0)
        ok = row >= step
        b = jnp.where(ok, a * b_s + b, b)
        a = jnp.where(ok, a * a_s, a)
        step *= 2
    h = a * hprev_ref[...] + b
    h_ref[...] = h
    hprev_ref[...] = h[tm - 1:tm]


def _rglru(u, conv_w, conv_b, wa_bd, ba, wx_bd, bx, lam):
    s = u.shape[0]
    tm = LRU_TM
    row = lambda v: v.reshape(1, LRU_WIDTH)
    full = lambda shape: pl.BlockSpec(shape, lambda i: (0, 0))
    return pl.pallas_call(
        _lru_kernel,
        grid=(s // tm,),
        in_specs=[pl.BlockSpec((tm, LRU_WIDTH), lambda i: (i, 0)),
                  full((CONV_WIDTH, LRU_WIDTH)), full((1, LRU_WIDTH)),
                  full((LRU_WIDTH, LRU_WIDTH)), full((1, LRU_WIDTH)),
                  full((LRU_WIDTH, LRU_WIDTH)), full((1, LRU_WIDTH)),
                  full((1, LRU_WIDTH))],
        out_specs=pl.BlockSpec((tm, LRU_WIDTH), lambda i: (i, 0)),
        out_shape=jax.ShapeDtypeStruct((s, LRU_WIDTH), F32),
        scratch_shapes=[pltpu.VMEM((8, LRU_WIDTH), F32), pltpu.VMEM((1, LRU_WIDTH), F32)],
        compiler_params=_cparams(1),
        name="rglru",
    )(u, conv_w, row(conv_b), wa_bd, row(ba), wx_bd, row(bx), row(lam))


OUT_TM = 256


def _out_kernel(x_ref, g_ref, oc_ref, os_ref, ow_ref, brt_ref, h_ref,
                wg_ref, wpa_ref, wpb_ref, wo_ref, y_ref):
    h_in = _normed_input(x_ref, g_ref)
    gate = lambda j: _dot(h_in, wg_ref[:, GATE_OFFS[j]:GATE_OFFS[j + 1]])
    brt = brt_ref[...]
    parts = []
    for h in range(N_HEADS):
        rows = slice(h * HEAD_DIM, (h + 1) * HEAD_DIM)
        acc = None
        for b, o_ref in enumerate((oc_ref, os_ref, ow_ref)):
            term = brt[b * N_HEADS + h:b * N_HEADS + h + 1, :] * o_ref[rows, :]
            acc = term if acc is None else acc + term
        parts.append(acc)
    gn = gate(0)
    ya = jnp.concatenate(parts, axis=0).T * (gn * jax.nn.sigmoid(gn))
    y_a = _dot(ya.astype(BF16), wpa_ref[...])
    gl = gate(1)
    y_b = _dot((h_ref[...] * (gl * jax.nn.sigmoid(gl))).astype(BF16), wpb_ref[...])
    mg = gate(2)
    m = jax.nn.sigmoid(mg[:, :D_MODEL]) * y_a + jax.nn.sigmoid(mg[:, D_MODEL:]) * y_b
    y_ref[...] = x_ref[...] + _dot(m.astype(BF16), wo_ref[...])


def _output(x2, norm_gain, oc_t, os_t, ow_t, br_t, h_lru, w_gate, wpa, wpb, wo):
    s = x2.shape[0]
    tm = OUT_TM
    rows = lambda n: pl.BlockSpec((tm, n), lambda i: (i, 0))
    cols = lambda n: pl.BlockSpec((n, tm), lambda i: (0, i))
    full = lambda a: pl.BlockSpec(a.shape, lambda i: (0, 0))
    return pl.pallas_call(
        _out_kernel,
        grid=(s // tm,),
        in_specs=[rows(D_MODEL), pl.BlockSpec((1, D_MODEL), lambda i: (0, 0)),
                  cols(NSA_WIDTH), cols(NSA_WIDTH), cols(NSA_WIDTH), cols(32), rows(LRU_WIDTH),
                  full(w_gate), full(wpa), full(wpb), full(wo)],
        out_specs=rows(D_MODEL),
        out_shape=jax.ShapeDtypeStruct((s, D_MODEL), F32),
        compiler_params=_cparams(1),
        name="output",
    )(x2, norm_gain.reshape(1, D_MODEL), oc_t, os_t, ow_t, br_t, h_lru, w_gate, wpa, wpb, wo)


def _t5_bucket_table(n_dist):
    n = np.arange(n_dist)
    max_exact = N_BUCKETS // 2
    nf = np.maximum(n, 1).astype(np.float32)
    large = max_exact + (np.log(nf / np.float32(max_exact)) / np.float32(math.log(MAX_DISTANCE / max_exact))
                         * np.float32(N_BUCKETS - max_exact)).astype(np.int32)
    return np.where(n < max_exact, n, np.minimum(large, N_BUCKETS - 1))


def _bias_tiles_kernel(relb_ref, bw_ref, bs_ref, bc_ref, ww_ref, ws_ref, wc_ref):
    h = pl.program_id(0)
    far = relb_ref[N_BUCKETS - 1, h]
    for bk_ref, out_ref, rel in ((bw_ref, ww_ref, False), (bs_ref, ws_ref, True), (bc_ref, wc_ref, True)):
        bk = bk_ref[...]
        acc = jnp.full(bk.shape, NEG, F32)
        for b in range(N_BUCKETS):
            val = relb_ref[b, h] - far if rel else relb_ref[b, h]
            acc = jnp.where(bk == b, val * LOG2E, acc)
        out_ref[...] = acc


def _bias_tables(rel_bias):
    bucket = _t5_bucket_table(2 * WINDOW)
    i = np.arange(Q_BLOCK)[None, :]

    def index_tile(dist, valid):
        return jnp.asarray(np.where(valid, bucket[np.clip(dist, 0, bucket.size - 1)], -1).astype(np.int32))

    u = np.arange(WINDOW + WIN_KEYS)[:, None]
    d_win = i - u + WINDOW
    bw = index_tile(d_win, (d_win >= 0) & (d_win < WINDOW))
    u = np.arange(SEL_NEAR + Q_BLOCK)[:, None]
    d_sel = i - u + Q_BLOCK
    bs = index_tile(d_sel, d_sel >= 0)
    u = np.arange(CMP_NEAR + 16)[:, None]
    d_cmp = i - CMP_STRIDE * (u - 16) - (CMP_BLOCK - 1)
    bc = index_tile(d_cmp, d_cmp >= 0)
    full = lambda a: pl.BlockSpec(a.shape, lambda h: (0, 0))
    per_head = lambda a: pl.BlockSpec((None,) + a.shape, lambda h: (h, 0, 0))
    ww, ws, wc = pl.pallas_call(
        _bias_tiles_kernel,
        grid=(N_HEADS,),
        in_specs=[pl.BlockSpec(memory_space=pltpu.SMEM), full(bw), full(bs), full(bc)],
        out_specs=[per_head(bw), per_head(bs), per_head(bc)],
        out_shape=[jax.ShapeDtypeStruct((N_HEADS,) + a.shape, F32) for a in (bw, bs, bc)],
        compiler_params=_cparams(1),
        name="bias_tiles",
    )(rel_bias, bw, bs, bc)
    b_far = rel_bias[N_BUCKETS - 1] * LOG2E
    hi = b_far.astype(BF16)
    lo = (b_far - hi.astype(F32)).astype(BF16)
    ext = jnp.zeros((N_HEADS, HEAD_DIM, Q_BLOCK), BF16)
    ext = ext.at[:, 0, :].set(hi[:, None]).at[:, 1, :].set(lo[:, None])
    ext = ext.reshape(N_GROUPS, GROUP_SIZE, HEAD_DIM, Q_BLOCK).transpose(0, 2, 1, 3)
    return ww, ws, wc, ext.reshape(N_GROUPS, HEAD_DIM, GQ)


def _block_diag(w):
    n, d, e = w.shape
    eye = jnp.eye(n, dtype=w.dtype)
    return (eye[:, None, :, None] * w[:, :, None, :]).reshape(n * d, n * e)


def kernel(x, norm_gain, w_in, q_norm_gain, k_norm_gain, cmp_pe, cmp_w1, cmp_b1, cmp_w2, rel_bias,
           conv_w, conv_b, lru_wa, lru_ba, lru_wx, lru_bx, lru_lambda, w_proj_a, w_proj_b, w_out):
    bsz, s, _ = x.shape
    assert bsz == 1 and s % 1024 == 0 and s >= 1024
    x2 = x.reshape(s, D_MODEL)
    nsb = s // SEL_BLOCK
    nsbp = -(-nsb // SUPER) * SUPER
    ncp = s // CMP_STRIDE

    o = IN_OFFS
    w16 = w_in.astype(BF16)
    pad = jnp.zeros((D_MODEL, BR_PAD - 3 * N_HEADS), BF16)
    w_front = jnp.concatenate([w16[:, o[0]:o[2]], w16[:, o[3]:o[4]], pad, w16[:, o[4]:o[5]]], axis=1)
    w_gate = jnp.concatenate([w16[:, o[2]:o[3]], w16[:, o[5]:o[7]]], axis=1)

    scale = HEAD_DIM ** -0.5 * LOG2E
    qgain_col = jnp.tile(q_norm_gain * scale, N_HEADS).reshape(NSA_WIDTH, 1)
    kgain_rows = jnp.tile(k_norm_gain, (1, N_GROUPS))
    (qhi_t, qlo_t, br_t, kaug, vt512, vt128, kwin, vwt, kcr, vcr, u_lru) = _front(
        x2, norm_gain, w_front, qgain_col, kgain_rows)

    w2p = jnp.pad(cmp_w2, ((0, 0), (0, 0), (0, 128 - HEAD_DIM)))
    kc_gain = jnp.pad(k_norm_gain[0], (0, 128 - HEAD_DIM)).reshape(1, 128)
    kc_cat, vct = _compress(kcr.reshape(N_GROUPS, ncp, CMP_STRIDE * HEAD_DIM),
                            vcr.reshape(N_GROUPS, ncp, CMP_STRIDE * HEAD_DIM),
                            cmp_w1, cmp_b1.reshape(2, 1, CMP_HIDDEN), w2p,
                            cmp_pe.reshape(2, 1, CMP_BLOCK * HEAD_DIM), kc_gain)

    ww_t, ws_t, wc_t, ext_q = _bias_tables(rel_bias)
    oc_t, neg_t = _cmp_attention(qhi_t, qlo_t, kc_cat, vct, wc_t, ext_q, nsbp)
    os_t, ow_t = _sel_win_attention(qhi_t, neg_t, kaug, vt512, vt128, ws_t, ext_q, kwin, vwt, ww_t)

    h_lru = _rglru(u_lru, conv_w, conv_b, _block_diag(lru_wa).astype(BF16), lru_ba,
                   _block_diag(lru_wx).astype(BF16), lru_bx, lru_lambda)

    y = _output(x2, norm_gain, oc_t, os_t, ow_t, br_t, h_lru, w_gate,
                w_proj_a.astype(BF16), w_proj_b.astype(BF16), w_out.astype(BF16))
    return y.reshape(bsz, s, D_MODEL)
```

```python
import functools
import math

import numpy as np
import jax
import jax.numpy as jnp
from jax import lax
from jax.experimental import pallas as pl
from jax.experimental.pallas import tpu as pltpu

F32 = jnp.float32
BF16 = jnp.bfloat16

D_MODEL = 1024
N_HEADS = 8
N_GROUPS = 2
GROUP_SIZE = N_HEADS // N_GROUPS
HEAD_DIM = 64
NSA_WIDTH = N_HEADS * HEAD_DIM
KV_WIDTH = N_GROUPS * HEAD_DIM
CMP_STRIDE = 16
CMP_BLOCK = 32
CMP_HIDDEN = 256
SEL_BLOCK = 64
SEL_PER_CMP = SEL_BLOCK // CMP_STRIDE
N_SELECT = 16
WINDOW = 512
Q_BLOCK = 128
LRU_WIDTH = 512
LRU_BLOCKS = 8
CONV_WIDTH = 4
LRU_C = 8.0
N_BUCKETS = 32
MAX_DISTANCE = 128
EPS = 1e-6
NEG = -1e30
M_INIT = -5e29
LOG2E = 1.4426950408889634

GQ = GROUP_SIZE * Q_BLOCK
SUPER = 128
KV_CHUNK = 512
SEL_NEAR = 2 * Q_BLOCK
WIN_KEYS = WINDOW + Q_BLOCK
CMP_NEAR = 24
N_PICK = N_SELECT - 3
PICKED = -2.0
VT_ROWS = 80

IN_OFFS = tuple(int(v) for v in np.cumsum(
    (0, NSA_WIDTH, 6 * KV_WIDTH, NSA_WIDTH, 3 * N_HEADS, LRU_WIDTH, LRU_WIDTH, 2 * D_MODEL)))
BR_PAD = 128
FRONT_OFFS = tuple(int(v) for v in np.cumsum((0, NSA_WIDTH, 6 * KV_WIDTH, BR_PAD, LRU_WIDTH)))
GATE_OFFS = tuple(int(v) for v in np.cumsum((0, NSA_WIDTH, LRU_WIDTH, 2 * D_MODEL)))

VMEM_LIMIT = 56 * 1024 * 1024


def _cparams(n_axes):
    return pltpu.CompilerParams(dimension_semantics=("arbitrary",) * n_axes,
                                vmem_limit_bytes=VMEM_LIMIT)


def _dot(a, b):
    return jnp.dot(a, b, preferred_element_type=F32)


def _sigmoid(x):
    return 0.5 * jnp.tanh(0.5 * x) + 0.5


FRONT_TM = 512


def _normed_input(x_ref, g_ref):
    x = x_ref[...]
    ms = jnp.mean(x * x, axis=-1, keepdims=True)
    return (x * lax.rsqrt(ms + EPS) * g_ref[...]).astype(BF16)


def _group_rms(k, gain_row):
    sq = k * k
    lane = lax.broadcasted_iota(jnp.int32, k.shape, 1)
    lo = lane < HEAD_DIM
    s0 = jnp.sum(jnp.where(lo, sq, 0.0), axis=-1, keepdims=True)
    s1 = jnp.sum(jnp.where(lo, 0.0, sq), axis=-1, keepdims=True)
    inv = jnp.where(lo, lax.rsqrt(s0 / HEAD_DIM + EPS), lax.rsqrt(s1 / HEAD_DIM + EPS))
    return k * inv * gain_row


def _front_kernel(x_ref, g_ref, w_ref, qg_ref, kg_ref,
                  qhi_ref, qlo_ref, brt_ref, kaug_ref, vt512_ref, vt128_ref,
                  kwin_ref, vwt_ref, kvc_ref, u_ref):
    i = pl.program_id(0)
    tm = FRONT_TM
    h_in = _normed_input(x_ref, g_ref)
    proj = lambda a, b: _dot(h_in, w_ref[:, a:b])
    qt = proj(FRONT_OFFS[0], FRONT_OFFS[1]).T
    for h in range(N_HEADS):
        blk = qt[h * HEAD_DIM:(h + 1) * HEAD_DIM]
        ms = jnp.mean(blk * blk, axis=0, keepdims=True)
        qn = blk * lax.rsqrt(ms + EPS) * qg_ref[h * HEAD_DIM:(h + 1) * HEAD_DIM, :]
        hi = qn.astype(BF16)
        qhi_ref[h * HEAD_DIM:(h + 1) * HEAD_DIM, :] = hi
        qlo_ref[h * HEAD_DIM:(h + 1) * HEAD_DIM, :] = (qn - hi.astype(F32)).astype(BF16)
    brt_ref[...] = _sigmoid(proj(FRONT_OFFS[2], FRONT_OFFS[3])).T[:32]
    u_ref[...] = proj(FRONT_OFFS[3], FRONT_OFFS[4])

    kv = proj(FRONT_OFFS[1], FRONT_OFFS[2])
    piece = lambda j: kv[:, j * KV_WIDTH:(j + 1) * KV_WIDTH]
    lane = lax.broadcasted_iota(jnp.int32, (tm, 128), 1)
    row = lax.broadcasted_iota(jnp.int32, (tm, 128), 0) + i * tm
    lo = lane < HEAD_DIM
    ones_cols = jnp.where((lane == HEAD_DIM) | (lane == HEAD_DIM + 1), 1.0, 0.0)
    onehot = jnp.where(lane == (row // SEL_BLOCK) % SUPER, 1.0, 0.0).astype(BF16)
    kvc_ref[0] = piece(0)
    kvc_ref[1] = piece(1)
    kslc = _group_rms(piece(2), kg_ref[1:2, :])
    kwin = _group_rms(piece(4), kg_ref[2:3, :])
    vslt = piece(3).T
    vwit = piece(5).T
    row_t = lax.broadcasted_iota(jnp.int32, (VT_ROWS - HEAD_DIM, tm), 0)
    ones_rows = jnp.where(row_t == 0, 1.0, 0.0)
    for g in range(N_GROUPS):
        sh = lambda a: a if g == 0 else pltpu.roll(a, HEAD_DIM, 1)
        kaug_ref[g, :, 0:128] = onehot
        kaug_ref[g, :, 128:256] = jnp.where(lo, sh(kslc), ones_cols).astype(BF16)
        kwin_ref[g] = jnp.where(lo, sh(kwin), 0.0).astype(BF16)
        vs = jnp.concatenate([vslt[g * HEAD_DIM:(g + 1) * HEAD_DIM], ones_rows], axis=0).astype(BF16)
        vw = jnp.concatenate([vwit[g * HEAD_DIM:(g + 1) * HEAD_DIM], ones_rows], axis=0).astype(BF16)
        for j in range(tm // KV_CHUNK):
            vt512_ref[g, j] = vs[:, j * KV_CHUNK:(j + 1) * KV_CHUNK]
        for j in range(tm // 128):
            vt128_ref[g, j] = vs[:, j * 128:(j + 1) * 128]
            vwt_ref[g, j] = vw[:, j * 128:(j + 1) * 128]


def _front(x2, norm_gain, w_front, qgain_col, kgain_rows):
    s = x2.shape[0]
    tm = FRONT_TM
    g = N_GROUPS
    outs = [
        jax.ShapeDtypeStruct((NSA_WIDTH, s), BF16),
        jax.ShapeDtypeStruct((NSA_WIDTH, s), BF16),
        jax.ShapeDtypeStruct((32, s), F32),
        jax.ShapeDtypeStruct((g, s, 256), BF16),
        jax.ShapeDtypeStruct((g, s // KV_CHUNK, VT_ROWS, KV_CHUNK), BF16),
        jax.ShapeDtypeStruct((g, s // 128, VT_ROWS, 128), BF16),
        jax.ShapeDtypeStruct((g, s, 128), BF16),
        jax.ShapeDtypeStruct((g, s // 128, VT_ROWS, 128), BF16),
        jax.ShapeDtypeStruct((2, s, KV_WIDTH), F32),
        jax.ShapeDtypeStruct((s, LRU_WIDTH), F32),
    ]
    out_specs = [
        pl.BlockSpec((NSA_WIDTH, tm), lambda i: (0, i)),
        pl.BlockSpec((NSA_WIDTH, tm), lambda i: (0, i)),
        pl.BlockSpec((32, tm), lambda i: (0, i)),
        pl.BlockSpec((g, tm, 256), lambda i: (0, i, 0)),
        pl.BlockSpec((g, tm // KV_CHUNK, VT_ROWS, KV_CHUNK), lambda i: (0, i, 0, 0)),
        pl.BlockSpec((g, tm // 128, VT_ROWS, 128), lambda i: (0, i, 0, 0)),
        pl.BlockSpec((g, tm, 128), lambda i: (0, i, 0)),
        pl.BlockSpec((g, tm // 128, VT_ROWS, 128), lambda i: (0, i, 0, 0)),
        pl.BlockSpec((2, tm, KV_WIDTH), lambda i: (0, i, 0)),
        pl.BlockSpec((tm, LRU_WIDTH), lambda i: (i, 0)),
    ]
    return pl.pallas_call(
        _front_kernel,
        grid=(s // tm,),
        in_specs=[pl.BlockSpec((tm, D_MODEL), lambda i: (i, 0)),
                  pl.BlockSpec((1, D_MODEL), lambda i: (0, 0)),
                  pl.BlockSpec(w_front.shape, lambda i: (0, 0)),
                  pl.BlockSpec((NSA_WIDTH, 1), lambda i: (0, 0)),
                  pl.BlockSpec((3, 128), lambda i: (0, 0))],
        out_specs=out_specs,
        out_shape=outs,
        compiler_params=_cparams(1),
        name="front",
    )(x2, norm_gain.reshape(1, D_MODEL), w_front, qgain_col, kgain_rows)


def _compress_kernel(kvc_ref, w1_ref, b1_ref, w2_ref, pe_ref, kg_ref, kc_ref, vct_ref):
    ncp = kc_ref.shape[1]
    half = CMP_STRIDE * HEAD_DIM
    lane = lax.broadcasted_iota(jnp.int32, (ncp, 128), 1)
    lo = lane < HEAD_DIM

    def chunk_rows(j, g):
        cols = []
        for pair in range(CMP_STRIDE // 2):
            even = kvc_ref[j, pl.ds(2 * pair, ncp, stride=CMP_STRIDE), :]
            odd = kvc_ref[j, pl.ds(2 * pair + 1, ncp, stride=CMP_STRIDE), :]
            if g == 0:
                cols.append(jnp.where(lo, even, pltpu.roll(odd, HEAD_DIM, 1)))
            else:
                cols.append(jnp.where(lo, pltpu.roll(even, HEAD_DIM, 1), odd))
        return jnp.concatenate(cols, axis=1)

    def phi(j, g):
        x = chunk_rows(j, g).astype(BF16)
        w1 = w1_ref[j].astype(BF16)
        a = _dot(x, w1[:half])
        b = _dot(x, w1[half:])
        b_next = pltpu.roll(b, ncp - 1, 0)
        pe8 = jnp.broadcast_to(pe_ref[j], (8, 2 * half)).astype(BF16)
        pe_term = _dot(pe8, w1)[0:1]
        hid = a + b_next + pe_term + b1_ref[j]
        act = (hid * _sigmoid(hid)).astype(BF16)
        return _dot(act, w2_ref[j].astype(BF16))

    ones_cols = jnp.where((lane == HEAD_DIM) | (lane == HEAD_DIM + 1), 1.0, 0.0)
    for g in range(N_GROUPS):
        kc = phi(0, g)
        ms = jnp.sum(kc * kc, axis=-1, keepdims=True) / HEAD_DIM
        kc = kc * lax.rsqrt(ms + EPS) * kg_ref[...]
        hi = kc.astype(BF16).astype(F32)
        lo_part = (kc - hi).astype(BF16).astype(F32)
        kc_ref[g, :, 0:128] = (hi + pltpu.roll(lo_part, HEAD_DIM, 1)).astype(BF16)
        kc_ref[g, :, 128:256] = jnp.where(lo, hi, ones_cols).astype(BF16)
        vt = phi(1, g).T
        row_t = lax.broadcasted_iota(jnp.int32, vt.shape, 0)
        vct_ref[g] = jnp.where(row_t == HEAD_DIM, 1.0, vt).astype(BF16)


def _compress(kvc, w1, b1, w2p, pe_flat, kgain_row):
    s = kvc.shape[1]
    g = N_GROUPS
    ncp = s // CMP_STRIDE
    full = lambda a: pl.BlockSpec(a.shape, lambda i: (0,) * a.ndim)
    return pl.pallas_call(
        _compress_kernel,
        grid=(1,),
        in_specs=[pl.BlockSpec(kvc.shape, lambda i: (0, 0, 0), pipeline_mode=pl.Buffered(1)),
                  full(w1), full(b1), full(w2p), full(pe_flat), full(kgain_row)],
        out_specs=[pl.BlockSpec((g, ncp, 256), lambda i: (0, 0, 0)),
                   pl.BlockSpec((g, 128, ncp), lambda i: (0, 0, 0))],
        out_shape=[jax.ShapeDtypeStruct((g, ncp, 256), BF16),
                   jax.ShapeDtypeStruct((g, 128, ncp), BF16)],
        compiler_params=_cparams(1),
        name="compress",
    )(kvc, w1, b1, w2p, pe_flat, kgain_row)


def _heads_to_lanes(q):
    return jnp.concatenate([q[r * HEAD_DIM:(r + 1) * HEAD_DIM] for r in range(GROUP_SIZE)], axis=1)


def _store_heads(out_ref, o):
    for r in range(GROUP_SIZE):
        out_ref[r * HEAD_DIM:(r + 1) * HEAD_DIM, :] = o[0:HEAD_DIM, r * Q_BLOCK:(r + 1) * Q_BLOCK]


def _bias_tile(w_ref, off, n, head0=0):
    return jnp.concatenate([w_ref[head0 + r, pl.ds(off, n), :] for r in range(GROUP_SIZE)], axis=1)


def _cmp_kernel(qhi_ref, qlo_ref, kc_ref, vct_ref, wc_ref, ext_ref, ocmp_ref, neg_ref, s_ref, imp_ref):
    qb = pl.program_id(0)
    ncp = kc_ref.shape[1]
    nsbp = neg_ref.shape[1]
    gr = GROUP_SIZE * HEAD_DIM
    ws = pl.multiple_of(jnp.maximum(8 * qb - 16, 0), 8)
    off = pl.multiple_of(jnp.maximum(16 - 8 * qb, 0), 8)
    lim = ws + CMP_NEAR
    n_cls = min(4, ncp // 128)
    per_cls = ncp // n_cls

    def attend(nrows):
        nblk = nrows // SEL_PER_CMP
        band = nrows - per_cls
        rows = lax.broadcasted_iota(jnp.int32, (per_cls, GQ), 0) + band
        blk = lax.broadcasted_iota(jnp.int32, (nblk, Q_BLOCK), 0)
        lane = lax.broadcasted_iota(jnp.int32, (nblk, Q_BLOCK), 1)
        cur = 2 * qb + jnp.where(lane >= SEL_BLOCK, 1, 0)
        forced = jnp.where(blk == 0, 1.0, jnp.where(blk == cur, 1.0, jnp.where(blk == cur - 1, 1.0, 0.0)))
        carry = []
        for g in range(N_GROUPS):
            qhi = qhi_ref[g * gr:(g + 1) * gr, :]
            qlo = qlo_ref[g * gr:(g + 1) * gr, :]
            cols = []
            for r in range(GROUP_SIZE):
                a = qhi[r * HEAD_DIM:(r + 1) * HEAD_DIM]
                b = qlo[r * HEAD_DIM:(r + 1) * HEAD_DIM]
                cols.append(jnp.concatenate([a, a, b], axis=0))
            qcat = jnp.concatenate([jnp.concatenate(cols, axis=1), ext_ref[g]], axis=0)
            s_g = s_ref.at[g]
            s_g[0:nrows, :] = _dot(kc_ref[g, 0:nrows, :], qcat)
            s_g[pl.ds(ws, CMP_NEAR), :] += _bias_tile(wc_ref, off, CMP_NEAR, g * GROUP_SIZE)
            s_g[band:nrows, :] = jnp.where(rows < lim, s_g[band:nrows, :], NEG)
            s = s_g[0:nrows, :]
            m = jnp.maximum(jnp.max(s, axis=0, keepdims=True), M_INIT)
            p = jnp.exp2(s - m)
            acc = _dot(vct_ref[g, 0:VT_ROWS, 0:nrows], p.astype(BF16))
            l = acc[HEAD_DIM:HEAD_DIM + 1]
            inv_l = jnp.where(l > 0.0, 1.0 / l, 0.0)
            _store_heads(ocmp_ref.at[g * gr:(g + 1) * gr], acc[0:HEAD_DIM] * inv_l)
            imp = None
            for r in range(GROUP_SIZE):
                cs = slice(r * Q_BLOCK, (r + 1) * Q_BLOCK)
                term = p[:, cs] * inv_l[:, cs]
                imp = term if imp is None else imp + term
            imp_g = imp_ref.at[g]
            imp_g[0:nrows, :] = imp
            strided = [imp_g[pl.ds(k, nblk, stride=SEL_PER_CMP), :] for k in range(SEL_PER_CMP)]
            prev_last = jnp.where(blk == 0, 0.0, pltpu.roll(strided[SEL_PER_CMP - 1], 1, 0))
            impb = prev_last + strided[0] + strided[1] + strided[2] + strided[3]
            carry.append(jnp.where(blk < cur - 1, jnp.where(blk > 0, impb, -1.0), -1.0))

        def pick_one(_, cands):
            out = []
            for sc in cands:
                mx = jnp.max(sc, axis=0, keepdims=True)
                hit_blk = jnp.where(sc == mx, jnp.where(mx >= 0.0, blk, nblk), nblk)
                first = jnp.min(hit_blk, axis=0, keepdims=True)
                out.append(jnp.where(blk == first, PICKED, sc))
            return tuple(out)

        if nblk * N_GROUPS <= 2 * SUPER:
            carry = lax.fori_loop(0, N_PICK, pick_one, tuple(carry))
        else:
            carry = [lax.fori_loop(0, N_PICK, pick_one, (c,))[0] for c in carry]
        for g in range(N_GROUPS):
            picked = carry[g] == PICKED
            sel_neg = jnp.where(picked, 0.0, jnp.where(blk <= cur, jnp.where(forced > 0.5, 0.0, NEG), NEG))
            neg_ref[g, 0:nblk, :] = sel_neg.astype(BF16)
            if nblk < nsbp:
                neg_ref[g, nblk:nsbp, :] = jnp.full((nsbp - nblk, Q_BLOCK), NEG, BF16)

    cls = (lim + per_cls - 1) // per_cls
    for k in range(1, n_cls + 1):
        pl.when(cls == k)(functools.partial(attend, k * per_cls))


def _cmp_attention(qhi_t, qlo_t, kc_cat, vct, wc_t, ext_q, nsbp):
    s = qhi_t.shape[1]
    g = N_GROUPS
    ncp = kc_cat.shape[1]
    full = lambda a: pl.BlockSpec(a.shape, lambda qb: (0,) * a.ndim)
    return pl.pallas_call(
        _cmp_kernel,
        grid=(s // Q_BLOCK,),
        in_specs=[pl.BlockSpec((NSA_WIDTH, Q_BLOCK), lambda qb: (0, qb)),
                  pl.BlockSpec((NSA_WIDTH, Q_BLOCK), lambda qb: (0, qb)),
                  full(kc_cat), full(vct), full(wc_t), full(ext_q)],
        out_specs=[pl.BlockSpec((NSA_WIDTH, Q_BLOCK), lambda qb: (0, qb)),
                   pl.BlockSpec((g, nsbp, Q_BLOCK), lambda qb: (0, 0, qb))],
        out_shape=[jax.ShapeDtypeStruct((NSA_WIDTH, s), F32),
                   jax.ShapeDtypeStruct((g, nsbp, s), BF16)],
        scratch_shapes=[pltpu.VMEM((g, ncp, GQ), F32), pltpu.VMEM((g, ncp, Q_BLOCK), F32)],
        compiler_params=_cparams(1),
        name="cmp_select",
    )(qhi_t, qlo_t, kc_cat, vct, wc_t, ext_q)


def _sel_kernel(qhi_ref, neg_ref, kaug_ref, vt512_ref, vt128_ref, ws_ref, ext_ref,
                kwin_ref, vwt_ref, ww_ref, out_ref, owin_ref,
                qaug_ref, m_ref, acc_ref, s0_ref, s1_ref, s2_ref, cm0_ref, cm1_ref, cm2_ref):
    qb = pl.program_id(1)
    nsbp = neg_ref.shape[0]
    nsc = nsbp // SUPER
    qg = _heads_to_lanes(qhi_ref[...])
    qx = jnp.concatenate([qg, ext_ref[...]], axis=0)
    blk = lax.broadcasted_iota(jnp.int32, (SUPER, Q_BLOCK), 0)

    def aug(neg_rows):
        tiled = jnp.concatenate([neg_rows] * GROUP_SIZE, axis=1).astype(BF16)
        return jnp.concatenate([tiled, qx], axis=0)

    for sc in range(nsc):
        neg_rows = neg_ref[sc * SUPER:(sc + 1) * SUPER, :].astype(F32)
        qaug_ref[sc] = aug(jnp.where(blk + sc * SUPER >= 2 * qb - 2, NEG, neg_rows))

    m_ref[...] = jnp.full(m_ref.shape, M_INIT, F32)
    acc_ref[...] = jnp.zeros(acc_ref.shape, F32)

    def update(s, col_max, vt):
        m_old = m_ref[...]
        m_new = jnp.maximum(m_old, col_max)
        p = jnp.exp2(s - m_new).astype(BF16)
        acc_ref[...] = acc_ref[...] * jnp.exp2(m_old - m_new) + _dot(vt, p)
        m_ref[...] = m_new

    n_far = (jnp.maximum(qb - 1, 0) * Q_BLOCK + KV_CHUNK - 1) // KV_CHUNK

    def scores(c, dst_ref, max_ref):
        k = kaug_ref[pl.ds(pl.multiple_of(c * KV_CHUNK, KV_CHUNK), KV_CHUNK), :]
        sc = (c * (KV_CHUNK // SEL_BLOCK)) // SUPER
        s = _dot(k, qaug_ref[sc])
        dst_ref[...] = s
        max_ref[...] = jnp.max(s, axis=0, keepdims=True)

    kb0 = jnp.maximum(qb - 1, 0)
    ws = pl.multiple_of(kb0 * Q_BLOCK, Q_BLOCK)
    off = pl.multiple_of(jnp.where(qb == 0, Q_BLOCK, 0), Q_BLOCK)
    b_lo = 2 * kb0
    sc_lo = pl.multiple_of((b_lo // SUPER) * SUPER, SUPER)
    sc_hi = pl.multiple_of(((b_lo + 3) // SUPER) * SUPER, SUPER)
    neg_lo = neg_ref[pl.ds(sc_lo, SUPER), :].astype(F32)
    neg_hi = neg_ref[pl.ds(sc_hi, SUPER), :].astype(F32)
    near_neg = jnp.where(blk >= SUPER // 2, neg_lo, neg_hi)
    k = kaug_ref[pl.ds(ws, SEL_NEAR), :]
    s_near = _dot(k, aug(near_neg)) + _bias_tile(ws_ref, off, SEL_NEAR)
    vt_near = jnp.concatenate([vt128_ref[kb0], vt128_ref[kb0 + 1]], axis=1)

    bufs = ((s0_ref, cm0_ref), (s1_ref, cm1_ref), (s2_ref, cm2_ref))
    last = jnp.maximum(n_far - 1, 0)
    scores(0, *bufs[0])
    scores(jnp.minimum(1, last), *bufs[1])
    update(s_near, jnp.max(s_near, axis=0, keepdims=True), vt_near)
    _window_branch(qb, qg, kwin_ref, vwt_ref, ww_ref, owin_ref)

    def far_triple(c):
        for j in range(3):
            scores(jnp.minimum(c + j + 2, last), *bufs[(j + 2) % 3])
            s_ref, cm_ref = bufs[j]
            update(s_ref[...], cm_ref[...], vt512_ref[c + j])

    def far_six(i, carry):
        far_triple(6 * i)
        far_triple(6 * i + 3)
        return carry

    lax.fori_loop(0, n_far // 6, far_six, 0)
    done = (n_far // 6) * 6

    @pl.when(n_far - done >= 3)
    def _():
        far_triple(done)

    done = (n_far // 3) * 3
    for j in range(2):
        @pl.when(n_far - done > j)
        def _():
            s_ref, cm_ref = bufs[j]
            update(s_ref[...], cm_ref[...], vt512_ref[done + j])

    acc = acc_ref[...]
    _store_heads(out_ref, acc[0:HEAD_DIM] * (1.0 / acc[HEAD_DIM:HEAD_DIM + 1]))


def _sel_win_attention(qhi_t, neg_t, kaug, vt512, vt128, ws_t, ext_q, kwin, vwt, ww_t):
    s = qhi_t.shape[1]
    g = N_GROUPS
    nqb = s // Q_BLOCK
    nsbp = neg_t.shape[1]
    gr = GROUP_SIZE * HEAD_DIM
    once = pl.Buffered(1)
    return pl.pallas_call(
        _sel_kernel,
        grid=(g, nqb),
        in_specs=[pl.BlockSpec((gr, Q_BLOCK), lambda gi, qb: (gi, qb)),
                  pl.BlockSpec((None, nsbp, Q_BLOCK), lambda gi, qb: (gi, 0, qb)),
                  pl.BlockSpec((None, s, 256), lambda gi, qb: (gi, 0, 0), pipeline_mode=once),
                  pl.BlockSpec((None, s // KV_CHUNK, VT_ROWS, KV_CHUNK), lambda gi, qb: (gi, 0, 0, 0),
                               pipeline_mode=once),
                  pl.BlockSpec((None, s // 128, VT_ROWS, 128), lambda gi, qb: (gi, 0, 0, 0),
                               pipeline_mode=once),
                  pl.BlockSpec((GROUP_SIZE,) + ws_t.shape[1:], lambda gi, qb: (gi, 0, 0)),
                  pl.BlockSpec((None, HEAD_DIM, GQ), lambda gi, qb: (gi, 0, 0)),
                  pl.BlockSpec((None, s, 128), lambda gi, qb: (gi, 0, 0), pipeline_mode=once),
                  pl.BlockSpec((None, s // 128, VT_ROWS, 128), lambda gi, qb: (gi, 0, 0, 0),
                               pipeline_mode=once),
                  pl.BlockSpec((GROUP_SIZE,) + ww_t.shape[1:], lambda gi, qb: (gi, 0, 0))],
        out_specs=[pl.BlockSpec((gr, Q_BLOCK), lambda gi, qb: (gi, qb)),
                   pl.BlockSpec((gr, Q_BLOCK), lambda gi, qb: (gi, qb))],
        out_shape=[jax.ShapeDtypeStruct((NSA_WIDTH, s), F32),
                   jax.ShapeDtypeStruct((NSA_WIDTH, s), F32)],
        scratch_shapes=[pltpu.VMEM((nsbp // SUPER, 256, GQ), BF16),
                        pltpu.VMEM((1, GQ), F32),
                        pltpu.VMEM((VT_ROWS, GQ), F32),
                        pltpu.VMEM((KV_CHUNK, GQ), F32),
                        pltpu.VMEM((KV_CHUNK, GQ), F32),
                        pltpu.VMEM((KV_CHUNK, GQ), F32),
                        pltpu.VMEM((1, GQ), F32),
                        pltpu.VMEM((1, GQ), F32),
                        pltpu.VMEM((1, GQ), F32)],
        compiler_params=_cparams(2),
        name="sel_win_attention",
    )(qhi_t, neg_t, kaug, vt512, vt128, ws_t, ext_q, kwin, vwt, ww_t)


def _window_branch(qb, qg, kwin_ref, vwt_ref, ww_ref, out_ref):
    kb0 = jnp.maximum(qb - WINDOW // Q_BLOCK, 0)
    ws = pl.multiple_of(kb0 * Q_BLOCK, Q_BLOCK)
    off = pl.multiple_of(jnp.maximum(WINDOW - qb * Q_BLOCK, 0), Q_BLOCK)
    qx = jnp.concatenate([qg, jnp.zeros_like(qg)], axis=0)
    s = _dot(kwin_ref[pl.ds(ws, WIN_KEYS), :], qx) + _bias_tile(ww_ref, off, WIN_KEYS)
    m = jnp.max(s, axis=0, keepdims=True)
    p = jnp.exp2(s - m).astype(BF16)
    vt = jnp.concatenate([vwt_ref[kb0 + j] for j in range(WIN_KEYS // Q_BLOCK)], axis=1)
    acc = _dot(vt, p)
    _store_heads(out_ref, acc[0:HEAD_DIM] * (1.0 / acc[HEAD_DIM:HEAD_DIM + 1]))


LRU_TM = 256


def _lru_kernel(u_ref, cw_ref, cb_ref, wa_ref, ba_ref, wx_ref, bx_ref, lam_ref, h_ref,
                tail_ref, hprev_ref, uc_ref):
    tm = LRU_TM

    @pl.when(pl.program_id(0) == 0)
    def _():
        tail_ref[...] = jnp.zeros(tail_ref.shape, F32)
        hprev_ref[...] = jnp.zeros(hprev_ref.shape, F32)

    u = u_ref[...]
    tail = tail_ref[...]
    row8 = lax.broadcasted_iota(jnp.int32, (8, LRU_WIDTH), 0)
    uc = cb_ref[...] + u * cw_ref[CONV_WIDTH - 1:CONV_WIDTH, :]
    uc_head = uc[0:8]
    for j in range(1, CONV_WIDTH):
        w_j = cw_ref[CONV_WIDTH - 1 - j:CONV_WIDTH - j, :]
        sh = pltpu.roll(u, j, 0)
        uc = uc + sh * w_j
        uc_head = uc_head + jnp.where(row8 < j, pltpu.roll(tail, j, 0), sh[0:8]) * w_j
    tail_ref[...] = u[tm - 8:tm]
    uc_ref[...] = uc
    uc_ref[0:8, :] = uc_head
    uc = uc_ref[...]

    ucb = uc.astype(BF16)
    r = _sigmoid(_dot(ucb, wa_ref[...]) + ba_ref[...])
    ig = _sigmoid(_dot(ucb, wx_ref[...]) + bx_ref[...])
    nl = -lam_ref[...]
    softplus = jnp.maximum(nl, 0.0) + jnp.log(1.0 + jnp.exp(-jnp.abs(nl)))
    a = jnp.exp((-LRU_C * r) * softplus)
    b = jnp.sqrt(1.0 - a * a) * (ig * uc)

    a = a.reshape(tm // 8, 8, LRU_WIDTH)
    b = b.reshape(tm // 8, 8, LRU_WIDTH)
    sub = lax.broadcasted_iota(jnp.int32, a.shape, 1)
    for step in (1, 2, 4):
        a_s = pltpu.roll(a, step, 1)
        b_s = pltpu.roll(b, step, 1)
        ok = sub >= step
        b = jnp.where(ok, a * b_s + b, b)
        a = jnp.where(ok, a * a_s, a)
    carry = hprev_ref[...]
    for j in range(tm // 8):
        hj = a[j] * carry + b[j]
        h_ref[8 * j:8 * j + 8, :] = hj
        carry = hj[7:8]
    hprev_ref[...] = carry


def _rglru(u, conv_w, conv_b, wa_bd, ba, wx_bd, bx, lam):
    s = u.shape[0]
    tm = LRU_TM
    row = lambda v: v.reshape(1, LRU_WIDTH)
    full = lambda shape: pl.BlockSpec(shape, lambda i: (0, 0))
    return pl.pallas_call(
        _lru_kernel,
        grid=(s // tm,),
        in_specs=[pl.BlockSpec((tm, LRU_WIDTH), lambda i: (i, 0)),
                  full((CONV_WIDTH, LRU_WIDTH)), full((1, LRU_WIDTH)),
                  full((LRU_WIDTH, LRU_WIDTH)), full((1, LRU_WIDTH)),
                  full((LRU_WIDTH, LRU_WIDTH)), full((1, LRU_WIDTH)),
                  full((1, LRU_WIDTH))],
        out_specs=pl.BlockSpec((tm, LRU_WIDTH), lambda i: (i, 0)),
        out_shape=jax.ShapeDtypeStruct((s, LRU_WIDTH), F32),
        scratch_shapes=[pltpu.VMEM((8, LRU_WIDTH), F32), pltpu.VMEM((1, LRU_WIDTH), F32),
                        pltpu.VMEM((tm, LRU_WIDTH), F32)],
        compiler_params=_cparams(1),
        name="rglru",
    )(u, conv_w, row(conv_b), wa_bd, row(ba), wx_bd, row(bx), row(lam))


OUT_TM = 256


def _out_kernel(x_ref, g_ref, oc_ref, os_ref, ow_ref, brt_ref, h_ref,
                wg_ref, wpa_ref, wpb_ref, wo_ref, y_ref):
    h_in = _normed_input(x_ref, g_ref)
    gate = lambda j: _dot(h_in, wg_ref[:, GATE_OFFS[j]:GATE_OFFS[j + 1]])
    brt = brt_ref[...]
    parts = []
    for h in range(N_HEADS):
        rows = slice(h * HEAD_DIM, (h + 1) * HEAD_DIM)
        acc = None
        for b, o_ref in enumerate((oc_ref, os_ref, ow_ref)):
            term = brt[b * N_HEADS + h:b * N_HEADS + h + 1, :] * o_ref[rows, :]
            acc = term if acc is None else acc + term
        parts.append(acc)
    gn = gate(0)
    ya = jnp.concatenate(parts, axis=0).T * (gn * _sigmoid(gn))
    y_a = _dot(ya.astype(BF16), wpa_ref[...])
    gl = gate(1)
    y_b = _dot((h_ref[...] * (gl * _sigmoid(gl))).astype(BF16), wpb_ref[...])
    mg = gate(2)
    m = _sigmoid(mg[:, :D_MODEL]) * y_a + _sigmoid(mg[:, D_MODEL:]) * y_b
    y_ref[...] = x_ref[...] + _dot(m.astype(BF16), wo_ref[...])


def _output(x2, norm_gain, oc_t, os_t, ow_t, br_t, h_lru, w_gate, wpa, wpb, wo):
    s = x2.shape[0]
    tm = OUT_TM
    rows = lambda n: pl.BlockSpec((tm, n), lambda i: (i, 0))
    cols = lambda n: pl.BlockSpec((n, tm), lambda i: (0, i))
    full = lambda a: pl.BlockSpec(a.shape, lambda i: (0, 0))
    return pl.pallas_call(
        _out_kernel,
        grid=(s // tm,),
        in_specs=[rows(D_MODEL), pl.BlockSpec((1, D_MODEL), lambda i: (0, 0)),
                  cols(NSA_WIDTH), cols(NSA_WIDTH), cols(NSA_WIDTH), cols(32), rows(LRU_WIDTH),
                  full(w_gate), full(wpa), full(wpb), full(wo)],
        out_specs=rows(D_MODEL),
        out_shape=jax.ShapeDtypeStruct((s, D_MODEL), F32),
        compiler_params=_cparams(1),
        name="output",
    )(x2, norm_gain.reshape(1, D_MODEL), oc_t, os_t, ow_t, br_t, h_lru, w_gate, wpa, wpb, wo)


def _t5_bucket_table(n_dist):
    n = np.arange(n_dist)
    max_exact = N_BUCKETS // 2
    nf = np.maximum(n, 1).astype(np.float32)
    large = max_exact + (np.log(nf / np.float32(max_exact)) / np.float32(math.log(MAX_DISTANCE / max_exact))
                         * np.float32(N_BUCKETS - max_exact)).astype(np.int32)
    return np.where(n < max_exact, n, np.minimum(large, N_BUCKETS - 1))


def _bias_tiles_kernel(relb_ref, bw_ref, bs_ref, bc_ref, ww_ref, ws_ref, wc_ref):
    h = pl.program_id(0)
    far = relb_ref[N_BUCKETS - 1, h]
    for bk_ref, out_ref, rel in ((bw_ref, ww_ref, False), (bs_ref, ws_ref, True), (bc_ref, wc_ref, True)):
        bk = bk_ref[...]
        acc = jnp.full(bk.shape, NEG, F32)
        for b in range(N_BUCKETS):
            val = relb_ref[b, h] - far if rel else relb_ref[b, h]
            acc = jnp.where(bk == b, val * LOG2E, acc)
        out_ref[...] = acc


def _bias_tables(rel_bias):
    bucket = _t5_bucket_table(2 * WINDOW)
    i = np.arange(Q_BLOCK)[None, :]

    def index_tile(dist, valid):
        return jnp.asarray(np.where(valid, bucket[np.clip(dist, 0, bucket.size - 1)], -1).astype(np.int32))

    u = np.arange(WINDOW + WIN_KEYS)[:, None]
    d_win = i - u + WINDOW
    bw = index_tile(d_win, (d_win >= 0) & (d_win < WINDOW))
    u = np.arange(SEL_NEAR + Q_BLOCK)[:, None]
    d_sel = i - u + Q_BLOCK
    bs = index_tile(d_sel, d_sel >= 0)
    u = np.arange(CMP_NEAR + 16)[:, None]
    d_cmp = i - CMP_STRIDE * (u - 16) - (CMP_BLOCK - 1)
    bc = index_tile(d_cmp, d_cmp >= 0)
    full = lambda a: pl.BlockSpec(a.shape, lambda h: (0, 0))
    per_head = lambda a: pl.BlockSpec((None,) + a.shape, lambda h: (h, 0, 0))
    ww, ws, wc = pl.pallas_call(
        _bias_tiles_kernel,
        grid=(N_HEADS,),
        in_specs=[pl.BlockSpec(memory_space=pltpu.SMEM), full(bw), full(bs), full(bc)],
        out_specs=[per_head(bw), per_head(bs), per_head(bc)],
        out_shape=[jax.ShapeDtypeStruct((N_HEADS,) + a.shape, F32) for a in (bw, bs, bc)],
        compiler_params=_cparams(1),
        name="bias_tiles",
    )(rel_bias, bw, bs, bc)
    b_far = rel_bias[N_BUCKETS - 1] * LOG2E
    hi = b_far.astype(BF16)
    lo = (b_far - hi.astype(F32)).astype(BF16)
    ext = jnp.zeros((N_HEADS, HEAD_DIM, Q_BLOCK), BF16)
    ext = ext.at[:, 0, :].set(hi[:, None]).at[:, 1, :].set(lo[:, None])
    ext = ext.reshape(N_GROUPS, GROUP_SIZE, HEAD_DIM, Q_BLOCK).transpose(0, 2, 1, 3)
    return ww, ws, wc, ext.reshape(N_GROUPS, HEAD_DIM, GQ)


def _block_diag(w):
    n, d, e = w.shape
    eye = jnp.eye(n, dtype=w.dtype)
    return (eye[:, None, :, None] * w[:, :, None, :]).reshape(n * d, n * e)


def kernel(x, norm_gain, w_in, q_norm_gain, k_norm_gain, cmp_pe, cmp_w1, cmp_b1, cmp_w2, rel_bias,
           conv_w, conv_b, lru_wa, lru_ba, lru_wx, lru_bx, lru_lambda, w_proj_a, w_proj_b, w_out):
    bsz, s, _ = x.shape
    assert bsz == 1 and s % 1024 == 0 and s >= 1024
    x2 = x.reshape(s, D_MODEL)
    nsb = s // SEL_BLOCK
    nsbp = -(-nsb // SUPER) * SUPER

    o = IN_OFFS
    w16 = w_in.astype(BF16)
    pad = jnp.zeros((D_MODEL, BR_PAD - 3 * N_HEADS), BF16)
    w_front = jnp.concatenate([w16[:, o[0]:o[2]], w16[:, o[3]:o[4]], pad, w16[:, o[4]:o[5]]], axis=1)
    w_gate = jnp.concatenate([w16[:, o[2]:o[3]], w16[:, o[5]:o[7]]], axis=1)

    scale = HEAD_DIM ** -0.5 * LOG2E
    qgain_col = jnp.tile(q_norm_gain * scale, N_HEADS).reshape(NSA_WIDTH, 1)
    kgain_rows = jnp.tile(k_norm_gain, (1, N_GROUPS))
    (qhi_t, qlo_t, br_t, kaug, vt512, vt128, kwin, vwt, kvc, u_lru) = _front(
        x2, norm_gain, w_front, qgain_col, kgain_rows)

    w2p = jnp.pad(cmp_w2, ((0, 0), (0, 0), (0, 128 - HEAD_DIM)))
    kc_gain = jnp.pad(k_norm_gain[0], (0, 128 - HEAD_DIM)).reshape(1, 128)
    kc_cat, vct = _compress(kvc, cmp_w1, cmp_b1.reshape(2, 1, CMP_HIDDEN), w2p,
                            cmp_pe.reshape(2, 1, CMP_BLOCK * HEAD_DIM), kc_gain)

    ww_t, ws_t, wc_t, ext_q = _bias_tables(rel_bias)
    oc_t, neg_t = _cmp_attention(qhi_t, qlo_t, kc_cat, vct, wc_t, ext_q, nsbp)
    os_t, ow_t = _sel_win_attention(qhi_t, neg_t, kaug, vt512, vt128, ws_t, ext_q, kwin, vwt, ww_t)

    h_lru = _rglru(u_lru, conv_w, conv_b, _block_diag(lru_wa).astype(BF16), lru_ba,
                   _block_diag(lru_wx).astype(BF16), lru_bx, lru_lambda)

    y = _output(x2, norm_gain, oc_t, os_t, ow_t, br_t, h_lru, w_gate,
                w_proj_a.astype(BF16), w_proj_b.astype(BF16), w_out.astype(BF16))
    return y.reshape(bsz, s, D_MODEL)
```

```python
import functools
import math

import numpy as np
import jax
import jax.numpy as jnp
from jax import lax
from jax.experimental import pallas as pl
from jax.experimental.pallas import tpu as pltpu

F32 = jnp.float32
BF16 = jnp.bfloat16

D_MODEL = 1024
N_HEADS = 8
N_GROUPS = 2
GROUP_SIZE = N_HEADS // N_GROUPS
HEAD_DIM = 64
NSA_WIDTH = N_HEADS * HEAD_DIM
KV_WIDTH = N_GROUPS * HEAD_DIM
CMP_STRIDE = 16
CMP_BLOCK = 32
CMP_HIDDEN = 256
SEL_BLOCK = 64
SEL_PER_CMP = SEL_BLOCK // CMP_STRIDE
N_SELECT = 16
WINDOW = 512
Q_BLOCK = 128
LRU_WIDTH = 512
LRU_BLOCKS = 8
CONV_WIDTH = 4
LRU_C = 8.0
N_BUCKETS = 32
MAX_DISTANCE = 128
EPS = 1e-6
NEG = -1e30
M_INIT = -5e29
LOG2E = 1.4426950408889634

GQ = GROUP_SIZE * Q_BLOCK
SUPER = 128
KV_CHUNK = 512
SEL_NEAR = 2 * Q_BLOCK
WIN_KEYS = WINDOW + Q_BLOCK
CMP_NEAR = 24
N_PICK = N_SELECT - 3
PICKED = -2.0
VT_ROWS = 80

IN_OFFS = tuple(int(v) for v in np.cumsum(
    (0, NSA_WIDTH, 6 * KV_WIDTH, NSA_WIDTH, 3 * N_HEADS, LRU_WIDTH, LRU_WIDTH, 2 * D_MODEL)))
BR_PAD = 128
FRONT_OFFS = tuple(int(v) for v in np.cumsum((0, NSA_WIDTH, 6 * KV_WIDTH, BR_PAD, LRU_WIDTH)))
GATE_OFFS = tuple(int(v) for v in np.cumsum((0, NSA_WIDTH, LRU_WIDTH, 2 * D_MODEL)))

VMEM_LIMIT = 56 * 1024 * 1024


def _cparams(n_axes):
    return pltpu.CompilerParams(dimension_semantics=("arbitrary",) * n_axes,
                                vmem_limit_bytes=VMEM_LIMIT)


def _dot(a, b):
    return jnp.dot(a, b, preferred_element_type=F32)


def _sigmoid(x):
    return 0.5 * jnp.tanh(0.5 * x) + 0.5


FRONT_TM = 512


def _normed_input(x_ref, g_ref):
    x = x_ref[...]
    ms = jnp.mean(x * x, axis=-1, keepdims=True)
    return (x * lax.rsqrt(ms + EPS) * g_ref[...]).astype(BF16)


def _group_rms(k, gain_row):
    sq = k * k
    lane = lax.broadcasted_iota(jnp.int32, k.shape, 1)
    lo = lane < HEAD_DIM
    s0 = jnp.sum(jnp.where(lo, sq, 0.0), axis=-1, keepdims=True)
    s1 = jnp.sum(jnp.where(lo, 0.0, sq), axis=-1, keepdims=True)
    inv = jnp.where(lo, lax.rsqrt(s0 / HEAD_DIM + EPS), lax.rsqrt(s1 / HEAD_DIM + EPS))
    return k * inv * gain_row


def _front_kernel(x_ref, g_ref, w_ref, qg_ref, kg_ref,
                  qhi_ref, qlo_ref, brt_ref, kaug_ref, vt512_ref, vt128_ref,
                  kwin_ref, vwt_ref, kvc_ref, u_ref):
    i = pl.program_id(0)
    tm = FRONT_TM
    h_in = _normed_input(x_ref, g_ref)
    proj = lambda a, b: _dot(h_in, w_ref[:, a:b])
    qt = proj(FRONT_OFFS[0], FRONT_OFFS[1]).T
    for h in range(N_HEADS):
        blk = qt[h * HEAD_DIM:(h + 1) * HEAD_DIM]
        ms = jnp.mean(blk * blk, axis=0, keepdims=True)
        qn = blk * lax.rsqrt(ms + EPS) * qg_ref[h * HEAD_DIM:(h + 1) * HEAD_DIM, :]
        hi = qn.astype(BF16)
        qhi_ref[h * HEAD_DIM:(h + 1) * HEAD_DIM, :] = hi
        qlo_ref[h * HEAD_DIM:(h + 1) * HEAD_DIM, :] = (qn - hi.astype(F32)).astype(BF16)
    brt_ref[...] = _sigmoid(proj(FRONT_OFFS[2], FRONT_OFFS[3])).T[:32]
    u_ref[...] = proj(FRONT_OFFS[3], FRONT_OFFS[4])

    kv = proj(FRONT_OFFS[1], FRONT_OFFS[2])
    piece = lambda j: kv[:, j * KV_WIDTH:(j + 1) * KV_WIDTH]
    lane = lax.broadcasted_iota(jnp.int32, (tm, 128), 1)
    row = lax.broadcasted_iota(jnp.int32, (tm, 128), 0) + i * tm
    lo = lane < HEAD_DIM
    ones_cols = jnp.where((lane == HEAD_DIM) | (lane == HEAD_DIM + 1), 1.0, 0.0)
    onehot = jnp.where(lane == (row // SEL_BLOCK) % SUPER, 1.0, 0.0).astype(BF16)
    kvc_ref[0] = piece(0)
    kvc_ref[1] = piece(1)
    kslc = _group_rms(piece(2), kg_ref[1:2, :])
    kwin = _group_rms(piece(4), kg_ref[2:3, :])
    vslt = piece(3).T
    vwit = piece(5).T
    row_t = lax.broadcasted_iota(jnp.int32, (VT_ROWS - HEAD_DIM, tm), 0)
    ones_rows = jnp.where(row_t == 0, 1.0, 0.0)
    for g in range(N_GROUPS):
        sh = lambda a: a if g == 0 else pltpu.roll(a, HEAD_DIM, 1)
        kaug_ref[g, :, 0:128] = onehot
        kaug_ref[g, :, 128:256] = jnp.where(lo, sh(kslc), ones_cols).astype(BF16)
        kwin_ref[g] = jnp.where(lo, sh(kwin), 0.0).astype(BF16)
        vs = jnp.concatenate([vslt[g * HEAD_DIM:(g + 1) * HEAD_DIM], ones_rows], axis=0).astype(BF16)
        vw = jnp.concatenate([vwit[g * HEAD_DIM:(g + 1) * HEAD_DIM], ones_rows], axis=0).astype(BF16)
        for j in range(tm // KV_CHUNK):
            vt512_ref[g, j] = vs[:, j * KV_CHUNK:(j + 1) * KV_CHUNK]
        for j in range(tm // 128):
            vt128_ref[g, j] = vs[:, j * 128:(j + 1) * 128]
            vwt_ref[g, j] = vw[:, j * 128:(j + 1) * 128]


def _front(x2, norm_gain, w_front, qgain_col, kgain_rows):
    s = x2.shape[0]
    tm = FRONT_TM
    g = N_GROUPS
    outs = [
        jax.ShapeDtypeStruct((NSA_WIDTH, s), BF16),
        jax.ShapeDtypeStruct((NSA_WIDTH, s), BF16),
        jax.ShapeDtypeStruct((32, s), F32),
        jax.ShapeDtypeStruct((g, s, 256), BF16),
        jax.ShapeDtypeStruct((g, s // KV_CHUNK, VT_ROWS, KV_CHUNK), BF16),
        jax.ShapeDtypeStruct((g, s // 128, VT_ROWS, 128), BF16),
        jax.ShapeDtypeStruct((g, s, 128), BF16),
        jax.ShapeDtypeStruct((g, s // 128, VT_ROWS, 128), BF16),
        jax.ShapeDtypeStruct((2, s, KV_WIDTH), F32),
        jax.ShapeDtypeStruct((s, LRU_WIDTH), F32),
    ]
    out_specs = [
        pl.BlockSpec((NSA_WIDTH, tm), lambda i: (0, i)),
        pl.BlockSpec((NSA_WIDTH, tm), lambda i: (0, i)),
        pl.BlockSpec((32, tm), lambda i: (0, i)),
        pl.BlockSpec((g, tm, 256), lambda i: (0, i, 0)),
        pl.BlockSpec((g, tm // KV_CHUNK, VT_ROWS, KV_CHUNK), lambda i: (0, i, 0, 0)),
        pl.BlockSpec((g, tm // 128, VT_ROWS, 128), lambda i: (0, i, 0, 0)),
        pl.BlockSpec((g, tm, 128), lambda i: (0, i, 0)),
        pl.BlockSpec((g, tm // 128, VT_ROWS, 128), lambda i: (0, i, 0, 0)),
        pl.BlockSpec((2, tm, KV_WIDTH), lambda i: (0, i, 0)),
        pl.BlockSpec((tm, LRU_WIDTH), lambda i: (i, 0)),
    ]
    return pl.pallas_call(
        _front_kernel,
        grid=(s // tm,),
        in_specs=[pl.BlockSpec((tm, D_MODEL), lambda i: (i, 0)),
                  pl.BlockSpec((1, D_MODEL), lambda i: (0, 0)),
                  pl.BlockSpec(w_front.shape, lambda i: (0, 0)),
                  pl.BlockSpec((NSA_WIDTH, 1), lambda i: (0, 0)),
                  pl.BlockSpec((3, 128), lambda i: (0, 0))],
        out_specs=out_specs,
        out_shape=outs,
        compiler_params=_cparams(1),
        name="front",
    )(x2, norm_gain.reshape(1, D_MODEL), w_front, qgain_col, kgain_rows)


def _compress_kernel(kvc_ref, w1_ref, b1_ref, w2_ref, pe_ref, kg_ref, kc_ref, vct_ref):
    ncp = kc_ref.shape[1]
    half = CMP_STRIDE * HEAD_DIM
    lane = lax.broadcasted_iota(jnp.int32, (ncp, 128), 1)
    lo = lane < HEAD_DIM

    def chunk_rows(j, g):
        cols = []
        for pair in range(CMP_STRIDE // 2):
            even = kvc_ref[j, pl.ds(2 * pair, ncp, stride=CMP_STRIDE), :]
            odd = kvc_ref[j, pl.ds(2 * pair + 1, ncp, stride=CMP_STRIDE), :]
            if g == 0:
                cols.append(jnp.where(lo, even, pltpu.roll(odd, HEAD_DIM, 1)))
            else:
                cols.append(jnp.where(lo, pltpu.roll(even, HEAD_DIM, 1), odd))
        return jnp.concatenate(cols, axis=1)

    def phi(j, g):
        x = chunk_rows(j, g).astype(BF16)
        w1 = w1_ref[j].astype(BF16)
        a = _dot(x, w1[:half])
        b = _dot(x, w1[half:])
        b_next = pltpu.roll(b, ncp - 1, 0)
        pe8 = jnp.broadcast_to(pe_ref[j], (8, 2 * half)).astype(BF16)
        pe_term = _dot(pe8, w1)[0:1]
        hid = a + b_next + pe_term + b1_ref[j]
        act = (hid * _sigmoid(hid)).astype(BF16)
        return _dot(act, w2_ref[j].astype(BF16))

    ones_cols = jnp.where((lane == HEAD_DIM) | (lane == HEAD_DIM + 1), 1.0, 0.0)
    for g in range(N_GROUPS):
        kc = phi(0, g)
        ms = jnp.sum(kc * kc, axis=-1, keepdims=True) / HEAD_DIM
        kc = kc * lax.rsqrt(ms + EPS) * kg_ref[...]
        hi = kc.astype(BF16).astype(F32)
        lo_part = (kc - hi).astype(BF16).astype(F32)
        kc_ref[g, :, 0:128] = (hi + pltpu.roll(lo_part, HEAD_DIM, 1)).astype(BF16)
        kc_ref[g, :, 128:256] = jnp.where(lo, hi, ones_cols).astype(BF16)
        vt = phi(1, g).T
        row_t = lax.broadcasted_iota(jnp.int32, vt.shape, 0)
        vct_ref[g] = jnp.where(row_t == HEAD_DIM, 1.0, vt).astype(BF16)


def _compress(kvc, w1, b1, w2p, pe_flat, kgain_row):
    s = kvc.shape[1]
    g = N_GROUPS
    ncp = s // CMP_STRIDE
    full = lambda a: pl.BlockSpec(a.shape, lambda i: (0,) * a.ndim)
    return pl.pallas_call(
        _compress_kernel,
        grid=(1,),
        in_specs=[pl.BlockSpec(kvc.shape, lambda i: (0, 0, 0), pipeline_mode=pl.Buffered(1)),
                  full(w1), full(b1), full(w2p), full(pe_flat), full(kgain_row)],
        out_specs=[pl.BlockSpec((g, ncp, 256), lambda i: (0, 0, 0)),
                   pl.BlockSpec((g, 128, ncp), lambda i: (0, 0, 0))],
        out_shape=[jax.ShapeDtypeStruct((g, ncp, 256), BF16),
                   jax.ShapeDtypeStruct((g, 128, ncp), BF16)],
        compiler_params=_cparams(1),
        name="compress",
    )(kvc, w1, b1, w2p, pe_flat, kgain_row)


def _heads_to_lanes(q):
    return jnp.concatenate([q[r * HEAD_DIM:(r + 1) * HEAD_DIM] for r in range(GROUP_SIZE)], axis=1)


def _store_heads(out_ref, o):
    for r in range(GROUP_SIZE):
        out_ref[r * HEAD_DIM:(r + 1) * HEAD_DIM, :] = o[0:HEAD_DIM, r * Q_BLOCK:(r + 1) * Q_BLOCK]


def _bias_tile(w_ref, off, n, head0=0):
    return jnp.concatenate([w_ref[head0 + r, pl.ds(off, n), :] for r in range(GROUP_SIZE)], axis=1)


def _cmp_kernel(qhi_ref, qlo_ref, kc_ref, vct_ref, wc_ref, ext_ref, ocmp_ref, neg_ref, s_ref, imp_ref):
    qb = pl.program_id(0)
    ncp = kc_ref.shape[1]
    nsbp = neg_ref.shape[1]
    gr = GROUP_SIZE * HEAD_DIM
    ws = pl.multiple_of(jnp.maximum(8 * qb - 16, 0), 8)
    off = pl.multiple_of(jnp.maximum(16 - 8 * qb, 0), 8)
    lim = ws + CMP_NEAR
    n_cls = min(4, ncp // 128)
    per_cls = ncp // n_cls

    def attend(nrows):
        nblk = nrows // SEL_PER_CMP
        band = nrows - per_cls
        rows = lax.broadcasted_iota(jnp.int32, (per_cls, GQ), 0) + band
        blk = lax.broadcasted_iota(jnp.int32, (nblk, Q_BLOCK), 0)
        lane = lax.broadcasted_iota(jnp.int32, (nblk, Q_BLOCK), 1)
        cur = 2 * qb + jnp.where(lane >= SEL_BLOCK, 1, 0)
        forced = jnp.where(blk == 0, 1.0, jnp.where(blk == cur, 1.0, jnp.where(blk == cur - 1, 1.0, 0.0)))
        carry = []
        for g in range(N_GROUPS):
            qhi = qhi_ref[g * gr:(g + 1) * gr, :]
            qlo = qlo_ref[g * gr:(g + 1) * gr, :]
            cols = []
            for r in range(GROUP_SIZE):
                a = qhi[r * HEAD_DIM:(r + 1) * HEAD_DIM]
                b = qlo[r * HEAD_DIM:(r + 1) * HEAD_DIM]
                cols.append(jnp.concatenate([a, a, b], axis=0))
            qcat = jnp.concatenate([jnp.concatenate(cols, axis=1), ext_ref[g]], axis=0)
            s_g = s_ref.at[g]
            s_g[0:nrows, :] = _dot(kc_ref[g, 0:nrows, :], qcat)
            s_g[pl.ds(ws, CMP_NEAR), :] += _bias_tile(wc_ref, off, CMP_NEAR, g * GROUP_SIZE)
            s_g[band:nrows, :] = jnp.where(rows < lim, s_g[band:nrows, :], NEG)
            s = s_g[0:nrows, :]
            m = jnp.maximum(jnp.max(s, axis=0, keepdims=True), M_INIT)
            p = jnp.exp2(s - m)
            acc = _dot(vct_ref[g, 0:VT_ROWS, 0:nrows], p.astype(BF16))
            l = acc[HEAD_DIM:HEAD_DIM + 1]
            inv_l = jnp.where(l > 0.0, 1.0 / l, 0.0)
            _store_heads(ocmp_ref.at[g * gr:(g + 1) * gr], acc[0:HEAD_DIM] * inv_l)
            imp = None
            for r in range(GROUP_SIZE):
                cs = slice(r * Q_BLOCK, (r + 1) * Q_BLOCK)
                term = p[:, cs] * inv_l[:, cs]
                imp = term if imp is None else imp + term
            imp_g = imp_ref.at[g]
            imp_g[0:nrows, :] = imp
            strided = [imp_g[pl.ds(k, nblk, stride=SEL_PER_CMP), :] for k in range(SEL_PER_CMP)]
            prev_last = jnp.where(blk == 0, 0.0, pltpu.roll(strided[SEL_PER_CMP - 1], 1, 0))
            impb = prev_last + strided[0] + strided[1] + strided[2] + strided[3]
            carry.append(jnp.where(blk < cur - 1, jnp.where(blk > 0, impb, -1.0), -1.0))

        def pick_one(_, cands):
            out = []
            for sc in cands:
                mx = jnp.max(sc, axis=0, keepdims=True)
                hit_blk = jnp.where(sc == mx, jnp.where(mx >= 0.0, blk, nblk), nblk)
                first = jnp.min(hit_blk, axis=0, keepdims=True)
                out.append(jnp.where(blk == first, PICKED, sc))
            return tuple(out)

        if nblk * N_GROUPS <= 2 * SUPER:
            carry = lax.fori_loop(0, N_PICK, pick_one, tuple(carry))
        else:
            carry = [lax.fori_loop(0, N_PICK, pick_one, (c,))[0] for c in carry]
        for g in range(N_GROUPS):
            picked = carry[g] == PICKED
            sel_neg = jnp.where(picked, 0.0, jnp.where(blk <= cur, jnp.where(forced > 0.5, 0.0, NEG), NEG))
            neg_ref[g, 0:nblk, :] = sel_neg.astype(BF16)
            if nblk < nsbp:
                neg_ref[g, nblk:nsbp, :] = jnp.full((nsbp - nblk, Q_BLOCK), NEG, BF16)

    cls = (lim + per_cls - 1) // per_cls
    for k in range(1, n_cls + 1):
        pl.when(cls == k)(functools.partial(attend, k * per_cls))


def _cmp_attention(qhi_t, qlo_t, kc_cat, vct, wc_t, ext_q, nsbp):
    s = qhi_t.shape[1]
    g = N_GROUPS
    ncp = kc_cat.shape[1]
    full = lambda a: pl.BlockSpec(a.shape, lambda qb: (0,) * a.ndim)
    return pl.pallas_call(
        _cmp_kernel,
        grid=(s // Q_BLOCK,),
        in_specs=[pl.BlockSpec((NSA_WIDTH, Q_BLOCK), lambda qb: (0, qb)),
                  pl.BlockSpec((NSA_WIDTH, Q_BLOCK), lambda qb: (0, qb)),
                  full(kc_cat), full(vct), full(wc_t), full(ext_q)],
        out_specs=[pl.BlockSpec((NSA_WIDTH, Q_BLOCK), lambda qb: (0, qb)),
                   pl.BlockSpec((g, nsbp, Q_BLOCK), lambda qb: (0, 0, qb))],
        out_shape=[jax.ShapeDtypeStruct((NSA_WIDTH, s), F32),
                   jax.ShapeDtypeStruct((g, nsbp, s), BF16)],
        scratch_shapes=[pltpu.VMEM((g, ncp, GQ), F32), pltpu.VMEM((g, ncp, Q_BLOCK), F32)],
        compiler_params=_cparams(1),
        name="cmp_select",
    )(qhi_t, qlo_t, kc_cat, vct, wc_t, ext_q)


def _sel_kernel(qhi_ref, neg_ref, kaug_ref, vt512_ref, vt128_ref, ws_ref, ext_ref,
                kwin_ref, vwt_ref, ww_ref, out_ref, owin_ref,
                qaug_ref, m_ref, acc_ref, s0_ref, s1_ref, s2_ref, cm0_ref, cm1_ref, cm2_ref):
    qb = pl.program_id(1)
    nsbp = neg_ref.shape[0]
    nsc = nsbp // SUPER
    qg = _heads_to_lanes(qhi_ref[...])
    qx = jnp.concatenate([qg, ext_ref[...]], axis=0)
    blk = lax.broadcasted_iota(jnp.int32, (SUPER, Q_BLOCK), 0)

    def aug(neg_rows):
        tiled = jnp.concatenate([neg_rows] * GROUP_SIZE, axis=1).astype(BF16)
        return jnp.concatenate([tiled, qx], axis=0)

    for sc in range(nsc):
        neg_rows = neg_ref[sc * SUPER:(sc + 1) * SUPER, :].astype(F32)
        qaug_ref[sc] = aug(jnp.where(blk + sc * SUPER >= 2 * qb - 2, NEG, neg_rows))

    m_ref[...] = jnp.full(m_ref.shape, M_INIT, F32)
    acc_ref[...] = jnp.zeros(acc_ref.shape, F32)

    def update(s, col_max, vt):
        m_old = m_ref[...]
        m_new = jnp.maximum(m_old, col_max)
        p = jnp.exp2(s - m_new).astype(BF16)
        acc_ref[...] = acc_ref[...] * jnp.exp2(m_old - m_new) + _dot(vt, p)
        m_ref[...] = m_new

    n_far = (jnp.maximum(qb - 1, 0) * Q_BLOCK + KV_CHUNK - 1) // KV_CHUNK

    def scores(c, dst_ref, max_ref):
        k = kaug_ref[pl.ds(pl.multiple_of(c * KV_CHUNK, KV_CHUNK), KV_CHUNK), :]
        sc = (c * (KV_CHUNK // SEL_BLOCK)) // SUPER
        s = _dot(k, qaug_ref[sc])
        dst_ref[...] = s
        max_ref[...] = jnp.max(s, axis=0, keepdims=True)

    kb0 = jnp.maximum(qb - 1, 0)
    ws = pl.multiple_of(kb0 * Q_BLOCK, Q_BLOCK)
    off = pl.multiple_of(jnp.where(qb == 0, Q_BLOCK, 0), Q_BLOCK)
    b_lo = 2 * kb0
    sc_lo = pl.multiple_of((b_lo // SUPER) * SUPER, SUPER)
    sc_hi = pl.multiple_of(((b_lo + 3) // SUPER) * SUPER, SUPER)
    neg_lo = neg_ref[pl.ds(sc_lo, SUPER), :].astype(F32)
    neg_hi = neg_ref[pl.ds(sc_hi, SUPER), :].astype(F32)
    near_neg = jnp.where(blk >= SUPER // 2, neg_lo, neg_hi)
    k = kaug_ref[pl.ds(ws, SEL_NEAR), :]
    s_near = _dot(k, aug(near_neg)) + _bias_tile(ws_ref, off, SEL_NEAR)
    vt_near = jnp.concatenate([vt128_ref[kb0], vt128_ref[kb0 + 1]], axis=1)

    s_win = _window_scores(qb, qg, kwin_ref, ww_ref)
    bufs = ((s0_ref, cm0_ref), (s1_ref, cm1_ref), (s2_ref, cm2_ref))
    last = jnp.maximum(n_far - 1, 0)
    scores(0, *bufs[0])
    scores(jnp.minimum(1, last), *bufs[1])
    update(s_near, jnp.max(s_near, axis=0, keepdims=True), vt_near)
    _window_finish(qb, s_win, vwt_ref, owin_ref)

    def far_triple(c):
        for j in range(3):
            scores(jnp.minimum(c + j + 2, last), *bufs[(j + 2) % 3])
            s_ref, cm_ref = bufs[j]
            update(s_ref[...], cm_ref[...], vt512_ref[c + j])

    def far_six(i, carry):
        far_triple(6 * i)
        far_triple(6 * i + 3)
        return carry

    lax.fori_loop(0, n_far // 6, far_six, 0)
    done = (n_far // 6) * 6

    @pl.when(n_far - done >= 3)
    def _():
        far_triple(done)

    done = (n_far // 3) * 3
    for j in range(2):
        @pl.when(n_far - done > j)
        def _():
            s_ref, cm_ref = bufs[j]
            update(s_ref[...], cm_ref[...], vt512_ref[done + j])

    acc = acc_ref[...]
    _store_heads(out_ref, acc[0:HEAD_DIM] * (1.0 / acc[HEAD_DIM:HEAD_DIM + 1]))


def _sel_win_attention(qhi_t, neg_t, kaug, vt512, vt128, ws_t, ext_q, kwin, vwt, ww_t):
    s = qhi_t.shape[1]
    g = N_GROUPS
    nqb = s // Q_BLOCK
    nsbp = neg_t.shape[1]
    gr = GROUP_SIZE * HEAD_DIM
    once = pl.Buffered(1)
    return pl.pallas_call(
        _sel_kernel,
        grid=(g, nqb),
        in_specs=[pl.BlockSpec((gr, Q_BLOCK), lambda gi, qb: (gi, qb)),
                  pl.BlockSpec((None, nsbp, Q_BLOCK), lambda gi, qb: (gi, 0, qb)),
                  pl.BlockSpec((None, s, 256), lambda gi, qb: (gi, 0, 0), pipeline_mode=once),
                  pl.BlockSpec((None, s // KV_CHUNK, VT_ROWS, KV_CHUNK), lambda gi, qb: (gi, 0, 0, 0),
                               pipeline_mode=once),
                  pl.BlockSpec((None, s // 128, VT_ROWS, 128), lambda gi, qb: (gi, 0, 0, 0),
                               pipeline_mode=once),
                  pl.BlockSpec((GROUP_SIZE,) + ws_t.shape[1:], lambda gi, qb: (gi, 0, 0)),
                  pl.BlockSpec((None, HEAD_DIM, GQ), lambda gi, qb: (gi, 0, 0)),
                  pl.BlockSpec((None, s, 128), lambda gi, qb: (gi, 0, 0), pipeline_mode=once),
                  pl.BlockSpec((None, s // 128, VT_ROWS, 128), lambda gi, qb: (gi, 0, 0, 0),
                               pipeline_mode=once),
                  pl.BlockSpec((GROUP_SIZE,) + ww_t.shape[1:], lambda gi, qb: (gi, 0, 0))],
        out_specs=[pl.BlockSpec((gr, Q_BLOCK), lambda gi, qb: (gi, qb)),
                   pl.BlockSpec((gr, Q_BLOCK), lambda gi, qb: (gi, qb))],
        out_shape=[jax.ShapeDtypeStruct((NSA_WIDTH, s), F32),
                   jax.ShapeDtypeStruct((NSA_WIDTH, s), F32)],
        scratch_shapes=[pltpu.VMEM((nsbp // SUPER, 256, GQ), BF16),
                        pltpu.VMEM((1, GQ), F32),
                        pltpu.VMEM((VT_ROWS, GQ), F32),
                        pltpu.VMEM((KV_CHUNK, GQ), F32),
                        pltpu.VMEM((KV_CHUNK, GQ), F32),
                        pltpu.VMEM((KV_CHUNK, GQ), F32),
                        pltpu.VMEM((1, GQ), F32),
                        pltpu.VMEM((1, GQ), F32),
                        pltpu.VMEM((1, GQ), F32)],
        compiler_params=_cparams(2),
        name="sel_win_attention",
    )(qhi_t, neg_t, kaug, vt512, vt128, ws_t, ext_q, kwin, vwt, ww_t)


def _window_scores(qb, qg, kwin_ref, ww_ref):
    kb0 = jnp.maximum(qb - WINDOW // Q_BLOCK, 0)
    ws = pl.multiple_of(kb0 * Q_BLOCK, Q_BLOCK)
    off = pl.multiple_of(jnp.maximum(WINDOW - qb * Q_BLOCK, 0), Q_BLOCK)
    qx = jnp.concatenate([qg, jnp.zeros_like(qg)], axis=0)
    return _dot(kwin_ref[pl.ds(ws, WIN_KEYS), :], qx) + _bias_tile(ww_ref, off, WIN_KEYS)


def _window_finish(qb, s, vwt_ref, out_ref):
    kb0 = jnp.maximum(qb - WINDOW // Q_BLOCK, 0)
    m = jnp.max(s, axis=0, keepdims=True)
    p = jnp.exp2(s - m).astype(BF16)
    vt = jnp.concatenate([vwt_ref[kb0 + j] for j in range(WIN_KEYS // Q_BLOCK)], axis=1)
    acc = _dot(vt, p)
    _store_heads(out_ref, acc[0:HEAD_DIM] * (1.0 / acc[HEAD_DIM:HEAD_DIM + 1]))


LRU_TM = 256


def _lru_kernel(u_ref, cw_ref, cb_ref, wa_ref, ba_ref, wx_ref, bx_ref, lam_ref, h_ref,
                tail_ref, hprev_ref, uc_ref):
    tm = LRU_TM

    @pl.when(pl.program_id(0) == 0)
    def _():
        tail_ref[...] = jnp.zeros(tail_ref.shape, F32)
        hprev_ref[...] = jnp.zeros(hprev_ref.shape, F32)

    u = u_ref[...]
    tail = tail_ref[...]
    row8 = lax.broadcasted_iota(jnp.int32, (8, LRU_WIDTH), 0)
    uc = cb_ref[...] + u * cw_ref[CONV_WIDTH - 1:CONV_WIDTH, :]
    uc_head = uc[0:8]
    for j in range(1, CONV_WIDTH):
        w_j = cw_ref[CONV_WIDTH - 1 - j:CONV_WIDTH - j, :]
        sh = pltpu.roll(u, j, 0)
        uc = uc + sh * w_j
        uc_head = uc_head + jnp.where(row8 < j, pltpu.roll(tail, j, 0), sh[0:8]) * w_j
    tail_ref[...] = u[tm - 8:tm]
    uc_ref[...] = uc
    uc_ref[0:8, :] = uc_head
    uc = uc_ref[...]

    ucb = uc.astype(BF16)
    r = _sigmoid(_dot(ucb, wa_ref[...]) + ba_ref[...])
    ig = _sigmoid(_dot(ucb, wx_ref[...]) + bx_ref[...])
    nl = -lam_ref[...]
    softplus = jnp.maximum(nl, 0.0) + jnp.log(1.0 + jnp.exp(-jnp.abs(nl)))
    a = jnp.exp((-LRU_C * r) * softplus)
    b = jnp.sqrt(1.0 - a * a) * (ig * uc)

    a = a.reshape(tm // 8, 8, LRU_WIDTH)
    b = b.reshape(tm // 8, 8, LRU_WIDTH)
    sub = lax.broadcasted_iota(jnp.int32, a.shape, 1)
    for step in (1, 2, 4):
        a_s = pltpu.roll(a, step, 1)
        b_s = pltpu.roll(b, step, 1)
        ok = sub >= step
        b = jnp.where(ok, a * b_s + b, b)
        a = jnp.where(ok, a * a_s, a)
    carry = hprev_ref[...]
    for j in range(tm // 8):
        hj = a[j] * carry + b[j]
        h_ref[8 * j:8 * j + 8, :] = hj
        carry = hj[7:8]
    hprev_ref[...] = carry


def _rglru(u, conv_w, conv_b, wa_bd, ba, wx_bd, bx, lam):
    s = u.shape[0]
    tm = LRU_TM
    row = lambda v: v.reshape(1, LRU_WIDTH)
    full = lambda shape: pl.BlockSpec(shape, lambda i: (0, 0))
    return pl.pallas_call(
        _lru_kernel,
        grid=(s // tm,),
        in_specs=[pl.BlockSpec((tm, LRU_WIDTH), lambda i: (i, 0)),
                  full((CONV_WIDTH, LRU_WIDTH)), full((1, LRU_WIDTH)),
                  full((LRU_WIDTH, LRU_WIDTH)), full((1, LRU_WIDTH)),
                  full((LRU_WIDTH, LRU_WIDTH)), full((1, LRU_WIDTH)),
                  full((1, LRU_WIDTH))],
        out_specs=pl.BlockSpec((tm, LRU_WIDTH), lambda i: (i, 0)),
        out_shape=jax.ShapeDtypeStruct((s, LRU_WIDTH), F32),
        scratch_shapes=[pltpu.VMEM((8, LRU_WIDTH), F32), pltpu.VMEM((1, LRU_WIDTH), F32),
                        pltpu.VMEM((tm, LRU_WIDTH), F32)],
        compiler_params=_cparams(1),
        name="rglru",
    )(u, conv_w, row(conv_b), wa_bd, row(ba), wx_bd, row(bx), row(lam))


OUT_TM = 512


def _out_kernel(x_ref, g_ref, oc_ref, os_ref, ow_ref, brt_ref, h_ref,
                wg_ref, wpa_ref, wpb_ref, wo_ref, y_ref):
    h_in = _normed_input(x_ref, g_ref)
    gate = lambda j: _dot(h_in, wg_ref[:, GATE_OFFS[j]:GATE_OFFS[j + 1]])
    brt = brt_ref[...]
    parts = []
    for h in range(N_HEADS):
        rows = slice(h * HEAD_DIM, (h + 1) * HEAD_DIM)
        acc = None
        for b, o_ref in enumerate((oc_ref, os_ref, ow_ref)):
            term = brt[b * N_HEADS + h:b * N_HEADS + h + 1, :] * o_ref[rows, :]
            acc = term if acc is None else acc + term
        parts.append(acc)
    gn = gate(0)
    ya = jnp.concatenate(parts, axis=0).T * (gn * _sigmoid(gn))
    y_a = _dot(ya.astype(BF16), wpa_ref[...])
    gl = gate(1)
    y_b = _dot((h_ref[...] * (gl * _sigmoid(gl))).astype(BF16), wpb_ref[...])
    mg = gate(2)
    m = _sigmoid(mg[:, :D_MODEL]) * y_a + _sigmoid(mg[:, D_MODEL:]) * y_b
    y_ref[...] = x_ref[...] + _dot(m.astype(BF16), wo_ref[...])


def _output(x2, norm_gain, oc_t, os_t, ow_t, br_t, h_lru, w_gate, wpa, wpb, wo):
    s = x2.shape[0]
    tm = OUT_TM
    rows = lambda n: pl.BlockSpec((tm, n), lambda i: (i, 0))
    cols = lambda n: pl.BlockSpec((n, tm), lambda i: (0, i))
    full = lambda a: pl.BlockSpec(a.shape, lambda i: (0, 0))
    return pl.pallas_call(
        _out_kernel,
        grid=(s // tm,),
        in_specs=[rows(D_MODEL), pl.BlockSpec((1, D_MODEL), lambda i: (0, 0)),
                  cols(NSA_WIDTH), cols(NSA_WIDTH), cols(NSA_WIDTH), cols(32), rows(LRU_WIDTH),
                  full(w_gate), full(wpa), full(wpb), full(wo)],
        out_specs=rows(D_MODEL),
        out_shape=jax.ShapeDtypeStruct((s, D_MODEL), F32),
        compiler_params=_cparams(1),
        name="output",
    )(x2, norm_gain.reshape(1, D_MODEL), oc_t, os_t, ow_t, br_t, h_lru, w_gate, wpa, wpb, wo)


def _t5_bucket_table(n_dist):
    n = np.arange(n_dist)
    max_exact = N_BUCKETS // 2
    nf = np.maximum(n, 1).astype(np.float32)
    large = max_exact + (np.log(nf / np.float32(max_exact)) / np.float32(math.log(MAX_DISTANCE / max_exact))
                         * np.float32(N_BUCKETS - max_exact)).astype(np.int32)
    return np.where(n < max_exact, n, np.minimum(large, N_BUCKETS - 1))


def _bias_tiles_kernel(relb_ref, bw_ref, bs_ref, bc_ref, ww_ref, ws_ref, wc_ref):
    h = pl.program_id(0)
    far = relb_ref[N_BUCKETS - 1, h]
    for bk_ref, out_ref, rel in ((bw_ref, ww_ref, False), (bs_ref, ws_ref, True), (bc_ref, wc_ref, True)):
        bk = bk_ref[...]
        acc = jnp.full(bk.shape, NEG, F32)
        for b in range(N_BUCKETS):
            val = relb_ref[b, h] - far if rel else relb_ref[b, h]
            acc = jnp.where(bk == b, val * LOG2E, acc)
        out_ref[...] = acc


def _bias_tables(rel_bias):
    bucket = _t5_bucket_table(2 * WINDOW)
    i = np.arange(Q_BLOCK)[None, :]

    def index_tile(dist, valid):
        return jnp.asarray(np.where(valid, bucket[np.clip(dist, 0, bucket.size - 1)], -1).astype(np.int32))

    u = np.arange(WINDOW + WIN_KEYS)[:, None]
    d_win = i - u + WINDOW
    bw = index_tile(d_win, (d_win >= 0) & (d_win < WINDOW))
    u = np.arange(SEL_NEAR + Q_BLOCK)[:, None]
    d_sel = i - u + Q_BLOCK
    bs = index_tile(d_sel, d_sel >= 0)
    u = np.arange(CMP_NEAR + 16)[:, None]
    d_cmp = i - CMP_STRIDE * (u - 16) - (CMP_BLOCK - 1)
    bc = index_tile(d_cmp, d_cmp >= 0)
    full = lambda a: pl.BlockSpec(a.shape, lambda h: (0, 0))
    per_head = lambda a: pl.BlockSpec((None,) + a.shape, lambda h: (h, 0, 0))
    ww, ws, wc = pl.pallas_call(
        _bias_tiles_kernel,
        grid=(N_HEADS,),
        in_specs=[pl.BlockSpec(memory_space=pltpu.SMEM), full(bw), full(bs), full(bc)],
        out_specs=[per_head(bw), per_head(bs), per_head(bc)],
        out_shape=[jax.ShapeDtypeStruct((N_HEADS,) + a.shape, F32) for a in (bw, bs, bc)],
        compiler_params=_cparams(1),
        name="bias_tiles",
    )(rel_bias, bw, bs, bc)
    b_far = rel_bias[N_BUCKETS - 1] * LOG2E
    hi = b_far.astype(BF16)
    lo = (b_far - hi.astype(F32)).astype(BF16)
    ext = jnp.zeros((N_HEADS, HEAD_DIM, Q_BLOCK), BF16)
    ext = ext.at[:, 0, :].set(hi[:, None]).at[:, 1, :].set(lo[:, None])
    ext = ext.reshape(N_GROUPS, GROUP_SIZE, HEAD_DIM, Q_BLOCK).transpose(0, 2, 1, 3)
    return ww, ws, wc, ext.reshape(N_GROUPS, HEAD_DIM, GQ)


def _block_diag(w):
    n, d, e = w.shape
    eye = jnp.eye(n, dtype=w.dtype)
    return (eye[:, None, :, None] * w[:, :, None, :]).reshape(n * d, n * e)


def kernel(x, norm_gain, w_in, q_norm_gain, k_norm_gain, cmp_pe, cmp_w1, cmp_b1, cmp_w2, rel_bias,
           conv_w, conv_b, lru_wa, lru_ba, lru_wx, lru_bx, lru_lambda, w_proj_a, w_proj_b, w_out):
    bsz, s, _ = x.shape
    assert bsz == 1 and s % 1024 == 0 and s >= 1024
    x2 = x.reshape(s, D_MODEL)
    nsb = s // SEL_BLOCK
    nsbp = -(-nsb // SUPER) * SUPER

    o = IN_OFFS
    w16 = w_in.astype(BF16)
    pad = jnp.zeros((D_MODEL, BR_PAD - 3 * N_HEADS), BF16)
    w_front = jnp.concatenate([w16[:, o[0]:o[2]], w16[:, o[3]:o[4]], pad, w16[:, o[4]:o[5]]], axis=1)
    w_gate = jnp.concatenate([w16[:, o[2]:o[3]], w16[:, o[5]:o[7]]], axis=1)

    scale = HEAD_DIM ** -0.5 * LOG2E
    qgain_col = jnp.tile(q_norm_gain * scale, N_HEADS).reshape(NSA_WIDTH, 1)
    kgain_rows = jnp.tile(k_norm_gain, (1, N_GROUPS))
    (qhi_t, qlo_t, br_t, kaug, vt512, vt128, kwin, vwt, kvc, u_lru) = _front(
        x2, norm_gain, w_front, qgain_col, kgain_rows)

    w2p = jnp.pad(cmp_w2, ((0, 0), (0, 0), (0, 128 - HEAD_DIM)))
    kc_gain = jnp.pad(k_norm_gain[0], (0, 128 - HEAD_DIM)).reshape(1, 128)
    kc_cat, vct = _compress(kvc, cmp_w1, cmp_b1.reshape(2, 1, CMP_HIDDEN), w2p,
                            cmp_pe.reshape(2, 1, CMP_BLOCK * HEAD_DIM), kc_gain)

    ww_t, ws_t, wc_t, ext_q = _bias_tables(rel_bias)
    oc_t, neg_t = _cmp_attention(qhi_t, qlo_t, kc_cat, vct, wc_t, ext_q, nsbp)
    os_t, ow_t = _sel_win_attention(qhi_t, neg_t, kaug, vt512, vt128, ws_t, ext_q, kwin, vwt, ww_t)

    h_lru = _rglru(u_lru, conv_w, conv_b, _block_diag(lru_wa).astype(BF16), lru_ba,
                   _block_diag(lru_wx).astype(BF16), lru_bx, lru_lambda)

    y = _output(x2, norm_gain, oc_t, os_t, ow_t, br_t, h_lru, w_gate,
                w_proj_a.astype(BF16), w_proj_b.astype(BF16), w_out.astype(BF16))
    return y.reshape(bsz, s, D_MODEL)
```

```python
import functools
import math

import numpy as np
import jax
import jax.numpy as jnp
from jax import lax
from jax.experimental import pallas as pl
from jax.experimental.pallas import tpu as pltpu

F32 = jnp.float32
BF16 = jnp.bfloat16

D_MODEL = 1024
N_HEADS = 8
N_GROUPS = 2
GROUP_SIZE = N_HEADS // N_GROUPS
HEAD_DIM = 64
NSA_WIDTH = N_HEADS * HEAD_DIM
KV_WIDTH = N_GROUPS * HEAD_DIM
CMP_STRIDE = 16
CMP_BLOCK = 32
CMP_HIDDEN = 256
SEL_BLOCK = 64
SEL_PER_CMP = SEL_BLOCK // CMP_STRIDE
N_SELECT = 16
WINDOW = 512
Q_BLOCK = 128
LRU_WIDTH = 512
LRU_BLOCKS = 8
CONV_WIDTH = 4
LRU_C = 8.0
N_BUCKETS = 32
MAX_DISTANCE = 128
EPS = 1e-6
NEG = -1e30
M_INIT = -5e29
LOG2E = 1.4426950408889634

GQ = GROUP_SIZE * Q_BLOCK
SUPER = 128
KV_CHUNK = 512
SEL_NEAR = 2 * Q_BLOCK
WIN_KEYS = WINDOW + Q_BLOCK
CMP_NEAR = 24
N_PICK = N_SELECT - 3
PICKED = -2.0
VT_ROWS = 80

IN_OFFS = tuple(int(v) for v in np.cumsum(
    (0, NSA_WIDTH, 6 * KV_WIDTH, NSA_WIDTH, 3 * N_HEADS, LRU_WIDTH, LRU_WIDTH, 2 * D_MODEL)))
BR_PAD = 128
FRONT_OFFS = tuple(int(v) for v in np.cumsum((0, NSA_WIDTH, 6 * KV_WIDTH, BR_PAD, LRU_WIDTH)))
GATE_OFFS = tuple(int(v) for v in np.cumsum((0, NSA_WIDTH, LRU_WIDTH, 2 * D_MODEL)))

VMEM_LIMIT = 56 * 1024 * 1024


def _cparams(n_axes):
    return pltpu.CompilerParams(dimension_semantics=("arbitrary",) * n_axes,
                                vmem_limit_bytes=VMEM_LIMIT)


def _dot(a, b):
    return jnp.dot(a, b, preferred_element_type=F32)


def _sigmoid(x):
    return 0.5 * jnp.tanh(0.5 * x) + 0.5


FRONT_TM = 512


def _normed_input(x_ref, g_ref):
    x = x_ref[...]
    ms = jnp.mean(x * x, axis=-1, keepdims=True)
    return (x * lax.rsqrt(ms + EPS) * g_ref[...]).astype(BF16)


def _group_rms(k, gain_row):
    sq = k * k
    lane = lax.broadcasted_iota(jnp.int32, k.shape, 1)
    lo = lane < HEAD_DIM
    s0 = jnp.sum(jnp.where(lo, sq, 0.0), axis=-1, keepdims=True)
    s1 = jnp.sum(jnp.where(lo, 0.0, sq), axis=-1, keepdims=True)
    inv = jnp.where(lo, lax.rsqrt(s0 / HEAD_DIM + EPS), lax.rsqrt(s1 / HEAD_DIM + EPS))
    return k * inv * gain_row


def _front_kernel(x_ref, g_ref, w_ref, qg_ref, kg_ref,
                  qhi_ref, qlo_ref, brt_ref, kaug_ref, vt128_ref,
                  kwin_ref, vwt_ref, kvc_ref, u_ref):
    i = pl.program_id(0)
    tm = FRONT_TM
    h_in = _normed_input(x_ref, g_ref)
    proj = lambda a, b: _dot(h_in, w_ref[:, a:b])
    qt = proj(FRONT_OFFS[0], FRONT_OFFS[1]).T
    for h in range(N_HEADS):
        blk = qt[h * HEAD_DIM:(h + 1) * HEAD_DIM]
        ms = jnp.mean(blk * blk, axis=0, keepdims=True)
        qn = blk * lax.rsqrt(ms + EPS) * qg_ref[h * HEAD_DIM:(h + 1) * HEAD_DIM, :]
        hi = qn.astype(BF16)
        qhi_ref[h * HEAD_DIM:(h + 1) * HEAD_DIM, :] = hi
        qlo_ref[h * HEAD_DIM:(h + 1) * HEAD_DIM, :] = (qn - hi.astype(F32)).astype(BF16)
    brt_ref[...] = _sigmoid(proj(FRONT_OFFS[2], FRONT_OFFS[3])).T[:32]
    u_ref[...] = proj(FRONT_OFFS[3], FRONT_OFFS[4])

    kv = proj(FRONT_OFFS[1], FRONT_OFFS[2])
    piece = lambda j: kv[:, j * KV_WIDTH:(j + 1) * KV_WIDTH]
    lane = lax.broadcasted_iota(jnp.int32, (tm, 128), 1)
    row = lax.broadcasted_iota(jnp.int32, (tm, 128), 0) + i * tm
    lo = lane < HEAD_DIM
    ones_cols = jnp.where((lane == HEAD_DIM) | (lane == HEAD_DIM + 1), 1.0, 0.0)
    onehot = jnp.where(lane == (row // SEL_BLOCK) % SUPER, 1.0, 0.0).astype(BF16)
    kvc_ref[0] = piece(0)
    kvc_ref[1] = piece(1)
    kslc = _group_rms(piece(2), kg_ref[1:2, :])
    kwin = _group_rms(piece(4), kg_ref[2:3, :])
    vslt = piece(3).T
    vwit = piece(5).T
    row_t = lax.broadcasted_iota(jnp.int32, (VT_ROWS - HEAD_DIM, tm), 0)
    ones_rows = jnp.where(row_t == 0, 1.0, 0.0)
    for g in range(N_GROUPS):
        sh = lambda a: a if g == 0 else pltpu.roll(a, HEAD_DIM, 1)
        kaug_ref[g, :, 0:128] = onehot
        kaug_ref[g, :, 128:256] = jnp.where(lo, sh(kslc), ones_cols).astype(BF16)
        kwin_ref[g] = jnp.where(lo, sh(kwin), 0.0).astype(BF16)
        vs = jnp.concatenate([vslt[g * HEAD_DIM:(g + 1) * HEAD_DIM], ones_rows], axis=0).astype(BF16)
        vw = jnp.concatenate([vwit[g * HEAD_DIM:(g + 1) * HEAD_DIM], ones_rows], axis=0).astype(BF16)
        for j in range(tm // 128):
            vt128_ref[g, j] = vs[:, j * 128:(j + 1) * 128]
            vwt_ref[g, j] = vw[:, j * 128:(j + 1) * 128]


def _front(x2, norm_gain, w_front, qgain_col, kgain_rows):
    s = x2.shape[0]
    tm = FRONT_TM
    g = N_GROUPS
    outs = [
        jax.ShapeDtypeStruct((NSA_WIDTH, s), BF16),
        jax.ShapeDtypeStruct((NSA_WIDTH, s), BF16),
        jax.ShapeDtypeStruct((32, s), F32),
        jax.ShapeDtypeStruct((g, s, 256), BF16),
        jax.ShapeDtypeStruct((g, s // 128, VT_ROWS, 128), BF16),
        jax.ShapeDtypeStruct((g, s, 128), BF16),
        jax.ShapeDtypeStruct((g, s // 128, VT_ROWS, 128), BF16),
        jax.ShapeDtypeStruct((2, s, KV_WIDTH), F32),
        jax.ShapeDtypeStruct((s, LRU_WIDTH), F32),
    ]
    out_specs = [
        pl.BlockSpec((NSA_WIDTH, tm), lambda i: (0, i)),
        pl.BlockSpec((NSA_WIDTH, tm), lambda i: (0, i)),
        pl.BlockSpec((32, tm), lambda i: (0, i)),
        pl.BlockSpec((g, tm, 256), lambda i: (0, i, 0)),
        pl.BlockSpec((g, tm // 128, VT_ROWS, 128), lambda i: (0, i, 0, 0)),
        pl.BlockSpec((g, tm, 128), lambda i: (0, i, 0)),
        pl.BlockSpec((g, tm // 128, VT_ROWS, 128), lambda i: (0, i, 0, 0)),
        pl.BlockSpec((2, tm, KV_WIDTH), lambda i: (0, i, 0)),
        pl.BlockSpec((tm, LRU_WIDTH), lambda i: (i, 0)),
    ]
    return pl.pallas_call(
        _front_kernel,
        grid=(s // tm,),
        in_specs=[pl.BlockSpec((tm, D_MODEL), lambda i: (i, 0)),
                  pl.BlockSpec((1, D_MODEL), lambda i: (0, 0)),
                  pl.BlockSpec(w_front.shape, lambda i: (0, 0)),
                  pl.BlockSpec((NSA_WIDTH, 1), lambda i: (0, 0)),
                  pl.BlockSpec((3, 128), lambda i: (0, 0))],
        out_specs=out_specs,
        out_shape=outs,
        compiler_params=_cparams(1),
        name="front",
    )(x2, norm_gain.reshape(1, D_MODEL), w_front, qgain_col, kgain_rows)


def _compress_kernel(kvc_ref, w1_ref, b1_ref, w2_ref, pe_ref, kg_ref, kc_ref, vct_ref):
    ncp = kc_ref.shape[1]
    half = CMP_STRIDE * HEAD_DIM
    lane = lax.broadcasted_iota(jnp.int32, (ncp, 128), 1)
    lo = lane < HEAD_DIM

    def chunk_rows(j, g):
        cols = []
        for pair in range(CMP_STRIDE // 2):
            even = kvc_ref[j, pl.ds(2 * pair, ncp, stride=CMP_STRIDE), :]
            odd = kvc_ref[j, pl.ds(2 * pair + 1, ncp, stride=CMP_STRIDE), :]
            if g == 0:
                cols.append(jnp.where(lo, even, pltpu.roll(odd, HEAD_DIM, 1)))
            else:
                cols.append(jnp.where(lo, pltpu.roll(even, HEAD_DIM, 1), odd))
        return jnp.concatenate(cols, axis=1)

    def phi(j, g):
        x = chunk_rows(j, g).astype(BF16)
        w1 = w1_ref[j].astype(BF16)
        a = _dot(x, w1[:half])
        b = _dot(x, w1[half:])
        b_next = pltpu.roll(b, ncp - 1, 0)
        pe8 = jnp.broadcast_to(pe_ref[j], (8, 2 * half)).astype(BF16)
        pe_term = _dot(pe8, w1)[0:1]
        hid = a + b_next + pe_term + b1_ref[j]
        act = (hid * _sigmoid(hid)).astype(BF16)
        return _dot(act, w2_ref[j].astype(BF16))

    ones_cols = jnp.where((lane == HEAD_DIM) | (lane == HEAD_DIM + 1), 1.0, 0.0)
    for g in range(N_GROUPS):
        kc = phi(0, g)
        ms = jnp.sum(kc * kc, axis=-1, keepdims=True) / HEAD_DIM
        kc = kc * lax.rsqrt(ms + EPS) * kg_ref[...]
        hi = kc.astype(BF16).astype(F32)
        lo_part = (kc - hi).astype(BF16).astype(F32)
        kc_ref[g, :, 0:128] = (hi + pltpu.roll(lo_part, HEAD_DIM, 1)).astype(BF16)
        kc_ref[g, :, 128:256] = jnp.where(lo, hi, ones_cols).astype(BF16)
        vt = phi(1, g).T
        row_t = lax.broadcasted_iota(jnp.int32, vt.shape, 0)
        vct_ref[g] = jnp.where(row_t == HEAD_DIM, 1.0, vt).astype(BF16)


def _compress(kvc, w1, b1, w2p, pe_flat, kgain_row):
    s = kvc.shape[1]
    g = N_GROUPS
    ncp = s // CMP_STRIDE
    full = lambda a: pl.BlockSpec(a.shape, lambda i: (0,) * a.ndim)
    return pl.pallas_call(
        _compress_kernel,
        grid=(1,),
        in_specs=[pl.BlockSpec(kvc.shape, lambda i: (0, 0, 0), pipeline_mode=pl.Buffered(1)),
                  full(w1), full(b1), full(w2p), full(pe_flat), full(kgain_row)],
        out_specs=[pl.BlockSpec((g, ncp, 256), lambda i: (0, 0, 0)),
                   pl.BlockSpec((g, 128, ncp), lambda i: (0, 0, 0))],
        out_shape=[jax.ShapeDtypeStruct((g, ncp, 256), BF16),
                   jax.ShapeDtypeStruct((g, 128, ncp), BF16)],
        compiler_params=_cparams(1),
        name="compress",
    )(kvc, w1, b1, w2p, pe_flat, kgain_row)


def _heads_to_lanes(q):
    return jnp.concatenate([q[r * HEAD_DIM:(r + 1) * HEAD_DIM] for r in range(GROUP_SIZE)], axis=1)


def _store_heads(out_ref, o):
    for r in range(GROUP_SIZE):
        out_ref[r * HEAD_DIM:(r + 1) * HEAD_DIM, :] = o[0:HEAD_DIM, r * Q_BLOCK:(r + 1) * Q_BLOCK]


def _bias_tile(w_ref, off, n, head0=0):
    return jnp.concatenate([w_ref[head0 + r, pl.ds(off, n), :] for r in range(GROUP_SIZE)], axis=1)


def _cmp_kernel(qhi_ref, qlo_ref, kc_ref, vct_ref, wc_ref, ext_ref, ocmp_ref, neg_ref, s_ref, imp_ref):
    qb = pl.program_id(0)
    ncp = kc_ref.shape[1]
    nsbp = neg_ref.shape[1]
    gr = GROUP_SIZE * HEAD_DIM
    ws = pl.multiple_of(jnp.maximum(8 * qb - 16, 0), 8)
    off = pl.multiple_of(jnp.maximum(16 - 8 * qb, 0), 8)
    lim = ws + CMP_NEAR
    n_cls = min(8, ncp // 128)
    per_cls = ncp // n_cls

    def attend(nrows):
        nblk = nrows // SEL_PER_CMP
        band = nrows - per_cls
        rows = lax.broadcasted_iota(jnp.int32, (per_cls, GQ), 0) + band
        blk = lax.broadcasted_iota(jnp.int32, (nblk, Q_BLOCK), 0)
        lane = lax.broadcasted_iota(jnp.int32, (nblk, Q_BLOCK), 1)
        cur = 2 * qb + jnp.where(lane >= SEL_BLOCK, 1, 0)
        forced = jnp.where(blk == 0, 1.0, jnp.where(blk == cur, 1.0, jnp.where(blk == cur - 1, 1.0, 0.0)))
        carry = []
        for g in range(N_GROUPS):
            qhi = qhi_ref[g * gr:(g + 1) * gr, :]
            qlo = qlo_ref[g * gr:(g + 1) * gr, :]
            cols = []
            for r in range(GROUP_SIZE):
                a = qhi[r * HEAD_DIM:(r + 1) * HEAD_DIM]
                b = qlo[r * HEAD_DIM:(r + 1) * HEAD_DIM]
                cols.append(jnp.concatenate([a, a, b], axis=0))
            qcat = jnp.concatenate([jnp.concatenate(cols, axis=1), ext_ref[g]], axis=0)
            s_g = s_ref.at[g]
            s_g[0:nrows, :] = _dot(kc_ref[g, 0:nrows, :], qcat)
            s_g[pl.ds(ws, CMP_NEAR), :] += _bias_tile(wc_ref, off, CMP_NEAR, g * GROUP_SIZE)
            s_g[band:nrows, :] = jnp.where(rows < lim, s_g[band:nrows, :], NEG)
            s = s_g[0:nrows, :]
            m = jnp.maximum(jnp.max(s, axis=0, keepdims=True), M_INIT)
            p = jnp.exp2(s - m)
            acc = _dot(vct_ref[g, 0:VT_ROWS, 0:nrows], p.astype(BF16))
            l = acc[HEAD_DIM:HEAD_DIM + 1]
            inv_l = jnp.where(l > 0.0, 1.0 / l, 0.0)
            _store_heads(ocmp_ref.at[g * gr:(g + 1) * gr], acc[0:HEAD_DIM] * inv_l)
            imp = None
            for r in range(GROUP_SIZE):
                cs = slice(r * Q_BLOCK, (r + 1) * Q_BLOCK)
                term = p[:, cs] * inv_l[:, cs]
                imp = term if imp is None else imp + term
            imp_g = imp_ref.at[g]
            imp_g[0:nrows, :] = imp
            strided = [imp_g[pl.ds(k, nblk, stride=SEL_PER_CMP), :] for k in range(SEL_PER_CMP)]
            prev_last = jnp.where(blk == 0, 0.0, pltpu.roll(strided[SEL_PER_CMP - 1], 1, 0))
            impb = prev_last + strided[0] + strided[1] + strided[2] + strided[3]
            carry.append(jnp.where(blk < cur - 1, jnp.where(blk > 0, impb, -1.0), -1.0))

        def pick_one(_, cands):
            out = []
            for sc in cands:
                mx = jnp.max(sc, axis=0, keepdims=True)
                hit_blk = jnp.where(sc == mx, jnp.where(mx >= 0.0, blk, nblk), nblk)
                first = jnp.min(hit_blk, axis=0, keepdims=True)
                out.append(jnp.where(blk == first, PICKED, sc))
            return tuple(out)

        if nblk * N_GROUPS <= 2 * SUPER:
            carry = lax.fori_loop(0, N_PICK, pick_one, tuple(carry))
        else:
            carry = [lax.fori_loop(0, N_PICK, pick_one, (c,))[0] for c in carry]
        for g in range(N_GROUPS):
            picked = carry[g] == PICKED
            sel_neg = jnp.where(picked, 0.0, jnp.where(blk <= cur, jnp.where(forced > 0.5, 0.0, NEG), NEG))
            neg_ref[g, 0:nblk, :] = sel_neg.astype(BF16)
            if nblk < nsbp:
                neg_ref[g, nblk:nsbp, :] = jnp.full((nsbp - nblk, Q_BLOCK), NEG, BF16)

    cls = (lim + per_cls - 1) // per_cls
    for k in range(1, n_cls + 1):
        pl.when(cls == k)(functools.partial(attend, k * per_cls))


def _cmp_attention(qhi_t, qlo_t, kc_cat, vct, wc_t, ext_q, nsbp):
    s = qhi_t.shape[1]
    g = N_GROUPS
    ncp = kc_cat.shape[1]
    full = lambda a: pl.BlockSpec(a.shape, lambda qb: (0,) * a.ndim)
    return pl.pallas_call(
        _cmp_kernel,
        grid=(s // Q_BLOCK,),
        in_specs=[pl.BlockSpec((NSA_WIDTH, Q_BLOCK), lambda qb: (0, qb)),
                  pl.BlockSpec((NSA_WIDTH, Q_BLOCK), lambda qb: (0, qb)),
                  full(kc_cat), full(vct), full(wc_t), full(ext_q)],
        out_specs=[pl.BlockSpec((NSA_WIDTH, Q_BLOCK), lambda qb: (0, qb)),
                   pl.BlockSpec((g, nsbp, Q_BLOCK), lambda qb: (0, 0, qb))],
        out_shape=[jax.ShapeDtypeStruct((NSA_WIDTH, s), F32),
                   jax.ShapeDtypeStruct((g, nsbp, s), BF16)],
        scratch_shapes=[pltpu.VMEM((g, ncp, GQ), F32), pltpu.VMEM((g, ncp, Q_BLOCK), F32)],
        compiler_params=_cparams(1),
        name="cmp_select",
    )(qhi_t, qlo_t, kc_cat, vct, wc_t, ext_q)


def _sel_kernel(qhi_ref, neg_ref, kaug_ref, vt128_ref, ws_ref, ext_ref,
                kwin_ref, vwt_ref, ww_ref, out_ref, owin_ref,
                qaug_ref, m_ref, acc_ref, s0_ref, s1_ref, s2_ref, cm0_ref, cm1_ref, cm2_ref):
    qb = pl.program_id(1)
    nsbp = neg_ref.shape[0]
    nsc = nsbp // SUPER
    qg = _heads_to_lanes(qhi_ref[...])
    qx = jnp.concatenate([qg, ext_ref[...]], axis=0)
    blk = lax.broadcasted_iota(jnp.int32, (SUPER, Q_BLOCK), 0)

    def aug(neg_rows):
        tiled = jnp.concatenate([neg_rows] * GROUP_SIZE, axis=1).astype(BF16)
        return jnp.concatenate([tiled, qx], axis=0)

    for sc in range(nsc):
        neg_rows = neg_ref[sc * SUPER:(sc + 1) * SUPER, :].astype(F32)
        qaug_ref[sc] = aug(jnp.where(blk + sc * SUPER >= 2 * qb - 2, NEG, neg_rows))

    m_ref[...] = jnp.full(m_ref.shape, M_INIT, F32)
    acc_ref[...] = jnp.zeros(acc_ref.shape, F32)

    def update(s, col_max, vt):
        m_old = m_ref[...]
        m_new = jnp.maximum(m_old, col_max)
        p = jnp.exp2(s - m_new).astype(BF16)
        acc_ref[...] = acc_ref[...] * jnp.exp2(m_old - m_new) + _dot(vt, p)
        m_ref[...] = m_new

    n_far = (jnp.maximum(qb - 1, 0) * Q_BLOCK + KV_CHUNK - 1) // KV_CHUNK

    def vt_chunk(c):
        n = KV_CHUNK // Q_BLOCK
        return jnp.concatenate([vt128_ref[c * n + j] for j in range(n)], axis=1)

    def scores(c, dst_ref, max_ref):
        k = kaug_ref[pl.ds(pl.multiple_of(c * KV_CHUNK, KV_CHUNK), KV_CHUNK), :]
        sc = (c * (KV_CHUNK // SEL_BLOCK)) // SUPER
        s = _dot(k, qaug_ref[sc])
        dst_ref[...] = s
        max_ref[...] = jnp.max(s, axis=0, keepdims=True)

    kb0 = jnp.maximum(qb - 1, 0)
    ws = pl.multiple_of(kb0 * Q_BLOCK, Q_BLOCK)
    off = pl.multiple_of(jnp.where(qb == 0, Q_BLOCK, 0), Q_BLOCK)
    b_lo = 2 * kb0
    sc_lo = pl.multiple_of((b_lo // SUPER) * SUPER, SUPER)
    sc_hi = pl.multiple_of(((b_lo + 3) // SUPER) * SUPER, SUPER)
    neg_lo = neg_ref[pl.ds(sc_lo, SUPER), :].astype(F32)
    neg_hi = neg_ref[pl.ds(sc_hi, SUPER), :].astype(F32)
    near_neg = jnp.where(blk >= SUPER // 2, neg_lo, neg_hi)
    k = kaug_ref[pl.ds(ws, SEL_NEAR), :]
    s_near = _dot(k, aug(near_neg)) + _bias_tile(ws_ref, off, SEL_NEAR)
    vt_near = jnp.concatenate([vt128_ref[kb0], vt128_ref[kb0 + 1]], axis=1)

    s_win = _window_scores(qb, qg, kwin_ref, ww_ref)
    bufs = ((s0_ref, cm0_ref), (s1_ref, cm1_ref), (s2_ref, cm2_ref))
    last = jnp.maximum(n_far - 1, 0)
    scores(0, *bufs[0])
    scores(jnp.minimum(1, last), *bufs[1])
    update(s_near, jnp.max(s_near, axis=0, keepdims=True), vt_near)
    _window_finish(qb, s_win, vwt_ref, owin_ref)

    def far_triple(c):
        for j in range(3):
            scores(jnp.minimum(c + j + 2, last), *bufs[(j + 2) % 3])
            s_ref, cm_ref = bufs[j]
            update(s_ref[...], cm_ref[...], vt_chunk(c + j))

    def far_six(i, carry):
        far_triple(6 * i)
        far_triple(6 * i + 3)
        return carry

    lax.fori_loop(0, n_far // 6, far_six, 0)
    done = (n_far // 6) * 6

    @pl.when(n_far - done >= 3)
    def _():
        far_triple(done)

    done = (n_far // 3) * 3
    for j in range(2):
        @pl.when(n_far - done > j)
        def _():
            s_ref, cm_ref = bufs[j]
            update(s_ref[...], cm_ref[...], vt_chunk(done + j))

    acc = acc_ref[...]
    _store_heads(out_ref, acc[0:HEAD_DIM] * (1.0 / acc[HEAD_DIM:HEAD_DIM + 1]))


def _sel_win_attention(qhi_t, neg_t, kaug, vt128, ws_t, ext_q, kwin, vwt, ww_t):
    s = qhi_t.shape[1]
    g = N_GROUPS
    nqb = s // Q_BLOCK
    nsbp = neg_t.shape[1]
    gr = GROUP_SIZE * HEAD_DIM
    once = pl.Buffered(1)
    return pl.pallas_call(
        _sel_kernel,
        grid=(g, nqb),
        in_specs=[pl.BlockSpec((gr, Q_BLOCK), lambda gi, qb: (gi, qb)),
                  pl.BlockSpec((None, nsbp, Q_BLOCK), lambda gi, qb: (gi, 0, qb)),
                  pl.BlockSpec((None, s, 256), lambda gi, qb: (gi, 0, 0), pipeline_mode=once),
                  pl.BlockSpec((None, s // 128, VT_ROWS, 128), lambda gi, qb: (gi, 0, 0, 0),
                               pipeline_mode=once),
                  pl.BlockSpec((GROUP_SIZE,) + ws_t.shape[1:], lambda gi, qb: (gi, 0, 0)),
                  pl.BlockSpec((None, HEAD_DIM, GQ), lambda gi, qb: (gi, 0, 0)),
                  pl.BlockSpec((None, s, 128), lambda gi, qb: (gi, 0, 0), pipeline_mode=once),
                  pl.BlockSpec((None, s // 128, VT_ROWS, 128), lambda gi, qb: (gi, 0, 0, 0),
                               pipeline_mode=once),
                  pl.BlockSpec((GROUP_SIZE,) + ww_t.shape[1:], lambda gi, qb: (gi, 0, 0))],
        out_specs=[pl.BlockSpec((gr, Q_BLOCK), lambda gi, qb: (gi, qb)),
                   pl.BlockSpec((gr, Q_BLOCK), lambda gi, qb: (gi, qb))],
        out_shape=[jax.ShapeDtypeStruct((NSA_WIDTH, s), F32),
                   jax.ShapeDtypeStruct((NSA_WIDTH, s), F32)],
        scratch_shapes=[pltpu.VMEM((nsbp // SUPER, 256, GQ), BF16),
                        pltpu.VMEM((1, GQ), F32),
                        pltpu.VMEM((VT_ROWS, GQ), F32),
                        pltpu.VMEM((KV_CHUNK, GQ), F32),
                        pltpu.VMEM((KV_CHUNK, GQ), F32),
                        pltpu.VMEM((KV_CHUNK, GQ), F32),
                        pltpu.VMEM((1, GQ), F32),
                        pltpu.VMEM((1, GQ), F32),
                        pltpu.VMEM((1, GQ), F32)],
        compiler_params=_cparams(2),
        name="sel_win_attention",
    )(qhi_t, neg_t, kaug, vt128, ws_t, ext_q, kwin, vwt, ww_t)


def _window_scores(qb, qg, kwin_ref, ww_ref):
    kb0 = jnp.maximum(qb - WINDOW // Q_BLOCK, 0)
    ws = pl.multiple_of(kb0 * Q_BLOCK, Q_BLOCK)
    off = pl.multiple_of(jnp.maximum(WINDOW - qb * Q_BLOCK, 0), Q_BLOCK)
    qx = jnp.concatenate([qg, jnp.zeros_like(qg)], axis=0)
    return _dot(kwin_ref[pl.ds(ws, WIN_KEYS), :], qx) + _bias_tile(ww_ref, off, WIN_KEYS)


def _window_finish(qb, s, vwt_ref, out_ref):
    kb0 = jnp.maximum(qb - WINDOW // Q_BLOCK, 0)
    m = jnp.max(s, axis=0, keepdims=True)
    p = jnp.exp2(s - m).astype(BF16)
    vt = jnp.concatenate([vwt_ref[kb0 + j] for j in range(WIN_KEYS // Q_BLOCK)], axis=1)
    acc = _dot(vt, p)
    _store_heads(out_ref, acc[0:HEAD_DIM] * (1.0 / acc[HEAD_DIM:HEAD_DIM + 1]))


LRU_TM = 256


def _lru_kernel(u_ref, cw_ref, cb_ref, wa_ref, ba_ref, wx_ref, bx_ref, lam_ref, h_ref,
                tail_ref, hprev_ref, uc_ref):
    tm = LRU_TM

    @pl.when(pl.program_id(0) == 0)
    def _():
        tail_ref[...] = jnp.zeros(tail_ref.shape, F32)
        hprev_ref[...] = jnp.zeros(hprev_ref.shape, F32)

    u = u_ref[...]
    tail = tail_ref[...]
    row8 = lax.broadcasted_iota(jnp.int32, (8, LRU_WIDTH), 0)
    uc = cb_ref[...] + u * cw_ref[CONV_WIDTH - 1:CONV_WIDTH, :]
    uc_head = uc[0:8]
    for j in range(1, CONV_WIDTH):
        w_j = cw_ref[CONV_WIDTH - 1 - j:CONV_WIDTH - j, :]
        sh = pltpu.roll(u, j, 0)
        uc = uc + sh * w_j
        uc_head = uc_head + jnp.where(row8 < j, pltpu.roll(tail, j, 0), sh[0:8]) * w_j
    tail_ref[...] = u[tm - 8:tm]
    uc_ref[...] = uc
    uc_ref[0:8, :] = uc_head
    uc = uc_ref[...]

    ucb = uc.astype(BF16)
    r = _sigmoid(_dot(ucb, wa_ref[...]) + ba_ref[...])
    ig = _sigmoid(_dot(ucb, wx_ref[...]) + bx_ref[...])
    nl = -lam_ref[...]
    softplus = jnp.maximum(nl, 0.0) + jnp.log(1.0 + jnp.exp(-jnp.abs(nl)))
    a = jnp.exp((-LRU_C * r) * softplus)
    b = jnp.sqrt(1.0 - a * a) * (ig * uc)

    a = a.reshape(tm // 8, 8, LRU_WIDTH)
    b = b.reshape(tm // 8, 8, LRU_WIDTH)
    sub = lax.broadcasted_iota(jnp.int32, a.shape, 1)
    for step in (1, 2, 4):
        a_s = pltpu.roll(a, step, 1)
        b_s = pltpu.roll(b, step, 1)
        ok = sub >= step
        b = jnp.where(ok, a * b_s + b, b)
        a = jnp.where(ok, a * a_s, a)
    carry = hprev_ref[...]
    for j in range(tm // 8):
        hj = a[j] * carry + b[j]
        h_ref[8 * j:8 * j + 8, :] = hj
        carry = hj[7:8]
    hprev_ref[...] = carry


def _rglru(u, conv_w, conv_b, wa_bd, ba, wx_bd, bx, lam):
    s = u.shape[0]
    tm = LRU_TM
    row = lambda v: v.reshape(1, LRU_WIDTH)
    full = lambda shape: pl.BlockSpec(shape, lambda i: (0, 0))
    return pl.pallas_call(
        _lru_kernel,
        grid=(s // tm,),
        in_specs=[pl.BlockSpec((tm, LRU_WIDTH), lambda i: (i, 0)),
                  full((CONV_WIDTH, LRU_WIDTH)), full((1, LRU_WIDTH)),
                  full((LRU_WIDTH, LRU_WIDTH)), full((1, LRU_WIDTH)),
                  full((LRU_WIDTH, LRU_WIDTH)), full((1, LRU_WIDTH)),
                  full((1, LRU_WIDTH))],
        out_specs=pl.BlockSpec((tm, LRU_WIDTH), lambda i: (i, 0)),
        out_shape=jax.ShapeDtypeStruct((s, LRU_WIDTH), F32),
        scratch_shapes=[pltpu.VMEM((8, LRU_WIDTH), F32), pltpu.VMEM((1, LRU_WIDTH), F32),
                        pltpu.VMEM((tm, LRU_WIDTH), F32)],
        compiler_params=_cparams(1),
        name="rglru",
    )(u, conv_w, row(conv_b), wa_bd, row(ba), wx_bd, row(bx), row(lam))


OUT_TM = 512


def _out_kernel(x_ref, g_ref, oc_ref, os_ref, ow_ref, brt_ref, h_ref,
                wg_ref, wpa_ref, wpb_ref, wo_ref, y_ref):
    h_in = _normed_input(x_ref, g_ref)
    gate = lambda j: _dot(h_in, wg_ref[:, GATE_OFFS[j]:GATE_OFFS[j + 1]])
    brt = brt_ref[...]
    parts = []
    for h in range(N_HEADS):
        rows = slice(h * HEAD_DIM, (h + 1) * HEAD_DIM)
        acc = None
        for b, o_ref in enumerate((oc_ref, os_ref, ow_ref)):
            term = brt[b * N_HEADS + h:b * N_HEADS + h + 1, :] * o_ref[rows, :]
            acc = term if acc is None else acc + term
        parts.append(acc)
    gn = gate(0)
    ya = jnp.concatenate(parts, axis=0).T * (gn * _sigmoid(gn))
    y_a = _dot(ya.astype(BF16), wpa_ref[...])
    gl = gate(1)
    y_b = _dot((h_ref[...] * (gl * _sigmoid(gl))).astype(BF16), wpb_ref[...])
    mg = gate(2)
    m = _sigmoid(mg[:, :D_MODEL]) * y_a + _sigmoid(mg[:, D_MODEL:]) * y_b
    y_ref[...] = x_ref[...] + _dot(m.astype(BF16), wo_ref[...])


def _output(x2, norm_gain, oc_t, os_t, ow_t, br_t, h_lru, w_gate, wpa, wpb, wo):
    s = x2.shape[0]
    tm = OUT_TM
    rows = lambda n: pl.BlockSpec((tm, n), lambda i: (i, 0))
    cols = lambda n: pl.BlockSpec((n, tm), lambda i: (0, i))
    full = lambda a: pl.BlockSpec(a.shape, lambda i: (0, 0))
    return pl.pallas_call(
        _out_kernel,
        grid=(s // tm,),
        in_specs=[rows(D_MODEL), pl.BlockSpec((1, D_MODEL), lambda i: (0, 0)),
                  cols(NSA_WIDTH), cols(NSA_WIDTH), cols(NSA_WIDTH), cols(32), rows(LRU_WIDTH),
                  full(w_gate), full(wpa), full(wpb), full(wo)],
        out_specs=rows(D_MODEL),
        out_shape=jax.ShapeDtypeStruct((s, D_MODEL), F32),
        compiler_params=_cparams(1),
        name="output",
    )(x2, norm_gain.reshape(1, D_MODEL), oc_t, os_t, ow_t, br_t, h_lru, w_gate, wpa, wpb, wo)


def _t5_bucket_table(n_dist):
    n = np.arange(n_dist)
    max_exact = N_BUCKETS // 2
    nf = np.maximum(n, 1).astype(np.float32)
    large = max_exact + (np.log(nf / np.float32(max_exact)) / np.float32(math.log(MAX_DISTANCE / max_exact))
                         * np.float32(N_BUCKETS - max_exact)).astype(np.int32)
    return np.where(n < max_exact, n, np.minimum(large, N_BUCKETS - 1))


def _bias_tiles_kernel(relb_ref, bw_ref, bs_ref, bc_ref, ww_ref, ws_ref, wc_ref):
    h = pl.program_id(0)
    far = relb_ref[N_BUCKETS - 1, h]
    for bk_ref, out_ref, rel in ((bw_ref, ww_ref, False), (bs_ref, ws_ref, True), (bc_ref, wc_ref, True)):
        bk = bk_ref[...]
        acc = jnp.full(bk.shape, NEG, F32)
        for b in range(N_BUCKETS):
            val = relb_ref[b, h] - far if rel else relb_ref[b, h]
            acc = jnp.where(bk == b, val * LOG2E, acc)
        out_ref[...] = acc


def _bias_tables(rel_bias):
    bucket = _t5_bucket_table(2 * WINDOW)
    i = np.arange(Q_BLOCK)[None, :]

    def index_tile(dist, valid):
        return jnp.asarray(np.where(valid, bucket[np.clip(dist, 0, bucket.size - 1)], -1).astype(np.int32))

    u = np.arange(WINDOW + WIN_KEYS)[:, None]
    d_win = i - u + WINDOW
    bw = index_tile(d_win, (d_win >= 0) & (d_win < WINDOW))
    u = np.arange(SEL_NEAR + Q_BLOCK)[:, None]
    d_sel = i - u + Q_BLOCK
    bs = index_tile(d_sel, d_sel >= 0)
    u = np.arange(CMP_NEAR + 16)[:, None]
    d_cmp = i - CMP_STRIDE * (u - 16) - (CMP_BLOCK - 1)
    bc = index_tile(d_cmp, d_cmp >= 0)
    full = lambda a: pl.BlockSpec(a.shape, lambda h: (0, 0))
    per_head = lambda a: pl.BlockSpec((None,) + a.shape, lambda h: (h, 0, 0))
    ww, ws, wc = pl.pallas_call(
        _bias_tiles_kernel,
        grid=(N_HEADS,),
        in_specs=[pl.BlockSpec(memory_space=pltpu.SMEM), full(bw), full(bs), full(bc)],
        out_specs=[per_head(bw), per_head(bs), per_head(bc)],
        out_shape=[jax.ShapeDtypeStruct((N_HEADS,) + a.shape, F32) for a in (bw, bs, bc)],
        compiler_params=_cparams(1),
        name="bias_tiles",
    )(rel_bias, bw, bs, bc)
    b_far = rel_bias[N_BUCKETS - 1] * LOG2E
    hi = b_far.astype(BF16)
    lo = (b_far - hi.astype(F32)).astype(BF16)
    ext = jnp.zeros((N_HEADS, HEAD_DIM, Q_BLOCK), BF16)
    ext = ext.at[:, 0, :].set(hi[:, None]).at[:, 1, :].set(lo[:, None])
    ext = ext.reshape(N_GROUPS, GROUP_SIZE, HEAD_DIM, Q_BLOCK).transpose(0, 2, 1, 3)
    return ww, ws, wc, ext.reshape(N_GROUPS, HEAD_DIM, GQ)


def _block_diag(w):
    n, d, e = w.shape
    eye = jnp.eye(n, dtype=w.dtype)
    return (eye[:, None, :, None] * w[:, :, None, :]).reshape(n * d, n * e)


def kernel(x, norm_gain, w_in, q_norm_gain, k_norm_gain, cmp_pe, cmp_w1, cmp_b1, cmp_w2, rel_bias,
           conv_w, conv_b, lru_wa, lru_ba, lru_wx, lru_bx, lru_lambda, w_proj_a, w_proj_b, w_out):
    bsz, s, _ = x.shape
    assert bsz == 1 and s % 1024 == 0 and s >= 1024
    x2 = x.reshape(s, D_MODEL)
    nsb = s // SEL_BLOCK
    nsbp = -(-nsb // SUPER) * SUPER

    o = IN_OFFS
    w16 = w_in.astype(BF16)
    pad = jnp.zeros((D_MODEL, BR_PAD - 3 * N_HEADS), BF16)
    w_front = jnp.concatenate([w16[:, o[0]:o[2]], w16[:, o[3]:o[4]], pad, w16[:, o[4]:o[5]]], axis=1)
    w_gate = jnp.concatenate([w16[:, o[2]:o[3]], w16[:, o[5]:o[7]]], axis=1)

    scale = HEAD_DIM ** -0.5 * LOG2E
    qgain_col = jnp.tile(q_norm_gain * scale, N_HEADS).reshape(NSA_WIDTH, 1)
    kgain_rows = jnp.tile(k_norm_gain, (1, N_GROUPS))
    (qhi_t, qlo_t, br_t, kaug, vt128, kwin, vwt, kvc, u_lru) = _front(
        x2, norm_gain, w_front, qgain_col, kgain_rows)

    w2p = jnp.pad(cmp_w2, ((0, 0), (0, 0), (0, 128 - HEAD_DIM)))
    kc_gain = jnp.pad(k_norm_gain[0], (0, 128 - HEAD_DIM)).reshape(1, 128)
    kc_cat, vct = _compress(kvc, cmp_w1, cmp_b1.reshape(2, 1, CMP_HIDDEN), w2p,
                            cmp_pe.reshape(2, 1, CMP_BLOCK * HEAD_DIM), kc_gain)

    ww_t, ws_t, wc_t, ext_q = _bias_tables(rel_bias)
    oc_t, neg_t = _cmp_attention(qhi_t, qlo_t, kc_cat, vct, wc_t, ext_q, nsbp)
    os_t, ow_t = _sel_win_attention(qhi_t, neg_t, kaug, vt128, ws_t, ext_q, kwin, vwt, ww_t)

    h_lru = _rglru(u_lru, conv_w, conv_b, _block_diag(lru_wa).astype(BF16), lru_ba,
                   _block_diag(lru_wx).astype(BF16), lru_bx, lru_lambda)

    y = _output(x2, norm_gain, oc_t, os_t, ow_t, br_t, h_lru, w_gate,
                w_proj_a.astype(BF16), w_proj_b.astype(BF16), w_out.astype(BF16))
    return y.reshape(bsz, s, D_MODEL)
```

```python
import functools
import math

import numpy as np
import jax
import jax.numpy as jnp
from jax import lax
from jax.experimental import pallas as pl
from jax.experimental.pallas import tpu as pltpu

F32 = jnp.float32
BF16 = jnp.bfloat16

D_MODEL = 1024
N_HEADS = 8
N_GROUPS = 2
GROUP_SIZE = N_HEADS // N_GROUPS
HEAD_DIM = 64
NSA_WIDTH = N_HEADS * HEAD_DIM
KV_WIDTH = N_GROUPS * HEAD_DIM
CMP_STRIDE = 16
CMP_BLOCK = 32
CMP_HIDDEN = 256
SEL_BLOCK = 64
SEL_PER_CMP = SEL_BLOCK // CMP_STRIDE
N_SELECT = 16
WINDOW = 512
Q_BLOCK = 128
LRU_WIDTH = 512
LRU_BLOCKS = 8
CONV_WIDTH = 4
LRU_C = 8.0
N_BUCKETS = 32
MAX_DISTANCE = 128
EPS = 1e-6
NEG = -1e30
M_INIT = -5e29
LOG2E = 1.4426950408889634

GQ = GROUP_SIZE * Q_BLOCK
SUPER = 128
KV_CHUNK = 512
SEL_NEAR = 2 * Q_BLOCK
WIN_KEYS = WINDOW + Q_BLOCK
CMP_NEAR = 24
N_PICK = N_SELECT - 3
PICKED = -2.0
VT_ROWS = 80

IN_OFFS = tuple(int(v) for v in np.cumsum(
    (0, NSA_WIDTH, 6 * KV_WIDTH, NSA_WIDTH, 3 * N_HEADS, LRU_WIDTH, LRU_WIDTH, 2 * D_MODEL)))
BR_PAD = 128
FRONT_OFFS = tuple(int(v) for v in np.cumsum((0, NSA_WIDTH, 6 * KV_WIDTH, BR_PAD, LRU_WIDTH)))
GATE_OFFS = tuple(int(v) for v in np.cumsum((0, NSA_WIDTH, LRU_WIDTH, 2 * D_MODEL)))

VMEM_LIMIT = 56 * 1024 * 1024


def _cparams(n_axes):
    return pltpu.CompilerParams(dimension_semantics=("arbitrary",) * n_axes,
                                vmem_limit_bytes=VMEM_LIMIT)


def _dot(a, b):
    return jnp.dot(a, b, preferred_element_type=F32)


def _sigmoid(x):
    return 0.5 * jnp.tanh(0.5 * x) + 0.5


FRONT_TM = 512


def _normed_input(x_ref, g_ref):
    x = x_ref[...]
    ms = jnp.mean(x * x, axis=-1, keepdims=True)
    return (x * lax.rsqrt(ms + EPS) * g_ref[...]).astype(BF16)


def _group_rms(k, gain_row):
    sq = k * k
    lane = lax.broadcasted_iota(jnp.int32, k.shape, 1)
    lo = lane < HEAD_DIM
    s0 = jnp.sum(jnp.where(lo, sq, 0.0), axis=-1, keepdims=True)
    s1 = jnp.sum(jnp.where(lo, 0.0, sq), axis=-1, keepdims=True)
    inv = jnp.where(lo, lax.rsqrt(s0 / HEAD_DIM + EPS), lax.rsqrt(s1 / HEAD_DIM + EPS))
    return k * inv * gain_row


def _front_kernel(x_ref, g_ref, w_ref, qg_ref, kg_ref,
                  qhi_ref, qlo_ref, brt_ref, kaug_ref, vt512_ref, vt128_ref,
                  kwin_ref, vwt_ref, kvc_ref, u_ref):
    i = pl.program_id(0)
    tm = FRONT_TM
    h_in = _normed_input(x_ref, g_ref)
    proj = lambda a, b: _dot(h_in, w_ref[:, a:b])
    qt = proj(FRONT_OFFS[0], FRONT_OFFS[1]).T
    for h in range(N_HEADS):
        blk = qt[h * HEAD_DIM:(h + 1) * HEAD_DIM]
        ms = jnp.mean(blk * blk, axis=0, keepdims=True)
        qn = blk * lax.rsqrt(ms + EPS) * qg_ref[h * HEAD_DIM:(h + 1) * HEAD_DIM, :]
        hi = qn.astype(BF16)
        qhi_ref[h * HEAD_DIM:(h + 1) * HEAD_DIM, :] = hi
        qlo_ref[h * HEAD_DIM:(h + 1) * HEAD_DIM, :] = (qn - hi.astype(F32)).astype(BF16)
    brt_ref[...] = _sigmoid(proj(FRONT_OFFS[2], FRONT_OFFS[3])).T[:32]
    u_ref[...] = proj(FRONT_OFFS[3], FRONT_OFFS[4])

    kv = proj(FRONT_OFFS[1], FRONT_OFFS[2])
    piece = lambda j: kv[:, j * KV_WIDTH:(j + 1) * KV_WIDTH]
    lane = lax.broadcasted_iota(jnp.int32, (tm, 128), 1)
    row = lax.broadcasted_iota(jnp.int32, (tm, 128), 0) + i * tm
    lo = lane < HEAD_DIM
    ones_cols = jnp.where((lane == HEAD_DIM) | (lane == HEAD_DIM + 1), 1.0, 0.0)
    onehot = jnp.where(lane == (row // SEL_BLOCK) % SUPER, 1.0, 0.0).astype(BF16)
    kvc_ref[0] = piece(0)
    kvc_ref[1] = piece(1)
    kslc = _group_rms(piece(2), kg_ref[1:2, :])
    kwin = _group_rms(piece(4), kg_ref[2:3, :])
    vslt = piece(3).T
    vwit = piece(5).T
    row_t = lax.broadcasted_iota(jnp.int32, (VT_ROWS - HEAD_DIM, tm), 0)
    ones_rows = jnp.where(row_t == 0, 1.0, 0.0)
    for g in range(N_GROUPS):
        sh = lambda a: a if g == 0 else pltpu.roll(a, HEAD_DIM, 1)
        kaug_ref[g, :, 0:128] = onehot
        kaug_ref[g, :, 128:256] = jnp.where(lo, sh(kslc), ones_cols).astype(BF16)
        kwin_ref[g] = jnp.where(lo, sh(kwin), 0.0).astype(BF16)
        vs = jnp.concatenate([vslt[g * HEAD_DIM:(g + 1) * HEAD_DIM], ones_rows], axis=0).astype(BF16)
        vw = jnp.concatenate([vwit[g * HEAD_DIM:(g + 1) * HEAD_DIM], ones_rows], axis=0).astype(BF16)
        for j in range(tm // KV_CHUNK):
            vt512_ref[g, j] = vs[:, j * KV_CHUNK:(j + 1) * KV_CHUNK]
        for j in range(tm // 128):
            vt128_ref[g, j] = vs[:, j * 128:(j + 1) * 128]
            vwt_ref[g, j] = vw[:, j * 128:(j + 1) * 128]


def _front(x2, norm_gain, w_front, qgain_col, kgain_rows):
    s = x2.shape[0]
    tm = FRONT_TM
    g = N_GROUPS
    outs = [
        jax.ShapeDtypeStruct((NSA_WIDTH, s), BF16),
        jax.ShapeDtypeStruct((NSA_WIDTH, s), BF16),
        jax.ShapeDtypeStruct((32, s), F32),
        jax.ShapeDtypeStruct((g, s, 256), BF16),
        jax.ShapeDtypeStruct((g, s // KV_CHUNK, VT_ROWS, KV_CHUNK), BF16),
        jax.ShapeDtypeStruct((g, s // 128, VT_ROWS, 128), BF16),
        jax.ShapeDtypeStruct((g, s, 128), BF16),
        jax.ShapeDtypeStruct((g, s // 128, VT_ROWS, 128), BF16),
        jax.ShapeDtypeStruct((2, s, KV_WIDTH), F32),
        jax.ShapeDtypeStruct((s, LRU_WIDTH), F32),
    ]
    out_specs = [
        pl.BlockSpec((NSA_WIDTH, tm), lambda i: (0, i)),
        pl.BlockSpec((NSA_WIDTH, tm), lambda i: (0, i)),
        pl.BlockSpec((32, tm), lambda i: (0, i)),
        pl.BlockSpec((g, tm, 256), lambda i: (0, i, 0)),
        pl.BlockSpec((g, tm // KV_CHUNK, VT_ROWS, KV_CHUNK), lambda i: (0, i, 0, 0)),
        pl.BlockSpec((g, tm // 128, VT_ROWS, 128), lambda i: (0, i, 0, 0)),
        pl.BlockSpec((g, tm, 128), lambda i: (0, i, 0)),
        pl.BlockSpec((g, tm // 128, VT_ROWS, 128), lambda i: (0, i, 0, 0)),
        pl.BlockSpec((2, tm, KV_WIDTH), lambda i: (0, i, 0)),
        pl.BlockSpec((tm, LRU_WIDTH), lambda i: (i, 0)),
    ]
    return pl.pallas_call(
        _front_kernel,
        grid=(s // tm,),
        in_specs=[pl.BlockSpec((tm, D_MODEL), lambda i: (i, 0)),
                  pl.BlockSpec((1, D_MODEL), lambda i: (0, 0)),
                  pl.BlockSpec(w_front.shape, lambda i: (0, 0)),
                  pl.BlockSpec((NSA_WIDTH, 1), lambda i: (0, 0)),
                  pl.BlockSpec((3, 128), lambda i: (0, 0))],
        out_specs=out_specs,
        out_shape=outs,
        compiler_params=_cparams(1),
        name="front",
    )(x2, norm_gain.reshape(1, D_MODEL), w_front, qgain_col, kgain_rows)


def _compress_kernel(kvc_ref, w1_ref, b1_ref, w2_ref, pe_ref, kg_ref, kc_ref, vct_ref):
    ncp = kc_ref.shape[1]
    half = CMP_STRIDE * HEAD_DIM
    lane = lax.broadcasted_iota(jnp.int32, (ncp, 128), 1)
    lo = lane < HEAD_DIM

    def chunk_rows(j, g):
        cols = []
        for pair in range(CMP_STRIDE // 2):
            even = kvc_ref[j, pl.ds(2 * pair, ncp, stride=CMP_STRIDE), :]
            odd = kvc_ref[j, pl.ds(2 * pair + 1, ncp, stride=CMP_STRIDE), :]
            if g == 0:
                cols.append(jnp.where(lo, even, pltpu.roll(odd, HEAD_DIM, 1)))
            else:
                cols.append(jnp.where(lo, pltpu.roll(even, HEAD_DIM, 1), odd))
        return jnp.concatenate(cols, axis=1)

    def phi(j, g):
        x = chunk_rows(j, g).astype(BF16)
        w1 = w1_ref[j].astype(BF16)
        a = _dot(x, w1[:half])
        b = _dot(x, w1[half:])
        b_next = pltpu.roll(b, ncp - 1, 0)
        pe8 = jnp.broadcast_to(pe_ref[j], (8, 2 * half)).astype(BF16)
        pe_term = _dot(pe8, w1)[0:1]
        hid = a + b_next + pe_term + b1_ref[j]
        act = (hid * _sigmoid(hid)).astype(BF16)
        return _dot(act, w2_ref[j].astype(BF16))

    ones_cols = jnp.where((lane == HEAD_DIM) | (lane == HEAD_DIM + 1), 1.0, 0.0)
    for g in range(N_GROUPS):
        kc = phi(0, g)
        ms = jnp.sum(kc * kc, axis=-1, keepdims=True) / HEAD_DIM
        kc = kc * lax.rsqrt(ms + EPS) * kg_ref[...]
        hi = kc.astype(BF16).astype(F32)
        lo_part = (kc - hi).astype(BF16).astype(F32)
        kc_ref[g, :, 0:128] = (hi + pltpu.roll(lo_part, HEAD_DIM, 1)).astype(BF16)
        kc_ref[g, :, 128:256] = jnp.where(lo, hi, ones_cols).astype(BF16)
        vt = phi(1, g).T
        row_t = lax.broadcasted_iota(jnp.int32, vt.shape, 0)
        vct_ref[g] = jnp.where(row_t == HEAD_DIM, 1.0, vt).astype(BF16)


def _compress(kvc, w1, b1, w2p, pe_flat, kgain_row):
    s = kvc.shape[1]
    g = N_GROUPS
    ncp = s // CMP_STRIDE
    full = lambda a: pl.BlockSpec(a.shape, lambda i: (0,) * a.ndim)
    return pl.pallas_call(
        _compress_kernel,
        grid=(1,),
        in_specs=[pl.BlockSpec(kvc.shape, lambda i: (0, 0, 0), pipeline_mode=pl.Buffered(1)),
                  full(w1), full(b1), full(w2p), full(pe_flat), full(kgain_row)],
        out_specs=[pl.BlockSpec((g, ncp, 256), lambda i: (0, 0, 0)),
                   pl.BlockSpec((g, 128, ncp), lambda i: (0, 0, 0))],
        out_shape=[jax.ShapeDtypeStruct((g, ncp, 256), BF16),
                   jax.ShapeDtypeStruct((g, 128, ncp), BF16)],
        compiler_params=_cparams(1),
        name="compress",
    )(kvc, w1, b1, w2p, pe_flat, kgain_row)


def _heads_to_lanes(q):
    return jnp.concatenate([q[r * HEAD_DIM:(r + 1) * HEAD_DIM] for r in range(GROUP_SIZE)], axis=1)


def _store_heads(out_ref, o):
    for r in range(GROUP_SIZE):
        out_ref[r * HEAD_DIM:(r + 1) * HEAD_DIM, :] = o[0:HEAD_DIM, r * Q_BLOCK:(r + 1) * Q_BLOCK]


def _bias_tile(w_ref, off, n, head0=0):
    return jnp.concatenate([w_ref[head0 + r, pl.ds(off, n), :] for r in range(GROUP_SIZE)], axis=1)


def _cmp_kernel(qhi_ref, qlo_ref, kc_ref, vct_ref, wc_ref, ext_ref, ocmp_ref, neg_ref, s_ref, imp_ref):
    qb = pl.program_id(0)
    ncp = kc_ref.shape[1]
    nsbp = neg_ref.shape[1]
    gr = GROUP_SIZE * HEAD_DIM
    ws = pl.multiple_of(jnp.maximum(8 * qb - 16, 0), 8)
    off = pl.multiple_of(jnp.maximum(16 - 8 * qb, 0), 8)
    lim = ws + CMP_NEAR
    n_cls = min(8, ncp // 128)
    per_cls = ncp // n_cls

    def attend(nrows):
        nblk = nrows // SEL_PER_CMP
        band = nrows - per_cls
        rows = lax.broadcasted_iota(jnp.int32, (per_cls, GQ), 0) + band
        blk = lax.broadcasted_iota(jnp.int32, (nblk, Q_BLOCK), 0)
        lane = lax.broadcasted_iota(jnp.int32, (nblk, Q_BLOCK), 1)
        cur = 2 * qb + jnp.where(lane >= SEL_BLOCK, 1, 0)
        forced = jnp.where(blk == 0, 1.0, jnp.where(blk == cur, 1.0, jnp.where(blk == cur - 1, 1.0, 0.0)))
        carry = []
        for g in range(N_GROUPS):
            qhi = qhi_ref[g * gr:(g + 1) * gr, :]
            qlo = qlo_ref[g * gr:(g + 1) * gr, :]
            cols = []
            for r in range(GROUP_SIZE):
                a = qhi[r * HEAD_DIM:(r + 1) * HEAD_DIM]
                b = qlo[r * HEAD_DIM:(r + 1) * HEAD_DIM]
                cols.append(jnp.concatenate([a, a, b], axis=0))
            qcat = jnp.concatenate([jnp.concatenate(cols, axis=1), ext_ref[g]], axis=0)
            s_g = s_ref.at[g]
            s_g[0:nrows, :] = _dot(kc_ref[g, 0:nrows, :], qcat)
            s_g[pl.ds(ws, CMP_NEAR), :] += _bias_tile(wc_ref, off, CMP_NEAR, g * GROUP_SIZE)
            s_g[band:nrows, :] = jnp.where(rows < lim, s_g[band:nrows, :], NEG)
            s = s_g[0:nrows, :]
            m = jnp.maximum(jnp.max(s, axis=0, keepdims=True), M_INIT)
            p = jnp.exp2(s - m)
            acc = _dot(vct_ref[g, 0:VT_ROWS, 0:nrows], p.astype(BF16))
            l = acc[HEAD_DIM:HEAD_DIM + 1]
            inv_l = jnp.where(l > 0.0, 1.0 / l, 0.0)
            _store_heads(ocmp_ref.at[g * gr:(g + 1) * gr], acc[0:HEAD_DIM] * inv_l)
            imp = None
            for r in range(GROUP_SIZE):
                cs = slice(r * Q_BLOCK, (r + 1) * Q_BLOCK)
                term = p[:, cs] * inv_l[:, cs]
                imp = term if imp is None else imp + term
            imp_g = imp_ref.at[g]
            imp_g[0:nrows, :] = imp
            strided = [imp_g[pl.ds(k, nblk, stride=SEL_PER_CMP), :] for k in range(SEL_PER_CMP)]
            prev_last = jnp.where(blk == 0, 0.0, pltpu.roll(strided[SEL_PER_CMP - 1], 1, 0))
            impb = prev_last + strided[0] + strided[1] + strided[2] + strided[3]
            carry.append(jnp.where(blk < cur - 1, jnp.where(blk > 0, impb, -1.0), -1.0))

        def pick_one(_, cands):
            out = []
            for sc in cands:
                mx = jnp.max(sc, axis=0, keepdims=True)
                hit_blk = jnp.where(sc == mx, jnp.where(mx >= 0.0, blk, nblk), nblk)
                first = jnp.min(hit_blk, axis=0, keepdims=True)
                out.append(jnp.where(blk == first, PICKED, sc))
            return tuple(out)

        if nblk * N_GROUPS <= 2 * SUPER:
            carry = lax.fori_loop(0, N_PICK, pick_one, tuple(carry))
        else:
            carry = [lax.fori_loop(0, N_PICK, pick_one, (c,))[0] for c in carry]
        for g in range(N_GROUPS):
            picked = carry[g] == PICKED
            sel_neg = jnp.where(picked, 0.0, jnp.where(blk <= cur, jnp.where(forced > 0.5, 0.0, NEG), NEG))
            neg_ref[g, 0:nblk, :] = sel_neg.astype(BF16)
            if nblk < nsbp:
                neg_ref[g, nblk:nsbp, :] = jnp.full((nsbp - nblk, Q_BLOCK), NEG, BF16)

    cls = (lim + per_cls - 1) // per_cls
    for k in range(1, n_cls + 1):
        pl.when(cls == k)(functools.partial(attend, k * per_cls))


def _cmp_attention(qhi_t, qlo_t, kc_cat, vct, wc_t, ext_q, nsbp):
    s = qhi_t.shape[1]
    g = N_GROUPS
    ncp = kc_cat.shape[1]
    full = lambda a: pl.BlockSpec(a.shape, lambda qb: (0,) * a.ndim)
    return pl.pallas_call(
        _cmp_kernel,
        grid=(s // Q_BLOCK,),
        in_specs=[pl.BlockSpec((NSA_WIDTH, Q_BLOCK), lambda qb: (0, qb)),
                  pl.BlockSpec((NSA_WIDTH, Q_BLOCK), lambda qb: (0, qb)),
                  full(kc_cat), full(vct), full(wc_t), full(ext_q)],
        out_specs=[pl.BlockSpec((NSA_WIDTH, Q_BLOCK), lambda qb: (0, qb)),
                   pl.BlockSpec((g, nsbp, Q_BLOCK), lambda qb: (0, 0, qb))],
        out_shape=[jax.ShapeDtypeStruct((NSA_WIDTH, s), F32),
                   jax.ShapeDtypeStruct((g, nsbp, s), BF16)],
        scratch_shapes=[pltpu.VMEM((g, ncp, GQ), F32), pltpu.VMEM((g, ncp, Q_BLOCK), F32)],
        compiler_params=_cparams(1),
        name="cmp_select",
    )(qhi_t, qlo_t, kc_cat, vct, wc_t, ext_q)


def _sel_kernel(qhi_ref, neg_ref, kaug_ref, vt512_ref, vt128_ref, ws_ref, ext_ref,
                kwin_ref, vwt_ref, ww_ref, out_ref, owin_ref,
                qaug_ref, m_ref, acc_ref, s0_ref, s1_ref, s2_ref, cm0_ref, cm1_ref, cm2_ref):
    qb = pl.program_id(1)
    nsbp = neg_ref.shape[0]
    nsc = nsbp // SUPER
    qg = _heads_to_lanes(qhi_ref[...])
    qx = jnp.concatenate([qg, ext_ref[...]], axis=0)
    blk = lax.broadcasted_iota(jnp.int32, (SUPER, Q_BLOCK), 0)

    def aug(neg_rows):
        tiled = jnp.concatenate([neg_rows] * GROUP_SIZE, axis=1).astype(BF16)
        return jnp.concatenate([tiled, qx], axis=0)

    for sc in range(nsc):
        neg_rows = neg_ref[sc * SUPER:(sc + 1) * SUPER, :].astype(F32)
        qaug_ref[sc] = aug(jnp.where(blk + sc * SUPER >= 2 * qb - 2, NEG, neg_rows))

    m_ref[...] = jnp.full(m_ref.shape, M_INIT, F32)
    acc_ref[...] = jnp.zeros(acc_ref.shape, F32)

    def update(s, col_max, vt):
        m_old = m_ref[...]
        m_new = jnp.maximum(m_old, col_max)
        p = jnp.exp2(s - m_new).astype(BF16)
        acc_ref[...] = acc_ref[...] * jnp.exp2(m_old - m_new) + _dot(vt, p)
        m_ref[...] = m_new

    n_far = (jnp.maximum(qb - 1, 0) * Q_BLOCK + KV_CHUNK - 1) // KV_CHUNK

    def scores(c, dst_ref, max_ref):
        k = kaug_ref[pl.ds(pl.multiple_of(c * KV_CHUNK, KV_CHUNK), KV_CHUNK), :]
        sc = (c * (KV_CHUNK // SEL_BLOCK)) // SUPER
        s = _dot(k, qaug_ref[sc])
        dst_ref[...] = s
        max_ref[...] = jnp.max(s, axis=0, keepdims=True)

    kb0 = jnp.maximum(qb - 1, 0)
    ws = pl.multiple_of(kb0 * Q_BLOCK, Q_BLOCK)
    off = pl.multiple_of(jnp.where(qb == 0, Q_BLOCK, 0), Q_BLOCK)
    b_lo = 2 * kb0
    sc_lo = pl.multiple_of((b_lo // SUPER) * SUPER, SUPER)
    sc_hi = pl.multiple_of(((b_lo + 3) // SUPER) * SUPER, SUPER)
    neg_lo = neg_ref[pl.ds(sc_lo, SUPER), :].astype(F32)
    neg_hi = neg_ref[pl.ds(sc_hi, SUPER), :].astype(F32)
    near_neg = jnp.where(blk >= SUPER // 2, neg_lo, neg_hi)
    k = kaug_ref[pl.ds(ws, SEL_NEAR), :]
    s_near = _dot(k, aug(near_neg)) + _bias_tile(ws_ref, off, SEL_NEAR)
    vt_near = jnp.concatenate([vt128_ref[kb0], vt128_ref[kb0 + 1]], axis=1)

    s_win = _window_scores(qb, qg, kwin_ref, ww_ref)
    bufs = ((s0_ref, cm0_ref), (s1_ref, cm1_ref), (s2_ref, cm2_ref))
    last = jnp.maximum(n_far - 1, 0)
    scores(0, *bufs[0])
    scores(jnp.minimum(1, last), *bufs[1])
    update(s_near, jnp.max(s_near, axis=0, keepdims=True), vt_near)
    _window_finish(qb, s_win, vwt_ref, owin_ref)

    def far_triple(c):
        for j in range(3):
            scores(jnp.minimum(c + j + 2, last), *bufs[(j + 2) % 3])
            s_ref, cm_ref = bufs[j]
            update(s_ref[...], cm_ref[...], vt512_ref[c + j])

    def far_six(i, carry):
        far_triple(6 * i)
        far_triple(6 * i + 3)
        return carry

    lax.fori_loop(0, n_far // 6, far_six, 0)
    done = (n_far // 6) * 6

    @pl.when(n_far - done >= 3)
    def _():
        far_triple(done)

    done = (n_far // 3) * 3
    for j in range(2):
        @pl.when(n_far - done > j)
        def _():
            s_ref, cm_ref = bufs[j]
            update(s_ref[...], cm_ref[...], vt512_ref[done + j])

    acc = acc_ref[...]
    _store_heads(out_ref, acc[0:HEAD_DIM] * (1.0 / acc[HEAD_DIM:HEAD_DIM + 1]))


def _sel_win_attention(qhi_t, neg_t, kaug, vt512, vt128, ws_t, ext_q, kwin, vwt, ww_t):
    s = qhi_t.shape[1]
    g = N_GROUPS
    nqb = s // Q_BLOCK
    nsbp = neg_t.shape[1]
    gr = GROUP_SIZE * HEAD_DIM
    once = pl.Buffered(1)
    return pl.pallas_call(
        _sel_kernel,
        grid=(g, nqb),
        in_specs=[pl.BlockSpec((gr, Q_BLOCK), lambda gi, qb: (gi, qb)),
                  pl.BlockSpec((None, nsbp, Q_BLOCK), lambda gi, qb: (gi, 0, qb)),
                  pl.BlockSpec((None, s, 256), lambda gi, qb: (gi, 0, 0), pipeline_mode=once),
                  pl.BlockSpec((None, s // KV_CHUNK, VT_ROWS, KV_CHUNK), lambda gi, qb: (gi, 0, 0, 0),
                               pipeline_mode=once),
                  pl.BlockSpec((None, s // 128, VT_ROWS, 128), lambda gi, qb: (gi, 0, 0, 0),
                               pipeline_mode=once),
                  pl.BlockSpec((GROUP_SIZE,) + ws_t.shape[1:], lambda gi, qb: (gi, 0, 0)),
                  pl.BlockSpec((None, HEAD_DIM, GQ), lambda gi, qb: (gi, 0, 0)),
                  pl.BlockSpec((None, s, 128), lambda gi, qb: (gi, 0, 0), pipeline_mode=once),
                  pl.BlockSpec((None, s // 128, VT_ROWS, 128), lambda gi, qb: (gi, 0, 0, 0),
                               pipeline_mode=once),
                  pl.BlockSpec((GROUP_SIZE,) + ww_t.shape[1:], lambda gi, qb: (gi, 0, 0))],
        out_specs=[pl.BlockSpec((gr, Q_BLOCK), lambda gi, qb: (gi, qb)),
                   pl.BlockSpec((gr, Q_BLOCK), lambda gi, qb: (gi, qb))],
        out_shape=[jax.ShapeDtypeStruct((NSA_WIDTH, s), F32),
                   jax.ShapeDtypeStruct((NSA_WIDTH, s), F32)],
        scratch_shapes=[pltpu.VMEM((nsbp // SUPER, 256, GQ), BF16),
                        pltpu.VMEM((1, GQ), F32),
                        pltpu.VMEM((VT_ROWS, GQ), F32),
                        pltpu.VMEM((KV_CHUNK, GQ), F32),
                        pltpu.VMEM((KV_CHUNK, GQ), F32),
                        pltpu.VMEM((KV_CHUNK, GQ), F32),
                        pltpu.VMEM((1, GQ), F32),
                        pltpu.VMEM((1, GQ), F32),
                        pltpu.VMEM((1, GQ), F32)],
        compiler_params=_cparams(2),
        name="sel_win_attention",
    )(qhi_t, neg_t, kaug, vt512, vt128, ws_t, ext_q, kwin, vwt, ww_t)


def _window_scores(qb, qg, kwin_ref, ww_ref):
    kb0 = jnp.maximum(qb - WINDOW // Q_BLOCK, 0)
    ws = pl.multiple_of(kb0 * Q_BLOCK, Q_BLOCK)
    off = pl.multiple_of(jnp.maximum(WINDOW - qb * Q_BLOCK, 0), Q_BLOCK)
    qx = jnp.concatenate([qg, jnp.zeros_like(qg)], axis=0)
    return _dot(kwin_ref[pl.ds(ws, WIN_KEYS), :], qx) + _bias_tile(ww_ref, off, WIN_KEYS)


def _window_finish(qb, s, vwt_ref, out_ref):
    kb0 = jnp.maximum(qb - WINDOW // Q_BLOCK, 0)
    m = jnp.max(s, axis=0, keepdims=True)
    p = jnp.exp2(s - m).astype(BF16)
    vt = jnp.concatenate([vwt_ref[kb0 + j] for j in range(WIN_KEYS // Q_BLOCK)], axis=1)
    acc = _dot(vt, p)
    _store_heads(out_ref, acc[0:HEAD_DIM] * (1.0 / acc[HEAD_DIM:HEAD_DIM + 1]))


LRU_TM = 256


def _lru_kernel(u_ref, cw_ref, cb_ref, wa_ref, ba_ref, wx_ref, bx_ref, lam_ref, h_ref,
                tail_ref, hprev_ref, uc_ref):
    tm = LRU_TM

    @pl.when(pl.program_id(0) == 0)
    def _():
        tail_ref[...] = jnp.zeros(tail_ref.shape, F32)
        hprev_ref[...] = jnp.zeros(hprev_ref.shape, F32)

    u = u_ref[...]
    tail = tail_ref[...]
    row8 = lax.broadcasted_iota(jnp.int32, (8, LRU_WIDTH), 0)
    uc = cb_ref[...] + u * cw_ref[CONV_WIDTH - 1:CONV_WIDTH, :]
    uc_head = uc[0:8]
    for j in range(1, CONV_WIDTH):
        w_j = cw_ref[CONV_WIDTH - 1 - j:CONV_WIDTH - j, :]
        sh = pltpu.roll(u, j, 0)
        uc = uc + sh * w_j
        uc_head = uc_head + jnp.where(row8 < j, pltpu.roll(tail, j, 0), sh[0:8]) * w_j
    tail_ref[...] = u[tm - 8:tm]
    uc_ref[...] = uc
    uc_ref[0:8, :] = uc_head
    uc = uc_ref[...]

    ucb = uc.astype(BF16)
    r = _sigmoid(_dot(ucb, wa_ref[...]) + ba_ref[...])
    ig = _sigmoid(_dot(ucb, wx_ref[...]) + bx_ref[...])
    nl = -lam_ref[...]
    softplus = jnp.maximum(nl, 0.0) + jnp.log(1.0 + jnp.exp(-jnp.abs(nl)))
    a = jnp.exp((-LRU_C * r) * softplus)
    b = jnp.sqrt(1.0 - a * a) * (ig * uc)

    a = a.reshape(tm // 8, 8, LRU_WIDTH)
    b = b.reshape(tm // 8, 8, LRU_WIDTH)
    sub = lax.broadcasted_iota(jnp.int32, a.shape, 1)
    for step in (1, 2, 4):
        a_s = pltpu.roll(a, step, 1)
        b_s = pltpu.roll(b, step, 1)
        ok = sub >= step
        b = jnp.where(ok, a * b_s + b, b)
        a = jnp.where(ok, a * a_s, a)
    carry = hprev_ref[...]
    for j in range(tm // 8):
        hj = a[j] * carry + b[j]
        h_ref[8 * j:8 * j + 8, :] = hj
        carry = hj[7:8]
    hprev_ref[...] = carry


def _rglru(u, conv_w, conv_b, wa_bd, ba, wx_bd, bx, lam):
    s = u.shape[0]
    tm = LRU_TM
    row = lambda v: v.reshape(1, LRU_WIDTH)
    full = lambda shape: pl.BlockSpec(shape, lambda i: (0, 0))
    return pl.pallas_call(
        _lru_kernel,
        grid=(s // tm,),
        in_specs=[pl.BlockSpec((tm, LRU_WIDTH), lambda i: (i, 0)),
                  full((CONV_WIDTH, LRU_WIDTH)), full((1, LRU_WIDTH)),
                  full((LRU_WIDTH, LRU_WIDTH)), full((1, LRU_WIDTH)),
                  full((LRU_WIDTH, LRU_WIDTH)), full((1, LRU_WIDTH)),
                  full((1, LRU_WIDTH))],
        out_specs=pl.BlockSpec((tm, LRU_WIDTH), lambda i: (i, 0)),
        out_shape=jax.ShapeDtypeStruct((s, LRU_WIDTH), F32),
        scratch_shapes=[pltpu.VMEM((8, LRU_WIDTH), F32), pltpu.VMEM((1, LRU_WIDTH), F32),
                        pltpu.VMEM((tm, LRU_WIDTH), F32)],
        compiler_params=_cparams(1),
        name="rglru",
    )(u, conv_w, row(conv_b), wa_bd, row(ba), wx_bd, row(bx), row(lam))


OUT_TM = 512


def _out_kernel(x_ref, g_ref, oc_ref, os_ref, ow_ref, brt_ref, h_ref,
                wg_ref, wpa_ref, wpb_ref, wo_ref, y_ref):
    h_in = _normed_input(x_ref, g_ref)
    gate = lambda j: _dot(h_in, wg_ref[:, GATE_OFFS[j]:GATE_OFFS[j + 1]])
    brt = brt_ref[...]
    parts = []
    for h in range(N_HEADS):
        rows = slice(h * HEAD_DIM, (h + 1) * HEAD_DIM)
        acc = None
        for b, o_ref in enumerate((oc_ref, os_ref, ow_ref)):
            term = brt[b * N_HEADS + h:b * N_HEADS + h + 1, :] * o_ref[rows, :]
            acc = term if acc is None else acc + term
        parts.append(acc)
    gn = gate(0)
    ya = jnp.concatenate(parts, axis=0).T * (gn * _sigmoid(gn))
    y_a = _dot(ya.astype(BF16), wpa_ref[...])
    gl = gate(1)
    y_b = _dot((h_ref[...] * (gl * _sigmoid(gl))).astype(BF16), wpb_ref[...])
    mg = gate(2)
    m = _sigmoid(mg[:, :D_MODEL]) * y_a + _sigmoid(mg[:, D_MODEL:]) * y_b
    y_ref[...] = x_ref[...] + _dot(m.astype(BF16), wo_ref[...])


def _output(x2, norm_gain, oc_t, os_t, ow_t, br_t, h_lru, w_gate, wpa, wpb, wo):
    s = x2.shape[0]
    tm = OUT_TM
    rows = lambda n: pl.BlockSpec((tm, n), lambda i: (i, 0))
    cols = lambda n: pl.BlockSpec((n, tm), lambda i: (0, i))
    full = lambda a: pl.BlockSpec(a.shape, lambda i: (0, 0))
    return pl.pallas_call(
        _out_kernel,
        grid=(s // tm,),
        in_specs=[rows(D_MODEL), pl.BlockSpec((1, D_MODEL), lambda i: (0, 0)),
                  cols(NSA_WIDTH), cols(NSA_WIDTH), cols(NSA_WIDTH), cols(32), rows(LRU_WIDTH),
                  full(w_gate), full(wpa), full(wpb), full(wo)],
        out_specs=rows(D_MODEL),
        out_shape=jax.ShapeDtypeStruct((s, D_MODEL), F32),
        compiler_params=_cparams(1),
        name="output",
    )(x2, norm_gain.reshape(1, D_MODEL), oc_t, os_t, ow_t, br_t, h_lru, w_gate, wpa, wpb, wo)


def _t5_bucket_table(n_dist):
    n = np.arange(n_dist)
    max_exact = N_BUCKETS // 2
    nf = np.maximum(n, 1).astype(np.float32)
    large = max_exact + (np.log(nf / np.float32(max_exact)) / np.float32(math.log(MAX_DISTANCE / max_exact))
                         * np.float32(N_BUCKETS - max_exact)).astype(np.int32)
    return np.where(n < max_exact, n, np.minimum(large, N_BUCKETS - 1))


def _bias_tiles_kernel(relb_ref, bw_ref, bs_ref, bc_ref, ww_ref, ws_ref, wc_ref):
    h = pl.program_id(0)
    far = relb_ref[N_BUCKETS - 1, h]
    for bk_ref, out_ref, rel in ((bw_ref, ww_ref, False), (bs_ref, ws_ref, True), (bc_ref, wc_ref, True)):
        bk = bk_ref[...]
        acc = jnp.full(bk.shape, NEG, F32)
        for b in range(N_BUCKETS):
            val = relb_ref[b, h] - far if rel else relb_ref[b, h]
            acc = jnp.where(bk == b, val * LOG2E, acc)
        out_ref[...] = acc


def _bias_tables(rel_bias):
    bucket = _t5_bucket_table(2 * WINDOW)
    i = np.arange(Q_BLOCK)[None, :]

    def index_tile(dist, valid):
        return jnp.asarray(np.where(valid, bucket[np.clip(dist, 0, bucket.size - 1)], -1).astype(np.int32))

    u = np.arange(WINDOW + WIN_KEYS)[:, None]
    d_win = i - u + WINDOW
    bw = index_tile(d_win, (d_win >= 0) & (d_win < WINDOW))
    u = np.arange(SEL_NEAR + Q_BLOCK)[:, None]
    d_sel = i - u + Q_BLOCK
    bs = index_tile(d_sel, d_sel >= 0)
    u = np.arange(CMP_NEAR + 16)[:, None]
    d_cmp = i - CMP_STRIDE * (u - 16) - (CMP_BLOCK - 1)
    bc = index_tile(d_cmp, d_cmp >= 0)
    full = lambda a: pl.BlockSpec(a.shape, lambda h: (0, 0))
    per_head = lambda a: pl.BlockSpec((None,) + a.shape, lambda h: (h, 0, 0))
    ww, ws, wc = pl.pallas_call(
        _bias_tiles_kernel,
        grid=(N_HEADS,),
        in_specs=[pl.BlockSpec(memory_space=pltpu.SMEM), full(bw), full(bs), full(bc)],
        out_specs=[per_head(bw), per_head(bs), per_head(bc)],
        out_shape=[jax.ShapeDtypeStruct((N_HEADS,) + a.shape, F32) for a in (bw, bs, bc)],
        compiler_params=_cparams(1),
        name="bias_tiles",
    )(rel_bias, bw, bs, bc)
    b_far = rel_bias[N_BUCKETS - 1] * LOG2E
    hi = b_far.astype(BF16)
    lo = (b_far - hi.astype(F32)).astype(BF16)
    ext = jnp.zeros((N_HEADS, HEAD_DIM, Q_BLOCK), BF16)
    ext = ext.at[:, 0, :].set(hi[:, None]).at[:, 1, :].set(lo[:, None])
    ext = ext.reshape(N_GROUPS, GROUP_SIZE, HEAD_DIM, Q_BLOCK).transpose(0, 2, 1, 3)
    return ww, ws, wc, ext.reshape(N_GROUPS, HEAD_DIM, GQ)


def _block_diag(w):
    n, d, e = w.shape
    eye = jnp.eye(n, dtype=w.dtype)
    return (eye[:, None, :, None] * w[:, :, None, :]).reshape(n * d, n * e)


def kernel(x, norm_gain, w_in, q_norm_gain, k_norm_gain, cmp_pe, cmp_w1, cmp_b1, cmp_w2, rel_bias,
           conv_w, conv_b, lru_wa, lru_ba, lru_wx, lru_bx, lru_lambda, w_proj_a, w_proj_b, w_out):
    bsz, s, _ = x.shape
    assert bsz == 1 and s % 1024 == 0 and s >= 1024
    x2 = x.reshape(s, D_MODEL)
    nsb = s // SEL_BLOCK
    nsbp = -(-nsb // SUPER) * SUPER

    o = IN_OFFS
    w16 = w_in.astype(BF16)
    pad = jnp.zeros((D_MODEL, BR_PAD - 3 * N_HEADS), BF16)
    w_front = jnp.concatenate([w16[:, o[0]:o[2]], w16[:, o[3]:o[4]], pad, w16[:, o[4]:o[5]]], axis=1)
    w_gate = jnp.concatenate([w16[:, o[2]:o[3]], w16[:, o[5]:o[7]]], axis=1)

    scale = HEAD_DIM ** -0.5 * LOG2E
    qgain_col = jnp.tile(q_norm_gain * scale, N_HEADS).reshape(NSA_WIDTH, 1)
    kgain_rows = jnp.tile(k_norm_gain, (1, N_GROUPS))
    (qhi_t, qlo_t, br_t, kaug, vt512, vt128, kwin, vwt, kvc, u_lru) = _front(
        x2, norm_gain, w_front, qgain_col, kgain_rows)

    w2p = jnp.pad(cmp_w2, ((0, 0), (0, 0), (0, 128 - HEAD_DIM)))
    kc_gain = jnp.pad(k_norm_gain[0], (0, 128 - HEAD_DIM)).reshape(1, 128)
    kc_cat, vct = _compress(kvc, cmp_w1, cmp_b1.reshape(2, 1, CMP_HIDDEN), w2p,
                            cmp_pe.reshape(2, 1, CMP_BLOCK * HEAD_DIM), kc_gain)

    ww_t, ws_t, wc_t, ext_q = _bias_tables(rel_bias)
    oc_t, neg_t = _cmp_attention(qhi_t, qlo_t, kc_cat, vct, wc_t, ext_q, nsbp)
    os_t, ow_t = _sel_win_attention(qhi_t, neg_t, kaug, vt512, vt128, ws_t, ext_q, kwin, vwt, ww_t)

    h_lru = _rglru(u_lru, conv_w, conv_b, _block_diag(lru_wa).astype(BF16), lru_ba,
                   _block_diag(lru_wx).astype(BF16), lru_bx, lru_lambda)

    y = _output(x2, norm_gain, oc_t, os_t, ow_t, br_t, h_lru, w_gate,
                w_proj_a.astype(BF16), w_proj_b.astype(BF16), w_out.astype(BF16))
    return y.reshape(bsz, s, D_MODEL)
```

```python
import functools
import math

import numpy as np
import jax
import jax.numpy as jnp
from jax import lax
from jax.experimental import pallas as pl
from jax.experimental.pallas import tpu as pltpu

F32 = jnp.float32
BF16 = jnp.bfloat16

D_MODEL = 1024
N_HEADS = 8
N_GROUPS = 2
GROUP_SIZE = N_HEADS // N_GROUPS
HEAD_DIM = 64
NSA_WIDTH = N_HEADS * HEAD_DIM
KV_WIDTH = N_GROUPS * HEAD_DIM
CMP_STRIDE = 16
CMP_BLOCK = 32
CMP_HIDDEN = 256
SEL_BLOCK = 64
SEL_PER_CMP = SEL_BLOCK // CMP_STRIDE
N_SELECT = 16
WINDOW = 512
Q_BLOCK = 128
LRU_WIDTH = 512
LRU_BLOCKS = 8
CONV_WIDTH = 4
LRU_C = 8.0
N_BUCKETS = 32
MAX_DISTANCE = 128
EPS = 1e-6
NEG = -1e30
M_INIT = -5e29
LOG2E = 1.4426950408889634

GQ = GROUP_SIZE * Q_BLOCK
SUPER = 128
KV_CHUNK = 512
SEL_NEAR = 2 * Q_BLOCK
WIN_KEYS = WINDOW + Q_BLOCK
CMP_NEAR = 24
N_PICK = N_SELECT - 3
PICKED = -2.0
VT_ROWS = 80

IN_OFFS = tuple(int(v) for v in np.cumsum(
    (0, NSA_WIDTH, 6 * KV_WIDTH, NSA_WIDTH, 3 * N_HEADS, LRU_WIDTH, LRU_WIDTH, 2 * D_MODEL)))
BR_PAD = 128
FRONT_OFFS = tuple(int(v) for v in np.cumsum((0, NSA_WIDTH, 6 * KV_WIDTH, BR_PAD, LRU_WIDTH)))
GATE_OFFS = tuple(int(v) for v in np.cumsum((0, NSA_WIDTH, LRU_WIDTH, 2 * D_MODEL)))

VMEM_LIMIT = 56 * 1024 * 1024


def _cparams(n_axes):
    return pltpu.CompilerParams(dimension_semantics=("arbitrary",) * n_axes,
                                vmem_limit_bytes=VMEM_LIMIT)


def _dot(a, b):
    return jnp.dot(a, b, preferred_element_type=F32)


def _sigmoid(x):
    return 0.5 * jnp.tanh(0.5 * x) + 0.5


FRONT_TM = 512


def _normed_input(x_ref, g_ref):
    x = x_ref[...]
    ms = jnp.mean(x * x, axis=-1, keepdims=True)
    return (x * lax.rsqrt(ms + EPS) * g_ref[...]).astype(BF16)


def _group_rms(k, gain_row):
    sq = k * k
    lane = lax.broadcasted_iota(jnp.int32, k.shape, 1)
    lo = lane < HEAD_DIM
    s0 = jnp.sum(jnp.where(lo, sq, 0.0), axis=-1, keepdims=True)
    s1 = jnp.sum(jnp.where(lo, 0.0, sq), axis=-1, keepdims=True)
    inv = jnp.where(lo, lax.rsqrt(s0 / HEAD_DIM + EPS), lax.rsqrt(s1 / HEAD_DIM + EPS))
    return k * inv * gain_row


def _front_kernel(x_ref, g_ref, w_ref, qg_ref, kg_ref,
                  qhi_ref, qlo_ref, brt_ref, kaug_ref, vt512_ref, vt128_ref,
                  kwin_ref, vwt_ref, kvc_ref, u_ref):
    i = pl.program_id(0)
    tm = FRONT_TM
    h_in = _normed_input(x_ref, g_ref)
    proj = lambda a, b: _dot(h_in, w_ref[:, a:b])
    kv = proj(FRONT_OFFS[1], FRONT_OFFS[2])
    piece = lambda j: kv[:, j * KV_WIDTH:(j + 1) * KV_WIDTH]
    lane = lax.broadcasted_iota(jnp.int32, (tm, 128), 1)
    row = lax.broadcasted_iota(jnp.int32, (tm, 128), 0) + i * tm
    lo = lane < HEAD_DIM
    ones_cols = jnp.where((lane == HEAD_DIM) | (lane == HEAD_DIM + 1), 1.0, 0.0)
    onehot = jnp.where(lane == (row // SEL_BLOCK) % SUPER, 1.0, 0.0).astype(BF16)
    kvc_ref[0] = piece(0)
    kvc_ref[1] = piece(1)
    kslc = _group_rms(piece(2), kg_ref[1:2, :])
    kwin = _group_rms(piece(4), kg_ref[2:3, :])
    vslt = piece(3).T
    vwit = piece(5).T
    row_t = lax.broadcasted_iota(jnp.int32, (VT_ROWS - HEAD_DIM, tm), 0)
    ones_rows = jnp.where(row_t == 0, 1.0, 0.0)
    for g in range(N_GROUPS):
        sh = lambda a: a if g == 0 else pltpu.roll(a, HEAD_DIM, 1)
        kaug_ref[g, :, 0:128] = onehot
        kaug_ref[g, :, 128:256] = jnp.where(lo, sh(kslc), ones_cols).astype(BF16)
        kwin_ref[g] = jnp.where(lo, sh(kwin), 0.0).astype(BF16)
        vs = jnp.concatenate([vslt[g * HEAD_DIM:(g + 1) * HEAD_DIM], ones_rows], axis=0).astype(BF16)
        vw = jnp.concatenate([vwit[g * HEAD_DIM:(g + 1) * HEAD_DIM], ones_rows], axis=0).astype(BF16)
        for j in range(tm // KV_CHUNK):
            vt512_ref[g, j] = vs[:, j * KV_CHUNK:(j + 1) * KV_CHUNK]
        for j in range(tm // 128):
            vt128_ref[g, j] = vs[:, j * 128:(j + 1) * 128]
            vwt_ref[g, j] = vw[:, j * 128:(j + 1) * 128]

    qt = proj(FRONT_OFFS[0], FRONT_OFFS[1]).T
    for h in range(N_HEADS):
        blk = qt[h * HEAD_DIM:(h + 1) * HEAD_DIM]
        ms = jnp.mean(blk * blk, axis=0, keepdims=True)
        qn = blk * lax.rsqrt(ms + EPS) * qg_ref[h * HEAD_DIM:(h + 1) * HEAD_DIM, :]
        hi = qn.astype(BF16)
        qhi_ref[h * HEAD_DIM:(h + 1) * HEAD_DIM, :] = hi
        qlo_ref[h * HEAD_DIM:(h + 1) * HEAD_DIM, :] = (qn - hi.astype(F32)).astype(BF16)
    brt_ref[...] = _sigmoid(proj(FRONT_OFFS[2], FRONT_OFFS[3])).T[:32]
    u_ref[...] = proj(FRONT_OFFS[3], FRONT_OFFS[4])


def _front(x2, norm_gain, w_front, qgain_col, kgain_rows):
    s = x2.shape[0]
    tm = FRONT_TM
    g = N_GROUPS
    outs = [
        jax.ShapeDtypeStruct((NSA_WIDTH, s), BF16),
        jax.ShapeDtypeStruct((NSA_WIDTH, s), BF16),
        jax.ShapeDtypeStruct((32, s), F32),
        jax.ShapeDtypeStruct((g, s, 256), BF16),
        jax.ShapeDtypeStruct((g, s // KV_CHUNK, VT_ROWS, KV_CHUNK), BF16),
        jax.ShapeDtypeStruct((g, s // 128, VT_ROWS, 128), BF16),
        jax.ShapeDtypeStruct((g, s, 128), BF16),
        jax.ShapeDtypeStruct((g, s // 128, VT_ROWS, 128), BF16),
        jax.ShapeDtypeStruct((2, s, KV_WIDTH), F32),
        jax.ShapeDtypeStruct((s, LRU_WIDTH), F32),
    ]
    out_specs = [
        pl.BlockSpec((NSA_WIDTH, tm), lambda i: (0, i)),
        pl.BlockSpec((NSA_WIDTH, tm), lambda i: (0, i)),
        pl.BlockSpec((32, tm), lambda i: (0, i)),
        pl.BlockSpec((g, tm, 256), lambda i: (0, i, 0)),
        pl.BlockSpec((g, tm // KV_CHUNK, VT_ROWS, KV_CHUNK), lambda i: (0, i, 0, 0)),
        pl.BlockSpec((g, tm // 128, VT_ROWS, 128), lambda i: (0, i, 0, 0)),
        pl.BlockSpec((g, tm, 128), lambda i: (0, i, 0)),
        pl.BlockSpec((g, tm // 128, VT_ROWS, 128), lambda i: (0, i, 0, 0)),
        pl.BlockSpec((2, tm, KV_WIDTH), lambda i: (0, i, 0)),
        pl.BlockSpec((tm, LRU_WIDTH), lambda i: (i, 0)),
    ]
    return pl.pallas_call(
        _front_kernel,
        grid=(s // tm,),
        in_specs=[pl.BlockSpec((tm, D_MODEL), lambda i: (i, 0)),
                  pl.BlockSpec((1, D_MODEL), lambda i: (0, 0)),
                  pl.BlockSpec(w_front.shape, lambda i: (0, 0)),
                  pl.BlockSpec((NSA_WIDTH, 1), lambda i: (0, 0)),
                  pl.BlockSpec((3, 128), lambda i: (0, 0))],
        out_specs=out_specs,
        out_shape=outs,
        compiler_params=_cparams(1),
        name="front",
    )(x2, norm_gain.reshape(1, D_MODEL), w_front, qgain_col, kgain_rows)


def _compress_kernel(kvc_ref, w1_ref, b1_ref, w2_ref, pe_ref, kg_ref, kc_ref, vct_ref):
    ncp = kc_ref.shape[1]
    half = CMP_STRIDE * HEAD_DIM
    lane = lax.broadcasted_iota(jnp.int32, (ncp, 128), 1)
    lo = lane < HEAD_DIM

    def chunk_rows(j, g):
        cols = []
        for pair in range(CMP_STRIDE // 2):
            even = kvc_ref[j, pl.ds(2 * pair, ncp, stride=CMP_STRIDE), :]
            odd = kvc_ref[j, pl.ds(2 * pair + 1, ncp, stride=CMP_STRIDE), :]
            if g == 0:
                cols.append(jnp.where(lo, even, pltpu.roll(odd, HEAD_DIM, 1)))
            else:
                cols.append(jnp.where(lo, pltpu.roll(even, HEAD_DIM, 1), odd))
        return jnp.concatenate(cols, axis=1)

    def phi(j, g):
        x = chunk_rows(j, g).astype(BF16)
        w1 = w1_ref[j].astype(BF16)
        a = _dot(x, w1[:half])
        b = _dot(x, w1[half:])
        b_next = pltpu.roll(b, ncp - 1, 0)
        pe8 = jnp.broadcast_to(pe_ref[j], (8, 2 * half)).astype(BF16)
        pe_term = _dot(pe8, w1)[0:1]
        hid = a + b_next + pe_term + b1_ref[j]
        act = (hid * _sigmoid(hid)).astype(BF16)
        return _dot(act, w2_ref[j].astype(BF16))

    ones_cols = jnp.where((lane == HEAD_DIM) | (lane == HEAD_DIM + 1), 1.0, 0.0)
    for g in range(N_GROUPS):
        kc = phi(0, g)
        ms = jnp.sum(kc * kc, axis=-1, keepdims=True) / HEAD_DIM
        kc = kc * lax.rsqrt(ms + EPS) * kg_ref[...]
        hi = kc.astype(BF16).astype(F32)
        lo_part = (kc - hi).astype(BF16).astype(F32)
        kc_ref[g, :, 0:128] = (hi + pltpu.roll(lo_part, HEAD_DIM, 1)).astype(BF16)
        kc_ref[g, :, 128:256] = jnp.where(lo, hi, ones_cols).astype(BF16)
        vt = phi(1, g).T
        row_t = lax.broadcasted_iota(jnp.int32, vt.shape, 0)
        vct_ref[g] = jnp.where(row_t == HEAD_DIM, 1.0, vt).astype(BF16)


def _compress(kvc, w1, b1, w2p, pe_flat, kgain_row):
    s = kvc.shape[1]
    g = N_GROUPS
    ncp = s // CMP_STRIDE
    full = lambda a: pl.BlockSpec(a.shape, lambda i: (0,) * a.ndim)
    return pl.pallas_call(
        _compress_kernel,
        grid=(1,),
        in_specs=[pl.BlockSpec(kvc.shape, lambda i: (0, 0, 0), pipeline_mode=pl.Buffered(1)),
                  full(w1), full(b1), full(w2p), full(pe_flat), full(kgain_row)],
        out_specs=[pl.BlockSpec((g, ncp, 256), lambda i: (0, 0, 0)),
                   pl.BlockSpec((g, 128, ncp), lambda i: (0, 0, 0))],
        out_shape=[jax.ShapeDtypeStruct((g, ncp, 256), BF16),
                   jax.ShapeDtypeStruct((g, 128, ncp), BF16)],
        compiler_params=_cparams(1),
        name="compress",
    )(kvc, w1, b1, w2p, pe_flat, kgain_row)


def _heads_to_lanes(q):
    return jnp.concatenate([q[r * HEAD_DIM:(r + 1) * HEAD_DIM] for r in range(GROUP_SIZE)], axis=1)


def _store_heads(out_ref, o):
    for r in range(GROUP_SIZE):
        out_ref[r * HEAD_DIM:(r + 1) * HEAD_DIM, :] = o[0:HEAD_DIM, r * Q_BLOCK:(r + 1) * Q_BLOCK]


def _bias_tile(w_ref, off, n, head0=0):
    return jnp.concatenate([w_ref[head0 + r, pl.ds(off, n), :] for r in range(GROUP_SIZE)], axis=1)


def _cmp_kernel(qhi_ref, qlo_ref, kc_ref, vct_ref, wc_ref, ext_ref, ocmp_ref, neg_ref, s_ref, imp_ref):
    qb = pl.program_id(0)
    ncp = kc_ref.shape[1]
    nsbp = neg_ref.shape[1]
    gr = GROUP_SIZE * HEAD_DIM
    ws = pl.multiple_of(jnp.maximum(8 * qb - 16, 0), 8)
    off = pl.multiple_of(jnp.maximum(16 - 8 * qb, 0), 8)
    lim = ws + CMP_NEAR
    n_cls = min(8, ncp // 128)
    per_cls = ncp // n_cls

    def attend(nrows):
        nblk = nrows // SEL_PER_CMP
        band = nrows - per_cls
        rows = lax.broadcasted_iota(jnp.int32, (per_cls, GQ), 0) + band
        blk = lax.broadcasted_iota(jnp.int32, (nblk, Q_BLOCK), 0)
        lane = lax.broadcasted_iota(jnp.int32, (nblk, Q_BLOCK), 1)
        cur = 2 * qb + jnp.where(lane >= SEL_BLOCK, 1, 0)
        forced = jnp.where(blk == 0, 1.0, jnp.where(blk == cur, 1.0, jnp.where(blk == cur - 1, 1.0, 0.0)))
        carry = []
        for g in range(N_GROUPS):
            qhi = qhi_ref[g * gr:(g + 1) * gr, :]
            qlo = qlo_ref[g * gr:(g + 1) * gr, :]
            cols = []
            for r in range(GROUP_SIZE):
                a = qhi[r * HEAD_DIM:(r + 1) * HEAD_DIM]
                b = qlo[r * HEAD_DIM:(r + 1) * HEAD_DIM]
                cols.append(jnp.concatenate([a, a, b], axis=0))
            qcat = jnp.concatenate([jnp.concatenate(cols, axis=1), ext_ref[g]], axis=0)
            s_g = s_ref.at[g]
            s_g[0:nrows, :] = _dot(kc_ref[g, 0:nrows, :], qcat)
            s_g[pl.ds(ws, CMP_NEAR), :] += _bias_tile(wc_ref, off, CMP_NEAR, g * GROUP_SIZE)
            s_g[band:nrows, :] = jnp.where(rows < lim, s_g[band:nrows, :], NEG)
            s = s_g[0:nrows, :]
            m = jnp.maximum(jnp.max(s, axis=0, keepdims=True), M_INIT)
            p = jnp.exp2(s - m)
            acc = _dot(vct_ref[g, 0:VT_ROWS, 0:nrows], p.astype(BF16))
            l = acc[HEAD_DIM:HEAD_DIM + 1]
            inv_l = jnp.where(l > 0.0, 1.0 / l, 0.0)
            _store_heads(ocmp_ref.at[g * gr:(g + 1) * gr], acc[0:HEAD_DIM] * inv_l)
            imp = None
            for r in range(GROUP_SIZE):
                cs = slice(r * Q_BLOCK, (r + 1) * Q_BLOCK)
                term = p[:, cs] * inv_l[:, cs]
                imp = term if imp is None else imp + term
            imp_g = imp_ref.at[g]
            imp_g[0:nrows, :] = imp
            strided = [imp_g[pl.ds(k, nblk, stride=SEL_PER_CMP), :] for k in range(SEL_PER_CMP)]
            prev_last = jnp.where(blk == 0, 0.0, pltpu.roll(strided[SEL_PER_CMP - 1], 1, 0))
            impb = prev_last + strided[0] + strided[1] + strided[2] + strided[3]
            carry.append(jnp.where(blk < cur - 1, jnp.where(blk > 0, impb, -1.0), -1.0))

        def pick_one(_, cands):
            out = []
            for sc in cands:
                mx = jnp.max(sc, axis=0, keepdims=True)
                hit_blk = jnp.where(sc == mx, jnp.where(mx >= 0.0, blk, nblk), nblk)
                first = jnp.min(hit_blk, axis=0, keepdims=True)
                out.append(jnp.where(blk == first, PICKED, sc))
            return tuple(out)

        if nblk * N_GROUPS <= 2 * SUPER:
            carry = lax.fori_loop(0, N_PICK, pick_one, tuple(carry))
        else:
            carry = [lax.fori_loop(0, N_PICK, pick_one, (c,))[0] for c in carry]
        for g in range(N_GROUPS):
            picked = carry[g] == PICKED
            sel_neg = jnp.where(picked, 0.0, jnp.where(blk <= cur, jnp.where(forced > 0.5, 0.0, NEG), NEG))
            neg_ref[g, 0:nblk, :] = sel_neg.astype(BF16)
            if nblk < nsbp:
                neg_ref[g, nblk:nsbp, :] = jnp.full((nsbp - nblk, Q_BLOCK), NEG, BF16)

    cls = (lim + per_cls - 1) // per_cls
    for k in range(1, n_cls + 1):
        pl.when(cls == k)(functools.partial(attend, k * per_cls))


def _cmp_attention(qhi_t, qlo_t, kc_cat, vct, wc_t, ext_q, nsbp):
    s = qhi_t.shape[1]
    g = N_GROUPS
    ncp = kc_cat.shape[1]
    full = lambda a: pl.BlockSpec(a.shape, lambda qb: (0,) * a.ndim)
    return pl.pallas_call(
        _cmp_kernel,
        grid=(s // Q_BLOCK,),
        in_specs=[pl.BlockSpec((NSA_WIDTH, Q_BLOCK), lambda qb: (0, qb)),
                  pl.BlockSpec((NSA_WIDTH, Q_BLOCK), lambda qb: (0, qb)),
                  full(kc_cat), full(vct), full(wc_t), full(ext_q)],
        out_specs=[pl.BlockSpec((NSA_WIDTH, Q_BLOCK), lambda qb: (0, qb)),
                   pl.BlockSpec((g, nsbp, Q_BLOCK), lambda qb: (0, 0, qb))],
        out_shape=[jax.ShapeDtypeStruct((NSA_WIDTH, s), F32),
                   jax.ShapeDtypeStruct((g, nsbp, s), BF16)],
        scratch_shapes=[pltpu.VMEM((g, ncp, GQ), F32), pltpu.VMEM((g, ncp, Q_BLOCK), F32)],
        compiler_params=_cparams(1),
        name="cmp_select",
    )(qhi_t, qlo_t, kc_cat, vct, wc_t, ext_q)


def _sel_kernel(qhi_ref, neg_ref, kaug_ref, vt512_ref, vt128_ref, ws_ref, ext_ref,
                kwin_ref, vwt_ref, ww_ref, out_ref, owin_ref,
                qaug_ref, m_ref, acc_ref, s0_ref, s1_ref, s2_ref, cm0_ref, cm1_ref, cm2_ref):
    qb = pl.program_id(1)
    nsbp = neg_ref.shape[0]
    nsc = nsbp // SUPER
    qg = _heads_to_lanes(qhi_ref[...])
    qx = jnp.concatenate([qg, ext_ref[...]], axis=0)
    blk = lax.broadcasted_iota(jnp.int32, (SUPER, Q_BLOCK), 0)

    def aug(neg_rows):
        tiled = jnp.concatenate([neg_rows] * GROUP_SIZE, axis=1).astype(BF16)
        return jnp.concatenate([tiled, qx], axis=0)

    for sc in range(nsc):
        neg_rows = neg_ref[sc * SUPER:(sc + 1) * SUPER, :].astype(F32)
        qaug_ref[sc] = aug(jnp.where(blk + sc * SUPER >= 2 * qb - 2, NEG, neg_rows))

    m_ref[...] = jnp.full(m_ref.shape, M_INIT, F32)
    acc_ref[...] = jnp.zeros(acc_ref.shape, F32)

    def update(s, col_max, vt):
        m_old = m_ref[...]
        m_new = jnp.maximum(m_old, col_max)
        p = jnp.exp2(s - m_new).astype(BF16)
        acc_ref[...] = acc_ref[...] * jnp.exp2(m_old - m_new) + _dot(vt, p)
        m_ref[...] = m_new

    n_far = (jnp.maximum(qb - 1, 0) * Q_BLOCK + KV_CHUNK - 1) // KV_CHUNK

    def scores(c, dst_ref, max_ref):
        k = kaug_ref[pl.ds(pl.multiple_of(c * KV_CHUNK, KV_CHUNK), KV_CHUNK), :]
        sc = (c * (KV_CHUNK // SEL_BLOCK)) // SUPER
        s = _dot(k, qaug_ref[sc])
        dst_ref[...] = s
        max_ref[...] = jnp.max(s, axis=0, keepdims=True)

    kb0 = jnp.maximum(qb - 1, 0)
    ws = pl.multiple_of(kb0 * Q_BLOCK, Q_BLOCK)
    off = pl.multiple_of(jnp.where(qb == 0, Q_BLOCK, 0), Q_BLOCK)
    b_lo = 2 * kb0
    sc_lo = pl.multiple_of((b_lo // SUPER) * SUPER, SUPER)
    sc_hi = pl.multiple_of(((b_lo + 3) // SUPER) * SUPER, SUPER)
    neg_lo = neg_ref[pl.ds(sc_lo, SUPER), :].astype(F32)
    neg_hi = neg_ref[pl.ds(sc_hi, SUPER), :].astype(F32)
    near_neg = jnp.where(blk >= SUPER // 2, neg_lo, neg_hi)
    k = kaug_ref[pl.ds(ws, SEL_NEAR), :]
    s_near = _dot(k, aug(near_neg)) + _bias_tile(ws_ref, off, SEL_NEAR)
    vt_near = jnp.concatenate([vt128_ref[kb0], vt128_ref[kb0 + 1]], axis=1)

    s_win = _window_scores(qb, qg, kwin_ref, ww_ref)
    bufs = ((s0_ref, cm0_ref), (s1_ref, cm1_ref), (s2_ref, cm2_ref))
    last = jnp.maximum(n_far - 1, 0)
    scores(0, *bufs[0])
    scores(jnp.minimum(1, last), *bufs[1])
    update(s_near, jnp.max(s_near, axis=0, keepdims=True), vt_near)
    _window_finish(qb, s_win, vwt_ref, owin_ref)

    def far_triple(c):
        for j in range(3):
            scores(jnp.minimum(c + j + 2, last), *bufs[(j + 2) % 3])
            s_ref, cm_ref = bufs[j]
            update(s_ref[...], cm_ref[...], vt512_ref[c + j])

    def far_six(i, carry):
        far_triple(6 * i)
        far_triple(6 * i + 3)
        return carry

    lax.fori_loop(0, n_far // 6, far_six, 0)
    done = (n_far // 6) * 6

    @pl.when(n_far - done >= 3)
    def _():
        far_triple(done)

    done = (n_far // 3) * 3
    for j in range(2):
        @pl.when(n_far - done > j)
        def _():
            s_ref, cm_ref = bufs[j]
            update(s_ref[...], cm_ref[...], vt512_ref[done + j])

    acc = acc_ref[...]
    _store_heads(out_ref, acc[0:HEAD_DIM] * (1.0 / acc[HEAD_DIM:HEAD_DIM + 1]))


def _sel_win_attention(qhi_t, neg_t, kaug, vt512, vt128, ws_t, ext_q, kwin, vwt, ww_t):
    s = qhi_t.shape[1]
    g = N_GROUPS
    nqb = s // Q_BLOCK
    nsbp = neg_t.shape[1]
    gr = GROUP_SIZE * HEAD_DIM
    once = pl.Buffered(1)
    return pl.pallas_call(
        _sel_kernel,
        grid=(g, nqb),
        in_specs=[pl.BlockSpec((gr, Q_BLOCK), lambda gi, qb: (gi, qb)),
                  pl.BlockSpec((None, nsbp, Q_BLOCK), lambda gi, qb: (gi, 0, qb)),
                  pl.BlockSpec((None, s, 256), lambda gi, qb: (gi, 0, 0), pipeline_mode=once),
                  pl.BlockSpec((None, s // KV_CHUNK, VT_ROWS, KV_CHUNK), lambda gi, qb: (gi, 0, 0, 0),
                               pipeline_mode=once),
                  pl.BlockSpec((None, s // 128, VT_ROWS, 128), lambda gi, qb: (gi, 0, 0, 0),
                               pipeline_mode=once),
                  pl.BlockSpec((GROUP_SIZE,) + ws_t.shape[1:], lambda gi, qb: (gi, 0, 0)),
                  pl.BlockSpec((None, HEAD_DIM, GQ), lambda gi, qb: (gi, 0, 0)),
                  pl.BlockSpec((None, s, 128), lambda gi, qb: (gi, 0, 0), pipeline_mode=once),
                  pl.BlockSpec((None, s // 128, VT_ROWS, 128), lambda gi, qb: (gi, 0, 0, 0),
                               pipeline_mode=once),
                  pl.BlockSpec((GROUP_SIZE,) + ww_t.shape[1:], lambda gi, qb: (gi, 0, 0))],
        out_specs=[pl.BlockSpec((gr, Q_BLOCK), lambda gi, qb: (gi, qb)),
                   pl.BlockSpec((gr, Q_BLOCK), lambda gi, qb: (gi, qb))],
        out_shape=[jax.ShapeDtypeStruct((NSA_WIDTH, s), F32),
                   jax.ShapeDtypeStruct((NSA_WIDTH, s), F32)],
        scratch_shapes=[pltpu.VMEM((nsbp // SUPER, 256, GQ), BF16),
                        pltpu.VMEM((1, GQ), F32),
                        pltpu.VMEM((VT_ROWS, GQ), F32),
                        pltpu.VMEM((KV_CHUNK, GQ), F32),
                        pltpu.VMEM((KV_CHUNK, GQ), F32),
                        pltpu.VMEM((KV_CHUNK, GQ), F32),
                        pltpu.VMEM((1, GQ), F32),
                        pltpu.VMEM((1, GQ), F32),
                        pltpu.VMEM((1, GQ), F32)],
        compiler_params=_cparams(2),
        name="sel_win_attention",
    )(qhi_t, neg_t, kaug, vt512, vt128, ws_t, ext_q, kwin, vwt, ww_t)


def _window_scores(qb, qg, kwin_ref, ww_ref):
    kb0 = jnp.maximum(qb - WINDOW // Q_BLOCK, 0)
    ws = pl.multiple_of(kb0 * Q_BLOCK, Q_BLOCK)
    off = pl.multiple_of(jnp.maximum(WINDOW - qb * Q_BLOCK, 0), Q_BLOCK)
    qx = jnp.concatenate([qg, jnp.zeros_like(qg)], axis=0)
    return _dot(kwin_ref[pl.ds(ws, WIN_KEYS), :], qx) + _bias_tile(ww_ref, off, WIN_KEYS)


def _window_finish(qb, s, vwt_ref, out_ref):
    kb0 = jnp.maximum(qb - WINDOW // Q_BLOCK, 0)
    m = jnp.max(s, axis=0, keepdims=True)
    p = jnp.exp2(s - m).astype(BF16)
    vt = jnp.concatenate([vwt_ref[kb0 + j] for j in range(WIN_KEYS // Q_BLOCK)], axis=1)
    acc = _dot(vt, p)
    _store_heads(out_ref, acc[0:HEAD_DIM] * (1.0 / acc[HEAD_DIM:HEAD_DIM + 1]))


LRU_TM = 256


def _lru_kernel(u_ref, cw_ref, cb_ref, wa_ref, ba_ref, wx_ref, bx_ref, lam_ref, h_ref,
                tail_ref, hprev_ref, uc_ref):
    tm = LRU_TM

    @pl.when(pl.program_id(0) == 0)
    def _():
        tail_ref[...] = jnp.zeros(tail_ref.shape, F32)
        hprev_ref[...] = jnp.zeros(hprev_ref.shape, F32)

    u = u_ref[...]
    tail = tail_ref[...]
    row8 = lax.broadcasted_iota(jnp.int32, (8, LRU_WIDTH), 0)
    uc = cb_ref[...] + u * cw_ref[CONV_WIDTH - 1:CONV_WIDTH, :]
    uc_head = uc[0:8]
    for j in range(1, CONV_WIDTH):
        w_j = cw_ref[CONV_WIDTH - 1 - j:CONV_WIDTH - j, :]
        sh = pltpu.roll(u, j, 0)
        uc = uc + sh * w_j
        uc_head = uc_head + jnp.where(row8 < j, pltpu.roll(tail, j, 0), sh[0:8]) * w_j
    tail_ref[...] = u[tm - 8:tm]
    uc_ref[...] = uc
    uc_ref[0:8, :] = uc_head
    uc = uc_ref[...]

    ucb = uc.astype(BF16)
    r = _sigmoid(_dot(ucb, wa_ref[...]) + ba_ref[...])
    ig = _sigmoid(_dot(ucb, wx_ref[...]) + bx_ref[...])
    nl = -lam_ref[...]
    softplus = jnp.maximum(nl, 0.0) + jnp.log(1.0 + jnp.exp(-jnp.abs(nl)))
    a = jnp.exp((-LRU_C * r) * softplus)
    b = jnp.sqrt(1.0 - a * a) * (ig * uc)

    a = a.reshape(tm // 8, 8, LRU_WIDTH)
    b = b.reshape(tm // 8, 8, LRU_WIDTH)
    sub = lax.broadcasted_iota(jnp.int32, a.shape, 1)
    for step in (1, 2, 4):
        a_s = pltpu.roll(a, step, 1)
        b_s = pltpu.roll(b, step, 1)
        ok = sub >= step
        b = jnp.where(ok, a * b_s + b, b)
        a = jnp.where(ok, a * a_s, a)
    carry = hprev_ref[...]
    for j in range(tm // 8):
        hj = a[j] * carry + b[j]
        h_ref[8 * j:8 * j + 8, :] = hj
        carry = hj[7:8]
    hprev_ref[...] = carry


def _rglru(u, conv_w, conv_b, wa_bd, ba, wx_bd, bx, lam):
    s = u.shape[0]
    tm = LRU_TM
    row = lambda v: v.reshape(1, LRU_WIDTH)
    full = lambda shape: pl.BlockSpec(shape, lambda i: (0, 0))
    return pl.pallas_call(
        _lru_kernel,
        grid=(s // tm,),
        in_specs=[pl.BlockSpec((tm, LRU_WIDTH), lambda i: (i, 0)),
                  full((CONV_WIDTH, LRU_WIDTH)), full((1, LRU_WIDTH)),
                  full((LRU_WIDTH, LRU_WIDTH)), full((1, LRU_WIDTH)),
                  full((LRU_WIDTH, LRU_WIDTH)), full((1, LRU_WIDTH)),
                  full((1, LRU_WIDTH))],
        out_specs=pl.BlockSpec((tm, LRU_WIDTH), lambda i: (i, 0)),
        out_shape=jax.ShapeDtypeStruct((s, LRU_WIDTH), F32),
        scratch_shapes=[pltpu.VMEM((8, LRU_WIDTH), F32), pltpu.VMEM((1, LRU_WIDTH), F32),
                        pltpu.VMEM((tm, LRU_WIDTH), F32)],
        compiler_params=_cparams(1),
        name="rglru",
    )(u, conv_w, row(conv_b), wa_bd, row(ba), wx_bd, row(bx), row(lam))


OUT_TM = 512


def _out_kernel(x_ref, g_ref, oc_ref, os_ref, ow_ref, brt_ref, h_ref,
                wg_ref, wpa_ref, wpb_ref, wo_ref, y_ref):
    h_in = _normed_input(x_ref, g_ref)
    gate = lambda j: _dot(h_in, wg_ref[:, GATE_OFFS[j]:GATE_OFFS[j + 1]])
    brt = brt_ref[...]
    parts = []
    for h in range(N_HEADS):
        rows = slice(h * HEAD_DIM, (h + 1) * HEAD_DIM)
        acc = None
        for b, o_ref in enumerate((oc_ref, os_ref, ow_ref)):
            term = brt[b * N_HEADS + h:b * N_HEADS + h + 1, :] * o_ref[rows, :]
            acc = term if acc is None else acc + term
        parts.append(acc)
    gn = gate(0)
    ya = jnp.concatenate(parts, axis=0).T * (gn * _sigmoid(gn))
    y_a = _dot(ya.astype(BF16), wpa_ref[...])
    gl = gate(1)
    y_b = _dot((h_ref[...] * (gl * _sigmoid(gl))).astype(BF16), wpb_ref[...])
    mg = gate(2)
    m = _sigmoid(mg[:, :D_MODEL]) * y_a + _sigmoid(mg[:, D_MODEL:]) * y_b
    y_ref[...] = x_ref[...] + _dot(m.astype(BF16), wo_ref[...])


def _output(x2, norm_gain, oc_t, os_t, ow_t, br_t, h_lru, w_gate, wpa, wpb, wo):
    s = x2.shape[0]
    tm = OUT_TM
    rows = lambda n: pl.BlockSpec((tm, n), lambda i: (i, 0))
    cols = lambda n: pl.BlockSpec((n, tm), lambda i: (0, i))
    full = lambda a: pl.BlockSpec(a.shape, lambda i: (0, 0))
    return pl.pallas_call(
        _out_kernel,
        grid=(s // tm,),
        in_specs=[rows(D_MODEL), pl.BlockSpec((1, D_MODEL), lambda i: (0, 0)),
                  cols(NSA_WIDTH), cols(NSA_WIDTH), cols(NSA_WIDTH), cols(32), rows(LRU_WIDTH),
                  full(w_gate), full(wpa), full(wpb), full(wo)],
        out_specs=rows(D_MODEL),
        out_shape=jax.ShapeDtypeStruct((s, D_MODEL), F32),
        compiler_params=_cparams(1),
        name="output",
    )(x2, norm_gain.reshape(1, D_MODEL), oc_t, os_t, ow_t, br_t, h_lru, w_gate, wpa, wpb, wo)


def _t5_bucket_table(n_dist):
    n = np.arange(n_dist)
    max_exact = N_BUCKETS // 2
    nf = np.maximum(n, 1).astype(np.float32)
    large = max_exact + (np.log(nf / np.float32(max_exact)) / np.float32(math.log(MAX_DISTANCE / max_exact))
                         * np.float32(N_BUCKETS - max_exact)).astype(np.int32)
    return np.where(n < max_exact, n, np.minimum(large, N_BUCKETS - 1))


def _bias_tiles_kernel(relb_ref, bw_ref, bs_ref, bc_ref, ww_ref, ws_ref, wc_ref):
    h = pl.program_id(0)
    far = relb_ref[N_BUCKETS - 1, h]
    for bk_ref, out_ref, rel in ((bw_ref, ww_ref, False), (bs_ref, ws_ref, True), (bc_ref, wc_ref, True)):
        bk = bk_ref[...]
        acc = jnp.full(bk.shape, NEG, F32)
        for b in range(N_BUCKETS):
            val = relb_ref[b, h] - far if rel else relb_ref[b, h]
            acc = jnp.where(bk == b, val * LOG2E, acc)
        out_ref[...] = acc


def _bias_tables(rel_bias):
    bucket = _t5_bucket_table(2 * WINDOW)
    i = np.arange(Q_BLOCK)[None, :]

    def index_tile(dist, valid):
        return jnp.asarray(np.where(valid, bucket[np.clip(dist, 0, bucket.size - 1)], -1).astype(np.int32))

    u = np.arange(WINDOW + WIN_KEYS)[:, None]
    d_win = i - u + WINDOW
    bw = index_tile(d_win, (d_win >= 0) & (d_win < WINDOW))
    u = np.arange(SEL_NEAR + Q_BLOCK)[:, None]
    d_sel = i - u + Q_BLOCK
    bs = index_tile(d_sel, d_sel >= 0)
    u = np.arange(CMP_NEAR + 16)[:, None]
    d_cmp = i - CMP_STRIDE * (u - 16) - (CMP_BLOCK - 1)
    bc = index_tile(d_cmp, d_cmp >= 0)
    full = lambda a: pl.BlockSpec(a.shape, lambda h: (0, 0))
    per_head = lambda a: pl.BlockSpec((None,) + a.shape, lambda h: (h, 0, 0))
    ww, ws, wc = pl.pallas_call(
        _bias_tiles_kernel,
        grid=(N_HEADS,),
        in_specs=[pl.BlockSpec(memory_space=pltpu.SMEM), full(bw), full(bs), full(bc)],
        out_specs=[per_head(bw), per_head(bs), per_head(bc)],
        out_shape=[jax.ShapeDtypeStruct((N_HEADS,) + a.shape, F32) for a in (bw, bs, bc)],
        compiler_params=_cparams(1),
        name="bias_tiles",
    )(rel_bias, bw, bs, bc)
    b_far = rel_bias[N_BUCKETS - 1] * LOG2E
    hi = b_far.astype(BF16)
    lo = (b_far - hi.astype(F32)).astype(BF16)
    ext = jnp.zeros((N_HEADS, HEAD_DIM, Q_BLOCK), BF16)
    ext = ext.at[:, 0, :].set(hi[:, None]).at[:, 1, :].set(lo[:, None])
    ext = ext.reshape(N_GROUPS, GROUP_SIZE, HEAD_DIM, Q_BLOCK).transpose(0, 2, 1, 3)
    return ww, ws, wc, ext.reshape(N_GROUPS, HEAD_DIM, GQ)


def _block_diag(w):
    n, d, e = w.shape
    eye = jnp.eye(n, dtype=w.dtype)
    return (eye[:, None, :, None] * w[:, :, None, :]).reshape(n * d, n * e)


def kernel(x, norm_gain, w_in, q_norm_gain, k_norm_gain, cmp_pe, cmp_w1, cmp_b1, cmp_w2, rel_bias,
           conv_w, conv_b, lru_wa, lru_ba, lru_wx, lru_bx, lru_lambda, w_proj_a, w_proj_b, w_out):
    bsz, s, _ = x.shape
    assert bsz == 1 and s % 1024 == 0 and s >= 1024
    x2 = x.reshape(s, D_MODEL)
    nsb = s // SEL_BLOCK
    nsbp = -(-nsb // SUPER) * SUPER

    o = IN_OFFS
    w16 = w_in.astype(BF16)
    pad = jnp.zeros((D_MODEL, BR_PAD - 3 * N_HEADS), BF16)
    w_front = jnp.concatenate([w16[:, o[0]:o[2]], w16[:, o[3]:o[4]], pad, w16[:, o[4]:o[5]]], axis=1)
    w_gate = jnp.concatenate([w16[:, o[2]:o[3]], w16[:, o[5]:o[7]]], axis=1)

    scale = HEAD_DIM ** -0.5 * LOG2E
    qgain_col = jnp.tile(q_norm_gain * scale, N_HEADS).reshape(NSA_WIDTH, 1)
    kgain_rows = jnp.tile(k_norm_gain, (1, N_GROUPS))
    (qhi_t, qlo_t, br_t, kaug, vt512, vt128, kwin, vwt, kvc, u_lru) = _front(
        x2, norm_gain, w_front, qgain_col, kgain_rows)

    w2p = jnp.pad(cmp_w2, ((0, 0), (0, 0), (0, 128 - HEAD_DIM)))
    kc_gain = jnp.pad(k_norm_gain[0], (0, 128 - HEAD_DIM)).reshape(1, 128)
    kc_cat, vct = _compress(kvc, cmp_w1, cmp_b1.reshape(2, 1, CMP_HIDDEN), w2p,
                            cmp_pe.reshape(2, 1, CMP_BLOCK * HEAD_DIM), kc_gain)

    ww_t, ws_t, wc_t, ext_q = _bias_tables(rel_bias)
    oc_t, neg_t = _cmp_attention(qhi_t, qlo_t, kc_cat, vct, wc_t, ext_q, nsbp)
    os_t, ow_t = _sel_win_attention(qhi_t, neg_t, kaug, vt512, vt128, ws_t, ext_q, kwin, vwt, ww_t)

    h_lru = _rglru(u_lru, conv_w, conv_b, _block_diag(lru_wa).astype(BF16), lru_ba,
                   _block_diag(lru_wx).astype(BF16), lru_bx, lru_lambda)

    y = _output(x2, norm_gain, oc_t, os_t, ow_t, br_t, h_lru, w_gate,
                w_proj_a.astype(BF16), w_proj_b.astype(BF16), w_out.astype(BF16))
    return y.reshape(bsz, s, D_MODEL)
```

```python
import functools
import math

import numpy as np
import jax
import jax.numpy as jnp
from jax import lax
from jax.experimental import pallas as pl
from jax.experimental.pallas import tpu as pltpu

F32 = jnp.float32
BF16 = jnp.bfloat16

D_MODEL = 1024
N_HEADS = 8
N_GROUPS = 2
GROUP_SIZE = N_HEADS // N_GROUPS
HEAD_DIM = 64
NSA_WIDTH = N_HEADS * HEAD_DIM
KV_WIDTH = N_GROUPS * HEAD_DIM
CMP_STRIDE = 16
CMP_BLOCK = 32
CMP_HIDDEN = 256
SEL_BLOCK = 64
SEL_PER_CMP = SEL_BLOCK // CMP_STRIDE
N_SELECT = 16
WINDOW = 512
Q_BLOCK = 128
LRU_WIDTH = 512
LRU_BLOCKS = 8
CONV_WIDTH = 4
LRU_C = 8.0
N_BUCKETS = 32
MAX_DISTANCE = 128
EPS = 1e-6
NEG = -1e30
M_INIT = -5e29
LOG2E = 1.4426950408889634

GQ = GROUP_SIZE * Q_BLOCK
SUPER = 128
KV_CHUNK = 512
SEL_NEAR = 2 * Q_BLOCK
WIN_KEYS = WINDOW + Q_BLOCK
CMP_NEAR = 24
N_PICK = N_SELECT - 3
PICKED = -2.0
VT_ROWS = 80

IN_OFFS = tuple(int(v) for v in np.cumsum(
    (0, NSA_WIDTH, 6 * KV_WIDTH, NSA_WIDTH, 3 * N_HEADS, LRU_WIDTH, LRU_WIDTH, 2 * D_MODEL)))
BR_PAD = 128
FRONT_OFFS = tuple(int(v) for v in np.cumsum((0, NSA_WIDTH, 6 * KV_WIDTH, BR_PAD, LRU_WIDTH)))
GATE_OFFS = tuple(int(v) for v in np.cumsum((0, NSA_WIDTH, LRU_WIDTH, 2 * D_MODEL)))

VMEM_LIMIT = 56 * 1024 * 1024


def _cparams(n_axes):
    return pltpu.CompilerParams(dimension_semantics=("arbitrary",) * n_axes,
                                vmem_limit_bytes=VMEM_LIMIT)


def _dot(a, b):
    return jnp.dot(a, b, preferred_element_type=F32)


def _sigmoid(x):
    return 0.5 * jnp.tanh(0.5 * x) + 0.5


FRONT_TM = 512


def _normed_input(x_ref, g_ref):
    x = x_ref[...]
    ms = jnp.mean(x * x, axis=-1, keepdims=True)
    return (x * lax.rsqrt(ms + EPS) * g_ref[...]).astype(BF16)


def _group_rms(k, gain_row):
    sq = k * k
    lane = lax.broadcasted_iota(jnp.int32, k.shape, 1)
    lo = lane < HEAD_DIM
    s0 = jnp.sum(jnp.where(lo, sq, 0.0), axis=-1, keepdims=True)
    s1 = jnp.sum(jnp.where(lo, 0.0, sq), axis=-1, keepdims=True)
    inv = jnp.where(lo, lax.rsqrt(s0 / HEAD_DIM + EPS), lax.rsqrt(s1 / HEAD_DIM + EPS))
    return k * inv * gain_row


def _front_kernel(x_ref, g_ref, w_ref, qg_ref, kg_ref,
                  qhi_ref, qlo_ref, brt_ref, kaug_ref, vt512_ref, vt128_ref,
                  kwin_ref, vwt_ref, kvc_ref, u_ref):
    i = pl.program_id(0)
    tm = FRONT_TM
    h_in = _normed_input(x_ref, g_ref)
    proj = lambda a, b: _dot(h_in, w_ref[:, a:b])
    kv = proj(FRONT_OFFS[1], FRONT_OFFS[2])
    piece = lambda j: kv[:, j * KV_WIDTH:(j + 1) * KV_WIDTH]
    lane = lax.broadcasted_iota(jnp.int32, (tm, 128), 1)
    row = lax.broadcasted_iota(jnp.int32, (tm, 128), 0) + i * tm
    lo = lane < HEAD_DIM
    ones_cols = jnp.where((lane == HEAD_DIM) | (lane == HEAD_DIM + 1), 1.0, 0.0)
    onehot = jnp.where(lane == (row // SEL_BLOCK) % SUPER, 1.0, 0.0).astype(BF16)
    kvc_ref[0] = piece(0)
    kvc_ref[1] = piece(1)
    kslc = _group_rms(piece(2), kg_ref[1:2, :])
    kwin = _group_rms(piece(4), kg_ref[2:3, :])
    vslt = piece(3).T
    vwit = piece(5).T
    row_t = lax.broadcasted_iota(jnp.int32, (VT_ROWS - HEAD_DIM, tm), 0)
    ones_rows = jnp.where(row_t == 0, 1.0, 0.0)
    for g in range(N_GROUPS):
        sh = lambda a: a if g == 0 else pltpu.roll(a, HEAD_DIM, 1)
        kaug_ref[g, :, 0:128] = onehot
        kaug_ref[g, :, 128:256] = jnp.where(lo, sh(kslc), ones_cols).astype(BF16)
        kwin_ref[g] = jnp.where(lo, sh(kwin), 0.0).astype(BF16)
        vs = jnp.concatenate([vslt[g * HEAD_DIM:(g + 1) * HEAD_DIM], ones_rows], axis=0).astype(BF16)
        vw = jnp.concatenate([vwit[g * HEAD_DIM:(g + 1) * HEAD_DIM], ones_rows], axis=0).astype(BF16)
        for j in range(tm // KV_CHUNK):
            vt512_ref[g, j] = vs[:, j * KV_CHUNK:(j + 1) * KV_CHUNK]
        for j in range(tm // 128):
            vt128_ref[g, j] = vs[:, j * 128:(j + 1) * 128]
            vwt_ref[g, j] = vw[:, j * 128:(j + 1) * 128]

    qt = proj(FRONT_OFFS[0], FRONT_OFFS[1]).T
    for h in range(N_HEADS):
        blk = qt[h * HEAD_DIM:(h + 1) * HEAD_DIM]
        ms = jnp.mean(blk * blk, axis=0, keepdims=True)
        qn = blk * lax.rsqrt(ms + EPS) * qg_ref[h * HEAD_DIM:(h + 1) * HEAD_DIM, :]
        hi = qn.astype(BF16)
        qhi_ref[h * HEAD_DIM:(h + 1) * HEAD_DIM, :] = hi
        qlo_ref[h * HEAD_DIM:(h + 1) * HEAD_DIM, :] = (qn - hi.astype(F32)).astype(BF16)
    brt_ref[...] = _sigmoid(proj(FRONT_OFFS[2], FRONT_OFFS[3])).T[:32]
    u_ref[...] = proj(FRONT_OFFS[3], FRONT_OFFS[4])


def _front(x2, norm_gain, w_front, qgain_col, kgain_rows):
    s = x2.shape[0]
    tm = FRONT_TM
    g = N_GROUPS
    outs = [
        jax.ShapeDtypeStruct((NSA_WIDTH, s), BF16),
        jax.ShapeDtypeStruct((NSA_WIDTH, s), BF16),
        jax.ShapeDtypeStruct((32, s), F32),
        jax.ShapeDtypeStruct((g, s, 256), BF16),
        jax.ShapeDtypeStruct((g, s // KV_CHUNK, VT_ROWS, KV_CHUNK), BF16),
        jax.ShapeDtypeStruct((g, s // 128, VT_ROWS, 128), BF16),
        jax.ShapeDtypeStruct((g, s, 128), BF16),
        jax.ShapeDtypeStruct((g, s // 128, VT_ROWS, 128), BF16),
        jax.ShapeDtypeStruct((2, s, KV_WIDTH), F32),
        jax.ShapeDtypeStruct((s, LRU_WIDTH), F32),
    ]
    out_specs = [
        pl.BlockSpec((NSA_WIDTH, tm), lambda i: (0, i)),
        pl.BlockSpec((NSA_WIDTH, tm), lambda i: (0, i)),
        pl.BlockSpec((32, tm), lambda i: (0, i)),
        pl.BlockSpec((g, tm, 256), lambda i: (0, i, 0)),
        pl.BlockSpec((g, tm // KV_CHUNK, VT_ROWS, KV_CHUNK), lambda i: (0, i, 0, 0)),
        pl.BlockSpec((g, tm // 128, VT_ROWS, 128), lambda i: (0, i, 0, 0)),
        pl.BlockSpec((g, tm, 128), lambda i: (0, i, 0)),
        pl.BlockSpec((g, tm // 128, VT_ROWS, 128), lambda i: (0, i, 0, 0)),
        pl.BlockSpec((2, tm, KV_WIDTH), lambda i: (0, i, 0)),
        pl.BlockSpec((tm, LRU_WIDTH), lambda i: (i, 0)),
    ]
    return pl.pallas_call(
        _front_kernel,
        grid=(s // tm,),
        in_specs=[pl.BlockSpec((tm, D_MODEL), lambda i: (i, 0)),
                  pl.BlockSpec((1, D_MODEL), lambda i: (0, 0)),
                  pl.BlockSpec(w_front.shape, lambda i: (0, 0)),
                  pl.BlockSpec((NSA_WIDTH, 1), lambda i: (0, 0)),
                  pl.BlockSpec((3, 128), lambda i: (0, 0))],
        out_specs=out_specs,
        out_shape=outs,
        compiler_params=_cparams(1),
        name="front",
    )(x2, norm_gain.reshape(1, D_MODEL), w_front, qgain_col, kgain_rows)


def _compress_kernel(kvc_ref, w1_ref, b1_ref, w2_ref, pe_ref, kg_ref, kc_ref, vct_ref):
    ncp = kc_ref.shape[1]
    half = CMP_STRIDE * HEAD_DIM
    lane = lax.broadcasted_iota(jnp.int32, (ncp, 128), 1)
    lo = lane < HEAD_DIM

    def chunk_rows(j, g):
        cols = []
        for pair in range(CMP_STRIDE // 2):
            even = kvc_ref[j, pl.ds(2 * pair, ncp, stride=CMP_STRIDE), :]
            odd = kvc_ref[j, pl.ds(2 * pair + 1, ncp, stride=CMP_STRIDE), :]
            if g == 0:
                cols.append(jnp.where(lo, even, pltpu.roll(odd, HEAD_DIM, 1)))
            else:
                cols.append(jnp.where(lo, pltpu.roll(even, HEAD_DIM, 1), odd))
        return jnp.concatenate(cols, axis=1)

    def phi(j, g):
        x = chunk_rows(j, g).astype(BF16)
        w1 = w1_ref[j].astype(BF16)
        a = _dot(x, w1[:half])
        b = _dot(x, w1[half:])
        b_next = pltpu.roll(b, ncp - 1, 0)
        pe8 = jnp.broadcast_to(pe_ref[j], (8, 2 * half)).astype(BF16)
        pe_term = _dot(pe8, w1)[0:1]
        hid = a + b_next + pe_term + b1_ref[j]
        act = (hid * _sigmoid(hid)).astype(BF16)
        return _dot(act, w2_ref[j].astype(BF16))

    ones_cols = jnp.where((lane == HEAD_DIM) | (lane == HEAD_DIM + 1), 1.0, 0.0)
    for g in range(N_GROUPS):
        kc = phi(0, g)
        ms = jnp.sum(kc * kc, axis=-1, keepdims=True) / HEAD_DIM
        kc = kc * lax.rsqrt(ms + EPS) * kg_ref[...]
        hi = kc.astype(BF16).astype(F32)
        lo_part = (kc - hi).astype(BF16).astype(F32)
        kc_ref[g, :, 0:128] = (hi + pltpu.roll(lo_part, HEAD_DIM, 1)).astype(BF16)
        kc_ref[g, :, 128:256] = jnp.where(lo, hi, ones_cols).astype(BF16)
        vt = phi(1, g).T
        row_t = lax.broadcasted_iota(jnp.int32, vt.shape, 0)
        vct_ref[g] = jnp.where(row_t == HEAD_DIM, 1.0, vt).astype(BF16)


def _compress(kvc, w1, b1, w2p, pe_flat, kgain_row):
    s = kvc.shape[1]
    g = N_GROUPS
    ncp = s // CMP_STRIDE
    full = lambda a: pl.BlockSpec(a.shape, lambda i: (0,) * a.ndim)
    return pl.pallas_call(
        _compress_kernel,
        grid=(1,),
        in_specs=[pl.BlockSpec(kvc.shape, lambda i: (0, 0, 0), pipeline_mode=pl.Buffered(1)),
                  full(w1), full(b1), full(w2p), full(pe_flat), full(kgain_row)],
        out_specs=[pl.BlockSpec((g, ncp, 256), lambda i: (0, 0, 0)),
                   pl.BlockSpec((g, 128, ncp), lambda i: (0, 0, 0))],
        out_shape=[jax.ShapeDtypeStruct((g, ncp, 256), BF16),
                   jax.ShapeDtypeStruct((g, 128, ncp), BF16)],
        compiler_params=_cparams(1),
        name="compress",
    )(kvc, w1, b1, w2p, pe_flat, kgain_row)


def _heads_to_lanes(q):
    return jnp.concatenate([q[r * HEAD_DIM:(r + 1) * HEAD_DIM] for r in range(GROUP_SIZE)], axis=1)


def _store_heads(out_ref, o):
    for r in range(GROUP_SIZE):
        out_ref[r * HEAD_DIM:(r + 1) * HEAD_DIM, :] = o[0:HEAD_DIM, r * Q_BLOCK:(r + 1) * Q_BLOCK]


def _bias_tile(w_ref, off, n, head0=0):
    return jnp.concatenate([w_ref[head0 + r, pl.ds(off, n), :] for r in range(GROUP_SIZE)], axis=1)


def _cmp_kernel(qhi_ref, qlo_ref, kc_ref, vct_ref, wc_ref, ext_ref, ocmp_ref, neg_ref, s_ref, imp_ref):
    qb = pl.program_id(0)
    ncp = kc_ref.shape[1]
    nsbp = neg_ref.shape[1]
    gr = GROUP_SIZE * HEAD_DIM
    ws = pl.multiple_of(jnp.maximum(8 * qb - 16, 0), 8)
    off = pl.multiple_of(jnp.maximum(16 - 8 * qb, 0), 8)
    lim = ws + CMP_NEAR
    n_cls = min(8, ncp // 128)
    per_cls = ncp // n_cls

    def attend(nrows):
        nblk = nrows // SEL_PER_CMP
        band = nrows - per_cls
        rows = lax.broadcasted_iota(jnp.int32, (per_cls, GQ), 0) + band
        blk = lax.broadcasted_iota(jnp.int32, (nblk, Q_BLOCK), 0)
        lane = lax.broadcasted_iota(jnp.int32, (nblk, Q_BLOCK), 1)
        cur = 2 * qb + jnp.where(lane >= SEL_BLOCK, 1, 0)
        forced = jnp.where(blk == 0, 1.0, jnp.where(blk == cur, 1.0, jnp.where(blk == cur - 1, 1.0, 0.0)))
        carry = []
        for g in range(N_GROUPS):
            qhi = qhi_ref[g * gr:(g + 1) * gr, :]
            qlo = qlo_ref[g * gr:(g + 1) * gr, :]
            cols = []
            for r in range(GROUP_SIZE):
                a = qhi[r * HEAD_DIM:(r + 1) * HEAD_DIM]
                b = qlo[r * HEAD_DIM:(r + 1) * HEAD_DIM]
                cols.append(jnp.concatenate([a, a, b], axis=0))
            qcat = jnp.concatenate([jnp.concatenate(cols, axis=1), ext_ref[g]], axis=0)
            s_g = s_ref.at[g]
            s_g[0:nrows, :] = _dot(kc_ref[g, 0:nrows, :], qcat)
            s_g[pl.ds(ws, CMP_NEAR), :] += _bias_tile(wc_ref, off, CMP_NEAR, g * GROUP_SIZE)
            s_g[band:nrows, :] = jnp.where(rows < lim, s_g[band:nrows, :], NEG)
            s = s_g[0:nrows, :]
            m = jnp.maximum(jnp.max(s, axis=0, keepdims=True), M_INIT)
            p = jnp.exp2(s - m)
            acc = _dot(vct_ref[g, 0:VT_ROWS, 0:nrows], p.astype(BF16))
            l = acc[HEAD_DIM:HEAD_DIM + 1]
            inv_l = jnp.where(l > 0.0, 1.0 / l, 0.0)
            _store_heads(ocmp_ref.at[g * gr:(g + 1) * gr], acc[0:HEAD_DIM] * inv_l)
            imp = None
            for r in range(GROUP_SIZE):
                cs = slice(r * Q_BLOCK, (r + 1) * Q_BLOCK)
                term = p[:, cs] * inv_l[:, cs]
                imp = term if imp is None else imp + term
            imp_g = imp_ref.at[g]
            imp_g[0:nrows, :] = imp
            strided = [imp_g[pl.ds(k, nblk, stride=SEL_PER_CMP), :] for k in range(SEL_PER_CMP)]
            prev_last = jnp.where(blk == 0, 0.0, pltpu.roll(strided[SEL_PER_CMP - 1], 1, 0))
            impb = prev_last + strided[0] + strided[1] + strided[2] + strided[3]
            carry.append(jnp.where(blk < cur - 1, jnp.where(blk > 0, impb, -1.0), -1.0))

        def pick_one(_, cands):
            out = []
            for sc in cands:
                mx = jnp.max(sc, axis=0, keepdims=True)
                hit_blk = jnp.where(sc == mx, jnp.where(mx >= 0.0, blk, nblk), nblk)
                first = jnp.min(hit_blk, axis=0, keepdims=True)
                out.append(jnp.where(blk == first, PICKED, sc))
            return tuple(out)

        if nblk * N_GROUPS <= 2 * SUPER:
            carry = lax.fori_loop(0, N_PICK, pick_one, tuple(carry))
        else:
            carry = [lax.fori_loop(0, N_PICK, pick_one, (c,))[0] for c in carry]
        for g in range(N_GROUPS):
            picked = carry[g] == PICKED
            sel_neg = jnp.where(picked, 0.0, jnp.where(blk <= cur, jnp.where(forced > 0.5, 0.0, NEG), NEG))
            neg_ref[g, 0:nblk, :] = sel_neg.astype(BF16)
            if nblk < nsbp:
                neg_ref[g, nblk:nsbp, :] = jnp.full((nsbp - nblk, Q_BLOCK), NEG, BF16)

    cls = (lim + per_cls - 1) // per_cls
    for k in range(1, n_cls + 1):
        pl.when(cls == k)(functools.partial(attend, k * per_cls))


def _cmp_attention(qhi_t, qlo_t, kc_cat, vct, wc_t, ext_q, nsbp):
    s = qhi_t.shape[1]
    g = N_GROUPS
    ncp = kc_cat.shape[1]
    full = lambda a: pl.BlockSpec(a.shape, lambda qb: (0,) * a.ndim)
    return pl.pallas_call(
        _cmp_kernel,
        grid=(s // Q_BLOCK,),
        in_specs=[pl.BlockSpec((NSA_WIDTH, Q_BLOCK), lambda qb: (0, qb)),
                  pl.BlockSpec((NSA_WIDTH, Q_BLOCK), lambda qb: (0, qb)),
                  full(kc_cat), full(vct), full(wc_t), full(ext_q)],
        out_specs=[pl.BlockSpec((NSA_WIDTH, Q_BLOCK), lambda qb: (0, qb)),
                   pl.BlockSpec((g, nsbp, Q_BLOCK), lambda qb: (0, 0, qb))],
        out_shape=[jax.ShapeDtypeStruct((NSA_WIDTH, s), F32),
                   jax.ShapeDtypeStruct((g, nsbp, s), BF16)],
        scratch_shapes=[pltpu.VMEM((g, ncp, GQ), F32), pltpu.VMEM((g, ncp, Q_BLOCK), F32)],
        compiler_params=_cparams(1),
        name="cmp_select",
    )(qhi_t, qlo_t, kc_cat, vct, wc_t, ext_q)


def _sel_kernel(qhi_ref, neg_ref, kaug_ref, vt512_ref, vt128_ref, ws_ref, ext_ref,
                kwin_ref, vwt_ref, ww_ref, out_ref, owin_ref,
                qaug_ref, m_ref, acc_ref, s0_ref, s1_ref, s2_ref, cm0_ref, cm1_ref, cm2_ref):
    qb = pl.program_id(1)
    nsbp = neg_ref.shape[0]
    nsc = nsbp // SUPER
    qg = _heads_to_lanes(qhi_ref[...])
    qx = jnp.concatenate([qg, ext_ref[...]], axis=0)
    blk = lax.broadcasted_iota(jnp.int32, (SUPER, Q_BLOCK), 0)

    def aug(neg_rows):
        tiled = jnp.concatenate([neg_rows] * GROUP_SIZE, axis=1).astype(BF16)
        return jnp.concatenate([tiled, qx], axis=0)

    for sc in range(nsc):
        neg_rows = neg_ref[sc * SUPER:(sc + 1) * SUPER, :].astype(F32)
        qaug_ref[sc] = aug(jnp.where(blk + sc * SUPER >= 2 * qb - 2, NEG, neg_rows))

    m_ref[...] = jnp.full(m_ref.shape, M_INIT, F32)
    acc_ref[...] = jnp.zeros(acc_ref.shape, F32)

    def update(s, col_max, vt):
        m_old = m_ref[...]
        m_new = jnp.maximum(m_old, col_max)
        p = jnp.exp2(s - m_new).astype(BF16)
        acc_ref[...] = acc_ref[...] * jnp.exp2(m_old - m_new) + _dot(vt, p)
        m_ref[...] = m_new

    n_far = (jnp.maximum(qb - 1, 0) * Q_BLOCK + KV_CHUNK - 1) // KV_CHUNK

    def scores(c, dst_ref, max_ref):
        k = kaug_ref[pl.ds(pl.multiple_of(c * KV_CHUNK, KV_CHUNK), KV_CHUNK), :]
        sc = (c * (KV_CHUNK // SEL_BLOCK)) // SUPER
        s = _dot(k, qaug_ref[sc])
        dst_ref[...] = s
        max_ref[...] = jnp.max(s, axis=0, keepdims=True)

    kb0 = jnp.maximum(qb - 1, 0)
    ws = pl.multiple_of(kb0 * Q_BLOCK, Q_BLOCK)
    off = pl.multiple_of(jnp.where(qb == 0, Q_BLOCK, 0), Q_BLOCK)
    b_lo = 2 * kb0
    sc_lo = pl.multiple_of((b_lo // SUPER) * SUPER, SUPER)
    sc_hi = pl.multiple_of(((b_lo + 3) // SUPER) * SUPER, SUPER)
    neg_lo = neg_ref[pl.ds(sc_lo, SUPER), :].astype(F32)
    neg_hi = neg_ref[pl.ds(sc_hi, SUPER), :].astype(F32)
    near_neg = jnp.where(blk >= SUPER // 2, neg_lo, neg_hi)
    k = kaug_ref[pl.ds(ws, SEL_NEAR), :]
    s_near = _dot(k, aug(near_neg)) + _bias_tile(ws_ref, off, SEL_NEAR)
    vt_near = jnp.concatenate([vt128_ref[kb0], vt128_ref[kb0 + 1]], axis=1)

    s_win = _window_scores(qb, qg, kwin_ref, ww_ref)
    bufs = ((s0_ref, cm0_ref), (s1_ref, cm1_ref), (s2_ref, cm2_ref))
    last = jnp.maximum(n_far - 1, 0)
    scores(0, *bufs[0])
    scores(jnp.minimum(1, last), *bufs[1])
    update(s_near, jnp.max(s_near, axis=0, keepdims=True), vt_near)
    _window_finish(qb, s_win, vwt_ref, owin_ref)

    def far_triple(c):
        for j in range(3):
            scores(jnp.minimum(c + j + 2, last), *bufs[(j + 2) % 3])
            s_ref, cm_ref = bufs[j]
            update(s_ref[...], cm_ref[...], vt512_ref[c + j])

    def far_six(i, carry):
        far_triple(6 * i)
        far_triple(6 * i + 3)
        return carry

    lax.fori_loop(0, n_far // 6, far_six, 0)
    done = (n_far // 6) * 6

    @pl.when(n_far - done >= 3)
    def _():
        far_triple(done)

    done = (n_far // 3) * 3
    for j in range(2):
        @pl.when(n_far - done > j)
        def _():
            s_ref, cm_ref = bufs[j]
            update(s_ref[...], cm_ref[...], vt512_ref[done + j])

    acc = acc_ref[...]
    _store_heads(out_ref, acc[0:HEAD_DIM] * (1.0 / acc[HEAD_DIM:HEAD_DIM + 1]))


def _sel_win_attention(qhi_t, neg_t, kaug, vt512, vt128, ws_t, ext_q, kwin, vwt, ww_t):
    s = qhi_t.shape[1]
    g = N_GROUPS
    nqb = s // Q_BLOCK
    nsbp = neg_t.shape[1]
    gr = GROUP_SIZE * HEAD_DIM
    once = pl.Buffered(1)
    return pl.pallas_call(
        _sel_kernel,
        grid=(g, nqb),
        in_specs=[pl.BlockSpec((gr, Q_BLOCK), lambda gi, qb: (gi, qb)),
                  pl.BlockSpec((None, nsbp, Q_BLOCK), lambda gi, qb: (gi, 0, qb)),
                  pl.BlockSpec((None, s, 256), lambda gi, qb: (gi, 0, 0), pipeline_mode=once),
                  pl.BlockSpec((None, s // KV_CHUNK, VT_ROWS, KV_CHUNK), lambda gi, qb: (gi, 0, 0, 0),
                               pipeline_mode=once),
                  pl.BlockSpec((None, s // 128, VT_ROWS, 128), lambda gi, qb: (gi, 0, 0, 0),
                               pipeline_mode=once),
                  pl.BlockSpec((GROUP_SIZE,) + ws_t.shape[1:], lambda gi, qb: (gi, 0, 0)),
                  pl.BlockSpec((None, HEAD_DIM, GQ), lambda gi, qb: (gi, 0, 0)),
                  pl.BlockSpec((None, s, 128), lambda gi, qb: (gi, 0, 0), pipeline_mode=once),
                  pl.BlockSpec((None, s // 128, VT_ROWS, 128), lambda gi, qb: (gi, 0, 0, 0),
                               pipeline_mode=once),
                  pl.BlockSpec((GROUP_SIZE,) + ww_t.shape[1:], lambda gi, qb: (gi, 0, 0))],
        out_specs=[pl.BlockSpec((gr, Q_BLOCK), lambda gi, qb: (gi, qb)),
                   pl.BlockSpec((gr, Q_BLOCK), lambda gi, qb: (gi, qb))],
        out_shape=[jax.ShapeDtypeStruct((NSA_WIDTH, s), F32),
                   jax.ShapeDtypeStruct((NSA_WIDTH, s), F32)],
        scratch_shapes=[pltpu.VMEM((nsbp // SUPER, 256, GQ), BF16),
                        pltpu.VMEM((1, GQ), F32),
                        pltpu.VMEM((VT_ROWS, GQ), F32),
                        pltpu.VMEM((KV_CHUNK, GQ), F32),
                        pltpu.VMEM((KV_CHUNK, GQ), F32),
                        pltpu.VMEM((KV_CHUNK, GQ), F32),
                        pltpu.VMEM((1, GQ), F32),
                        pltpu.VMEM((1, GQ), F32),
                        pltpu.VMEM((1, GQ), F32)],
        compiler_params=_cparams(2),
        name="sel_win_attention",
    )(qhi_t, neg_t, kaug, vt512, vt128, ws_t, ext_q, kwin, vwt, ww_t)


def _window_scores(qb, qg, kwin_ref, ww_ref):
    kb0 = jnp.maximum(qb - WINDOW // Q_BLOCK, 0)
    ws = pl.multiple_of(kb0 * Q_BLOCK, Q_BLOCK)
    off = pl.multiple_of(jnp.maximum(WINDOW - qb * Q_BLOCK, 0), Q_BLOCK)
    qx = jnp.concatenate([qg, jnp.zeros_like(qg)], axis=0)
    return _dot(kwin_ref[pl.ds(ws, WIN_KEYS), :], qx) + _bias_tile(ww_ref, off, WIN_KEYS)


def _window_finish(qb, s, vwt_ref, out_ref):
    kb0 = jnp.maximum(qb - WINDOW // Q_BLOCK, 0)
    m = jnp.max(s, axis=0, keepdims=True)
    p = jnp.exp2(s - m).astype(BF16)
    vt = jnp.concatenate([vwt_ref[kb0 + j] for j in range(WIN_KEYS // Q_BLOCK)], axis=1)
    acc = _dot(vt, p)
    _store_heads(out_ref, acc[0:HEAD_DIM] * (1.0 / acc[HEAD_DIM:HEAD_DIM + 1]))


def _lru_rows(u_ref, cw_ref, cb_ref, wa_ref, ba_ref, wx_ref, bx_ref, lam_ref, h_ref,
              tail_ref, hprev_ref, uc_ref):
    tm = u_ref.shape[0]

    @pl.when(pl.program_id(0) == 0)
    def _():
        tail_ref[...] = jnp.zeros(tail_ref.shape, F32)
        hprev_ref[...] = jnp.zeros(hprev_ref.shape, F32)

    u = u_ref[...]
    tail = tail_ref[...]
    row8 = lax.broadcasted_iota(jnp.int32, (8, LRU_WIDTH), 0)
    uc = cb_ref[...] + u * cw_ref[CONV_WIDTH - 1:CONV_WIDTH, :]
    uc_head = uc[0:8]
    for j in range(1, CONV_WIDTH):
        w_j = cw_ref[CONV_WIDTH - 1 - j:CONV_WIDTH - j, :]
        sh = pltpu.roll(u, j, 0)
        uc = uc + sh * w_j
        uc_head = uc_head + jnp.where(row8 < j, pltpu.roll(tail, j, 0), sh[0:8]) * w_j
    tail_ref[...] = u[tm - 8:tm]
    uc_ref[...] = uc
    uc_ref[0:8, :] = uc_head
    uc = uc_ref[...]

    ucb = uc.astype(BF16)
    r = _sigmoid(_dot(ucb, wa_ref[...]) + ba_ref[...])
    ig = _sigmoid(_dot(ucb, wx_ref[...]) + bx_ref[...])
    nl = -lam_ref[...]
    softplus = jnp.maximum(nl, 0.0) + jnp.log(1.0 + jnp.exp(-jnp.abs(nl)))
    a = jnp.exp((-LRU_C * r) * softplus)
    b = jnp.sqrt(1.0 - a * a) * (ig * uc)

    a = a.reshape(tm // 8, 8, LRU_WIDTH)
    b = b.reshape(tm // 8, 8, LRU_WIDTH)
    sub = lax.broadcasted_iota(jnp.int32, a.shape, 1)
    for step in (1, 2, 4):
        a_s = pltpu.roll(a, step, 1)
        b_s = pltpu.roll(b, step, 1)
        ok = sub >= step
        b = jnp.where(ok, a * b_s + b, b)
        a = jnp.where(ok, a * a_s, a)
    carry = hprev_ref[...]
    for j in range(tm // 8):
        hj = a[j] * carry + b[j]
        h_ref[8 * j:8 * j + 8, :] = hj
        carry = hj[7:8]
    hprev_ref[...] = carry


OUT_TM = 512


def _out_kernel(x_ref, g_ref, oc_ref, os_ref, ow_ref, brt_ref, u_ref,
                cw_ref, cb_ref, wa_ref, ba_ref, wx_ref, bx_ref, lam_ref,
                wg_ref, wpa_ref, wpb_ref, wo_ref, y_ref,
                h_ref, tail_ref, hprev_ref, uc_ref):
    _lru_rows(u_ref, cw_ref, cb_ref, wa_ref, ba_ref, wx_ref, bx_ref, lam_ref, h_ref,
              tail_ref, hprev_ref, uc_ref)
    h_in = _normed_input(x_ref, g_ref)
    gate = lambda j: _dot(h_in, wg_ref[:, GATE_OFFS[j]:GATE_OFFS[j + 1]])
    brt = brt_ref[...]
    parts = []
    for h in range(N_HEADS):
        rows = slice(h * HEAD_DIM, (h + 1) * HEAD_DIM)
        acc = None
        for b, o_ref in enumerate((oc_ref, os_ref, ow_ref)):
            term = brt[b * N_HEADS + h:b * N_HEADS + h + 1, :] * o_ref[rows, :]
            acc = term if acc is None else acc + term
        parts.append(acc)
    gn = gate(0)
    ya = jnp.concatenate(parts, axis=0).T * (gn * _sigmoid(gn))
    y_a = _dot(ya.astype(BF16), wpa_ref[...])
    gl = gate(1)
    y_b = _dot((h_ref[...] * (gl * _sigmoid(gl))).astype(BF16), wpb_ref[...])
    mg = gate(2)
    m = _sigmoid(mg[:, :D_MODEL]) * y_a + _sigmoid(mg[:, D_MODEL:]) * y_b
    y_ref[...] = x_ref[...] + _dot(m.astype(BF16), wo_ref[...])


def _output(x2, norm_gain, oc_t, os_t, ow_t, br_t, u_lru, lru_params, w_gate, wpa, wpb, wo):
    s = x2.shape[0]
    tm = OUT_TM
    rows = lambda n: pl.BlockSpec((tm, n), lambda i: (i, 0))
    cols = lambda n: pl.BlockSpec((n, tm), lambda i: (0, i))
    full = lambda a: pl.BlockSpec(a.shape, lambda i: (0, 0))
    return pl.pallas_call(
        _out_kernel,
        grid=(s // tm,),
        in_specs=[rows(D_MODEL), pl.BlockSpec((1, D_MODEL), lambda i: (0, 0)),
                  cols(NSA_WIDTH), cols(NSA_WIDTH), cols(NSA_WIDTH), cols(32), rows(LRU_WIDTH)]
        + [full(p) for p in lru_params]
        + [full(w_gate), full(wpa), full(wpb), full(wo)],
        out_specs=rows(D_MODEL),
        out_shape=jax.ShapeDtypeStruct((s, D_MODEL), F32),
        scratch_shapes=[pltpu.VMEM((tm, LRU_WIDTH), F32),
                        pltpu.VMEM((8, LRU_WIDTH), F32),
                        pltpu.VMEM((1, LRU_WIDTH), F32),
                        pltpu.VMEM((tm, LRU_WIDTH), F32)],
        compiler_params=_cparams(1),
        name="output",
    )(x2, norm_gain.reshape(1, D_MODEL), oc_t, os_t, ow_t, br_t, u_lru, *lru_params,
      w_gate, wpa, wpb, wo)


def _t5_bucket_table(n_dist):
    n = np.arange(n_dist)
    max_exact = N_BUCKETS // 2
    nf = np.maximum(n, 1).astype(np.float32)
    large = max_exact + (np.log(nf / np.float32(max_exact)) / np.float32(math.log(MAX_DISTANCE / max_exact))
                         * np.float32(N_BUCKETS - max_exact)).astype(np.int32)
    return np.where(n < max_exact, n, np.minimum(large, N_BUCKETS - 1))


def _bias_tiles_kernel(relb_ref, bw_ref, bs_ref, bc_ref, ww_ref, ws_ref, wc_ref):
    h = pl.program_id(0)
    far = relb_ref[N_BUCKETS - 1, h]
    for bk_ref, out_ref, rel in ((bw_ref, ww_ref, False), (bs_ref, ws_ref, True), (bc_ref, wc_ref, True)):
        bk = bk_ref[...]
        acc = jnp.full(bk.shape, NEG, F32)
        for b in range(N_BUCKETS):
            val = relb_ref[b, h] - far if rel else relb_ref[b, h]
            acc = jnp.where(bk == b, val * LOG2E, acc)
        out_ref[...] = acc


def _bias_tables(rel_bias):
    bucket = _t5_bucket_table(2 * WINDOW)
    i = np.arange(Q_BLOCK)[None, :]

    def index_tile(dist, valid):
        return jnp.asarray(np.where(valid, bucket[np.clip(dist, 0, bucket.size - 1)], -1).astype(np.int32))

    u = np.arange(WINDOW + WIN_KEYS)[:, None]
    d_win = i - u + WINDOW
    bw = index_tile(d_win, (d_win >= 0) & (d_win < WINDOW))
    u = np.arange(SEL_NEAR + Q_BLOCK)[:, None]
    d_sel = i - u + Q_BLOCK
    bs = index_tile(d_sel, d_sel >= 0)
    u = np.arange(CMP_NEAR + 16)[:, None]
    d_cmp = i - CMP_STRIDE * (u - 16) - (CMP_BLOCK - 1)
    bc = index_tile(d_cmp, d_cmp >= 0)
    full = lambda a: pl.BlockSpec(a.shape, lambda h: (0, 0))
    per_head = lambda a: pl.BlockSpec((None,) + a.shape, lambda h: (h, 0, 0))
    ww, ws, wc = pl.pallas_call(
        _bias_tiles_kernel,
        grid=(N_HEADS,),
        in_specs=[pl.BlockSpec(memory_space=pltpu.SMEM), full(bw), full(bs), full(bc)],
        out_specs=[per_head(bw), per_head(bs), per_head(bc)],
        out_shape=[jax.ShapeDtypeStruct((N_HEADS,) + a.shape, F32) for a in (bw, bs, bc)],
        compiler_params=_cparams(1),
        name="bias_tiles",
    )(rel_bias, bw, bs, bc)
    b_far = rel_bias[N_BUCKETS - 1] * LOG2E
    hi = b_far.astype(BF16)
    lo = (b_far - hi.astype(F32)).astype(BF16)
    ext = jnp.zeros((N_HEADS, HEAD_DIM, Q_BLOCK), BF16)
    ext = ext.at[:, 0, :].set(hi[:, None]).at[:, 1, :].set(lo[:, None])
    ext = ext.reshape(N_GROUPS, GROUP_SIZE, HEAD_DIM, Q_BLOCK).transpose(0, 2, 1, 3)
    return ww, ws, wc, ext.reshape(N_GROUPS, HEAD_DIM, GQ)


def _block_diag(w):
    n, d, e = w.shape
    eye = jnp.eye(n, dtype=w.dtype)
    return (eye[:, None, :, None] * w[:, :, None, :]).reshape(n * d, n * e)


def kernel(x, norm_gain, w_in, q_norm_gain, k_norm_gain, cmp_pe, cmp_w1, cmp_b1, cmp_w2, rel_bias,
           conv_w, conv_b, lru_wa, lru_ba, lru_wx, lru_bx, lru_lambda, w_proj_a, w_proj_b, w_out):
    bsz, s, _ = x.shape
    assert bsz == 1 and s % 1024 == 0 and s >= 1024
    x2 = x.reshape(s, D_MODEL)
    nsb = s // SEL_BLOCK
    nsbp = -(-nsb // SUPER) * SUPER

    o = IN_OFFS
    w16 = w_in.astype(BF16)
    pad = jnp.zeros((D_MODEL, BR_PAD - 3 * N_HEADS), BF16)
    w_front = jnp.concatenate([w16[:, o[0]:o[2]], w16[:, o[3]:o[4]], pad, w16[:, o[4]:o[5]]], axis=1)
    w_gate = jnp.concatenate([w16[:, o[2]:o[3]], w16[:, o[5]:o[7]]], axis=1)

    scale = HEAD_DIM ** -0.5 * LOG2E
    qgain_col = jnp.tile(q_norm_gain * scale, N_HEADS).reshape(NSA_WIDTH, 1)
    kgain_rows = jnp.tile(k_norm_gain, (1, N_GROUPS))
    (qhi_t, qlo_t, br_t, kaug, vt512, vt128, kwin, vwt, kvc, u_lru) = _front(
        x2, norm_gain, w_front, qgain_col, kgain_rows)

    w2p = jnp.pad(cmp_w2, ((0, 0), (0, 0), (0, 128 - HEAD_DIM)))
    kc_gain = jnp.pad(k_norm_gain[0], (0, 128 - HEAD_DIM)).reshape(1, 128)
    kc_cat, vct = _compress(kvc, cmp_w1, cmp_b1.reshape(2, 1, CMP_HIDDEN), w2p,
                            cmp_pe.reshape(2, 1, CMP_BLOCK * HEAD_DIM), kc_gain)

    ww_t, ws_t, wc_t, ext_q = _bias_tables(rel_bias)
    oc_t, neg_t = _cmp_attention(qhi_t, qlo_t, kc_cat, vct, wc_t, ext_q, nsbp)
    os_t, ow_t = _sel_win_attention(qhi_t, neg_t, kaug, vt512, vt128, ws_t, ext_q, kwin, vwt, ww_t)

    row = lambda v: v.reshape(1, LRU_WIDTH)
    lru_params = (conv_w, row(conv_b), _block_diag(lru_wa).astype(BF16), row(lru_ba),
                  _block_diag(lru_wx).astype(BF16), row(lru_bx), row(lru_lambda))
    y = _output(x2, norm_gain, oc_t, os_t, ow_t, br_t, u_lru, lru_params, w_gate,
                w_proj_a.astype(BF16), w_proj_b.astype(BF16), w_out.astype(BF16))
    return y.reshape(bsz, s, D_MODEL)
```

```python
import functools
import math

import numpy as np
import jax
import jax.numpy as jnp
from jax import lax
from jax.experimental import pallas as pl
from jax.experimental.pallas import tpu as pltpu

F32 = jnp.float32
BF16 = jnp.bfloat16

D_MODEL = 1024
N_HEADS = 8
N_GROUPS = 2
GROUP_SIZE = N_HEADS // N_GROUPS
HEAD_DIM = 64
NSA_WIDTH = N_HEADS * HEAD_DIM
KV_WIDTH = N_GROUPS * HEAD_DIM
CMP_STRIDE = 16
CMP_BLOCK = 32
CMP_HIDDEN = 256
SEL_BLOCK = 64
SEL_PER_CMP = SEL_BLOCK // CMP_STRIDE
N_SELECT = 16
WINDOW = 512
Q_BLOCK = 128
LRU_WIDTH = 512
LRU_BLOCKS = 8
CONV_WIDTH = 4
LRU_C = 8.0
N_BUCKETS = 32
MAX_DISTANCE = 128
EPS = 1e-6
NEG = -1e30
M_INIT = -5e29
LOG2E = 1.4426950408889634

GQ = GROUP_SIZE * Q_BLOCK
SUPER = 128
KV_CHUNK = 512
SEL_NEAR = 2 * Q_BLOCK
WIN_KEYS = WINDOW + Q_BLOCK
CMP_NEAR = 24
N_PICK = N_SELECT - 3
PICKED = -2.0
VT_ROWS = 80

IN_OFFS = tuple(int(v) for v in np.cumsum(
    (0, NSA_WIDTH, 6 * KV_WIDTH, NSA_WIDTH, 3 * N_HEADS, LRU_WIDTH, LRU_WIDTH, 2 * D_MODEL)))
BR_PAD = 128
FRONT_OFFS = tuple(int(v) for v in np.cumsum((0, NSA_WIDTH, 6 * KV_WIDTH, BR_PAD, LRU_WIDTH)))
GATE_OFFS = tuple(int(v) for v in np.cumsum((0, NSA_WIDTH, LRU_WIDTH, 2 * D_MODEL)))

VMEM_LIMIT = 56 * 1024 * 1024


def _cparams(n_axes):
    return pltpu.CompilerParams(dimension_semantics=("arbitrary",) * n_axes,
                                vmem_limit_bytes=VMEM_LIMIT)


def _dot(a, b):
    return jnp.dot(a, b, preferred_element_type=F32)


def _sigmoid(x):
    return 0.5 * jnp.tanh(0.5 * x) + 0.5


FRONT_TM = 512


def _normed_input(x_ref, g_ref):
    x = x_ref[...]
    ms = jnp.mean(x * x, axis=-1, keepdims=True)
    return (x * lax.rsqrt(ms + EPS) * g_ref[...]).astype(BF16)


def _group_rms(k, gain_row):
    sq = k * k
    lane = lax.broadcasted_iota(jnp.int32, k.shape, 1)
    lo = lane < HEAD_DIM
    s0 = jnp.sum(jnp.where(lo, sq, 0.0), axis=-1, keepdims=True)
    s1 = jnp.sum(jnp.where(lo, 0.0, sq), axis=-1, keepdims=True)
    inv = jnp.where(lo, lax.rsqrt(s0 / HEAD_DIM + EPS), lax.rsqrt(s1 / HEAD_DIM + EPS))
    return k * inv * gain_row


def _front_kernel(x_ref, g_ref, w_ref, qg_ref, kg_ref,
                  qhi_ref, qlo_ref, brt_ref, kaug_ref, vt512_ref, vt128_ref,
                  kwin_ref, vwt_ref, kvc_ref, u_ref):
    i = pl.program_id(0)
    tm = FRONT_TM
    h_in = _normed_input(x_ref, g_ref)
    proj = lambda a, b: _dot(h_in, w_ref[:, a:b])
    kv = proj(FRONT_OFFS[1], FRONT_OFFS[2])
    piece = lambda j: kv[:, j * KV_WIDTH:(j + 1) * KV_WIDTH]
    lane = lax.broadcasted_iota(jnp.int32, (tm, 128), 1)
    row = lax.broadcasted_iota(jnp.int32, (tm, 128), 0) + i * tm
    lo = lane < HEAD_DIM
    ones_cols = jnp.where((lane == HEAD_DIM) | (lane == HEAD_DIM + 1), 1.0, 0.0)
    onehot = jnp.where(lane == (row // SEL_BLOCK) % SUPER, 1.0, 0.0).astype(BF16)
    kvc_ref[0] = piece(0)
    kvc_ref[1] = piece(1)
    kslc = _group_rms(piece(2), kg_ref[1:2, :])
    kwin = _group_rms(piece(4), kg_ref[2:3, :])
    vslt = piece(3).T
    vwit = piece(5).T
    row_t = lax.broadcasted_iota(jnp.int32, (VT_ROWS - HEAD_DIM, tm), 0)
    ones_rows = jnp.where(row_t == 0, 1.0, 0.0)
    for g in range(N_GROUPS):
        sh = lambda a: a if g == 0 else pltpu.roll(a, HEAD_DIM, 1)
        kaug_ref[g, :, 0:128] = onehot
        kaug_ref[g, :, 128:256] = jnp.where(lo, sh(kslc), ones_cols).astype(BF16)
        kwin_ref[g] = jnp.where(lo, sh(kwin), 0.0).astype(BF16)
        vs = jnp.concatenate([vslt[g * HEAD_DIM:(g + 1) * HEAD_DIM], ones_rows], axis=0).astype(BF16)
        vw = jnp.concatenate([vwit[g * HEAD_DIM:(g + 1) * HEAD_DIM], ones_rows], axis=0).astype(BF16)
        for j in range(tm // KV_CHUNK):
            vt512_ref[g, j] = vs[:, j * KV_CHUNK:(j + 1) * KV_CHUNK]
        for j in range(tm // 128):
            vt128_ref[g, j] = vs[:, j * 128:(j + 1) * 128]
            vwt_ref[g, j] = vw[:, j * 128:(j + 1) * 128]

    qt = proj(FRONT_OFFS[0], FRONT_OFFS[1]).T
    for h in range(N_HEADS):
        blk = qt[h * HEAD_DIM:(h + 1) * HEAD_DIM]
        ms = jnp.mean(blk * blk, axis=0, keepdims=True)
        qn = blk * lax.rsqrt(ms + EPS) * qg_ref[h * HEAD_DIM:(h + 1) * HEAD_DIM, :]
        hi = qn.astype(BF16)
        qhi_ref[h * HEAD_DIM:(h + 1) * HEAD_DIM, :] = hi
        qlo_ref[h * HEAD_DIM:(h + 1) * HEAD_DIM, :] = (qn - hi.astype(F32)).astype(BF16)
    brt_ref[...] = _sigmoid(proj(FRONT_OFFS[2], FRONT_OFFS[3])).T[:32]
    u_ref[...] = proj(FRONT_OFFS[3], FRONT_OFFS[4])


def _front(x2, norm_gain, w_front, qgain_col, kgain_rows):
    s = x2.shape[0]
    tm = FRONT_TM
    g = N_GROUPS
    outs = [
        jax.ShapeDtypeStruct((NSA_WIDTH, s), BF16),
        jax.ShapeDtypeStruct((NSA_WIDTH, s), BF16),
        jax.ShapeDtypeStruct((32, s), F32),
        jax.ShapeDtypeStruct((g, s, 256), BF16),
        jax.ShapeDtypeStruct((g, s // KV_CHUNK, VT_ROWS, KV_CHUNK), BF16),
        jax.ShapeDtypeStruct((g, s // 128, VT_ROWS, 128), BF16),
        jax.ShapeDtypeStruct((g, s, 128), BF16),
        jax.ShapeDtypeStruct((g, s // 128, VT_ROWS, 128), BF16),
        jax.ShapeDtypeStruct((2, s, KV_WIDTH), F32),
        jax.ShapeDtypeStruct((s, LRU_WIDTH), F32),
    ]
    out_specs = [
        pl.BlockSpec((NSA_WIDTH, tm), lambda i: (0, i)),
        pl.BlockSpec((NSA_WIDTH, tm), lambda i: (0, i)),
        pl.BlockSpec((32, tm), lambda i: (0, i)),
        pl.BlockSpec((g, tm, 256), lambda i: (0, i, 0)),
        pl.BlockSpec((g, tm // KV_CHUNK, VT_ROWS, KV_CHUNK), lambda i: (0, i, 0, 0)),
        pl.BlockSpec((g, tm // 128, VT_ROWS, 128), lambda i: (0, i, 0, 0)),
        pl.BlockSpec((g, tm, 128), lambda i: (0, i, 0)),
        pl.BlockSpec((g, tm // 128, VT_ROWS, 128), lambda i: (0, i, 0, 0)),
        pl.BlockSpec((2, tm, KV_WIDTH), lambda i: (0, i, 0)),
        pl.BlockSpec((tm, LRU_WIDTH), lambda i: (i, 0)),
    ]
    return pl.pallas_call(
        _front_kernel,
        grid=(s // tm,),
        in_specs=[pl.BlockSpec((tm, D_MODEL), lambda i: (i, 0)),
                  pl.BlockSpec((1, D_MODEL), lambda i: (0, 0)),
                  pl.BlockSpec(w_front.shape, lambda i: (0, 0)),
                  pl.BlockSpec((NSA_WIDTH, 1), lambda i: (0, 0)),
                  pl.BlockSpec((3, 128), lambda i: (0, 0))],
        out_specs=out_specs,
        out_shape=outs,
        compiler_params=_cparams(1),
        name="front",
    )(x2, norm_gain.reshape(1, D_MODEL), w_front, qgain_col, kgain_rows)


def _compress_kernel(kvc_ref, w1_ref, b1_ref, w2_ref, pe_ref, kg_ref, kc_ref, vct_ref):
    ncp = kc_ref.shape[1]
    half = CMP_STRIDE * HEAD_DIM
    lane = lax.broadcasted_iota(jnp.int32, (ncp, 128), 1)
    lo = lane < HEAD_DIM

    def chunk_rows(j, g):
        cols = []
        for pair in range(CMP_STRIDE // 2):
            even = kvc_ref[j, pl.ds(2 * pair, ncp, stride=CMP_STRIDE), :]
            odd = kvc_ref[j, pl.ds(2 * pair + 1, ncp, stride=CMP_STRIDE), :]
            if g == 0:
                cols.append(jnp.where(lo, even, pltpu.roll(odd, HEAD_DIM, 1)))
            else:
                cols.append(jnp.where(lo, pltpu.roll(even, HEAD_DIM, 1), odd))
        return jnp.concatenate(cols, axis=1)

    def phi(j, g):
        x = chunk_rows(j, g).astype(BF16)
        w1 = w1_ref[j].astype(BF16)
        a = _dot(x, w1[:half])
        b = _dot(x, w1[half:])
        b_next = pltpu.roll(b, ncp - 1, 0)
        pe8 = jnp.broadcast_to(pe_ref[j], (8, 2 * half)).astype(BF16)
        pe_term = _dot(pe8, w1)[0:1]
        hid = a + b_next + pe_term + b1_ref[j]
        act = (hid * _sigmoid(hid)).astype(BF16)
        return _dot(act, w2_ref[j].astype(BF16))

    ones_cols = jnp.where((lane == HEAD_DIM) | (lane == HEAD_DIM + 1), 1.0, 0.0)
    for g in range(N_GROUPS):
        kc = phi(0, g)
        ms = jnp.sum(kc * kc, axis=-1, keepdims=True) / HEAD_DIM
        kc = kc * lax.rsqrt(ms + EPS) * kg_ref[...]
        hi = kc.astype(BF16).astype(F32)
        lo_part = (kc - hi).astype(BF16).astype(F32)
        kc_ref[g, :, 0:128] = (hi + pltpu.roll(lo_part, HEAD_DIM, 1)).astype(BF16)
        kc_ref[g, :, 128:256] = jnp.where(lo, hi, ones_cols).astype(BF16)
        vt = phi(1, g).T
        row_t = lax.broadcasted_iota(jnp.int32, vt.shape, 0)
        vct_ref[g] = jnp.where(row_t == HEAD_DIM, 1.0, vt).astype(BF16)


def _compress(kvc, w1, b1, w2p, pe_flat, kgain_row):
    s = kvc.shape[1]
    g = N_GROUPS
    ncp = s // CMP_STRIDE
    full = lambda a: pl.BlockSpec(a.shape, lambda i: (0,) * a.ndim)
    return pl.pallas_call(
        _compress_kernel,
        grid=(1,),
        in_specs=[pl.BlockSpec(kvc.shape, lambda i: (0, 0, 0), pipeline_mode=pl.Buffered(1)),
                  full(w1), full(b1), full(w2p), full(pe_flat), full(kgain_row)],
        out_specs=[pl.BlockSpec((g, ncp, 256), lambda i: (0, 0, 0)),
                   pl.BlockSpec((g, 128, ncp), lambda i: (0, 0, 0))],
        out_shape=[jax.ShapeDtypeStruct((g, ncp, 256), BF16),
                   jax.ShapeDtypeStruct((g, 128, ncp), BF16)],
        compiler_params=_cparams(1),
        name="compress",
    )(kvc, w1, b1, w2p, pe_flat, kgain_row)


def _heads_to_lanes(q):
    return jnp.concatenate([q[r * HEAD_DIM:(r + 1) * HEAD_DIM] for r in range(GROUP_SIZE)], axis=1)


def _store_heads(out_ref, o):
    for r in range(GROUP_SIZE):
        out_ref[r * HEAD_DIM:(r + 1) * HEAD_DIM, :] = o[0:HEAD_DIM, r * Q_BLOCK:(r + 1) * Q_BLOCK]


def _bias_tile(w_ref, off, n, head0=0):
    return jnp.concatenate([w_ref[head0 + r, pl.ds(off, n), :] for r in range(GROUP_SIZE)], axis=1)


def _cmp_kernel(qhi_ref, qlo_ref, kc_ref, vct_ref, wc_ref, ext_ref, ocmp_ref, neg_ref,
                s_ref, imp_ref, cand_ref):
    qb = pl.program_id(0)
    ncp = kc_ref.shape[1]
    nsbp = neg_ref.shape[1]
    gr = GROUP_SIZE * HEAD_DIM
    ws = pl.multiple_of(jnp.maximum(8 * qb - 16, 0), 8)
    off = pl.multiple_of(jnp.maximum(16 - 8 * qb, 0), 8)
    lim = ws + CMP_NEAR
    n_cls = min(8, ncp // 128)
    per_cls = ncp // n_cls

    def attend(nrows):
        nblk = nrows // SEL_PER_CMP
        band = nrows - per_cls
        rows = lax.broadcasted_iota(jnp.int32, (per_cls, GQ), 0) + band
        blk = lax.broadcasted_iota(jnp.int32, (nblk, Q_BLOCK), 0)
        lane = lax.broadcasted_iota(jnp.int32, (nblk, Q_BLOCK), 1)
        cur = 2 * qb + jnp.where(lane >= SEL_BLOCK, 1, 0)
        forced = jnp.where(blk == 0, 1.0, jnp.where(blk == cur, 1.0, jnp.where(blk == cur - 1, 1.0, 0.0)))
        carry = []
        for g in range(N_GROUPS):
            qhi = qhi_ref[g * gr:(g + 1) * gr, :]
            qlo = qlo_ref[g * gr:(g + 1) * gr, :]
            cols = []
            for r in range(GROUP_SIZE):
                a = qhi[r * HEAD_DIM:(r + 1) * HEAD_DIM]
                b = qlo[r * HEAD_DIM:(r + 1) * HEAD_DIM]
                cols.append(jnp.concatenate([a, a, b], axis=0))
            qcat = jnp.concatenate([jnp.concatenate(cols, axis=1), ext_ref[g]], axis=0)
            s_g = s_ref.at[g]
            s_g[0:nrows, :] = _dot(kc_ref[g, 0:nrows, :], qcat)
            s_g[pl.ds(ws, CMP_NEAR), :] += _bias_tile(wc_ref, off, CMP_NEAR, g * GROUP_SIZE)
            s_g[band:nrows, :] = jnp.where(rows < lim, s_g[band:nrows, :], NEG)
            s = s_g[0:nrows, :]
            m = jnp.maximum(jnp.max(s, axis=0, keepdims=True), M_INIT)
            p = jnp.exp2(s - m)
            acc = _dot(vct_ref[g, 0:VT_ROWS, 0:nrows], p.astype(BF16))
            l = acc[HEAD_DIM:HEAD_DIM + 1]
            inv_l = jnp.where(l > 0.0, 1.0 / l, 0.0)
            _store_heads(ocmp_ref.at[g * gr:(g + 1) * gr], acc[0:HEAD_DIM] * inv_l)
            imp = None
            for r in range(GROUP_SIZE):
                cs = slice(r * Q_BLOCK, (r + 1) * Q_BLOCK)
                term = p[:, cs] * inv_l[:, cs]
                imp = term if imp is None else imp + term
            imp_g = imp_ref.at[g]
            imp_g[0:nrows, :] = imp
            strided = [imp_g[pl.ds(k, nblk, stride=SEL_PER_CMP), :] for k in range(SEL_PER_CMP)]
            prev_last = jnp.where(blk == 0, 0.0, pltpu.roll(strided[SEL_PER_CMP - 1], 1, 0))
            impb = prev_last + strided[0] + strided[1] + strided[2] + strided[3]
            cand = jnp.where(blk < cur - 1, jnp.where(blk > 0, impb, -1.0), -1.0)
            cand_ref[g, 0:nblk, :] = cand
            carry.append(cand)

        def pick_one(_, cands):
            out = []
            for sc in cands:
                mx = jnp.max(sc, axis=0, keepdims=True)
                hit_blk = jnp.where(sc == mx, jnp.where(mx >= 0.0, blk, nblk), nblk)
                first = jnp.min(hit_blk, axis=0, keepdims=True)
                out.append(jnp.where(blk == first, PICKED, sc))
            return tuple(out)

        def pick_maxima(_, cands):
            out = []
            for sc in cands:
                mx = jnp.max(sc, axis=0, keepdims=True)
                out.append(jnp.where(sc == jnp.where(mx >= 0.0, mx, jnp.inf), PICKED, sc))
            return tuple(out)

        def rounds(pick, cands):
            if nblk * N_GROUPS <= 2 * SUPER:
                return lax.fori_loop(0, N_PICK, pick, tuple(cands))
            return [lax.fori_loop(0, N_PICK, pick, (c,))[0] for c in cands]

        def write_selection(cands):
            for g in range(N_GROUPS):
                picked = cands[g] == PICKED
                sel_neg = jnp.where(picked, 0.0,
                                    jnp.where(blk <= cur, jnp.where(forced > 0.5, 0.0, NEG), NEG))
                neg_ref[g, 0:nblk, :] = sel_neg.astype(BF16)
                if nblk < nsbp:
                    neg_ref[g, nblk:nsbp, :] = jnp.full((nsbp - nblk, Q_BLOCK), NEG, BF16)

        fast = rounds(pick_maxima, carry)
        expected = jnp.clip(cur[0:1, :] - 2, 0, N_PICK).astype(F32)
        excess = None
        for g in range(N_GROUPS):
            count = jnp.sum(jnp.where(fast[g] == PICKED, 1.0, 0.0), axis=0, keepdims=True)
            excess = count - expected if excess is None else jnp.maximum(excess, count - expected)
        tied = jnp.max(excess) > 0.5

        @pl.when(jnp.logical_not(tied))
        def _():
            write_selection(fast)

        @pl.when(tied)
        def _():
            write_selection(rounds(pick_one, [cand_ref[g, 0:nblk, :] for g in range(N_GROUPS)]))

    cls = (lim + per_cls - 1) // per_cls
    for k in range(1, n_cls + 1):
        pl.when(cls == k)(functools.partial(attend, k * per_cls))


def _cmp_attention(qhi_t, qlo_t, kc_cat, vct, wc_t, ext_q, nsbp):
    s = qhi_t.shape[1]
    g = N_GROUPS
    ncp = kc_cat.shape[1]
    full = lambda a: pl.BlockSpec(a.shape, lambda qb: (0,) * a.ndim)
    return pl.pallas_call(
        _cmp_kernel,
        grid=(s // Q_BLOCK,),
        in_specs=[pl.BlockSpec((NSA_WIDTH, Q_BLOCK), lambda qb: (0, qb)),
                  pl.BlockSpec((NSA_WIDTH, Q_BLOCK), lambda qb: (0, qb)),
                  full(kc_cat), full(vct), full(wc_t), full(ext_q)],
        out_specs=[pl.BlockSpec((NSA_WIDTH, Q_BLOCK), lambda qb: (0, qb)),
                   pl.BlockSpec((g, nsbp, Q_BLOCK), lambda qb: (0, 0, qb))],
        out_shape=[jax.ShapeDtypeStruct((NSA_WIDTH, s), F32),
                   jax.ShapeDtypeStruct((g, nsbp, s), BF16)],
        scratch_shapes=[pltpu.VMEM((g, ncp, GQ), F32), pltpu.VMEM((g, ncp, Q_BLOCK), F32),
                        pltpu.VMEM((g, nsbp, Q_BLOCK), F32)],
        compiler_params=_cparams(1),
        name="cmp_select",
    )(qhi_t, qlo_t, kc_cat, vct, wc_t, ext_q)


def _sel_kernel(qhi_ref, neg_ref, kaug_ref, vt512_ref, vt128_ref, ws_ref, ext_ref,
                kwin_ref, vwt_ref, ww_ref, out_ref, owin_ref,
                qaug_ref, m_ref, acc_ref, s0_ref, s1_ref, s2_ref, cm0_ref, cm1_ref, cm2_ref):
    qb = pl.program_id(1)
    nsbp = neg_ref.shape[0]
    nsc = nsbp // SUPER
    qg = _heads_to_lanes(qhi_ref[...])
    qx = jnp.concatenate([qg, ext_ref[...]], axis=0)
    blk = lax.broadcasted_iota(jnp.int32, (SUPER, Q_BLOCK), 0)

    def aug(neg_rows):
        tiled = jnp.concatenate([neg_rows] * GROUP_SIZE, axis=1).astype(BF16)
        return jnp.concatenate([tiled, qx], axis=0)

    for sc in range(nsc):
        neg_rows = neg_ref[sc * SUPER:(sc + 1) * SUPER, :].astype(F32)
        qaug_ref[sc] = aug(jnp.where(blk + sc * SUPER >= 2 * qb - 2, NEG, neg_rows))

    m_ref[...] = jnp.full(m_ref.shape, M_INIT, F32)
    acc_ref[...] = jnp.zeros(acc_ref.shape, F32)

    def update(s, col_max, vt):
        m_old = m_ref[...]
        m_new = jnp.maximum(m_old, col_max)
        p = jnp.exp2(s - m_new).astype(BF16)
        acc_ref[...] = acc_ref[...] * jnp.exp2(m_old - m_new) + _dot(vt, p)
        m_ref[...] = m_new

    n_far = (jnp.maximum(qb - 1, 0) * Q_BLOCK + KV_CHUNK - 1) // KV_CHUNK

    def scores(c, dst_ref, max_ref):
        k = kaug_ref[pl.ds(pl.multiple_of(c * KV_CHUNK, KV_CHUNK), KV_CHUNK), :]
        sc = (c * (KV_CHUNK // SEL_BLOCK)) // SUPER
        s = _dot(k, qaug_ref[sc])
        dst_ref[...] = s
        max_ref[...] = jnp.max(s, axis=0, keepdims=True)

    kb0 = jnp.maximum(qb - 1, 0)
    ws = pl.multiple_of(kb0 * Q_BLOCK, Q_BLOCK)
    off = pl.multiple_of(jnp.where(qb == 0, Q_BLOCK, 0), Q_BLOCK)
    b_lo = 2 * kb0
    sc_lo = pl.multiple_of((b_lo // SUPER) * SUPER, SUPER)
    sc_hi = pl.multiple_of(((b_lo + 3) // SUPER) * SUPER, SUPER)
    neg_lo = neg_ref[pl.ds(sc_lo, SUPER), :].astype(F32)
    neg_hi = neg_ref[pl.ds(sc_hi, SUPER), :].astype(F32)
    near_neg = jnp.where(blk >= SUPER // 2, neg_lo, neg_hi)
    k = kaug_ref[pl.ds(ws, SEL_NEAR), :]
    s_near = _dot(k, aug(near_neg)) + _bias_tile(ws_ref, off, SEL_NEAR)
    vt_near = jnp.concatenate([vt128_ref[kb0], vt128_ref[kb0 + 1]], axis=1)

    s_win = _window_scores(qb, qg, kwin_ref, ww_ref)
    bufs = ((s0_ref, cm0_ref), (s1_ref, cm1_ref), (s2_ref, cm2_ref))
    last = jnp.maximum(n_far - 1, 0)
    scores(0, *bufs[0])
    scores(jnp.minimum(1, last), *bufs[1])
    update(s_near, jnp.max(s_near, axis=0, keepdims=True), vt_near)
    _window_finish(qb, s_win, vwt_ref, owin_ref)

    def far_triple(c):
        for j in range(3):
            scores(jnp.minimum(c + j + 2, last), *bufs[(j + 2) % 3])
            s_ref, cm_ref = bufs[j]
            update(s_ref[...], cm_ref[...], vt512_ref[c + j])

    def far_six(i, carry):
        far_triple(6 * i)
        far_triple(6 * i + 3)
        return carry

    lax.fori_loop(0, n_far // 6, far_six, 0)
    done = (n_far // 6) * 6

    @pl.when(n_far - done >= 3)
    def _():
        far_triple(done)

    done = (n_far // 3) * 3
    for j in range(2):
        @pl.when(n_far - done > j)
        def _():
            s_ref, cm_ref = bufs[j]
            update(s_ref[...], cm_ref[...], vt512_ref[done + j])

    acc = acc_ref[...]
    _store_heads(out_ref, acc[0:HEAD_DIM] * (1.0 / acc[HEAD_DIM:HEAD_DIM + 1]))


def _sel_win_attention(qhi_t, neg_t, kaug, vt512, vt128, ws_t, ext_q, kwin, vwt, ww_t):
    s = qhi_t.shape[1]
    g = N_GROUPS
    nqb = s // Q_BLOCK
    nsbp = neg_t.shape[1]
    gr = GROUP_SIZE * HEAD_DIM
    once = pl.Buffered(1)
    return pl.pallas_call(
        _sel_kernel,
        grid=(g, nqb),
        in_specs=[pl.BlockSpec((gr, Q_BLOCK), lambda gi, qb: (gi, qb)),
                  pl.BlockSpec((None, nsbp, Q_BLOCK), lambda gi, qb: (gi, 0, qb)),
                  pl.BlockSpec((None, s, 256), lambda gi, qb: (gi, 0, 0), pipeline_mode=once),
                  pl.BlockSpec((None, s // KV_CHUNK, VT_ROWS, KV_CHUNK), lambda gi, qb: (gi, 0, 0, 0),
                               pipeline_mode=once),
                  pl.BlockSpec((None, s // 128, VT_ROWS, 128), lambda gi, qb: (gi, 0, 0, 0),
                               pipeline_mode=once),
                  pl.BlockSpec((GROUP_SIZE,) + ws_t.shape[1:], lambda gi, qb: (gi, 0, 0)),
                  pl.BlockSpec((None, HEAD_DIM, GQ), lambda gi, qb: (gi, 0, 0)),
                  pl.BlockSpec((None, s, 128), lambda gi, qb: (gi, 0, 0), pipeline_mode=once),
                  pl.BlockSpec((None, s // 128, VT_ROWS, 128), lambda gi, qb: (gi, 0, 0, 0),
                               pipeline_mode=once),
                  pl.BlockSpec((GROUP_SIZE,) + ww_t.shape[1:], lambda gi, qb: (gi, 0, 0))],
        out_specs=[pl.BlockSpec((gr, Q_BLOCK), lambda gi, qb: (gi, qb)),
                   pl.BlockSpec((gr, Q_BLOCK), lambda gi, qb: (gi, qb))],
        out_shape=[jax.ShapeDtypeStruct((NSA_WIDTH, s), F32),
                   jax.ShapeDtypeStruct((NSA_WIDTH, s), F32)],
        scratch_shapes=[pltpu.VMEM((nsbp // SUPER, 256, GQ), BF16),
                        pltpu.VMEM((1, GQ), F32),
                        pltpu.VMEM((VT_ROWS, GQ), F32),
                        pltpu.VMEM((KV_CHUNK, GQ), F32),
                        pltpu.VMEM((KV_CHUNK, GQ), F32),
                        pltpu.VMEM((KV_CHUNK, GQ), F32),
                        pltpu.VMEM((1, GQ), F32),
                        pltpu.VMEM((1, GQ), F32),
                        pltpu.VMEM((1, GQ), F32)],
        compiler_params=_cparams(2),
        name="sel_win_attention",
    )(qhi_t, neg_t, kaug, vt512, vt128, ws_t, ext_q, kwin, vwt, ww_t)


def _window_scores(qb, qg, kwin_ref, ww_ref):
    kb0 = jnp.maximum(qb - WINDOW // Q_BLOCK, 0)
    ws = pl.multiple_of(kb0 * Q_BLOCK, Q_BLOCK)
    off = pl.multiple_of(jnp.maximum(WINDOW - qb * Q_BLOCK, 0), Q_BLOCK)
    qx = jnp.concatenate([qg, jnp.zeros_like(qg)], axis=0)
    return _dot(kwin_ref[pl.ds(ws, WIN_KEYS), :], qx) + _bias_tile(ww_ref, off, WIN_KEYS)


def _window_finish(qb, s, vwt_ref, out_ref):
    kb0 = jnp.maximum(qb - WINDOW // Q_BLOCK, 0)
    m = jnp.max(s, axis=0, keepdims=True)
    p = jnp.exp2(s - m).astype(BF16)
    vt = jnp.concatenate([vwt_ref[kb0 + j] for j in range(WIN_KEYS // Q_BLOCK)], axis=1)
    acc = _dot(vt, p)
    _store_heads(out_ref, acc[0:HEAD_DIM] * (1.0 / acc[HEAD_DIM:HEAD_DIM + 1]))


def _lru_rows(u_ref, cw_ref, cb_ref, wa_ref, ba_ref, wx_ref, bx_ref, lam_ref, h_ref,
              tail_ref, hprev_ref, uc_ref):
    tm = u_ref.shape[0]

    @pl.when(pl.program_id(0) == 0)
    def _():
        tail_ref[...] = jnp.zeros(tail_ref.shape, F32)
        hprev_ref[...] = jnp.zeros(hprev_ref.shape, F32)

    u = u_ref[...]
    tail = tail_ref[...]
    row8 = lax.broadcasted_iota(jnp.int32, (8, LRU_WIDTH), 0)
    uc = cb_ref[...] + u * cw_ref[CONV_WIDTH - 1:CONV_WIDTH, :]
    uc_head = uc[0:8]
    for j in range(1, CONV_WIDTH):
        w_j = cw_ref[CONV_WIDTH - 1 - j:CONV_WIDTH - j, :]
        sh = pltpu.roll(u, j, 0)
        uc = uc + sh * w_j
        uc_head = uc_head + jnp.where(row8 < j, pltpu.roll(tail, j, 0), sh[0:8]) * w_j
    tail_ref[...] = u[tm - 8:tm]
    uc_ref[...] = uc
    uc_ref[0:8, :] = uc_head
    uc = uc_ref[...]

    ucb = uc.astype(BF16)
    r = _sigmoid(_dot(ucb, wa_ref[...]) + ba_ref[...])
    ig = _sigmoid(_dot(ucb, wx_ref[...]) + bx_ref[...])
    nl = -lam_ref[...]
    softplus = jnp.maximum(nl, 0.0) + jnp.log(1.0 + jnp.exp(-jnp.abs(nl)))
    a = jnp.exp((-LRU_C * r) * softplus)
    b = jnp.sqrt(1.0 - a * a) * (ig * uc)

    a = a.reshape(tm // 8, 8, LRU_WIDTH)
    b = b.reshape(tm // 8, 8, LRU_WIDTH)
    sub = lax.broadcasted_iota(jnp.int32, a.shape, 1)
    for step in (1, 2, 4):
        a_s = pltpu.roll(a, step, 1)
        b_s = pltpu.roll(b, step, 1)
        ok = sub >= step
        b = jnp.where(ok, a * b_s + b, b)
        a = jnp.where(ok, a * a_s, a)
    carry = hprev_ref[...]
    for j in range(tm // 8):
        hj = a[j] * carry + b[j]
        h_ref[8 * j:8 * j + 8, :] = hj
        carry = hj[7:8]
    hprev_ref[...] = carry


OUT_TM = 512


def _out_kernel(x_ref, g_ref, oc_ref, os_ref, ow_ref, brt_ref, u_ref,
                cw_ref, cb_ref, wa_ref, ba_ref, wx_ref, bx_ref, lam_ref,
                wg_ref, wpa_ref, wpb_ref, wo_ref, y_ref,
                h_ref, tail_ref, hprev_ref, uc_ref):
    _lru_rows(u_ref, cw_ref, cb_ref, wa_ref, ba_ref, wx_ref, bx_ref, lam_ref, h_ref,
              tail_ref, hprev_ref, uc_ref)
    h_in = _normed_input(x_ref, g_ref)
    gate = lambda j: _dot(h_in, wg_ref[:, GATE_OFFS[j]:GATE_OFFS[j + 1]])
    mg = gate(2)
    brt = brt_ref[...]
    parts = []
    for h in range(N_HEADS):
        rows = slice(h * HEAD_DIM, (h + 1) * HEAD_DIM)
        acc = None
        for b, o_ref in enumerate((oc_ref, os_ref, ow_ref)):
            term = brt[b * N_HEADS + h:b * N_HEADS + h + 1, :] * o_ref[rows, :]
            acc = term if acc is None else acc + term
        parts.append(acc)
    gn = gate(0)
    ya = jnp.concatenate(parts, axis=0).T * (gn * _sigmoid(gn))
    y_a = _dot(ya.astype(BF16), wpa_ref[...])
    gl = gate(1)
    y_b = _dot((h_ref[...] * (gl * _sigmoid(gl))).astype(BF16), wpb_ref[...])
    m = _sigmoid(mg[:, :D_MODEL]) * y_a + _sigmoid(mg[:, D_MODEL:]) * y_b
    y_ref[...] = x_ref[...] + _dot(m.astype(BF16), wo_ref[...])


def _output(x2, norm_gain, oc_t, os_t, ow_t, br_t, u_lru, lru_params, w_gate, wpa, wpb, wo):
    s = x2.shape[0]
    tm = OUT_TM
    rows = lambda n: pl.BlockSpec((tm, n), lambda i: (i, 0))
    cols = lambda n: pl.BlockSpec((n, tm), lambda i: (0, i))
    full = lambda a: pl.BlockSpec(a.shape, lambda i: (0, 0))
    return pl.pallas_call(
        _out_kernel,
        grid=(s // tm,),
        in_specs=[rows(D_MODEL), pl.BlockSpec((1, D_MODEL), lambda i: (0, 0)),
                  cols(NSA_WIDTH), cols(NSA_WIDTH), cols(NSA_WIDTH), cols(32), rows(LRU_WIDTH)]
        + [full(p) for p in lru_params]
        + [full(w_gate), full(wpa), full(wpb), full(wo)],
        out_specs=rows(D_MODEL),
        out_shape=jax.ShapeDtypeStruct((s, D_MODEL), F32),
        scratch_shapes=[pltpu.VMEM((tm, LRU_WIDTH), F32),
                        pltpu.VMEM((8, LRU_WIDTH), F32),
                        pltpu.VMEM((1, LRU_WIDTH), F32),
                        pltpu.VMEM((tm, LRU_WIDTH), F32)],
        compiler_params=_cparams(1),
        name="output",
    )(x2, norm_gain.reshape(1, D_MODEL), oc_t, os_t, ow_t, br_t, u_lru, *lru_params,
      w_gate, wpa, wpb, wo)


def _t5_bucket_table(n_dist):
    n = np.arange(n_dist)
    max_exact = N_BUCKETS // 2
    nf = np.maximum(n, 1).astype(np.float32)
    large = max_exact + (np.log(nf / np.float32(max_exact)) / np.float32(math.log(MAX_DISTANCE / max_exact))
                         * np.float32(N_BUCKETS - max_exact)).astype(np.int32)
    return np.where(n < max_exact, n, np.minimum(large, N_BUCKETS - 1))


def _bias_tiles_kernel(relb_ref, bw_ref, bs_ref, bc_ref, ww_ref, ws_ref, wc_ref):
    h = pl.program_id(0)
    far = relb_ref[N_BUCKETS - 1, h]
    for bk_ref, out_ref, rel in ((bw_ref, ww_ref, False), (bs_ref, ws_ref, True), (bc_ref, wc_ref, True)):
        bk = bk_ref[...]
        acc = jnp.full(bk.shape, NEG, F32)
        for b in range(N_BUCKETS):
            val = relb_ref[b, h] - far if rel else relb_ref[b, h]
            acc = jnp.where(bk == b, val * LOG2E, acc)
        out_ref[...] = acc


def _bias_tables(rel_bias):
    bucket = _t5_bucket_table(2 * WINDOW)
    i = np.arange(Q_BLOCK)[None, :]

    def index_tile(dist, valid):
        return jnp.asarray(np.where(valid, bucket[np.clip(dist, 0, bucket.size - 1)], -1).astype(np.int32))

    u = np.arange(WINDOW + WIN_KEYS)[:, None]
    d_win = i - u + WINDOW
    bw = index_tile(d_win, (d_win >= 0) & (d_win < WINDOW))
    u = np.arange(SEL_NEAR + Q_BLOCK)[:, None]
    d_sel = i - u + Q_BLOCK
    bs = index_tile(d_sel, d_sel >= 0)
    u = np.arange(CMP_NEAR + 16)[:, None]
    d_cmp = i - CMP_STRIDE * (u - 16) - (CMP_BLOCK - 1)
    bc = index_tile(d_cmp, d_cmp >= 0)
    full = lambda a: pl.BlockSpec(a.shape, lambda h: (0, 0))
    per_head = lambda a: pl.BlockSpec((None,) + a.shape, lambda h: (h, 0, 0))
    ww, ws, wc = pl.pallas_call(
        _bias_tiles_kernel,
        grid=(N_HEADS,),
        in_specs=[pl.BlockSpec(memory_space=pltpu.SMEM), full(bw), full(bs), full(bc)],
        out_specs=[per_head(bw), per_head(bs), per_head(bc)],
        out_shape=[jax.ShapeDtypeStruct((N_HEADS,) + a.shape, F32) for a in (bw, bs, bc)],
        compiler_params=_cparams(1),
        name="bias_tiles",
    )(rel_bias, bw, bs, bc)
    b_far = rel_bias[N_BUCKETS - 1] * LOG2E
    hi = b_far.astype(BF16)
    lo = (b_far - hi.astype(F32)).astype(BF16)
    ext = jnp.zeros((N_HEADS, HEAD_DIM, Q_BLOCK), BF16)
    ext = ext.at[:, 0, :].set(hi[:, None]).at[:, 1, :].set(lo[:, None])
    ext = ext.reshape(N_GROUPS, GROUP_SIZE, HEAD_DIM, Q_BLOCK).transpose(0, 2, 1, 3)
    return ww, ws, wc, ext.reshape(N_GROUPS, HEAD_DIM, GQ)


def _block_diag(w):
    n, d, e = w.shape
    eye = jnp.eye(n, dtype=w.dtype)
    return (eye[:, None, :, None] * w[:, :, None, :]).reshape(n * d, n * e)


def kernel(x, norm_gain, w_in, q_norm_gain, k_norm_gain, cmp_pe, cmp_w1, cmp_b1, cmp_w2, rel_bias,
           conv_w, conv_b, lru_wa, lru_ba, lru_wx, lru_bx, lru_lambda, w_proj_a, w_proj_b, w_out):
    bsz, s, _ = x.shape
    assert bsz == 1 and s % 1024 == 0 and s >= 1024
    x2 = x.reshape(s, D_MODEL)
    nsb = s // SEL_BLOCK
    nsbp = -(-nsb // SUPER) * SUPER

    o = IN_OFFS
    w16 = w_in.astype(BF16)
    pad = jnp.zeros((D_MODEL, BR_PAD - 3 * N_HEADS), BF16)
    w_front = jnp.concatenate([w16[:, o[0]:o[2]], w16[:, o[3]:o[4]], pad, w16[:, o[4]:o[5]]], axis=1)
    w_gate = jnp.concatenate([w16[:, o[2]:o[3]], w16[:, o[5]:o[7]]], axis=1)

    scale = HEAD_DIM ** -0.5 * LOG2E
    qgain_col = jnp.tile(q_norm_gain * scale, N_HEADS).reshape(NSA_WIDTH, 1)
    kgain_rows = jnp.tile(k_norm_gain, (1, N_GROUPS))
    (qhi_t, qlo_t, br_t, kaug, vt512, vt128, kwin, vwt, kvc, u_lru) = _front(
        x2, norm_gain, w_front, qgain_col, kgain_rows)

    w2p = jnp.pad(cmp_w2, ((0, 0), (0, 0), (0, 128 - HEAD_DIM)))
    kc_gain = jnp.pad(k_norm_gain[0], (0, 128 - HEAD_DIM)).reshape(1, 128)
    kc_cat, vct = _compress(kvc, cmp_w1, cmp_b1.reshape(2, 1, CMP_HIDDEN), w2p,
                            cmp_pe.reshape(2, 1, CMP_BLOCK * HEAD_DIM), kc_gain)

    ww_t, ws_t, wc_t, ext_q = _bias_tables(rel_bias)
    oc_t, neg_t = _cmp_attention(qhi_t, qlo_t, kc_cat, vct, wc_t, ext_q, nsbp)
    os_t, ow_t = _sel_win_attention(qhi_t, neg_t, kaug, vt512, vt128, ws_t, ext_q, kwin, vwt, ww_t)

    row = lambda v: v.reshape(1, LRU_WIDTH)
    lru_params = (conv_w, row(conv_b), _block_diag(lru_wa).astype(BF16), row(lru_ba),
                  _block_diag(lru_wx).astype(BF16), row(lru_bx), row(lru_lambda))
    y = _output(x2, norm_gain, oc_t, os_t, ow_t, br_t, u_lru, lru_params, w_gate,
                w_proj_a.astype(BF16), w_proj_b.astype(BF16), w_out.astype(BF16))
    return y.reshape(bsz, s, D_MODEL)
```

```python
import functools
import math

import numpy as np
import jax
import jax.numpy as jnp
from jax import lax
from jax.experimental import pallas as pl
from jax.experimental.pallas import tpu as pltpu

F32 = jnp.float32
BF16 = jnp.bfloat16

D_MODEL = 1024
N_HEADS = 8
N_GROUPS = 2
GROUP_SIZE = N_HEADS // N_GROUPS
HEAD_DIM = 64
NSA_WIDTH = N_HEADS * HEAD_DIM
KV_WIDTH = N_GROUPS * HEAD_DIM
CMP_STRIDE = 16
CMP_BLOCK = 32
CMP_HIDDEN = 256
SEL_BLOCK = 64
SEL_PER_CMP = SEL_BLOCK // CMP_STRIDE
N_SELECT = 16
WINDOW = 512
Q_BLOCK = 128
LRU_WIDTH = 512
LRU_BLOCKS = 8
CONV_WIDTH = 4
LRU_C = 8.0
N_BUCKETS = 32
MAX_DISTANCE = 128
EPS = 1e-6
NEG = -1e30
M_INIT = -5e29
LOG2E = 1.4426950408889634

GQ = GROUP_SIZE * Q_BLOCK
SUPER = 128
KV_CHUNK = 512
SEL_NEAR = 2 * Q_BLOCK
WIN_KEYS = WINDOW + Q_BLOCK
CMP_NEAR = 24
N_PICK = N_SELECT - 3
PICKED = -2.0
VT_ROWS = 80

IN_OFFS = tuple(int(v) for v in np.cumsum(
    (0, NSA_WIDTH, 6 * KV_WIDTH, NSA_WIDTH, 3 * N_HEADS, LRU_WIDTH, LRU_WIDTH, 2 * D_MODEL)))
BR_PAD = 128
FRONT_OFFS = tuple(int(v) for v in np.cumsum((0, NSA_WIDTH, 6 * KV_WIDTH, BR_PAD, LRU_WIDTH)))
GATE_OFFS = tuple(int(v) for v in np.cumsum((0, NSA_WIDTH, LRU_WIDTH, 2 * D_MODEL)))

VMEM_LIMIT = 56 * 1024 * 1024


def _cparams(n_axes):
    return pltpu.CompilerParams(dimension_semantics=("arbitrary",) * n_axes,
                                vmem_limit_bytes=VMEM_LIMIT)


def _dot(a, b):
    return jnp.dot(a, b, preferred_element_type=F32)


def _sigmoid(x):
    return 0.5 * jnp.tanh(0.5 * x) + 0.5


FRONT_TM = 512


def _normed_input(x_ref, g_ref):
    x = x_ref[...]
    ms = jnp.mean(x * x, axis=-1, keepdims=True)
    return (x * lax.rsqrt(ms + EPS) * g_ref[...]).astype(BF16)


def _group_rms(k, gain_row):
    sq = k * k
    lane = lax.broadcasted_iota(jnp.int32, k.shape, 1)
    lo = lane < HEAD_DIM
    s0 = jnp.sum(jnp.where(lo, sq, 0.0), axis=-1, keepdims=True)
    s1 = jnp.sum(jnp.where(lo, 0.0, sq), axis=-1, keepdims=True)
    inv = jnp.where(lo, lax.rsqrt(s0 / HEAD_DIM + EPS), lax.rsqrt(s1 / HEAD_DIM + EPS))
    return k * inv * gain_row


def _front_kernel(x_ref, g_ref, w_ref, qg_ref, kg_ref,
                  qhi_ref, qlo_ref, brt_ref, kaug_ref, vt512_ref, vt128_ref,
                  kwin_ref, vwt_ref, kvc_ref, u_ref):
    i = pl.program_id(0)
    tm = FRONT_TM
    h_in = _normed_input(x_ref, g_ref)
    proj = lambda a, b: _dot(h_in, w_ref[:, a:b])
    kv = proj(FRONT_OFFS[1], FRONT_OFFS[2])
    piece = lambda j: kv[:, j * KV_WIDTH:(j + 1) * KV_WIDTH]
    lane = lax.broadcasted_iota(jnp.int32, (tm, 128), 1)
    row = lax.broadcasted_iota(jnp.int32, (tm, 128), 0) + i * tm
    lo = lane < HEAD_DIM
    ones_cols = jnp.where((lane == HEAD_DIM) | (lane == HEAD_DIM + 1), 1.0, 0.0)
    onehot = jnp.where(lane == (row // SEL_BLOCK) % SUPER, 1.0, 0.0).astype(BF16)
    kvc_ref[0] = piece(0)
    kvc_ref[1] = piece(1)
    kslc = _group_rms(piece(2), kg_ref[1:2, :])
    kwin = _group_rms(piece(4), kg_ref[2:3, :])
    vslt = piece(3).T
    vwit = piece(5).T
    row_t = lax.broadcasted_iota(jnp.int32, (VT_ROWS - HEAD_DIM, tm), 0)
    ones_rows = jnp.where(row_t == 0, 1.0, 0.0)
    for g in range(N_GROUPS):
        sh = lambda a: a if g == 0 else pltpu.roll(a, HEAD_DIM, 1)
        kaug_ref[g, :, 0:128] = onehot
        kaug_ref[g, :, 128:256] = jnp.where(lo, sh(kslc), ones_cols).astype(BF16)
        kwin_ref[g] = jnp.where(lo, sh(kwin), 0.0).astype(BF16)
        vs = jnp.concatenate([vslt[g * HEAD_DIM:(g + 1) * HEAD_DIM], ones_rows], axis=0).astype(BF16)
        vw = jnp.concatenate([vwit[g * HEAD_DIM:(g + 1) * HEAD_DIM], ones_rows], axis=0).astype(BF16)
        for j in range(tm // KV_CHUNK):
            vt512_ref[g, j] = vs[:, j * KV_CHUNK:(j + 1) * KV_CHUNK]
        for j in range(tm // 128):
            vt128_ref[g, j] = vs[:, j * 128:(j + 1) * 128]
            vwt_ref[g, j] = vw[:, j * 128:(j + 1) * 128]

    qt = proj(FRONT_OFFS[0], FRONT_OFFS[1]).T
    for h in range(N_HEADS):
        blk = qt[h * HEAD_DIM:(h + 1) * HEAD_DIM]
        ms = jnp.mean(blk * blk, axis=0, keepdims=True)
        qn = blk * lax.rsqrt(ms + EPS) * qg_ref[h * HEAD_DIM:(h + 1) * HEAD_DIM, :]
        hi = qn.astype(BF16)
        qhi_ref[h * HEAD_DIM:(h + 1) * HEAD_DIM, :] = hi
        qlo_ref[h * HEAD_DIM:(h + 1) * HEAD_DIM, :] = (qn - hi.astype(F32)).astype(BF16)
    brt_ref[...] = _sigmoid(proj(FRONT_OFFS[2], FRONT_OFFS[3])).T[:32]
    u_ref[...] = proj(FRONT_OFFS[3], FRONT_OFFS[4])


def _front(x2, norm_gain, w_front, qgain_col, kgain_rows):
    s = x2.shape[0]
    tm = FRONT_TM
    g = N_GROUPS
    outs = [
        jax.ShapeDtypeStruct((NSA_WIDTH, s), BF16),
        jax.ShapeDtypeStruct((NSA_WIDTH, s), BF16),
        jax.ShapeDtypeStruct((32, s), F32),
        jax.ShapeDtypeStruct((g, s, 256), BF16),
        jax.ShapeDtypeStruct((g, s // KV_CHUNK, VT_ROWS, KV_CHUNK), BF16),
        jax.ShapeDtypeStruct((g, s // 128, VT_ROWS, 128), BF16),
        jax.ShapeDtypeStruct((g, s, 128), BF16),
        jax.ShapeDtypeStruct((g, s // 128, VT_ROWS, 128), BF16),
        jax.ShapeDtypeStruct((2, s, KV_WIDTH), F32),
        jax.ShapeDtypeStruct((s, LRU_WIDTH), F32),
    ]
    out_specs = [
        pl.BlockSpec((NSA_WIDTH, tm), lambda i: (0, i)),
        pl.BlockSpec((NSA_WIDTH, tm), lambda i: (0, i)),
        pl.BlockSpec((32, tm), lambda i: (0, i)),
        pl.BlockSpec((g, tm, 256), lambda i: (0, i, 0)),
        pl.BlockSpec((g, tm // KV_CHUNK, VT_ROWS, KV_CHUNK), lambda i: (0, i, 0, 0)),
        pl.BlockSpec((g, tm // 128, VT_ROWS, 128), lambda i: (0, i, 0, 0)),
        pl.BlockSpec((g, tm, 128), lambda i: (0, i, 0)),
        pl.BlockSpec((g, tm // 128, VT_ROWS, 128), lambda i: (0, i, 0, 0)),
        pl.BlockSpec((2, tm, KV_WIDTH), lambda i: (0, i, 0)),
        pl.BlockSpec((tm, LRU_WIDTH), lambda i: (i, 0)),
    ]
    return pl.pallas_call(
        _front_kernel,
        grid=(s // tm,),
        in_specs=[pl.BlockSpec((tm, D_MODEL), lambda i: (i, 0)),
                  pl.BlockSpec((1, D_MODEL), lambda i: (0, 0)),
                  pl.BlockSpec(w_front.shape, lambda i: (0, 0)),
                  pl.BlockSpec((NSA_WIDTH, 1), lambda i: (0, 0)),
                  pl.BlockSpec((3, 128), lambda i: (0, 0))],
        out_specs=out_specs,
        out_shape=outs,
        compiler_params=_cparams(1),
        name="front",
    )(x2, norm_gain.reshape(1, D_MODEL), w_front, qgain_col, kgain_rows)


def _compress_kernel(kvc_ref, w1_ref, b1_ref, w2_ref, pe_ref, kg_ref, kc_ref, vct_ref):
    ncp = kc_ref.shape[1]
    half = CMP_STRIDE * HEAD_DIM
    lane = lax.broadcasted_iota(jnp.int32, (ncp, 128), 1)
    lo = lane < HEAD_DIM

    def chunk_rows(j, g):
        cols = []
        for pair in range(CMP_STRIDE // 2):
            even = kvc_ref[j, pl.ds(2 * pair, ncp, stride=CMP_STRIDE), :]
            odd = kvc_ref[j, pl.ds(2 * pair + 1, ncp, stride=CMP_STRIDE), :]
            if g == 0:
                cols.append(jnp.where(lo, even, pltpu.roll(odd, HEAD_DIM, 1)))
            else:
                cols.append(jnp.where(lo, pltpu.roll(even, HEAD_DIM, 1), odd))
        return jnp.concatenate(cols, axis=1)

    def phi(j, g):
        x = chunk_rows(j, g).astype(BF16)
        w1 = w1_ref[j].astype(BF16)
        a = _dot(x, w1[:half])
        b = _dot(x, w1[half:])
        b_next = pltpu.roll(b, ncp - 1, 0)
        pe8 = jnp.broadcast_to(pe_ref[j], (8, 2 * half)).astype(BF16)
        pe_term = _dot(pe8, w1)[0:1]
        hid = a + b_next + pe_term + b1_ref[j]
        act = (hid * _sigmoid(hid)).astype(BF16)
        return _dot(act, w2_ref[j].astype(BF16))

    ones_cols = jnp.where((lane == HEAD_DIM) | (lane == HEAD_DIM + 1), 1.0, 0.0)
    for g in range(N_GROUPS):
        kc = phi(0, g)
        ms = jnp.sum(kc * kc, axis=-1, keepdims=True) / HEAD_DIM
        kc = kc * lax.rsqrt(ms + EPS) * kg_ref[...]
        hi = kc.astype(BF16).astype(F32)
        lo_part = (kc - hi).astype(BF16).astype(F32)
        kc_ref[g, :, 0:128] = (hi + pltpu.roll(lo_part, HEAD_DIM, 1)).astype(BF16)
        kc_ref[g, :, 128:256] = jnp.where(lo, hi, ones_cols).astype(BF16)
        vt = phi(1, g).T
        row_t = lax.broadcasted_iota(jnp.int32, vt.shape, 0)
        vct_ref[g] = jnp.where(row_t == HEAD_DIM, 1.0, vt).astype(BF16)


def _compress(kvc, w1, b1, w2p, pe_flat, kgain_row):
    s = kvc.shape[1]
    g = N_GROUPS
    ncp = s // CMP_STRIDE
    full = lambda a: pl.BlockSpec(a.shape, lambda i: (0,) * a.ndim)
    return pl.pallas_call(
        _compress_kernel,
        grid=(1,),
        in_specs=[pl.BlockSpec(kvc.shape, lambda i: (0, 0, 0), pipeline_mode=pl.Buffered(1)),
                  full(w1), full(b1), full(w2p), full(pe_flat), full(kgain_row)],
        out_specs=[pl.BlockSpec((g, ncp, 256), lambda i: (0, 0, 0)),
                   pl.BlockSpec((g, 128, ncp), lambda i: (0, 0, 0))],
        out_shape=[jax.ShapeDtypeStruct((g, ncp, 256), BF16),
                   jax.ShapeDtypeStruct((g, 128, ncp), BF16)],
        compiler_params=_cparams(1),
        name="compress",
    )(kvc, w1, b1, w2p, pe_flat, kgain_row)


def _heads_to_lanes(q):
    return jnp.concatenate([q[r * HEAD_DIM:(r + 1) * HEAD_DIM] for r in range(GROUP_SIZE)], axis=1)


def _store_heads(out_ref, o):
    for r in range(GROUP_SIZE):
        out_ref[r * HEAD_DIM:(r + 1) * HEAD_DIM, :] = o[0:HEAD_DIM, r * Q_BLOCK:(r + 1) * Q_BLOCK]


def _bias_tile(w_ref, off, n, head0=0):
    return jnp.concatenate([w_ref[head0 + r, pl.ds(off, n), :] for r in range(GROUP_SIZE)], axis=1)


def _cmp_kernel(qhi_ref, qlo_ref, kc_ref, vct_ref, wc_ref, ext_ref, ocmp_ref, neg_ref,
                s_ref, imp_ref, cand_ref):
    qb = pl.program_id(0)
    ncp = kc_ref.shape[1]
    nsbp = neg_ref.shape[1]
    gr = GROUP_SIZE * HEAD_DIM
    ws = pl.multiple_of(jnp.maximum(8 * qb - 16, 0), 8)
    off = pl.multiple_of(jnp.maximum(16 - 8 * qb, 0), 8)
    lim = ws + CMP_NEAR
    n_cls = min(8, ncp // 128)
    per_cls = ncp // n_cls

    def attend(nrows):
        nblk = nrows // SEL_PER_CMP
        band = nrows - per_cls
        rows = lax.broadcasted_iota(jnp.int32, (per_cls, GQ), 0) + band
        blk = lax.broadcasted_iota(jnp.int32, (nblk, Q_BLOCK), 0)
        lane = lax.broadcasted_iota(jnp.int32, (nblk, Q_BLOCK), 1)
        cur = 2 * qb + jnp.where(lane >= SEL_BLOCK, 1, 0)
        forced = jnp.where(blk == 0, 1.0, jnp.where(blk == cur, 1.0, jnp.where(blk == cur - 1, 1.0, 0.0)))
        carry = []
        for g in range(N_GROUPS):
            qhi = qhi_ref[g * gr:(g + 1) * gr, :]
            qlo = qlo_ref[g * gr:(g + 1) * gr, :]
            cols = []
            for r in range(GROUP_SIZE):
                a = qhi[r * HEAD_DIM:(r + 1) * HEAD_DIM]
                b = qlo[r * HEAD_DIM:(r + 1) * HEAD_DIM]
                cols.append(jnp.concatenate([a, a, b], axis=0))
            qcat = jnp.concatenate([jnp.concatenate(cols, axis=1), ext_ref[g]], axis=0)
            s_g = s_ref.at[g]
            s_g[0:nrows, :] = _dot(kc_ref[g, 0:nrows, :], qcat)
            s_g[pl.ds(ws, CMP_NEAR), :] += _bias_tile(wc_ref, off, CMP_NEAR, g * GROUP_SIZE)
            s_g[band:nrows, :] = jnp.where(rows < lim, s_g[band:nrows, :], NEG)
            s = s_g[0:nrows, :]
            m = jnp.maximum(jnp.max(s, axis=0, keepdims=True), M_INIT)
            p = jnp.exp2(s - m)
            acc = _dot(vct_ref[g, 0:VT_ROWS, 0:nrows], p.astype(BF16))
            l = acc[HEAD_DIM:HEAD_DIM + 1]
            inv_l = jnp.where(l > 0.0, 1.0 / l, 0.0)
            _store_heads(ocmp_ref.at[g * gr:(g + 1) * gr], acc[0:HEAD_DIM] * inv_l)
            imp = None
            for r in range(GROUP_SIZE):
                cs = slice(r * Q_BLOCK, (r + 1) * Q_BLOCK)
                term = p[:, cs] * inv_l[:, cs]
                imp = term if imp is None else imp + term
            imp_g = imp_ref.at[g]
            imp_g[0:nrows, :] = imp
            strided = [imp_g[pl.ds(k, nblk, stride=SEL_PER_CMP), :] for k in range(SEL_PER_CMP)]
            prev_last = jnp.where(blk == 0, 0.0, pltpu.roll(strided[SEL_PER_CMP - 1], 1, 0))
            impb = prev_last + strided[0] + strided[1] + strided[2] + strided[3]
            cand = jnp.where(blk < cur - 1, jnp.where(blk > 0, impb, -1.0), -1.0)
            cand_ref[g, 0:nblk, :] = cand
            carry.append(cand)

        def pick_one(_, cands):
            out = []
            for sc in cands:
                mx = jnp.max(sc, axis=0, keepdims=True)
                hit_blk = jnp.where(sc == mx, jnp.where(mx >= 0.0, blk, nblk), nblk)
                first = jnp.min(hit_blk, axis=0, keepdims=True)
                out.append(jnp.where(blk == first, PICKED, sc))
            return tuple(out)

        def pick_maxima(_, cands):
            out = []
            for sc in cands:
                mx = jnp.max(sc, axis=0, keepdims=True)
                out.append(jnp.where(sc == jnp.where(mx >= 0.0, mx, jnp.inf), PICKED, sc))
            return tuple(out)

        def rounds(pick, cands, max_rows=2 * SUPER):
            if nblk * N_GROUPS <= max_rows:
                return lax.fori_loop(0, N_PICK, pick, tuple(cands))
            return [lax.fori_loop(0, N_PICK, pick, (c,))[0] for c in cands]

        def write_selection(cands):
            for g in range(N_GROUPS):
                picked = cands[g] == PICKED
                sel_neg = jnp.where(picked, 0.0,
                                    jnp.where(blk <= cur, jnp.where(forced > 0.5, 0.0, NEG), NEG))
                neg_ref[g, 0:nblk, :] = sel_neg.astype(BF16)
                if nblk < nsbp:
                    neg_ref[g, nblk:nsbp, :] = jnp.full((nsbp - nblk, Q_BLOCK), NEG, BF16)

        fast = rounds(pick_maxima, carry, 4 * SUPER)
        expected = jnp.clip(cur[0:1, :] - 2, 0, N_PICK).astype(F32)
        excess = None
        for g in range(N_GROUPS):
            count = jnp.sum(jnp.where(fast[g] == PICKED, 1.0, 0.0), axis=0, keepdims=True)
            excess = count - expected if excess is None else jnp.maximum(excess, count - expected)
        tied = jnp.max(excess) > 0.5

        @pl.when(jnp.logical_not(tied))
        def _():
            write_selection(fast)

        @pl.when(tied)
        def _():
            write_selection(rounds(pick_one, [cand_ref[g, 0:nblk, :] for g in range(N_GROUPS)]))

    cls = (lim + per_cls - 1) // per_cls
    for k in range(1, n_cls + 1):
        pl.when(cls == k)(functools.partial(attend, k * per_cls))


def _cmp_attention(qhi_t, qlo_t, kc_cat, vct, wc_t, ext_q, nsbp):
    s = qhi_t.shape[1]
    g = N_GROUPS
    ncp = kc_cat.shape[1]
    full = lambda a: pl.BlockSpec(a.shape, lambda qb: (0,) * a.ndim)
    return pl.pallas_call(
        _cmp_kernel,
        grid=(s // Q_BLOCK,),
        in_specs=[pl.BlockSpec((NSA_WIDTH, Q_BLOCK), lambda qb: (0, qb)),
                  pl.BlockSpec((NSA_WIDTH, Q_BLOCK), lambda qb: (0, qb)),
                  full(kc_cat), full(vct), full(wc_t), full(ext_q)],
        out_specs=[pl.BlockSpec((NSA_WIDTH, Q_BLOCK), lambda qb: (0, qb)),
                   pl.BlockSpec((g, nsbp, Q_BLOCK), lambda qb: (0, 0, qb))],
        out_shape=[jax.ShapeDtypeStruct((NSA_WIDTH, s), F32),
                   jax.ShapeDtypeStruct((g, nsbp, s), BF16)],
        scratch_shapes=[pltpu.VMEM((g, ncp, GQ), F32), pltpu.VMEM((g, ncp, Q_BLOCK), F32),
                        pltpu.VMEM((g, nsbp, Q_BLOCK), F32)],
        compiler_params=_cparams(1),
        name="cmp_select",
    )(qhi_t, qlo_t, kc_cat, vct, wc_t, ext_q)


def _sel_kernel(qhi_ref, neg_ref, kaug_ref, vt512_ref, vt128_ref, ws_ref, ext_ref,
                kwin_ref, vwt_ref, ww_ref, out_ref, owin_ref,
                qaug_ref, m_ref, acc_ref, s0_ref, s1_ref, s2_ref, cm0_ref, cm1_ref, cm2_ref):
    qb = pl.program_id(1)
    nsbp = neg_ref.shape[0]
    nsc = nsbp // SUPER
    qg = _heads_to_lanes(qhi_ref[...])
    qx = jnp.concatenate([qg, ext_ref[...]], axis=0)
    blk = lax.broadcasted_iota(jnp.int32, (SUPER, Q_BLOCK), 0)

    def aug(neg_rows):
        tiled = jnp.concatenate([neg_rows] * GROUP_SIZE, axis=1).astype(BF16)
        return jnp.concatenate([tiled, qx], axis=0)

    for sc in range(nsc):
        neg_rows = neg_ref[sc * SUPER:(sc + 1) * SUPER, :].astype(F32)
        qaug_ref[sc] = aug(jnp.where(blk + sc * SUPER >= 2 * qb - 2, NEG, neg_rows))

    m_ref[...] = jnp.full(m_ref.shape, M_INIT, F32)
    acc_ref[...] = jnp.zeros(acc_ref.shape, F32)

    def update(s, col_max, vt):
        m_old = m_ref[...]
        m_new = jnp.maximum(m_old, col_max)
        p = jnp.exp2(s - m_new).astype(BF16)
        acc_ref[...] = acc_ref[...] * jnp.exp2(m_old - m_new) + _dot(vt, p)
        m_ref[...] = m_new

    n_far = (jnp.maximum(qb - 1, 0) * Q_BLOCK + KV_CHUNK - 1) // KV_CHUNK

    def scores(c, dst_ref, max_ref):
        k = kaug_ref[pl.ds(pl.multiple_of(c * KV_CHUNK, KV_CHUNK), KV_CHUNK), :]
        sc = (c * (KV_CHUNK // SEL_BLOCK)) // SUPER
        s = _dot(k, qaug_ref[sc])
        dst_ref[...] = s
        max_ref[...] = jnp.max(s, axis=0, keepdims=True)

    kb0 = jnp.maximum(qb - 1, 0)
    ws = pl.multiple_of(kb0 * Q_BLOCK, Q_BLOCK)
    off = pl.multiple_of(jnp.where(qb == 0, Q_BLOCK, 0), Q_BLOCK)
    b_lo = 2 * kb0
    sc_lo = pl.multiple_of((b_lo // SUPER) * SUPER, SUPER)
    sc_hi = pl.multiple_of(((b_lo + 3) // SUPER) * SUPER, SUPER)
    neg_lo = neg_ref[pl.ds(sc_lo, SUPER), :].astype(F32)
    neg_hi = neg_ref[pl.ds(sc_hi, SUPER), :].astype(F32)
    near_neg = jnp.where(blk >= SUPER // 2, neg_lo, neg_hi)
    k = kaug_ref[pl.ds(ws, SEL_NEAR), :]
    s_near = _dot(k, aug(near_neg)) + _bias_tile(ws_ref, off, SEL_NEAR)
    vt_near = jnp.concatenate([vt128_ref[kb0], vt128_ref[kb0 + 1]], axis=1)

    s_win = _window_scores(qb, qg, kwin_ref, ww_ref)
    bufs = ((s0_ref, cm0_ref), (s1_ref, cm1_ref), (s2_ref, cm2_ref))
    last = jnp.maximum(n_far - 1, 0)
    scores(0, *bufs[0])
    scores(jnp.minimum(1, last), *bufs[1])
    update(s_near, jnp.max(s_near, axis=0, keepdims=True), vt_near)
    _window_finish(qb, s_win, vwt_ref, owin_ref)

    def far_triple(c):
        for j in range(3):
            scores(jnp.minimum(c + j + 2, last), *bufs[(j + 2) % 3])
            s_ref, cm_ref = bufs[j]
            update(s_ref[...], cm_ref[...], vt512_ref[c + j])

    def far_six(i, carry):
        far_triple(6 * i)
        far_triple(6 * i + 3)
        return carry

    lax.fori_loop(0, n_far // 6, far_six, 0)
    done = (n_far // 6) * 6

    @pl.when(n_far - done >= 3)
    def _():
        far_triple(done)

    done = (n_far // 3) * 3
    for j in range(2):
        @pl.when(n_far - done > j)
        def _():
            s_ref, cm_ref = bufs[j]
            update(s_ref[...], cm_ref[...], vt512_ref[done + j])

    acc = acc_ref[...]
    _store_heads(out_ref, acc[0:HEAD_DIM] * (1.0 / acc[HEAD_DIM:HEAD_DIM + 1]))


def _sel_win_attention(qhi_t, neg_t, kaug, vt512, vt128, ws_t, ext_q, kwin, vwt, ww_t):
    s = qhi_t.shape[1]
    g = N_GROUPS
    nqb = s // Q_BLOCK
    nsbp = neg_t.shape[1]
    gr = GROUP_SIZE * HEAD_DIM
    once = pl.Buffered(1)
    return pl.pallas_call(
        _sel_kernel,
        grid=(g, nqb),
        in_specs=[pl.BlockSpec((gr, Q_BLOCK), lambda gi, qb: (gi, qb)),
                  pl.BlockSpec((None, nsbp, Q_BLOCK), lambda gi, qb: (gi, 0, qb)),
                  pl.BlockSpec((None, s, 256), lambda gi, qb: (gi, 0, 0), pipeline_mode=once),
                  pl.BlockSpec((None, s // KV_CHUNK, VT_ROWS, KV_CHUNK), lambda gi, qb: (gi, 0, 0, 0),
                               pipeline_mode=once),
                  pl.BlockSpec((None, s // 128, VT_ROWS, 128), lambda gi, qb: (gi, 0, 0, 0),
                               pipeline_mode=once),
                  pl.BlockSpec((GROUP_SIZE,) + ws_t.shape[1:], lambda gi, qb: (gi, 0, 0)),
                  pl.BlockSpec((None, HEAD_DIM, GQ), lambda gi, qb: (gi, 0, 0)),
                  pl.BlockSpec((None, s, 128), lambda gi, qb: (gi, 0, 0), pipeline_mode=once),
                  pl.BlockSpec((None, s // 128, VT_ROWS, 128), lambda gi, qb: (gi, 0, 0, 0),
                               pipeline_mode=once),
                  pl.BlockSpec((GROUP_SIZE,) + ww_t.shape[1:], lambda gi, qb: (gi, 0, 0))],
        out_specs=[pl.BlockSpec((gr, Q_BLOCK), lambda gi, qb: (gi, qb)),
                   pl.BlockSpec((gr, Q_BLOCK), lambda gi, qb: (gi, qb))],
        out_shape=[jax.ShapeDtypeStruct((NSA_WIDTH, s), F32),
                   jax.ShapeDtypeStruct((NSA_WIDTH, s), F32)],
        scratch_shapes=[pltpu.VMEM((nsbp // SUPER, 256, GQ), BF16),
                        pltpu.VMEM((1, GQ), F32),
                        pltpu.VMEM((VT_ROWS, GQ), F32),
                        pltpu.VMEM((KV_CHUNK, GQ), F32),
                        pltpu.VMEM((KV_CHUNK, GQ), F32),
                        pltpu.VMEM((KV_CHUNK, GQ), F32),
                        pltpu.VMEM((1, GQ), F32),
                        pltpu.VMEM((1, GQ), F32),
                        pltpu.VMEM((1, GQ), F32)],
        compiler_params=_cparams(2),
        name="sel_win_attention",
    )(qhi_t, neg_t, kaug, vt512, vt128, ws_t, ext_q, kwin, vwt, ww_t)


def _window_scores(qb, qg, kwin_ref, ww_ref):
    kb0 = jnp.maximum(qb - WINDOW // Q_BLOCK, 0)
    ws = pl.multiple_of(kb0 * Q_BLOCK, Q_BLOCK)
    off = pl.multiple_of(jnp.maximum(WINDOW - qb * Q_BLOCK, 0), Q_BLOCK)
    qx = jnp.concatenate([qg, jnp.zeros_like(qg)], axis=0)
    return _dot(kwin_ref[pl.ds(ws, WIN_KEYS), :], qx) + _bias_tile(ww_ref, off, WIN_KEYS)


def _window_finish(qb, s, vwt_ref, out_ref):
    kb0 = jnp.maximum(qb - WINDOW // Q_BLOCK, 0)
    m = jnp.max(s, axis=0, keepdims=True)
    p = jnp.exp2(s - m).astype(BF16)
    vt = jnp.concatenate([vwt_ref[kb0 + j] for j in range(WIN_KEYS // Q_BLOCK)], axis=1)
    acc = _dot(vt, p)
    _store_heads(out_ref, acc[0:HEAD_DIM] * (1.0 / acc[HEAD_DIM:HEAD_DIM + 1]))


def _lru_rows(u_ref, cw_ref, cb_ref, wa_ref, ba_ref, wx_ref, bx_ref, lam_ref, h_ref,
              tail_ref, hprev_ref, uc_ref):
    tm = u_ref.shape[0]

    @pl.when(pl.program_id(0) == 0)
    def _():
        tail_ref[...] = jnp.zeros(tail_ref.shape, F32)
        hprev_ref[...] = jnp.zeros(hprev_ref.shape, F32)

    u = u_ref[...]
    tail = tail_ref[...]
    row8 = lax.broadcasted_iota(jnp.int32, (8, LRU_WIDTH), 0)
    uc = cb_ref[...] + u * cw_ref[CONV_WIDTH - 1:CONV_WIDTH, :]
    uc_head = uc[0:8]
    for j in range(1, CONV_WIDTH):
        w_j = cw_ref[CONV_WIDTH - 1 - j:CONV_WIDTH - j, :]
        sh = pltpu.roll(u, j, 0)
        uc = uc + sh * w_j
        uc_head = uc_head + jnp.where(row8 < j, pltpu.roll(tail, j, 0), sh[0:8]) * w_j
    tail_ref[...] = u[tm - 8:tm]
    uc_ref[...] = uc
    uc_ref[0:8, :] = uc_head
    uc = uc_ref[...]

    ucb = uc.astype(BF16)
    r = _sigmoid(_dot(ucb, wa_ref[...]) + ba_ref[...])
    ig = _sigmoid(_dot(ucb, wx_ref[...]) + bx_ref[...])
    nl = -lam_ref[...]
    softplus = jnp.maximum(nl, 0.0) + jnp.log(1.0 + jnp.exp(-jnp.abs(nl)))
    a = jnp.exp((-LRU_C * r) * softplus)
    b = jnp.sqrt(1.0 - a * a) * (ig * uc)

    a = a.reshape(tm // 8, 8, LRU_WIDTH)
    b = b.reshape(tm // 8, 8, LRU_WIDTH)
    sub = lax.broadcasted_iota(jnp.int32, a.shape, 1)
    for step in (1, 2, 4):
        a_s = pltpu.roll(a, step, 1)
        b_s = pltpu.roll(b, step, 1)
        ok = sub >= step
        b = jnp.where(ok, a * b_s + b, b)
        a = jnp.where(ok, a * a_s, a)
    carry = hprev_ref[...]
    for j in range(tm // 8):
        hj = a[j] * carry + b[j]
        h_ref[8 * j:8 * j + 8, :] = hj
        carry = hj[7:8]
    hprev_ref[...] = carry


OUT_TM = 512


def _out_kernel(x_ref, g_ref, oc_ref, os_ref, ow_ref, brt_ref, u_ref,
                cw_ref, cb_ref, wa_ref, ba_ref, wx_ref, bx_ref, lam_ref,
                wg_ref, wpa_ref, wpb_ref, wo_ref, y_ref,
                h_ref, tail_ref, hprev_ref, uc_ref):
    _lru_rows(u_ref, cw_ref, cb_ref, wa_ref, ba_ref, wx_ref, bx_ref, lam_ref, h_ref,
              tail_ref, hprev_ref, uc_ref)
    h_in = _normed_input(x_ref, g_ref)
    gate = lambda j: _dot(h_in, wg_ref[:, GATE_OFFS[j]:GATE_OFFS[j + 1]])
    mg = gate(2)
    brt = brt_ref[...]
    parts = []
    for h in range(N_HEADS):
        rows = slice(h * HEAD_DIM, (h + 1) * HEAD_DIM)
        acc = None
        for b, o_ref in enumerate((oc_ref, os_ref, ow_ref)):
            term = brt[b * N_HEADS + h:b * N_HEADS + h + 1, :] * o_ref[rows, :]
            acc = term if acc is None else acc + term
        parts.append(acc)
    gn = gate(0)
    ya = jnp.concatenate(parts, axis=0).T * (gn * _sigmoid(gn))
    y_a = _dot(ya.astype(BF16), wpa_ref[...])
    gl = gate(1)
    y_b = _dot((h_ref[...] * (gl * _sigmoid(gl))).astype(BF16), wpb_ref[...])
    m = _sigmoid(mg[:, :D_MODEL]) * y_a + _sigmoid(mg[:, D_MODEL:]) * y_b
    y_ref[...] = x_ref[...] + _dot(m.astype(BF16), wo_ref[...])


def _output(x2, norm_gain, oc_t, os_t, ow_t, br_t, u_lru, lru_params, w_gate, wpa, wpb, wo):
    s = x2.shape[0]
    tm = OUT_TM
    rows = lambda n: pl.BlockSpec((tm, n), lambda i: (i, 0))
    cols = lambda n: pl.BlockSpec((n, tm), lambda i: (0, i))
    full = lambda a: pl.BlockSpec(a.shape, lambda i: (0, 0))
    return pl.pallas_call(
        _out_kernel,
        grid=(s // tm,),
        in_specs=[rows(D_MODEL), pl.BlockSpec((1, D_MODEL), lambda i: (0, 0)),
                  cols(NSA_WIDTH), cols(NSA_WIDTH), cols(NSA_WIDTH), cols(32), rows(LRU_WIDTH)]
        + [full(p) for p in lru_params]
        + [full(w_gate), full(wpa), full(wpb), full(wo)],
        out_specs=rows(D_MODEL),
        out_shape=jax.ShapeDtypeStruct((s, D_MODEL), F32),
        scratch_shapes=[pltpu.VMEM((tm, LRU_WIDTH), F32),
                        pltpu.VMEM((8, LRU_WIDTH), F32),
                        pltpu.VMEM((1, LRU_WIDTH), F32),
                        pltpu.VMEM((tm, LRU_WIDTH), F32)],
        compiler_params=_cparams(1),
        name="output",
    )(x2, norm_gain.reshape(1, D_MODEL), oc_t, os_t, ow_t, br_t, u_lru, *lru_params,
      w_gate, wpa, wpb, wo)


def _t5_bucket_table(n_dist):
    n = np.arange(n_dist)
    max_exact = N_BUCKETS // 2
    nf = np.maximum(n, 1).astype(np.float32)
    large = max_exact + (np.log(nf / np.float32(max_exact)) / np.float32(math.log(MAX_DISTANCE / max_exact))
                         * np.float32(N_BUCKETS - max_exact)).astype(np.int32)
    return np.where(n < max_exact, n, np.minimum(large, N_BUCKETS - 1))


def _bias_tiles_kernel(relb_ref, bw_ref, bs_ref, bc_ref, ww_ref, ws_ref, wc_ref):
    h = pl.program_id(0)
    far = relb_ref[N_BUCKETS - 1, h]
    for bk_ref, out_ref, rel in ((bw_ref, ww_ref, False), (bs_ref, ws_ref, True), (bc_ref, wc_ref, True)):
        bk = bk_ref[...]
        acc = jnp.full(bk.shape, NEG, F32)
        for b in range(N_BUCKETS):
            val = relb_ref[b, h] - far if rel else relb_ref[b, h]
            acc = jnp.where(bk == b, val * LOG2E, acc)
        out_ref[...] = acc


def _bias_tables(rel_bias):
    bucket = _t5_bucket_table(2 * WINDOW)
    i = np.arange(Q_BLOCK)[None, :]

    def index_tile(dist, valid):
        return jnp.asarray(np.where(valid, bucket[np.clip(dist, 0, bucket.size - 1)], -1).astype(np.int32))

    u = np.arange(WINDOW + WIN_KEYS)[:, None]
    d_win = i - u + WINDOW
    bw = index_tile(d_win, (d_win >= 0) & (d_win < WINDOW))
    u = np.arange(SEL_NEAR + Q_BLOCK)[:, None]
    d_sel = i - u + Q_BLOCK
    bs = index_tile(d_sel, d_sel >= 0)
    u = np.arange(CMP_NEAR + 16)[:, None]
    d_cmp = i - CMP_STRIDE * (u - 16) - (CMP_BLOCK - 1)
    bc = index_tile(d_cmp, d_cmp >= 0)
    full = lambda a: pl.BlockSpec(a.shape, lambda h: (0, 0))
    per_head = lambda a: pl.BlockSpec((None,) + a.shape, lambda h: (h, 0, 0))
    ww, ws, wc = pl.pallas_call(
        _bias_tiles_kernel,
        grid=(N_HEADS,),
        in_specs=[pl.BlockSpec(memory_space=pltpu.SMEM), full(bw), full(bs), full(bc)],
        out_specs=[per_head(bw), per_head(bs), per_head(bc)],
        out_shape=[jax.ShapeDtypeStruct((N_HEADS,) + a.shape, F32) for a in (bw, bs, bc)],
        compiler_params=_cparams(1),
        name="bias_tiles",
    )(rel_bias, bw, bs, bc)
    b_far = rel_bias[N_BUCKETS - 1] * LOG2E
    hi = b_far.astype(BF16)
    lo = (b_far - hi.astype(F32)).astype(BF16)
    ext = jnp.zeros((N_HEADS, HEAD_DIM, Q_BLOCK), BF16)
    ext = ext.at[:, 0, :].set(hi[:, None]).at[:, 1, :].set(lo[:, None])
    ext = ext.reshape(N_GROUPS, GROUP_SIZE, HEAD_DIM, Q_BLOCK).transpose(0, 2, 1, 3)
    return ww, ws, wc, ext.reshape(N_GROUPS, HEAD_DIM, GQ)


def _block_diag(w):
    n, d, e = w.shape
    eye = jnp.eye(n, dtype=w.dtype)
    return (eye[:, None, :, None] * w[:, :, None, :]).reshape(n * d, n * e)


def kernel(x, norm_gain, w_in, q_norm_gain, k_norm_gain, cmp_pe, cmp_w1, cmp_b1, cmp_w2, rel_bias,
           conv_w, conv_b, lru_wa, lru_ba, lru_wx, lru_bx, lru_lambda, w_proj_a, w_proj_b, w_out):
    bsz, s, _ = x.shape
    assert bsz == 1 and s % 1024 == 0 and s >= 1024
    x2 = x.reshape(s, D_MODEL)
    nsb = s // SEL_BLOCK
    nsbp = -(-nsb // SUPER) * SUPER

    o = IN_OFFS
    w16 = w_in.astype(BF16)
    pad = jnp.zeros((D_MODEL, BR_PAD - 3 * N_HEADS), BF16)
    w_front = jnp.concatenate([w16[:, o[0]:o[2]], w16[:, o[3]:o[4]], pad, w16[:, o[4]:o[5]]], axis=1)
    w_gate = jnp.concatenate([w16[:, o[2]:o[3]], w16[:, o[5]:o[7]]], axis=1)

    scale = HEAD_DIM ** -0.5 * LOG2E
    qgain_col = jnp.tile(q_norm_gain * scale, N_HEADS).reshape(NSA_WIDTH, 1)
    kgain_rows = jnp.tile(k_norm_gain, (1, N_GROUPS))
    (qhi_t, qlo_t, br_t, kaug, vt512, vt128, kwin, vwt, kvc, u_lru) = _front(
        x2, norm_gain, w_front, qgain_col, kgain_rows)

    w2p = jnp.pad(cmp_w2, ((0, 0), (0, 0), (0, 128 - HEAD_DIM)))
    kc_gain = jnp.pad(k_norm_gain[0], (0, 128 - HEAD_DIM)).reshape(1, 128)
    kc_cat, vct = _compress(kvc, cmp_w1, cmp_b1.reshape(2, 1, CMP_HIDDEN), w2p,
                            cmp_pe.reshape(2, 1, CMP_BLOCK * HEAD_DIM), kc_gain)

    ww_t, ws_t, wc_t, ext_q = _bias_tables(rel_bias)
    oc_t, neg_t = _cmp_attention(qhi_t, qlo_t, kc_cat, vct, wc_t, ext_q, nsbp)
    os_t, ow_t = _sel_win_attention(qhi_t, neg_t, kaug, vt512, vt128, ws_t, ext_q, kwin, vwt, ww_t)

    row = lambda v: v.reshape(1, LRU_WIDTH)
    lru_params = (conv_w, row(conv_b), _block_diag(lru_wa).astype(BF16), row(lru_ba),
                  _block_diag(lru_wx).astype(BF16), row(lru_bx), row(lru_lambda))
    y = _output(x2, norm_gain, oc_t, os_t, ow_t, br_t, u_lru, lru_params, w_gate,
                w_proj_a.astype(BF16), w_proj_b.astype(BF16), w_out.astype(BF16))
    return y.reshape(bsz, s, D_MODEL)
```

```python
import functools
import math

import numpy as np
import jax
import jax.numpy as jnp
from jax import lax
from jax.experimental import pallas as pl
from jax.experimental.pallas import tpu as pltpu

F32 = jnp.float32
BF16 = jnp.bfloat16

D_MODEL = 1024
N_HEADS = 8
N_GROUPS = 2
GROUP_SIZE = N_HEADS // N_GROUPS
HEAD_DIM = 64
NSA_WIDTH = N_HEADS * HEAD_DIM
KV_WIDTH = N_GROUPS * HEAD_DIM
CMP_STRIDE = 16
CMP_BLOCK = 32
CMP_HIDDEN = 256
SEL_BLOCK = 64
SEL_PER_CMP = SEL_BLOCK // CMP_STRIDE
N_SELECT = 16
WINDOW = 512
Q_BLOCK = 128
LRU_WIDTH = 512
LRU_BLOCKS = 8
CONV_WIDTH = 4
LRU_C = 8.0
N_BUCKETS = 32
MAX_DISTANCE = 128
EPS = 1e-6
NEG = -1e30
M_INIT = -5e29
LOG2E = 1.4426950408889634

GQ = GROUP_SIZE * Q_BLOCK
SUPER = 128
KV_CHUNK = 512
SEL_NEAR = 2 * Q_BLOCK
WIN_KEYS = WINDOW + Q_BLOCK
CMP_NEAR = 24
N_PICK = N_SELECT - 3
PICKED = -2.0
VT_ROWS = 80

IN_OFFS = tuple(int(v) for v in np.cumsum(
    (0, NSA_WIDTH, 6 * KV_WIDTH, NSA_WIDTH, 3 * N_HEADS, LRU_WIDTH, LRU_WIDTH, 2 * D_MODEL)))
BR_PAD = 128
FRONT_OFFS = tuple(int(v) for v in np.cumsum((0, NSA_WIDTH, 6 * KV_WIDTH, BR_PAD, LRU_WIDTH)))
GATE_OFFS = tuple(int(v) for v in np.cumsum((0, NSA_WIDTH, LRU_WIDTH, 2 * D_MODEL)))

VMEM_LIMIT = 56 * 1024 * 1024


def _cparams(n_axes):
    return pltpu.CompilerParams(dimension_semantics=("arbitrary",) * n_axes,
                                vmem_limit_bytes=VMEM_LIMIT)


def _dot(a, b):
    return jnp.dot(a, b, preferred_element_type=F32)


def _sigmoid(x):
    return 0.5 * jnp.tanh(0.5 * x) + 0.5


FRONT_TM = 512


def _normed_input(x_ref, g_ref):
    x = x_ref[...]
    ms = jnp.mean(x * x, axis=-1, keepdims=True)
    return (x * lax.rsqrt(ms + EPS) * g_ref[...]).astype(BF16)


def _group_rms(k, gain_row):
    sq = k * k
    lane = lax.broadcasted_iota(jnp.int32, k.shape, 1)
    lo = lane < HEAD_DIM
    s0 = jnp.sum(jnp.where(lo, sq, 0.0), axis=-1, keepdims=True)
    s1 = jnp.sum(jnp.where(lo, 0.0, sq), axis=-1, keepdims=True)
    inv = jnp.where(lo, lax.rsqrt(s0 / HEAD_DIM + EPS), lax.rsqrt(s1 / HEAD_DIM + EPS))
    return k * inv * gain_row


def _front_kernel(x_ref, g_ref, w_ref, qg_ref, kg_ref,
                  qhi_ref, qlo_ref, brt_ref, kaug_ref, vt512_ref, vt128_ref,
                  kwin_ref, vwt_ref, kvc_ref, u_ref):
    i = pl.program_id(0)
    tm = FRONT_TM
    h_in = _normed_input(x_ref, g_ref)
    proj = lambda a, b: _dot(h_in, w_ref[:, a:b])
    kv = proj(FRONT_OFFS[1], FRONT_OFFS[2])
    piece = lambda j: kv[:, j * KV_WIDTH:(j + 1) * KV_WIDTH]
    lane = lax.broadcasted_iota(jnp.int32, (tm, 128), 1)
    row = lax.broadcasted_iota(jnp.int32, (tm, 128), 0) + i * tm
    lo = lane < HEAD_DIM
    ones_cols = jnp.where((lane == HEAD_DIM) | (lane == HEAD_DIM + 1), 1.0, 0.0)
    onehot = jnp.where(lane == (row // SEL_BLOCK) % SUPER, 1.0, 0.0).astype(BF16)
    kvc_ref[0] = piece(0)
    kvc_ref[1] = piece(1)
    kslc = _group_rms(piece(2), kg_ref[1:2, :])
    kwin = _group_rms(piece(4), kg_ref[2:3, :])
    vslt = piece(3).T
    vwit = piece(5).T
    row_t = lax.broadcasted_iota(jnp.int32, (VT_ROWS - HEAD_DIM, tm), 0)
    ones_rows = jnp.where(row_t == 0, 1.0, 0.0)
    for g in range(N_GROUPS):
        sh = lambda a: a if g == 0 else pltpu.roll(a, HEAD_DIM, 1)
        kaug_ref[g, :, 0:128] = onehot
        kaug_ref[g, :, 128:256] = jnp.where(lo, sh(kslc), ones_cols).astype(BF16)
        kwin_ref[g] = jnp.where(lo, sh(kwin), 0.0).astype(BF16)
        vs = jnp.concatenate([vslt[g * HEAD_DIM:(g + 1) * HEAD_DIM], ones_rows], axis=0).astype(BF16)
        vw = jnp.concatenate([vwit[g * HEAD_DIM:(g + 1) * HEAD_DIM], ones_rows], axis=0).astype(BF16)
        for j in range(tm // KV_CHUNK):
            vt512_ref[g, j] = vs[:, j * KV_CHUNK:(j + 1) * KV_CHUNK]
        for j in range(tm // 128):
            vt128_ref[g, j] = vs[:, j * 128:(j + 1) * 128]
            vwt_ref[g, j] = vw[:, j * 128:(j + 1) * 128]

    qt = proj(FRONT_OFFS[0], FRONT_OFFS[1]).T
    for h in range(N_HEADS):
        blk = qt[h * HEAD_DIM:(h + 1) * HEAD_DIM]
        ms = jnp.mean(blk * blk, axis=0, keepdims=True)
        qn = blk * lax.rsqrt(ms + EPS) * qg_ref[h * HEAD_DIM:(h + 1) * HEAD_DIM, :]
        hi = qn.astype(BF16)
        qhi_ref[h * HEAD_DIM:(h + 1) * HEAD_DIM, :] = hi
        qlo_ref[h * HEAD_DIM:(h + 1) * HEAD_DIM, :] = (qn - hi.astype(F32)).astype(BF16)
    brt_ref[...] = _sigmoid(proj(FRONT_OFFS[2], FRONT_OFFS[3])).T[:32]
    u_ref[...] = proj(FRONT_OFFS[3], FRONT_OFFS[4])


def _front(x2, norm_gain, w_front, qgain_col, kgain_rows):
    s = x2.shape[0]
    tm = FRONT_TM
    g = N_GROUPS
    outs = [
        jax.ShapeDtypeStruct((NSA_WIDTH, s), BF16),
        jax.ShapeDtypeStruct((NSA_WIDTH, s), BF16),
        jax.ShapeDtypeStruct((32, s), F32),
        jax.ShapeDtypeStruct((g, s, 256), BF16),
        jax.ShapeDtypeStruct((g, s // KV_CHUNK, VT_ROWS, KV_CHUNK), BF16),
        jax.ShapeDtypeStruct((g, s // 128, VT_ROWS, 128), BF16),
        jax.ShapeDtypeStruct((g, s, 128), BF16),
        jax.ShapeDtypeStruct((g, s // 128, VT_ROWS, 128), BF16),
        jax.ShapeDtypeStruct((2, s, KV_WIDTH), F32),
        jax.ShapeDtypeStruct((s, LRU_WIDTH), F32),
    ]
    out_specs = [
        pl.BlockSpec((NSA_WIDTH, tm), lambda i: (0, i)),
        pl.BlockSpec((NSA_WIDTH, tm), lambda i: (0, i)),
        pl.BlockSpec((32, tm), lambda i: (0, i)),
        pl.BlockSpec((g, tm, 256), lambda i: (0, i, 0)),
        pl.BlockSpec((g, tm // KV_CHUNK, VT_ROWS, KV_CHUNK), lambda i: (0, i, 0, 0)),
        pl.BlockSpec((g, tm // 128, VT_ROWS, 128), lambda i: (0, i, 0, 0)),
        pl.BlockSpec((g, tm, 128), lambda i: (0, i, 0)),
        pl.BlockSpec((g, tm // 128, VT_ROWS, 128), lambda i: (0, i, 0, 0)),
        pl.BlockSpec((2, tm, KV_WIDTH), lambda i: (0, i, 0)),
        pl.BlockSpec((tm, LRU_WIDTH), lambda i: (i, 0)),
    ]
    return pl.pallas_call(
        _front_kernel,
        grid=(s // tm,),
        in_specs=[pl.BlockSpec((tm, D_MODEL), lambda i: (i, 0)),
                  pl.BlockSpec((1, D_MODEL), lambda i: (0, 0)),
                  pl.BlockSpec(w_front.shape, lambda i: (0, 0)),
                  pl.BlockSpec((NSA_WIDTH, 1), lambda i: (0, 0)),
                  pl.BlockSpec((3, 128), lambda i: (0, 0))],
        out_specs=out_specs,
        out_shape=outs,
        compiler_params=_cparams(1),
        name="front",
    )(x2, norm_gain.reshape(1, D_MODEL), w_front, qgain_col, kgain_rows)


def _compress_kernel(kvc_ref, w1_ref, b1_ref, w2_ref, pe_ref, kg_ref, kc_ref, vct_ref):
    ncp = kc_ref.shape[1]
    half = CMP_STRIDE * HEAD_DIM
    lane = lax.broadcasted_iota(jnp.int32, (ncp, 128), 1)
    lo = lane < HEAD_DIM

    def chunk_rows(j, g):
        cols = []
        for pair in range(CMP_STRIDE // 2):
            even = kvc_ref[j, pl.ds(2 * pair, ncp, stride=CMP_STRIDE), :]
            odd = kvc_ref[j, pl.ds(2 * pair + 1, ncp, stride=CMP_STRIDE), :]
            if g == 0:
                cols.append(jnp.where(lo, even, pltpu.roll(odd, HEAD_DIM, 1)))
            else:
                cols.append(jnp.where(lo, pltpu.roll(even, HEAD_DIM, 1), odd))
        return jnp.concatenate(cols, axis=1)

    def phi(j, g):
        x = chunk_rows(j, g).astype(BF16)
        w1 = w1_ref[j].astype(BF16)
        a = _dot(x, w1[:half])
        b = _dot(x, w1[half:])
        b_next = pltpu.roll(b, ncp - 1, 0)
        pe8 = jnp.broadcast_to(pe_ref[j], (8, 2 * half)).astype(BF16)
        pe_term = _dot(pe8, w1)[0:1]
        hid = a + b_next + pe_term + b1_ref[j]
        act = (hid * _sigmoid(hid)).astype(BF16)
        return _dot(act, w2_ref[j].astype(BF16))

    ones_cols = jnp.where((lane == HEAD_DIM) | (lane == HEAD_DIM + 1), 1.0, 0.0)
    for g in range(N_GROUPS):
        kc = phi(0, g)
        ms = jnp.sum(kc * kc, axis=-1, keepdims=True) / HEAD_DIM
        kc = kc * lax.rsqrt(ms + EPS) * kg_ref[...]
        hi = kc.astype(BF16).astype(F32)
        lo_part = (kc - hi).astype(BF16).astype(F32)
        kc_ref[g, :, 0:128] = (hi + pltpu.roll(lo_part, HEAD_DIM, 1)).astype(BF16)
        kc_ref[g, :, 128:256] = jnp.where(lo, hi, ones_cols).astype(BF16)
        vt = phi(1, g).T
        row_t = lax.broadcasted_iota(jnp.int32, vt.shape, 0)
        vct_ref[g] = jnp.where(row_t == HEAD_DIM, 1.0, vt).astype(BF16)


def _compress(kvc, w1, b1, w2p, pe_flat, kgain_row):
    s = kvc.shape[1]
    g = N_GROUPS
    ncp = s // CMP_STRIDE
    full = lambda a: pl.BlockSpec(a.shape, lambda i: (0,) * a.ndim)
    return pl.pallas_call(
        _compress_kernel,
        grid=(1,),
        in_specs=[pl.BlockSpec(kvc.shape, lambda i: (0, 0, 0), pipeline_mode=pl.Buffered(1)),
                  full(w1), full(b1), full(w2p), full(pe_flat), full(kgain_row)],
        out_specs=[pl.BlockSpec((g, ncp, 256), lambda i: (0, 0, 0)),
                   pl.BlockSpec((g, 128, ncp), lambda i: (0, 0, 0))],
        out_shape=[jax.ShapeDtypeStruct((g, ncp, 256), BF16),
                   jax.ShapeDtypeStruct((g, 128, ncp), BF16)],
        compiler_params=_cparams(1),
        name="compress",
    )(kvc, w1, b1, w2p, pe_flat, kgain_row)


def _heads_to_lanes(q):
    return jnp.concatenate([q[r * HEAD_DIM:(r + 1) * HEAD_DIM] for r in range(GROUP_SIZE)], axis=1)


def _store_heads(out_ref, o):
    for r in range(GROUP_SIZE):
        out_ref[r * HEAD_DIM:(r + 1) * HEAD_DIM, :] = o[0:HEAD_DIM, r * Q_BLOCK:(r + 1) * Q_BLOCK]


def _bias_tile(w_ref, off, n, head0=0):
    return jnp.concatenate([w_ref[head0 + r, pl.ds(off, n), :] for r in range(GROUP_SIZE)], axis=1)


def _cmp_kernel(qhi_ref, qlo_ref, kc_ref, vct_ref, wc_ref, ext_ref, ocmp_ref, neg_ref,
                s_ref, imp_ref, cand_ref):
    qb = pl.program_id(0)
    ncp = kc_ref.shape[1]
    nsbp = neg_ref.shape[1]
    gr = GROUP_SIZE * HEAD_DIM
    ws = pl.multiple_of(jnp.maximum(8 * qb - 16, 0), 8)
    off = pl.multiple_of(jnp.maximum(16 - 8 * qb, 0), 8)
    lim = ws + CMP_NEAR
    n_cls = min(8, ncp // 128)
    per_cls = ncp // n_cls

    def attend(nrows):
        nblk = nrows // SEL_PER_CMP
        band = nrows - per_cls
        rows = lax.broadcasted_iota(jnp.int32, (per_cls, GQ), 0) + band
        blk = lax.broadcasted_iota(jnp.int32, (nblk, Q_BLOCK), 0)
        lane = lax.broadcasted_iota(jnp.int32, (nblk, Q_BLOCK), 1)
        cur = 2 * qb + jnp.where(lane >= SEL_BLOCK, 1, 0)
        forced = jnp.where(blk == 0, 1.0, jnp.where(blk == cur, 1.0, jnp.where(blk == cur - 1, 1.0, 0.0)))
        carry = []
        for g in range(N_GROUPS):
            qhi = qhi_ref[g * gr:(g + 1) * gr, :]
            qlo = qlo_ref[g * gr:(g + 1) * gr, :]
            cols = []
            for r in range(GROUP_SIZE):
                a = qhi[r * HEAD_DIM:(r + 1) * HEAD_DIM]
                b = qlo[r * HEAD_DIM:(r + 1) * HEAD_DIM]
                cols.append(jnp.concatenate([a, a, b], axis=0))
            qcat = jnp.concatenate([jnp.concatenate(cols, axis=1), ext_ref[g]], axis=0)
            s_ref[g, 0:nrows, :] = _dot(kc_ref[g, 0:nrows, :], qcat)
        for g in range(N_GROUPS):
            s_g = s_ref.at[g]
            s_g[pl.ds(ws, CMP_NEAR), :] += _bias_tile(wc_ref, off, CMP_NEAR, g * GROUP_SIZE)
            s_g[band:nrows, :] = jnp.where(rows < lim, s_g[band:nrows, :], NEG)
            s = s_g[0:nrows, :]
            m = jnp.maximum(jnp.max(s, axis=0, keepdims=True), M_INIT)
            p = jnp.exp2(s - m)
            acc = _dot(vct_ref[g, 0:VT_ROWS, 0:nrows], p.astype(BF16))
            l = acc[HEAD_DIM:HEAD_DIM + 1]
            inv_l = jnp.where(l > 0.0, 1.0 / l, 0.0)
            _store_heads(ocmp_ref.at[g * gr:(g + 1) * gr], acc[0:HEAD_DIM] * inv_l)
            imp = None
            for r in range(GROUP_SIZE):
                cs = slice(r * Q_BLOCK, (r + 1) * Q_BLOCK)
                term = p[:, cs] * inv_l[:, cs]
                imp = term if imp is None else imp + term
            imp_g = imp_ref.at[g]
            imp_g[0:nrows, :] = imp
            strided = [imp_g[pl.ds(k, nblk, stride=SEL_PER_CMP), :] for k in range(SEL_PER_CMP)]
            prev_last = jnp.where(blk == 0, 0.0, pltpu.roll(strided[SEL_PER_CMP - 1], 1, 0))
            impb = prev_last + strided[0] + strided[1] + strided[2] + strided[3]
            cand = jnp.where(blk < cur - 1, jnp.where(blk > 0, impb, -1.0), -1.0)
            cand_ref[g, 0:nblk, :] = cand
            carry.append(cand)

        def pick_one(_, cands):
            out = []
            for sc in cands:
                mx = jnp.max(sc, axis=0, keepdims=True)
                hit_blk = jnp.where(sc == mx, jnp.where(mx >= 0.0, blk, nblk), nblk)
                first = jnp.min(hit_blk, axis=0, keepdims=True)
                out.append(jnp.where(blk == first, PICKED, sc))
            return tuple(out)

        def pick_maxima(_, cands):
            out = []
            for sc in cands:
                mx = jnp.max(sc, axis=0, keepdims=True)
                out.append(jnp.where(sc == jnp.where(mx >= 0.0, mx, jnp.inf), PICKED, sc))
            return tuple(out)

        def rounds(pick, cands):
            if nblk * N_GROUPS <= 2 * SUPER:
                return lax.fori_loop(0, N_PICK, pick, tuple(cands))
            return [lax.fori_loop(0, N_PICK, pick, (c,))[0] for c in cands]

        def write_selection(cands):
            for g in range(N_GROUPS):
                picked = cands[g] == PICKED
                sel_neg = jnp.where(picked, 0.0,
                                    jnp.where(blk <= cur, jnp.where(forced > 0.5, 0.0, NEG), NEG))
                neg_ref[g, 0:nblk, :] = sel_neg.astype(BF16)
                if nblk < nsbp:
                    neg_ref[g, nblk:nsbp, :] = jnp.full((nsbp - nblk, Q_BLOCK), NEG, BF16)

        fast = rounds(pick_maxima, carry)
        expected = jnp.clip(cur[0:1, :] - 2, 0, N_PICK).astype(F32)
        excess = None
        for g in range(N_GROUPS):
            count = jnp.sum(jnp.where(fast[g] == PICKED, 1.0, 0.0), axis=0, keepdims=True)
            excess = count - expected if excess is None else jnp.maximum(excess, count - expected)
        tied = jnp.max(excess) > 0.5

        @pl.when(jnp.logical_not(tied))
        def _():
            write_selection(fast)

        @pl.when(tied)
        def _():
            write_selection(rounds(pick_one, [cand_ref[g, 0:nblk, :] for g in range(N_GROUPS)]))

    cls = (lim + per_cls - 1) // per_cls
    for k in range(1, n_cls + 1):
        pl.when(cls == k)(functools.partial(attend, k * per_cls))


def _cmp_attention(qhi_t, qlo_t, kc_cat, vct, wc_t, ext_q, nsbp):
    s = qhi_t.shape[1]
    g = N_GROUPS
    ncp = kc_cat.shape[1]
    full = lambda a: pl.BlockSpec(a.shape, lambda qb: (0,) * a.ndim)
    return pl.pallas_call(
        _cmp_kernel,
        grid=(s // Q_BLOCK,),
        in_specs=[pl.BlockSpec((NSA_WIDTH, Q_BLOCK), lambda qb: (0, qb)),
                  pl.BlockSpec((NSA_WIDTH, Q_BLOCK), lambda qb: (0, qb)),
                  full(kc_cat), full(vct), full(wc_t), full(ext_q)],
        out_specs=[pl.BlockSpec((NSA_WIDTH, Q_BLOCK), lambda qb: (0, qb)),
                   pl.BlockSpec((g, nsbp, Q_BLOCK), lambda qb: (0, 0, qb))],
        out_shape=[jax.ShapeDtypeStruct((NSA_WIDTH, s), F32),
                   jax.ShapeDtypeStruct((g, nsbp, s), BF16)],
        scratch_shapes=[pltpu.VMEM((g, ncp, GQ), F32), pltpu.VMEM((g, ncp, Q_BLOCK), F32),
                        pltpu.VMEM((g, nsbp, Q_BLOCK), F32)],
        compiler_params=_cparams(1),
        name="cmp_select",
    )(qhi_t, qlo_t, kc_cat, vct, wc_t, ext_q)


def _sel_kernel(qhi_ref, neg_ref, kaug_ref, vt512_ref, vt128_ref, ws_ref, ext_ref,
                kwin_ref, vwt_ref, ww_ref, out_ref, owin_ref,
                qaug_ref, m_ref, acc_ref, s0_ref, s1_ref, s2_ref, cm0_ref, cm1_ref, cm2_ref):
    qb = pl.program_id(1)
    nsbp = neg_ref.shape[0]
    nsc = nsbp // SUPER
    qg = _heads_to_lanes(qhi_ref[...])
    qx = jnp.concatenate([qg, ext_ref[...]], axis=0)
    blk = lax.broadcasted_iota(jnp.int32, (SUPER, Q_BLOCK), 0)

    def aug(neg_rows):
        tiled = jnp.concatenate([neg_rows] * GROUP_SIZE, axis=1).astype(BF16)
        return jnp.concatenate([tiled, qx], axis=0)

    for sc in range(nsc):
        neg_rows = neg_ref[sc * SUPER:(sc + 1) * SUPER, :].astype(F32)
        qaug_ref[sc] = aug(jnp.where(blk + sc * SUPER >= 2 * qb - 2, NEG, neg_rows))

    m_ref[...] = jnp.full(m_ref.shape, M_INIT, F32)
    acc_ref[...] = jnp.zeros(acc_ref.shape, F32)

    def update(s, col_max, vt):
        m_old = m_ref[...]
        m_new = jnp.maximum(m_old, col_max)
        p = jnp.exp2(s - m_new).astype(BF16)
        acc_ref[...] = acc_ref[...] * jnp.exp2(m_old - m_new) + _dot(vt, p)
        m_ref[...] = m_new

    n_far = (jnp.maximum(qb - 1, 0) * Q_BLOCK + KV_CHUNK - 1) // KV_CHUNK

    def scores(c, dst_ref, max_ref):
        k = kaug_ref[pl.ds(pl.multiple_of(c * KV_CHUNK, KV_CHUNK), KV_CHUNK), :]
        sc = (c * (KV_CHUNK // SEL_BLOCK)) // SUPER
        s = _dot(k, qaug_ref[sc])
        dst_ref[...] = s
        max_ref[...] = jnp.max(s, axis=0, keepdims=True)

    kb0 = jnp.maximum(qb - 1, 0)
    ws = pl.multiple_of(kb0 * Q_BLOCK, Q_BLOCK)
    off = pl.multiple_of(jnp.where(qb == 0, Q_BLOCK, 0), Q_BLOCK)
    b_lo = 2 * kb0
    sc_lo = pl.multiple_of((b_lo // SUPER) * SUPER, SUPER)
    sc_hi = pl.multiple_of(((b_lo + 3) // SUPER) * SUPER, SUPER)
    neg_lo = neg_ref[pl.ds(sc_lo, SUPER), :].astype(F32)
    neg_hi = neg_ref[pl.ds(sc_hi, SUPER), :].astype(F32)
    near_neg = jnp.where(blk >= SUPER // 2, neg_lo, neg_hi)
    k = kaug_ref[pl.ds(ws, SEL_NEAR), :]
    s_near = _dot(k, aug(near_neg)) + _bias_tile(ws_ref, off, SEL_NEAR)
    vt_near = jnp.concatenate([vt128_ref[kb0], vt128_ref[kb0 + 1]], axis=1)

    s_win = _window_scores(qb, qg, kwin_ref, ww_ref)
    bufs = ((s0_ref, cm0_ref), (s1_ref, cm1_ref), (s2_ref, cm2_ref))
    last = jnp.maximum(n_far - 1, 0)
    scores(0, *bufs[0])
    scores(jnp.minimum(1, last), *bufs[1])
    update(s_near, jnp.max(s_near, axis=0, keepdims=True), vt_near)
    _window_finish(qb, s_win, vwt_ref, owin_ref)

    def far_triple(c):
        for j in range(3):
            scores(jnp.minimum(c + j + 2, last), *bufs[(j + 2) % 3])
            s_ref, cm_ref = bufs[j]
            update(s_ref[...], cm_ref[...], vt512_ref[c + j])

    def far_six(i, carry):
        far_triple(6 * i)
        far_triple(6 * i + 3)
        return carry

    lax.fori_loop(0, n_far // 6, far_six, 0)
    done = (n_far // 6) * 6

    @pl.when(n_far - done >= 3)
    def _():
        far_triple(done)

    done = (n_far // 3) * 3
    for j in range(2):
        @pl.when(n_far - done > j)
        def _():
            s_ref, cm_ref = bufs[j]
            update(s_ref[...], cm_ref[...], vt512_ref[done + j])

    acc = acc_ref[...]
    _store_heads(out_ref, acc[0:HEAD_DIM] * (1.0 / acc[HEAD_DIM:HEAD_DIM + 1]))


def _sel_win_attention(qhi_t, neg_t, kaug, vt512, vt128, ws_t, ext_q, kwin, vwt, ww_t):
    s = qhi_t.shape[1]
    g = N_GROUPS
    nqb = s // Q_BLOCK
    nsbp = neg_t.shape[1]
    gr = GROUP_SIZE * HEAD_DIM
    once = pl.Buffered(1)
    return pl.pallas_call(
        _sel_kernel,
        grid=(g, nqb),
        in_specs=[pl.BlockSpec((gr, Q_BLOCK), lambda gi, qb: (gi, qb)),
                  pl.BlockSpec((None, nsbp, Q_BLOCK), lambda gi, qb: (gi, 0, qb)),
                  pl.BlockSpec((None, s, 256), lambda gi, qb: (gi, 0, 0), pipeline_mode=once),
                  pl.BlockSpec((None, s // KV_CHUNK, VT_ROWS, KV_CHUNK), lambda gi, qb: (gi, 0, 0, 0),
                               pipeline_mode=once),
                  pl.BlockSpec((None, s // 128, VT_ROWS, 128), lambda gi, qb: (gi, 0, 0, 0),
                               pipeline_mode=once),
                  pl.BlockSpec((GROUP_SIZE,) + ws_t.shape[1:], lambda gi, qb: (gi, 0, 0)),
                  pl.BlockSpec((None, HEAD_DIM, GQ), lambda gi, qb: (gi, 0, 0)),
                  pl.BlockSpec((None, s, 128), lambda gi, qb: (gi, 0, 0), pipeline_mode=once),
                  pl.BlockSpec((None, s // 128, VT_ROWS, 128), lambda gi, qb: (gi, 0, 0, 0),
                               pipeline_mode=once),
                  pl.BlockSpec((GROUP_SIZE,) + ww_t.shape[1:], lambda gi, qb: (gi, 0, 0))],
        out_specs=[pl.BlockSpec((gr, Q_BLOCK), lambda gi, qb: (gi, qb)),
                   pl.BlockSpec((gr, Q_BLOCK), lambda gi, qb: (gi, qb))],
        out_shape=[jax.ShapeDtypeStruct((NSA_WIDTH, s), F32),
                   jax.ShapeDtypeStruct((NSA_WIDTH, s), F32)],
        scratch_shapes=[pltpu.VMEM((nsbp // SUPER, 256, GQ), BF16),
                        pltpu.VMEM((1, GQ), F32),
                        pltpu.VMEM((VT_ROWS, GQ), F32),
                        pltpu.VMEM((KV_CHUNK, GQ), F32),
                        pltpu.VMEM((KV_CHUNK, GQ), F32),
                        pltpu.VMEM((KV_CHUNK, GQ), F32),
                        pltpu.VMEM((1, GQ), F32),
                        pltpu.VMEM((1, GQ), F32),
                        pltpu.VMEM((1, GQ), F32)],
        compiler_params=_cparams(2),
        name="sel_win_attention",
    )(qhi_t, neg_t, kaug, vt512, vt128, ws_t, ext_q, kwin, vwt, ww_t)


def _window_scores(qb, qg, kwin_ref, ww_ref):
    kb0 = jnp.maximum(qb - WINDOW // Q_BLOCK, 0)
    ws = pl.multiple_of(kb0 * Q_BLOCK, Q_BLOCK)
    off = pl.multiple_of(jnp.maximum(WINDOW - qb * Q_BLOCK, 0), Q_BLOCK)
    qx = jnp.concatenate([qg, jnp.zeros_like(qg)], axis=0)
    return _dot(kwin_ref[pl.ds(ws, WIN_KEYS), :], qx) + _bias_tile(ww_ref, off, WIN_KEYS)


def _window_finish(qb, s, vwt_ref, out_ref):
    kb0 = jnp.maximum(qb - WINDOW // Q_BLOCK, 0)
    m = jnp.max(s, axis=0, keepdims=True)
    p = jnp.exp2(s - m).astype(BF16)
    vt = jnp.concatenate([vwt_ref[kb0 + j] for j in range(WIN_KEYS // Q_BLOCK)], axis=1)
    acc = _dot(vt, p)
    _store_heads(out_ref, acc[0:HEAD_DIM] * (1.0 / acc[HEAD_DIM:HEAD_DIM + 1]))


def _lru_rows(u_ref, cw_ref, cb_ref, wa_ref, ba_ref, wx_ref, bx_ref, lam_ref, h_ref,
              tail_ref, hprev_ref, uc_ref):
    tm = u_ref.shape[0]

    @pl.when(pl.program_id(0) == 0)
    def _():
        tail_ref[...] = jnp.zeros(tail_ref.shape, F32)
        hprev_ref[...] = jnp.zeros(hprev_ref.shape, F32)

    u = u_ref[...]
    tail = tail_ref[...]
    row8 = lax.broadcasted_iota(jnp.int32, (8, LRU_WIDTH), 0)
    uc = cb_ref[...] + u * cw_ref[CONV_WIDTH - 1:CONV_WIDTH, :]
    uc_head = uc[0:8]
    for j in range(1, CONV_WIDTH):
        w_j = cw_ref[CONV_WIDTH - 1 - j:CONV_WIDTH - j, :]
        sh = pltpu.roll(u, j, 0)
        uc = uc + sh * w_j
        uc_head = uc_head + jnp.where(row8 < j, pltpu.roll(tail, j, 0), sh[0:8]) * w_j
    tail_ref[...] = u[tm - 8:tm]
    uc_ref[...] = uc
    uc_ref[0:8, :] = uc_head
    uc = uc_ref[...]

    ucb = uc.astype(BF16)
    r = _sigmoid(_dot(ucb, wa_ref[...]) + ba_ref[...])
    ig = _sigmoid(_dot(ucb, wx_ref[...]) + bx_ref[...])
    nl = -lam_ref[...]
    softplus = jnp.maximum(nl, 0.0) + jnp.log(1.0 + jnp.exp(-jnp.abs(nl)))
    a = jnp.exp((-LRU_C * r) * softplus)
    b = jnp.sqrt(1.0 - a * a) * (ig * uc)

    a = a.reshape(tm // 8, 8, LRU_WIDTH)
    b = b.reshape(tm // 8, 8, LRU_WIDTH)
    sub = lax.broadcasted_iota(jnp.int32, a.shape, 1)
    for step in (1, 2, 4):
        a_s = pltpu.roll(a, step, 1)
        b_s = pltpu.roll(b, step, 1)
        ok = sub >= step
        b = jnp.where(ok, a * b_s + b, b)
        a = jnp.where(ok, a * a_s, a)
    carry = hprev_ref[...]
    for j in range(tm // 8):
        hj = a[j] * carry + b[j]
        h_ref[8 * j:8 * j + 8, :] = hj
        carry = hj[7:8]
    hprev_ref[...] = carry


OUT_TM = 512


def _out_kernel(x_ref, g_ref, oc_ref, os_ref, ow_ref, brt_ref, u_ref,
                cw_ref, cb_ref, wa_ref, ba_ref, wx_ref, bx_ref, lam_ref,
                wg_ref, wpa_ref, wpb_ref, wo_ref, y_ref,
                h_ref, tail_ref, hprev_ref, uc_ref):
    _lru_rows(u_ref, cw_ref, cb_ref, wa_ref, ba_ref, wx_ref, bx_ref, lam_ref, h_ref,
              tail_ref, hprev_ref, uc_ref)
    h_in = _normed_input(x_ref, g_ref)
    gate = lambda j: _dot(h_in, wg_ref[:, GATE_OFFS[j]:GATE_OFFS[j + 1]])
    mg = gate(2)
    brt = brt_ref[...]
    parts = []
    for h in range(N_HEADS):
        rows = slice(h * HEAD_DIM, (h + 1) * HEAD_DIM)
        acc = None
        for b, o_ref in enumerate((oc_ref, os_ref, ow_ref)):
            term = brt[b * N_HEADS + h:b * N_HEADS + h + 1, :] * o_ref[rows, :]
            acc = term if acc is None else acc + term
        parts.append(acc)
    gn = gate(0)
    ya = jnp.concatenate(parts, axis=0).T * (gn * _sigmoid(gn))
    y_a = _dot(ya.astype(BF16), wpa_ref[...])
    gl = gate(1)
    y_b = _dot((h_ref[...] * (gl * _sigmoid(gl))).astype(BF16), wpb_ref[...])
    m = _sigmoid(mg[:, :D_MODEL]) * y_a + _sigmoid(mg[:, D_MODEL:]) * y_b
    y_ref[...] = x_ref[...] + _dot(m.astype(BF16), wo_ref[...])


def _output(x2, norm_gain, oc_t, os_t, ow_t, br_t, u_lru, lru_params, w_gate, wpa, wpb, wo):
    s = x2.shape[0]
    tm = OUT_TM
    rows = lambda n: pl.BlockSpec((tm, n), lambda i: (i, 0))
    cols = lambda n: pl.BlockSpec((n, tm), lambda i: (0, i))
    full = lambda a: pl.BlockSpec(a.shape, lambda i: (0, 0))
    return pl.pallas_call(
        _out_kernel,
        grid=(s // tm,),
        in_specs=[rows(D_MODEL), pl.BlockSpec((1, D_MODEL), lambda i: (0, 0)),
                  cols(NSA_WIDTH), cols(NSA_WIDTH), cols(NSA_WIDTH), cols(32), rows(LRU_WIDTH)]
        + [full(p) for p in lru_params]
        + [full(w_gate), full(wpa), full(wpb), full(wo)],
        out_specs=rows(D_MODEL),
        out_shape=jax.ShapeDtypeStruct((s, D_MODEL), F32),
        scratch_shapes=[pltpu.VMEM((tm, LRU_WIDTH), F32),
                        pltpu.VMEM((8, LRU_WIDTH), F32),
                        pltpu.VMEM((1, LRU_WIDTH), F32),
                        pltpu.VMEM((tm, LRU_WIDTH), F32)],
        compiler_params=_cparams(1),
        name="output",
    )(x2, norm_gain.reshape(1, D_MODEL), oc_t, os_t, ow_t, br_t, u_lru, *lru_params,
      w_gate, wpa, wpb, wo)


def _t5_bucket_table(n_dist):
    n = np.arange(n_dist)
    max_exact = N_BUCKETS // 2
    nf = np.maximum(n, 1).astype(np.float32)
    large = max_exact + (np.log(nf / np.float32(max_exact)) / np.float32(math.log(MAX_DISTANCE / max_exact))
                         * np.float32(N_BUCKETS - max_exact)).astype(np.int32)
    return np.where(n < max_exact, n, np.minimum(large, N_BUCKETS - 1))


def _bias_tiles_kernel(relb_ref, bw_ref, bs_ref, bc_ref, ww_ref, ws_ref, wc_ref):
    h = pl.program_id(0)
    far = relb_ref[N_BUCKETS - 1, h]
    for bk_ref, out_ref, rel in ((bw_ref, ww_ref, False), (bs_ref, ws_ref, True), (bc_ref, wc_ref, True)):
        bk = bk_ref[...]
        acc = jnp.full(bk.shape, NEG, F32)
        for b in range(N_BUCKETS):
            val = relb_ref[b, h] - far if rel else relb_ref[b, h]
            acc = jnp.where(bk == b, val * LOG2E, acc)
        out_ref[...] = acc


def _bias_tables(rel_bias):
    bucket = _t5_bucket_table(2 * WINDOW)
    i = np.arange(Q_BLOCK)[None, :]

    def index_tile(dist, valid):
        return jnp.asarray(np.where(valid, bucket[np.clip(dist, 0, bucket.size - 1)], -1).astype(np.int32))

    u = np.arange(WINDOW + WIN_KEYS)[:, None]
    d_win = i - u + WINDOW
    bw = index_tile(d_win, (d_win >= 0) & (d_win < WINDOW))
    u = np.arange(SEL_NEAR + Q_BLOCK)[:, None]
    d_sel = i - u + Q_BLOCK
    bs = index_tile(d_sel, d_sel >= 0)
    u = np.arange(CMP_NEAR + 16)[:, None]
    d_cmp = i - CMP_STRIDE * (u - 16) - (CMP_BLOCK - 1)
    bc = index_tile(d_cmp, d_cmp >= 0)
    full = lambda a: pl.BlockSpec(a.shape, lambda h: (0, 0))
    per_head = lambda a: pl.BlockSpec((None,) + a.shape, lambda h: (h, 0, 0))
    ww, ws, wc = pl.pallas_call(
        _bias_tiles_kernel,
        grid=(N_HEADS,),
        in_specs=[pl.BlockSpec(memory_space=pltpu.SMEM), full(bw), full(bs), full(bc)],
        out_specs=[per_head(bw), per_head(bs), per_head(bc)],
        out_shape=[jax.ShapeDtypeStruct((N_HEADS,) + a.shape, F32) for a in (bw, bs, bc)],
        compiler_params=_cparams(1),
        name="bias_tiles",
    )(rel_bias, bw, bs, bc)
    b_far = rel_bias[N_BUCKETS - 1] * LOG2E
    hi = b_far.astype(BF16)
    lo = (b_far - hi.astype(F32)).astype(BF16)
    ext = jnp.zeros((N_HEADS, HEAD_DIM, Q_BLOCK), BF16)
    ext = ext.at[:, 0, :].set(hi[:, None]).at[:, 1, :].set(lo[:, None])
    ext = ext.reshape(N_GROUPS, GROUP_SIZE, HEAD_DIM, Q_BLOCK).transpose(0, 2, 1, 3)
    return ww, ws, wc, ext.reshape(N_GROUPS, HEAD_DIM, GQ)


def _block_diag(w):
    n, d, e = w.shape
    eye = jnp.eye(n, dtype=w.dtype)
    return (eye[:, None, :, None] * w[:, :, None, :]).reshape(n * d, n * e)


def kernel(x, norm_gain, w_in, q_norm_gain, k_norm_gain, cmp_pe, cmp_w1, cmp_b1, cmp_w2, rel_bias,
           conv_w, conv_b, lru_wa, lru_ba, lru_wx, lru_bx, lru_lambda, w_proj_a, w_proj_b, w_out):
    bsz, s, _ = x.shape
    assert bsz == 1 and s % 1024 == 0 and s >= 1024
    x2 = x.reshape(s, D_MODEL)
    nsb = s // SEL_BLOCK
    nsbp = -(-nsb // SUPER) * SUPER

    o = IN_OFFS
    w16 = w_in.astype(BF16)
    pad = jnp.zeros((D_MODEL, BR_PAD - 3 * N_HEADS), BF16)
    w_front = jnp.concatenate([w16[:, o[0]:o[2]], w16[:, o[3]:o[4]], pad, w16[:, o[4]:o[5]]], axis=1)
    w_gate = jnp.concatenate([w16[:, o[2]:o[3]], w16[:, o[5]:o[7]]], axis=1)

    scale = HEAD_DIM ** -0.5 * LOG2E
    qgain_col = jnp.tile(q_norm_gain * scale, N_HEADS).reshape(NSA_WIDTH, 1)
    kgain_rows = jnp.tile(k_norm_gain, (1, N_GROUPS))
    (qhi_t, qlo_t, br_t, kaug, vt512, vt128, kwin, vwt, kvc, u_lru) = _front(
        x2, norm_gain, w_front, qgain_col, kgain_rows)

    w2p = jnp.pad(cmp_w2, ((0, 0), (0, 0), (0, 128 - HEAD_DIM)))
    kc_gain = jnp.pad(k_norm_gain[0], (0, 128 - HEAD_DIM)).reshape(1, 128)
    kc_cat, vct = _compress(kvc, cmp_w1, cmp_b1.reshape(2, 1, CMP_HIDDEN), w2p,
                            cmp_pe.reshape(2, 1, CMP_BLOCK * HEAD_DIM), kc_gain)

    ww_t, ws_t, wc_t, ext_q = _bias_tables(rel_bias)
    oc_t, neg_t = _cmp_attention(qhi_t, qlo_t, kc_cat, vct, wc_t, ext_q, nsbp)
    os_t, ow_t = _sel_win_attention(qhi_t, neg_t, kaug, vt512, vt128, ws_t, ext_q, kwin, vwt, ww_t)

    row = lambda v: v.reshape(1, LRU_WIDTH)
    lru_params = (conv_w, row(conv_b), _block_diag(lru_wa).astype(BF16), row(lru_ba),
                  _block_diag(lru_wx).astype(BF16), row(lru_bx), row(lru_lambda))
    y = _output(x2, norm_gain, oc_t, os_t, ow_t, br_t, u_lru, lru_params, w_gate,
                w_proj_a.astype(BF16), w_proj_b.astype(BF16), w_out.astype(BF16))
    return y.reshape(bsz, s, D_MODEL)
```

```python
import functools
import math

import numpy as np
import jax
import jax.numpy as jnp
from jax import lax
from jax.experimental import pallas as pl
from jax.experimental.pallas import tpu as pltpu

F32 = jnp.float32
BF16 = jnp.bfloat16

D_MODEL = 1024
N_HEADS = 8
N_GROUPS = 2
GROUP_SIZE = N_HEADS // N_GROUPS
HEAD_DIM = 64
NSA_WIDTH = N_HEADS * HEAD_DIM
KV_WIDTH = N_GROUPS * HEAD_DIM
CMP_STRIDE = 16
CMP_BLOCK = 32
CMP_HIDDEN = 256
SEL_BLOCK = 64
SEL_PER_CMP = SEL_BLOCK // CMP_STRIDE
N_SELECT = 16
WINDOW = 512
Q_BLOCK = 128
LRU_WIDTH = 512
LRU_BLOCKS = 8
CONV_WIDTH = 4
LRU_C = 8.0
N_BUCKETS = 32
MAX_DISTANCE = 128
EPS = 1e-6
NEG = -1e30
M_INIT = -5e29
LOG2E = 1.4426950408889634

GQ = GROUP_SIZE * Q_BLOCK
SUPER = 128
KV_CHUNK = 512
SEL_NEAR = 2 * Q_BLOCK
WIN_KEYS = WINDOW + Q_BLOCK
CMP_NEAR = 24
N_PICK = N_SELECT - 3
PICKED = -2.0
VT_ROWS = 80

IN_OFFS = tuple(int(v) for v in np.cumsum(
    (0, NSA_WIDTH, 6 * KV_WIDTH, NSA_WIDTH, 3 * N_HEADS, LRU_WIDTH, LRU_WIDTH, 2 * D_MODEL)))
BR_PAD = 128
FRONT_OFFS = tuple(int(v) for v in np.cumsum((0, NSA_WIDTH, 6 * KV_WIDTH, BR_PAD, LRU_WIDTH)))
GATE_OFFS = tuple(int(v) for v in np.cumsum((0, NSA_WIDTH, LRU_WIDTH, 2 * D_MODEL)))

VMEM_LIMIT = 56 * 1024 * 1024


def _cparams(n_axes):
    return pltpu.CompilerParams(dimension_semantics=("arbitrary",) * n_axes,
                                vmem_limit_bytes=VMEM_LIMIT)


def _dot(a, b):
    return jnp.dot(a, b, preferred_element_type=F32)


def _sigmoid(x):
    return 0.5 * jnp.tanh(0.5 * x) + 0.5


FRONT_TM = 512


def _normed_input(x_ref, g_ref):
    x = x_ref[...]
    ms = jnp.mean(x * x, axis=-1, keepdims=True)
    return (x * lax.rsqrt(ms + EPS) * g_ref[...]).astype(BF16)


def _group_rms(k, gain_row):
    sq = k * k
    lane = lax.broadcasted_iota(jnp.int32, k.shape, 1)
    lo = lane < HEAD_DIM
    s0 = jnp.sum(jnp.where(lo, sq, 0.0), axis=-1, keepdims=True)
    s1 = jnp.sum(jnp.where(lo, 0.0, sq), axis=-1, keepdims=True)
    inv = jnp.where(lo, lax.rsqrt(s0 / HEAD_DIM + EPS), lax.rsqrt(s1 / HEAD_DIM + EPS))
    return k * inv * gain_row


def _front_kernel(x_ref, g_ref, w_ref, qg_ref, kg_ref,
                  qhi_ref, qlo_ref, brt_ref, kaug_ref, vt512_ref, vt128_ref,
                  kwin_ref, vwt_ref, kvc_ref, u_ref):
    i = pl.program_id(0)
    tm = FRONT_TM
    h_in = _normed_input(x_ref, g_ref)
    proj = lambda a, b: _dot(h_in, w_ref[:, a:b])
    kv = proj(FRONT_OFFS[1], FRONT_OFFS[2])
    piece = lambda j: kv[:, j * KV_WIDTH:(j + 1) * KV_WIDTH]
    lane = lax.broadcasted_iota(jnp.int32, (tm, 128), 1)
    row = lax.broadcasted_iota(jnp.int32, (tm, 128), 0) + i * tm
    lo = lane < HEAD_DIM
    ones_cols = jnp.where((lane == HEAD_DIM) | (lane == HEAD_DIM + 1), 1.0, 0.0)
    onehot = jnp.where(lane == (row // SEL_BLOCK) % SUPER, 1.0, 0.0).astype(BF16)
    kvc_ref[0] = piece(0)
    kvc_ref[1] = piece(1)
    kslc = _group_rms(piece(2), kg_ref[1:2, :])
    kwin = _group_rms(piece(4), kg_ref[2:3, :])
    vslt = piece(3).T
    vwit = piece(5).T
    row_t = lax.broadcasted_iota(jnp.int32, (VT_ROWS - HEAD_DIM, tm), 0)
    ones_rows = jnp.where(row_t == 0, 1.0, 0.0)
    for g in range(N_GROUPS):
        sh = lambda a: a if g == 0 else pltpu.roll(a, HEAD_DIM, 1)
        kaug_ref[g, :, 0:128] = onehot
        kaug_ref[g, :, 128:256] = jnp.where(lo, sh(kslc), ones_cols).astype(BF16)
        kwin_ref[g] = jnp.where(lo, sh(kwin), 0.0).astype(BF16)
        vs = jnp.concatenate([vslt[g * HEAD_DIM:(g + 1) * HEAD_DIM], ones_rows], axis=0).astype(BF16)
        vw = jnp.concatenate([vwit[g * HEAD_DIM:(g + 1) * HEAD_DIM], ones_rows], axis=0).astype(BF16)
        for j in range(tm // KV_CHUNK):
            vt512_ref[g, j] = vs[:, j * KV_CHUNK:(j + 1) * KV_CHUNK]
        for j in range(tm // 128):
            vt128_ref[g, j] = vs[:, j * 128:(j + 1) * 128]
            vwt_ref[g, j] = vw[:, j * 128:(j + 1) * 128]

    qt = proj(FRONT_OFFS[0], FRONT_OFFS[1]).T
    for h in range(N_HEADS):
        blk = qt[h * HEAD_DIM:(h + 1) * HEAD_DIM]
        ms = jnp.mean(blk * blk, axis=0, keepdims=True)
        qn = blk * lax.rsqrt(ms + EPS) * qg_ref[h * HEAD_DIM:(h + 1) * HEAD_DIM, :]
        hi = qn.astype(BF16)
        qhi_ref[h * HEAD_DIM:(h + 1) * HEAD_DIM, :] = hi
        qlo_ref[h * HEAD_DIM:(h + 1) * HEAD_DIM, :] = (qn - hi.astype(F32)).astype(BF16)
    brt_ref[...] = _sigmoid(proj(FRONT_OFFS[2], FRONT_OFFS[3])).T[:32]
    u_ref[...] = proj(FRONT_OFFS[3], FRONT_OFFS[4])


def _front(x2, norm_gain, w_front, qgain_col, kgain_rows):
    s = x2.shape[0]
    tm = FRONT_TM
    g = N_GROUPS
    outs = [
        jax.ShapeDtypeStruct((NSA_WIDTH, s), BF16),
        jax.ShapeDtypeStruct((NSA_WIDTH, s), BF16),
        jax.ShapeDtypeStruct((32, s), F32),
        jax.ShapeDtypeStruct((g, s, 256), BF16),
        jax.ShapeDtypeStruct((g, s // KV_CHUNK, VT_ROWS, KV_CHUNK), BF16),
        jax.ShapeDtypeStruct((g, s // 128, VT_ROWS, 128), BF16),
        jax.ShapeDtypeStruct((g, s, 128), BF16),
        jax.ShapeDtypeStruct((g, s // 128, VT_ROWS, 128), BF16),
        jax.ShapeDtypeStruct((2, s, KV_WIDTH), F32),
        jax.ShapeDtypeStruct((s, LRU_WIDTH), F32),
    ]
    out_specs = [
        pl.BlockSpec((NSA_WIDTH, tm), lambda i: (0, i)),
        pl.BlockSpec((NSA_WIDTH, tm), lambda i: (0, i)),
        pl.BlockSpec((32, tm), lambda i: (0, i)),
        pl.BlockSpec((g, tm, 256), lambda i: (0, i, 0)),
        pl.BlockSpec((g, tm // KV_CHUNK, VT_ROWS, KV_CHUNK), lambda i: (0, i, 0, 0)),
        pl.BlockSpec((g, tm // 128, VT_ROWS, 128), lambda i: (0, i, 0, 0)),
        pl.BlockSpec((g, tm, 128), lambda i: (0, i, 0)),
        pl.BlockSpec((g, tm // 128, VT_ROWS, 128), lambda i: (0, i, 0, 0)),
        pl.BlockSpec((2, tm, KV_WIDTH), lambda i: (0, i, 0)),
        pl.BlockSpec((tm, LRU_WIDTH), lambda i: (i, 0)),
    ]
    return pl.pallas_call(
        _front_kernel,
        grid=(s // tm,),
        in_specs=[pl.BlockSpec((tm, D_MODEL), lambda i: (i, 0)),
                  pl.BlockSpec((1, D_MODEL), lambda i: (0, 0)),
                  pl.BlockSpec(w_front.shape, lambda i: (0, 0)),
                  pl.BlockSpec((NSA_WIDTH, 1), lambda i: (0, 0)),
                  pl.BlockSpec((3, 128), lambda i: (0, 0))],
        out_specs=out_specs,
        out_shape=outs,
        compiler_params=_cparams(1),
        name="front",
    )(x2, norm_gain.reshape(1, D_MODEL), w_front, qgain_col, kgain_rows)


def _compress_kernel(kvc_ref, w1_ref, b1_ref, w2_ref, pe_ref, kg_ref, kc_ref, vct_ref):
    ncp = kc_ref.shape[1]
    half = CMP_STRIDE * HEAD_DIM
    lane = lax.broadcasted_iota(jnp.int32, (ncp, 128), 1)
    lo = lane < HEAD_DIM

    def chunk_rows(j, g):
        cols = []
        for pair in range(CMP_STRIDE // 2):
            even = kvc_ref[j, pl.ds(2 * pair, ncp, stride=CMP_STRIDE), :]
            odd = kvc_ref[j, pl.ds(2 * pair + 1, ncp, stride=CMP_STRIDE), :]
            if g == 0:
                cols.append(jnp.where(lo, even, pltpu.roll(odd, HEAD_DIM, 1)))
            else:
                cols.append(jnp.where(lo, pltpu.roll(even, HEAD_DIM, 1), odd))
        return jnp.concatenate(cols, axis=1)

    def phi(j, g):
        x = chunk_rows(j, g).astype(BF16)
        w1 = w1_ref[j].astype(BF16)
        a = _dot(x, w1[:half])
        b = _dot(x, w1[half:])
        b_next = pltpu.roll(b, ncp - 1, 0)
        pe8 = jnp.broadcast_to(pe_ref[j], (8, 2 * half)).astype(BF16)
        pe_term = _dot(pe8, w1)[0:1]
        hid = a + b_next + pe_term + b1_ref[j]
        act = (hid * _sigmoid(hid)).astype(BF16)
        return _dot(act, w2_ref[j].astype(BF16))

    ones_cols = jnp.where((lane == HEAD_DIM) | (lane == HEAD_DIM + 1), 1.0, 0.0)
    for g in range(N_GROUPS):
        kc = phi(0, g)
        ms = jnp.sum(kc * kc, axis=-1, keepdims=True) / HEAD_DIM
        kc = kc * lax.rsqrt(ms + EPS) * kg_ref[...]
        hi = kc.astype(BF16).astype(F32)
        lo_part = (kc - hi).astype(BF16).astype(F32)
        kc_ref[g, :, 0:128] = (hi + pltpu.roll(lo_part, HEAD_DIM, 1)).astype(BF16)
        kc_ref[g, :, 128:256] = jnp.where(lo, hi, ones_cols).astype(BF16)
        vt = phi(1, g).T
        row_t = lax.broadcasted_iota(jnp.int32, vt.shape, 0)
        vct_ref[g] = jnp.where(row_t == HEAD_DIM, 1.0, vt).astype(BF16)


def _compress(kvc, w1, b1, w2p, pe_flat, kgain_row):
    s = kvc.shape[1]
    g = N_GROUPS
    ncp = s // CMP_STRIDE
    full = lambda a: pl.BlockSpec(a.shape, lambda i: (0,) * a.ndim)
    return pl.pallas_call(
        _compress_kernel,
        grid=(1,),
        in_specs=[pl.BlockSpec(kvc.shape, lambda i: (0, 0, 0), pipeline_mode=pl.Buffered(1)),
                  full(w1), full(b1), full(w2p), full(pe_flat), full(kgain_row)],
        out_specs=[pl.BlockSpec((g, ncp, 256), lambda i: (0, 0, 0)),
                   pl.BlockSpec((g, 128, ncp), lambda i: (0, 0, 0))],
        out_shape=[jax.ShapeDtypeStruct((g, ncp, 256), BF16),
                   jax.ShapeDtypeStruct((g, 128, ncp), BF16)],
        compiler_params=_cparams(1),
        name="compress",
    )(kvc, w1, b1, w2p, pe_flat, kgain_row)


def _heads_to_lanes(q):
    return jnp.concatenate([q[r * HEAD_DIM:(r + 1) * HEAD_DIM] for r in range(GROUP_SIZE)], axis=1)


def _store_heads(out_ref, o):
    for r in range(GROUP_SIZE):
        out_ref[r * HEAD_DIM:(r + 1) * HEAD_DIM, :] = o[0:HEAD_DIM, r * Q_BLOCK:(r + 1) * Q_BLOCK]


def _bias_tile(w_ref, off, n, head0=0):
    return jnp.concatenate([w_ref[head0 + r, pl.ds(off, n), :] for r in range(GROUP_SIZE)], axis=1)


def _cmp_kernel(qhi_ref, qlo_ref, kc_ref, vct_ref, wc_ref, ext_ref, ocmp_ref, neg_ref,
                s_ref, imp_ref, cand_ref):
    qb = pl.program_id(0)
    ncp = kc_ref.shape[1]
    nsbp = neg_ref.shape[1]
    gr = GROUP_SIZE * HEAD_DIM
    ws = pl.multiple_of(jnp.maximum(8 * qb - 16, 0), 8)
    off = pl.multiple_of(jnp.maximum(16 - 8 * qb, 0), 8)
    lim = ws + CMP_NEAR
    n_cls = min(8, ncp // 128)
    per_cls = ncp // n_cls

    def attend(nrows):
        nblk = nrows // SEL_PER_CMP
        band = nrows - per_cls
        rows = lax.broadcasted_iota(jnp.int32, (per_cls, GQ), 0) + band
        blk = lax.broadcasted_iota(jnp.int32, (nblk, Q_BLOCK), 0)
        lane = lax.broadcasted_iota(jnp.int32, (nblk, Q_BLOCK), 1)
        cur = 2 * qb + jnp.where(lane >= SEL_BLOCK, 1, 0)
        forced = jnp.where(blk == 0, 1.0, jnp.where(blk == cur, 1.0, jnp.where(blk == cur - 1, 1.0, 0.0)))
        carry = []
        for g in range(N_GROUPS):
            qhi = qhi_ref[g * gr:(g + 1) * gr, :]
            qlo = qlo_ref[g * gr:(g + 1) * gr, :]
            cols = []
            for r in range(GROUP_SIZE):
                a = qhi[r * HEAD_DIM:(r + 1) * HEAD_DIM]
                b = qlo[r * HEAD_DIM:(r + 1) * HEAD_DIM]
                cols.append(jnp.concatenate([a, a, b], axis=0))
            qcat = jnp.concatenate([jnp.concatenate(cols, axis=1), ext_ref[g]], axis=0)
            s_ref[g, 0:nrows, :] = _dot(kc_ref[g, 0:nrows, :], qcat)
        for g in range(N_GROUPS):
            s_g = s_ref.at[g]
            s_g[pl.ds(ws, CMP_NEAR), :] += _bias_tile(wc_ref, off, CMP_NEAR, g * GROUP_SIZE)
            s_g[band:nrows, :] = jnp.where(rows < lim, s_g[band:nrows, :], NEG)
            s = s_g[0:nrows, :]
            m = jnp.maximum(jnp.max(s, axis=0, keepdims=True), M_INIT)
            p = jnp.exp2(s - m)
            acc = _dot(vct_ref[g, 0:VT_ROWS, 0:nrows], p.astype(BF16))
            l = acc[HEAD_DIM:HEAD_DIM + 1]
            inv_l = jnp.where(l > 0.0, 1.0 / l, 0.0)
            _store_heads(ocmp_ref.at[g * gr:(g + 1) * gr], acc[0:HEAD_DIM] * inv_l)
            imp = None
            for r in range(GROUP_SIZE):
                cs = slice(r * Q_BLOCK, (r + 1) * Q_BLOCK)
                term = p[:, cs] * inv_l[:, cs]
                imp = term if imp is None else imp + term
            imp_g = imp_ref.at[g]
            imp_g[0:nrows, :] = imp
            strided = [imp_g[pl.ds(k, nblk, stride=SEL_PER_CMP), :] for k in range(SEL_PER_CMP)]
            prev_last = jnp.where(blk == 0, 0.0, pltpu.roll(strided[SEL_PER_CMP - 1], 1, 0))
            impb = prev_last + strided[0] + strided[1] + strided[2] + strided[3]
            cand = jnp.where(blk < cur - 1, jnp.where(blk > 0, impb, -1.0), -1.0)
            cand_ref[g, 0:nblk, :] = cand
            carry.append(cand)

        def pick_one(_, cands):
            out = []
            for sc in cands:
                mx = jnp.max(sc, axis=0, keepdims=True)
                hit_blk = jnp.where(sc == mx, jnp.where(mx >= 0.0, blk, nblk), nblk)
                first = jnp.min(hit_blk, axis=0, keepdims=True)
                out.append(jnp.where(blk == first, PICKED, sc))
            return tuple(out)

        def pick_maxima(_, cands):
            out = []
            for sc in cands:
                mx = jnp.max(sc, axis=0, keepdims=True)
                out.append(jnp.where(sc == jnp.where(mx >= 0.0, mx, jnp.inf), PICKED, sc))
            return tuple(out)

        def rounds(pick, cands):
            if nblk * N_GROUPS <= 2 * SUPER:
                return lax.fori_loop(0, N_PICK, pick, tuple(cands))
            return [lax.fori_loop(0, N_PICK, pick, (c,))[0] for c in cands]

        def write_selection(cands):
            for g in range(N_GROUPS):
                picked = cands[g] == PICKED
                sel_neg = jnp.where(picked, 0.0,
                                    jnp.where(blk <= cur, jnp.where(forced > 0.5, 0.0, NEG), NEG))
                neg_ref[g, 0:nblk, :] = sel_neg.astype(BF16)
                if nblk < nsbp:
                    neg_ref[g, nblk:nsbp, :] = jnp.full((nsbp - nblk, Q_BLOCK), NEG, BF16)

        fast = rounds(pick_maxima, carry)
        expected = jnp.clip(cur[0:1, :] - 2, 0, N_PICK).astype(F32)
        excess = None
        for g in range(N_GROUPS):
            count = jnp.sum(jnp.where(fast[g] == PICKED, 1.0, 0.0), axis=0, keepdims=True)
            excess = count - expected if excess is None else jnp.maximum(excess, count - expected)
        tied = jnp.max(excess) > 0.5

        @pl.when(jnp.logical_not(tied))
        def _():
            write_selection(fast)

        @pl.when(tied)
        def _():
            write_selection(rounds(pick_one, [cand_ref[g, 0:nblk, :] for g in range(N_GROUPS)]))

    cls = (lim + per_cls - 1) // per_cls
    for k in range(1, n_cls + 1):
        pl.when(cls == k)(functools.partial(attend, k * per_cls))


def _cmp_attention(qhi_t, qlo_t, kc_cat, vct, wc_t, ext_q, nsbp):
    s = qhi_t.shape[1]
    g = N_GROUPS
    ncp = kc_cat.shape[1]
    full = lambda a: pl.BlockSpec(a.shape, lambda qb: (0,) * a.ndim)
    return pl.pallas_call(
        _cmp_kernel,
        grid=(s // Q_BLOCK,),
        in_specs=[pl.BlockSpec((NSA_WIDTH, Q_BLOCK), lambda qb: (0, qb)),
                  pl.BlockSpec((NSA_WIDTH, Q_BLOCK), lambda qb: (0, qb)),
                  full(kc_cat), full(vct), full(wc_t), full(ext_q)],
        out_specs=[pl.BlockSpec((NSA_WIDTH, Q_BLOCK), lambda qb: (0, qb)),
                   pl.BlockSpec((g, nsbp, Q_BLOCK), lambda qb: (0, 0, qb))],
        out_shape=[jax.ShapeDtypeStruct((NSA_WIDTH, s), F32),
                   jax.ShapeDtypeStruct((g, nsbp, s), BF16)],
        scratch_shapes=[pltpu.VMEM((g, ncp, GQ), F32), pltpu.VMEM((g, ncp, Q_BLOCK), F32),
                        pltpu.VMEM((g, nsbp, Q_BLOCK), F32)],
        compiler_params=_cparams(1),
        name="cmp_select",
    )(qhi_t, qlo_t, kc_cat, vct, wc_t, ext_q)


def _sel_kernel(qhi_ref, neg_ref, kaug_ref, vt512_ref, vt128_ref, ws_ref, ext_ref,
                kwin_ref, vwt_ref, ww_ref, out_ref, owin_ref,
                qaug_ref, m_ref, acc_ref, s0_ref, s1_ref, s2_ref, cm0_ref, cm1_ref, cm2_ref):
    qb = pl.program_id(1)
    nsbp = neg_ref.shape[0]
    nsc = nsbp // SUPER
    qg = _heads_to_lanes(qhi_ref[...])
    qx = jnp.concatenate([qg, ext_ref[...]], axis=0)
    s_win = _window_scores(qb, qg, kwin_ref, ww_ref)
    blk = lax.broadcasted_iota(jnp.int32, (SUPER, Q_BLOCK), 0)

    def aug(neg_rows):
        tiled = jnp.concatenate([neg_rows] * GROUP_SIZE, axis=1).astype(BF16)
        return jnp.concatenate([tiled, qx], axis=0)

    for sc in range(nsc):
        neg_rows = neg_ref[sc * SUPER:(sc + 1) * SUPER, :].astype(F32)
        qaug_ref[sc] = aug(jnp.where(blk + sc * SUPER >= 2 * qb - 2, NEG, neg_rows))

    m_ref[...] = jnp.full(m_ref.shape, M_INIT, F32)
    acc_ref[...] = jnp.zeros(acc_ref.shape, F32)

    def update(s, col_max, vt):
        m_old = m_ref[...]
        m_new = jnp.maximum(m_old, col_max)
        p = jnp.exp2(s - m_new).astype(BF16)
        acc_ref[...] = acc_ref[...] * jnp.exp2(m_old - m_new) + _dot(vt, p)
        m_ref[...] = m_new

    n_far = (jnp.maximum(qb - 1, 0) * Q_BLOCK + KV_CHUNK - 1) // KV_CHUNK

    def scores(c, dst_ref, max_ref):
        k = kaug_ref[pl.ds(pl.multiple_of(c * KV_CHUNK, KV_CHUNK), KV_CHUNK), :]
        sc = (c * (KV_CHUNK // SEL_BLOCK)) // SUPER
        s = _dot(k, qaug_ref[sc])
        dst_ref[...] = s
        max_ref[...] = jnp.max(s, axis=0, keepdims=True)

    kb0 = jnp.maximum(qb - 1, 0)
    ws = pl.multiple_of(kb0 * Q_BLOCK, Q_BLOCK)
    off = pl.multiple_of(jnp.where(qb == 0, Q_BLOCK, 0), Q_BLOCK)
    b_lo = 2 * kb0
    sc_lo = pl.multiple_of((b_lo // SUPER) * SUPER, SUPER)
    sc_hi = pl.multiple_of(((b_lo + 3) // SUPER) * SUPER, SUPER)
    neg_lo = neg_ref[pl.ds(sc_lo, SUPER), :].astype(F32)
    neg_hi = neg_ref[pl.ds(sc_hi, SUPER), :].astype(F32)
    near_neg = jnp.where(blk >= SUPER // 2, neg_lo, neg_hi)
    k = kaug_ref[pl.ds(ws, SEL_NEAR), :]
    s_near = _dot(k, aug(near_neg)) + _bias_tile(ws_ref, off, SEL_NEAR)
    vt_near = jnp.concatenate([vt128_ref[kb0], vt128_ref[kb0 + 1]], axis=1)

    bufs = ((s0_ref, cm0_ref), (s1_ref, cm1_ref), (s2_ref, cm2_ref))
    last = jnp.maximum(n_far - 1, 0)
    scores(0, *bufs[0])
    scores(jnp.minimum(1, last), *bufs[1])
    update(s_near, jnp.max(s_near, axis=0, keepdims=True), vt_near)
    _window_finish(qb, s_win, vwt_ref, owin_ref)

    def far_triple(c):
        for j in range(3):
            scores(jnp.minimum(c + j + 2, last), *bufs[(j + 2) % 3])
            s_ref, cm_ref = bufs[j]
            update(s_ref[...], cm_ref[...], vt512_ref[c + j])

    def far_six(i, carry):
        far_triple(6 * i)
        far_triple(6 * i + 3)
        return carry

    lax.fori_loop(0, n_far // 6, far_six, 0)
    done = (n_far // 6) * 6

    @pl.when(n_far - done >= 3)
    def _():
        far_triple(done)

    done = (n_far // 3) * 3
    for j in range(2):
        @pl.when(n_far - done > j)
        def _():
            s_ref, cm_ref = bufs[j]
            update(s_ref[...], cm_ref[...], vt512_ref[done + j])

    acc = acc_ref[...]
    _store_heads(out_ref, acc[0:HEAD_DIM] * (1.0 / acc[HEAD_DIM:HEAD_DIM + 1]))


def _sel_win_attention(qhi_t, neg_t, kaug, vt512, vt128, ws_t, ext_q, kwin, vwt, ww_t):
    s = qhi_t.shape[1]
    g = N_GROUPS
    nqb = s // Q_BLOCK
    nsbp = neg_t.shape[1]
    gr = GROUP_SIZE * HEAD_DIM
    once = pl.Buffered(1)
    return pl.pallas_call(
        _sel_kernel,
        grid=(g, nqb),
        in_specs=[pl.BlockSpec((gr, Q_BLOCK), lambda gi, qb: (gi, qb)),
                  pl.BlockSpec((None, nsbp, Q_BLOCK), lambda gi, qb: (gi, 0, qb)),
                  pl.BlockSpec((None, s, 256), lambda gi, qb: (gi, 0, 0), pipeline_mode=once),
                  pl.BlockSpec((None, s // KV_CHUNK, VT_ROWS, KV_CHUNK), lambda gi, qb: (gi, 0, 0, 0),
                               pipeline_mode=once),
                  pl.BlockSpec((None, s // 128, VT_ROWS, 128), lambda gi, qb: (gi, 0, 0, 0),
                               pipeline_mode=once),
                  pl.BlockSpec((GROUP_SIZE,) + ws_t.shape[1:], lambda gi, qb: (gi, 0, 0)),
                  pl.BlockSpec((None, HEAD_DIM, GQ), lambda gi, qb: (gi, 0, 0)),
                  pl.BlockSpec((None, s, 128), lambda gi, qb: (gi, 0, 0), pipeline_mode=once),
                  pl.BlockSpec((None, s // 128, VT_ROWS, 128), lambda gi, qb: (gi, 0, 0, 0),
                               pipeline_mode=once),
                  pl.BlockSpec((GROUP_SIZE,) + ww_t.shape[1:], lambda gi, qb: (gi, 0, 0))],
        out_specs=[pl.BlockSpec((gr, Q_BLOCK), lambda gi, qb: (gi, qb)),
                   pl.BlockSpec((gr, Q_BLOCK), lambda gi, qb: (gi, qb))],
        out_shape=[jax.ShapeDtypeStruct((NSA_WIDTH, s), F32),
                   jax.ShapeDtypeStruct((NSA_WIDTH, s), F32)],
        scratch_shapes=[pltpu.VMEM((nsbp // SUPER, 256, GQ), BF16),
                        pltpu.VMEM((1, GQ), F32),
                        pltpu.VMEM((VT_ROWS, GQ), F32),
                        pltpu.VMEM((KV_CHUNK, GQ), F32),
                        pltpu.VMEM((KV_CHUNK, GQ), F32),
                        pltpu.VMEM((KV_CHUNK, GQ), F32),
                        pltpu.VMEM((1, GQ), F32),
                        pltpu.VMEM((1, GQ), F32),
                        pltpu.VMEM((1, GQ), F32)],
        compiler_params=_cparams(2),
        name="sel_win_attention",
    )(qhi_t, neg_t, kaug, vt512, vt128, ws_t, ext_q, kwin, vwt, ww_t)


def _window_scores(qb, qg, kwin_ref, ww_ref):
    kb0 = jnp.maximum(qb - WINDOW // Q_BLOCK, 0)
    ws = pl.multiple_of(kb0 * Q_BLOCK, Q_BLOCK)
    off = pl.multiple_of(jnp.maximum(WINDOW - qb * Q_BLOCK, 0), Q_BLOCK)
    qx = jnp.concatenate([qg, jnp.zeros_like(qg)], axis=0)
    return _dot(kwin_ref[pl.ds(ws, WIN_KEYS), :], qx) + _bias_tile(ww_ref, off, WIN_KEYS)


def _window_finish(qb, s, vwt_ref, out_ref):
    kb0 = jnp.maximum(qb - WINDOW // Q_BLOCK, 0)
    m = jnp.max(s, axis=0, keepdims=True)
    p = jnp.exp2(s - m).astype(BF16)
    vt = jnp.concatenate([vwt_ref[kb0 + j] for j in range(WIN_KEYS // Q_BLOCK)], axis=1)
    acc = _dot(vt, p)
    _store_heads(out_ref, acc[0:HEAD_DIM] * (1.0 / acc[HEAD_DIM:HEAD_DIM + 1]))


def _lru_rows(u_ref, cw_ref, cb_ref, wa_ref, ba_ref, wx_ref, bx_ref, lam_ref, h_ref,
              tail_ref, hprev_ref, uc_ref):
    tm = u_ref.shape[0]

    @pl.when(pl.program_id(0) == 0)
    def _():
        tail_ref[...] = jnp.zeros(tail_ref.shape, F32)
        hprev_ref[...] = jnp.zeros(hprev_ref.shape, F32)

    u = u_ref[...]
    tail = tail_ref[...]
    row8 = lax.broadcasted_iota(jnp.int32, (8, LRU_WIDTH), 0)
    uc = cb_ref[...] + u * cw_ref[CONV_WIDTH - 1:CONV_WIDTH, :]
    uc_head = uc[0:8]
    for j in range(1, CONV_WIDTH):
        w_j = cw_ref[CONV_WIDTH - 1 - j:CONV_WIDTH - j, :]
        sh = pltpu.roll(u, j, 0)
        uc = uc + sh * w_j
        uc_head = uc_head + jnp.where(row8 < j, pltpu.roll(tail, j, 0), sh[0:8]) * w_j
    tail_ref[...] = u[tm - 8:tm]
    uc_ref[...] = uc
    uc_ref[0:8, :] = uc_head
    uc = uc_ref[...]

    ucb = uc.astype(BF16)
    r = _sigmoid(_dot(ucb, wa_ref[...]) + ba_ref[...])
    ig = _sigmoid(_dot(ucb, wx_ref[...]) + bx_ref[...])
    nl = -lam_ref[...]
    softplus = jnp.maximum(nl, 0.0) + jnp.log(1.0 + jnp.exp(-jnp.abs(nl)))
    a = jnp.exp((-LRU_C * r) * softplus)
    b = jnp.sqrt(1.0 - a * a) * (ig * uc)

    a = a.reshape(tm // 8, 8, LRU_WIDTH)
    b = b.reshape(tm // 8, 8, LRU_WIDTH)
    sub = lax.broadcasted_iota(jnp.int32, a.shape, 1)
    for step in (1, 2, 4):
        a_s = pltpu.roll(a, step, 1)
        b_s = pltpu.roll(b, step, 1)
        ok = sub >= step
        b = jnp.where(ok, a * b_s + b, b)
        a = jnp.where(ok, a * a_s, a)
    carry = hprev_ref[...]
    for j in range(tm // 8):
        hj = a[j] * carry + b[j]
        h_ref[8 * j:8 * j + 8, :] = hj
        carry = hj[7:8]
    hprev_ref[...] = carry


OUT_TM = 512


def _out_kernel(x_ref, g_ref, oc_ref, os_ref, ow_ref, brt_ref, u_ref,
                cw_ref, cb_ref, wa_ref, ba_ref, wx_ref, bx_ref, lam_ref,
                wg_ref, wpa_ref, wpb_ref, wo_ref, y_ref,
                h_ref, tail_ref, hprev_ref, uc_ref):
    _lru_rows(u_ref, cw_ref, cb_ref, wa_ref, ba_ref, wx_ref, bx_ref, lam_ref, h_ref,
              tail_ref, hprev_ref, uc_ref)
    h_in = _normed_input(x_ref, g_ref)
    gate = lambda j: _dot(h_in, wg_ref[:, GATE_OFFS[j]:GATE_OFFS[j + 1]])
    mg = gate(2)
    brt = brt_ref[...]
    parts = []
    for h in range(N_HEADS):
        rows = slice(h * HEAD_DIM, (h + 1) * HEAD_DIM)
        acc = None
        for b, o_ref in enumerate((oc_ref, os_ref, ow_ref)):
            term = brt[b * N_HEADS + h:b * N_HEADS + h + 1, :] * o_ref[rows, :]
            acc = term if acc is None else acc + term
        parts.append(acc)
    gn = gate(0)
    ya = jnp.concatenate(parts, axis=0).T * (gn * _sigmoid(gn))
    y_a = _dot(ya.astype(BF16), wpa_ref[...])
    gl = gate(1)
    y_b = _dot((h_ref[...] * (gl * _sigmoid(gl))).astype(BF16), wpb_ref[...])
    m = _sigmoid(mg[:, :D_MODEL]) * y_a + _sigmoid(mg[:, D_MODEL:]) * y_b
    y_ref[...] = x_ref[...] + _dot(m.astype(BF16), wo_ref[...])


def _output(x2, norm_gain, oc_t, os_t, ow_t, br_t, u_lru, lru_params, w_gate, wpa, wpb, wo):
    s = x2.shape[0]
    tm = OUT_TM
    rows = lambda n: pl.BlockSpec((tm, n), lambda i: (i, 0))
    cols = lambda n: pl.BlockSpec((n, tm), lambda i: (0, i))
    full = lambda a: pl.BlockSpec(a.shape, lambda i: (0, 0))
    return pl.pallas_call(
        _out_kernel,
        grid=(s // tm,),
        in_specs=[rows(D_MODEL), pl.BlockSpec((1, D_MODEL), lambda i: (0, 0)),
                  cols(NSA_WIDTH), cols(NSA_WIDTH), cols(NSA_WIDTH), cols(32), rows(LRU_WIDTH)]
        + [full(p) for p in lru_params]
        + [full(w_gate), full(wpa), full(wpb), full(wo)],
        out_specs=rows(D_MODEL),
        out_shape=jax.ShapeDtypeStruct((s, D_MODEL), F32),
        scratch_shapes=[pltpu.VMEM((tm, LRU_WIDTH), F32),
                        pltpu.VMEM((8, LRU_WIDTH), F32),
                        pltpu.VMEM((1, LRU_WIDTH), F32),
                        pltpu.VMEM((tm, LRU_WIDTH), F32)],
        compiler_params=_cparams(1),
        name="output",
    )(x2, norm_gain.reshape(1, D_MODEL), oc_t, os_t, ow_t, br_t, u_lru, *lru_params,
      w_gate, wpa, wpb, wo)


def _t5_bucket_table(n_dist):
    n = np.arange(n_dist)
    max_exact = N_BUCKETS // 2
    nf = np.maximum(n, 1).astype(np.float32)
    large = max_exact + (np.log(nf / np.float32(max_exact)) / np.float32(math.log(MAX_DISTANCE / max_exact))
                         * np.float32(N_BUCKETS - max_exact)).astype(np.int32)
    return np.where(n < max_exact, n, np.minimum(large, N_BUCKETS - 1))


def _bias_tiles_kernel(relb_ref, bw_ref, bs_ref, bc_ref, ww_ref, ws_ref, wc_ref):
    h = pl.program_id(0)
    far = relb_ref[N_BUCKETS - 1, h]
    for bk_ref, out_ref, rel in ((bw_ref, ww_ref, False), (bs_ref, ws_ref, True), (bc_ref, wc_ref, True)):
        bk = bk_ref[...]
        acc = jnp.full(bk.shape, NEG, F32)
        for b in range(N_BUCKETS):
            val = relb_ref[b, h] - far if rel else relb_ref[b, h]
            acc = jnp.where(bk == b, val * LOG2E, acc)
        out_ref[...] = acc


def _bias_tables(rel_bias):
    bucket = _t5_bucket_table(2 * WINDOW)
    i = np.arange(Q_BLOCK)[None, :]

    def index_tile(dist, valid):
        return jnp.asarray(np.where(valid, bucket[np.clip(dist, 0, bucket.size - 1)], -1).astype(np.int32))

    u = np.arange(WINDOW + WIN_KEYS)[:, None]
    d_win = i - u + WINDOW
    bw = index_tile(d_win, (d_win >= 0) & (d_win < WINDOW))
    u = np.arange(SEL_NEAR + Q_BLOCK)[:, None]
    d_sel = i - u + Q_BLOCK
    bs = index_tile(d_sel, d_sel >= 0)
    u = np.arange(CMP_NEAR + 16)[:, None]
    d_cmp = i - CMP_STRIDE * (u - 16) - (CMP_BLOCK - 1)
    bc = index_tile(d_cmp, d_cmp >= 0)
    full = lambda a: pl.BlockSpec(a.shape, lambda h: (0, 0))
    per_head = lambda a: pl.BlockSpec((None,) + a.shape, lambda h: (h, 0, 0))
    ww, ws, wc = pl.pallas_call(
        _bias_tiles_kernel,
        grid=(N_HEADS,),
        in_specs=[pl.BlockSpec(memory_space=pltpu.SMEM), full(bw), full(bs), full(bc)],
        out_specs=[per_head(bw), per_head(bs), per_head(bc)],
        out_shape=[jax.ShapeDtypeStruct((N_HEADS,) + a.shape, F32) for a in (bw, bs, bc)],
        compiler_params=_cparams(1),
        name="bias_tiles",
    )(rel_bias, bw, bs, bc)
    b_far = rel_bias[N_BUCKETS - 1] * LOG2E
    hi = b_far.astype(BF16)
    lo = (b_far - hi.astype(F32)).astype(BF16)
    ext = jnp.zeros((N_HEADS, HEAD_DIM, Q_BLOCK), BF16)
    ext = ext.at[:, 0, :].set(hi[:, None]).at[:, 1, :].set(lo[:, None])
    ext = ext.reshape(N_GROUPS, GROUP_SIZE, HEAD_DIM, Q_BLOCK).transpose(0, 2, 1, 3)
    return ww, ws, wc, ext.reshape(N_GROUPS, HEAD_DIM, GQ)


def _block_diag(w):
    n, d, e = w.shape
    eye = jnp.eye(n, dtype=w.dtype)
    return (eye[:, None, :, None] * w[:, :, None, :]).reshape(n * d, n * e)


def kernel(x, norm_gain, w_in, q_norm_gain, k_norm_gain, cmp_pe, cmp_w1, cmp_b1, cmp_w2, rel_bias,
           conv_w, conv_b, lru_wa, lru_ba, lru_wx, lru_bx, lru_lambda, w_proj_a, w_proj_b, w_out):
    bsz, s, _ = x.shape
    assert bsz == 1 and s % 1024 == 0 and s >= 1024
    x2 = x.reshape(s, D_MODEL)
    nsb = s // SEL_BLOCK
    nsbp = -(-nsb // SUPER) * SUPER

    o = IN_OFFS
    w16 = w_in.astype(BF16)
    pad = jnp.zeros((D_MODEL, BR_PAD - 3 * N_HEADS), BF16)
    w_front = jnp.concatenate([w16[:, o[0]:o[2]], w16[:, o[3]:o[4]], pad, w16[:, o[4]:o[5]]], axis=1)
    w_gate = jnp.concatenate([w16[:, o[2]:o[3]], w16[:, o[5]:o[7]]], axis=1)

    scale = HEAD_DIM ** -0.5 * LOG2E
    qgain_col = jnp.tile(q_norm_gain * scale, N_HEADS).reshape(NSA_WIDTH, 1)
    kgain_rows = jnp.tile(k_norm_gain, (1, N_GROUPS))
    (qhi_t, qlo_t, br_t, kaug, vt512, vt128, kwin, vwt, kvc, u_lru) = _front(
        x2, norm_gain, w_front, qgain_col, kgain_rows)

    w2p = jnp.pad(cmp_w2, ((0, 0), (0, 0), (0, 128 - HEAD_DIM)))
    kc_gain = jnp.pad(k_norm_gain[0], (0, 128 - HEAD_DIM)).reshape(1, 128)
    kc_cat, vct = _compress(kvc, cmp_w1, cmp_b1.reshape(2, 1, CMP_HIDDEN), w2p,
                            cmp_pe.reshape(2, 1, CMP_BLOCK * HEAD_DIM), kc_gain)

    ww_t, ws_t, wc_t, ext_q = _bias_tables(rel_bias)
    oc_t, neg_t = _cmp_attention(qhi_t, qlo_t, kc_cat, vct, wc_t, ext_q, nsbp)
    os_t, ow_t = _sel_win_attention(qhi_t, neg_t, kaug, vt512, vt128, ws_t, ext_q, kwin, vwt, ww_t)

    row = lambda v: v.reshape(1, LRU_WIDTH)
    lru_params = (conv_w, row(conv_b), _block_diag(lru_wa).astype(BF16), row(lru_ba),
                  _block_diag(lru_wx).astype(BF16), row(lru_bx), row(lru_lambda))
    y = _output(x2, norm_gain, oc_t, os_t, ow_t, br_t, u_lru, lru_params, w_gate,
                w_proj_a.astype(BF16), w_proj_b.astype(BF16), w_out.astype(BF16))
    return y.reshape(bsz, s, D_MODEL)
```
